```python
import jax, jax.numpy as jnp
from jax import lax
import numpy as np

D_MODEL = 2048
BATCH = 1
SEQ = 16384
DEPTH = 2

N_EVEN = (DEPTH + 1) // 2
N_ODD = DEPTH // 2
EPS = 1e-6
ROPE_THETA = 10000.0
Q_BLOCK = 128

M_HEADS = 4
M_QK = 128
M_V = 256
M_CHUNK = 64
GATE_SOFTCAP = 15.0

A_HEADS = 8
A_NOPE = 128
A_ROPE = 64
A_V = 128
A_Q_LORA = 768
A_KV_LORA = 512

S_HEADS = 32
S_KV_HEADS = 4
S_HD = 64
S_WINDOW = 128

N_EXPERTS = 64
TOP_K = 8
N_GROUPS = 8
TOPK_GROUPS = 4
E_FF = 256
SHARED_FF = 256
ROUTED_SCALE = 2.5
E_BLOCK = 256

IN_SIZES = (
    M_HEADS * M_QK,
    M_HEADS * M_QK,
    M_HEADS * M_V,
    M_HEADS * M_V,
    M_HEADS,
    M_HEADS,
    A_Q_LORA,
    A_KV_LORA,
    A_ROPE,
)
IN_DIM = sum(IN_SIZES)
MIX_OUT = M_HEADS * M_V + A_HEADS * A_V
SWA_QKV = (S_HEADS + 2 * S_KV_HEADS) * S_HD

kernel_name = "hybrid_mlstm_mla_swa_moe_adaln"


def rms_norm(x, gain=None):
    xf = x.astype(jnp.float32)
    y = xf * lax.rsqrt(jnp.mean(xf * xf, axis=-1, keepdims=True) + EPS)
    if gain is not None:
        y = y * gain.astype(jnp.float32)
    return y.astype(x.dtype)


def rope(x, positions):
    d = x.shape[-1]
    half = d // 2
    inv_freq = jnp.power(ROPE_THETA, -jnp.arange(half, dtype=jnp.float32) / half)
    ang = positions.astype(jnp.float32)[..., None] * inv_freq
    cos = jnp.cos(ang)[:, :, None, :]
    sin = jnp.sin(ang)[:, :, None, :]
    xf = x.astype(jnp.float32)
    x1, x2 = xf[..., :half], xf[..., half:]
    out = jnp.concatenate([x1 * cos - x2 * sin, x2 * cos + x1 * sin], axis=-1)
    return out.astype(x.dtype)


def modulate(x, shift, scale):
    return rms_norm(x) * (1 + scale[:, None, :]) + shift[:, None, :]


def mlstm_chunkwise(q, k, v, i_pre, f_pre):
    B, S, H, dk = q.shape
    dv = v.shape[-1]
    L = M_CHUNK
    nc = S // L
    q = q.astype(jnp.float32)
    k = k.astype(jnp.float32) * (dk ** -0.5)
    v = v.astype(jnp.float32)
    ig = GATE_SOFTCAP * jnp.tanh(i_pre.astype(jnp.float32) / GATE_SOFTCAP)
    lf = jax.nn.log_sigmoid(GATE_SOFTCAP * jnp.tanh(f_pre.astype(jnp.float32) / GATE_SOFTCAP))

    def to_chunks(t):
        return t.reshape(B, nc, L, H, t.shape[-1]).transpose(1, 0, 3, 2, 4)

    def gate_chunks(t):
        return t.reshape(B, nc, L, H).transpose(1, 0, 3, 2)

    causal = jnp.tril(jnp.ones((L, L), dtype=bool))

    def step(carry, xs):
        C, n, m = carry
        qc, kc, vc, igc, lfc = xs
        b = jnp.cumsum(lfc, axis=-1)
        Dm = b[..., :, None] - b[..., None, :] + igc[..., None, :]
        Dm = jnp.where(causal, Dm, -jnp.inf)
        inter = b + m[..., None]
        m_row = jnp.maximum(inter, jnp.max(Dm, axis=-1))
        w_intra = jnp.exp(Dm - m_row[..., None])
        w_inter = jnp.exp(inter - m_row)
        s = jnp.einsum('bhjd,bhsd->bhjs', qc, kc) * w_intra
        num = (jnp.einsum('bhjs,bhsv->bhjv', s, vc)
               + w_inter[..., None] * jnp.einsum('bhjd,bhvd->bhjv', qc, C))
        den = jnp.sum(s, axis=-1) + w_inter * jnp.einsum('bhjd,bhd->bhj', qc, n)
        h = num / jnp.maximum(jnp.abs(den), jnp.exp(-m_row))[..., None]
        bL = b[..., -1]
        g = bL[..., None] - b + igc
        m_new = jnp.maximum(bL + m, jnp.max(g, axis=-1))
        ws = jnp.exp(g - m_new[..., None])
        decay = jnp.exp(bL + m - m_new)
        C_new = decay[..., None, None] * C + jnp.einsum('bhs,bhsv,bhsd->bhvd', ws, vc, kc)
        n_new = decay[..., None] * n + jnp.einsum('bhs,bhsd->bhd', ws, kc)
        return (C_new, n_new, m_new), h

    init = (jnp.zeros((B, H, dv, dk), jnp.float32),
            jnp.zeros((B, H, dk), jnp.float32),
            jnp.zeros((B, H), jnp.float32))
    _, h = lax.scan(step, init, (to_chunks(q), to_chunks(k), to_chunks(v),
                                 gate_chunks(ig), gate_chunks(lf)))
    return h.transpose(1, 0, 3, 2, 4).reshape(B, S, H, dv)


def causal_attention_blocks(q, k, v, scale):
    B, S, H, dq = q.shape
    nb = S // Q_BLOCK
    outs = []
    for bi in range(nb):
        lo = bi * Q_BLOCK
        hi = lo + Q_BLOCK
        s = jnp.einsum('bqhd,bkhd->bhqk', q[:, lo:hi], k[:, :hi],
                       preferred_element_type=jnp.float32) * scale
        mask = jnp.arange(hi)[None, :] <= (lo + jnp.arange(Q_BLOCK))[:, None]
        p = jax.nn.softmax(jnp.where(mask, s, -jnp.inf), axis=-1)
        outs.append(jnp.einsum('bhqk,bkhd->bqhd', p.astype(v.dtype), v[:, :hi]))
    return jnp.concatenate(outs, axis=1)


def mla(cq, ckv, kr, positions, q_norm, kv_norm, w_uq, w_ukv):
    B, S, _ = cq.shape
    q = (rms_norm(cq, q_norm) @ w_uq).reshape(B, S, A_HEADS, A_NOPE + A_ROPE)
    q = jnp.concatenate([q[..., :A_NOPE], rope(q[..., A_NOPE:], positions)], axis=-1)
    kv = (rms_norm(ckv, kv_norm) @ w_ukv).reshape(B, S, A_HEADS, A_NOPE + A_V)
    k_nope, v = kv[..., :A_NOPE], kv[..., A_NOPE:]
    k_pe = rope(kr[:, :, None, :], positions)
    k = jnp.concatenate([k_nope, jnp.broadcast_to(k_pe, (B, S, A_HEADS, A_ROPE))], axis=-1)
    o = causal_attention_blocks(q, k, v, (A_NOPE + A_ROPE) ** -0.5)
    return o.reshape(B, S, A_HEADS * A_V)


def even_mixer(h, positions, w_in, b_if, mlstm_norm, q_norm, kv_norm, w_uq, w_ukv, w_out):
    B, S, _ = h.shape
    proj = h @ w_in
    parts = []
    off = 0
    for sz in IN_SIZES:
        parts.append(proj[..., off:off + sz])
        off += sz
    mq, mk, mv, mo, mi, mf, cq, ckv, kr = parts
    hm = mlstm_chunkwise(mq.reshape(B, S, M_HEADS, M_QK), mk.reshape(B, S, M_HEADS, M_QK),
                         mv.reshape(B, S, M_HEADS, M_V),
                         mi + b_if[:M_HEADS], mf + b_if[M_HEADS:])
    hm = rms_norm(hm).reshape(B, S, M_HEADS * M_V) * mlstm_norm.astype(jnp.float32)
    hm = (hm * jax.nn.sigmoid(mo.astype(jnp.float32))).astype(h.dtype)
    ha = mla(cq, ckv, kr, positions, q_norm, kv_norm, w_uq, w_ukv)
    return jnp.concatenate([hm, ha], axis=-1) @ w_out


def swa_sinks(q, k, v, sinks):
    B, S, Hq, D = q.shape
    Hkv = k.shape[2]
    G = Hq // Hkv
    nb = S // Q_BLOCK
    qb = q.reshape(B, nb, Q_BLOCK, Hkv, G, D).transpose(1, 0, 2, 3, 4, 5)

    def window(t):
        tb = t.reshape(B, nb, Q_BLOCK, Hkv, D)
        prev = jnp.concatenate([jnp.zeros_like(tb[:, :1]), tb[:, :-1]], axis=1)
        return jnp.concatenate([prev, tb], axis=2).transpose(1, 0, 2, 3, 4)

    kw, vw = window(k), window(v)
    dist = Q_BLOCK + jnp.arange(Q_BLOCK)[:, None] - jnp.arange(2 * Q_BLOCK)[None, :]
    band = (dist >= 0) & (dist < S_WINDOW)
    sink = sinks.astype(jnp.float32).reshape(Hkv, G)[None, :, :, None, None]
    scale = D ** -0.5

    def one(args):
        qi, ki, vi, bidx = args
        s = jnp.einsum('bqhgd,bkhd->bhgqk', qi, ki, preferred_element_type=jnp.float32) * scale
        kpos = bidx * Q_BLOCK - Q_BLOCK + jnp.arange(2 * Q_BLOCK)
        valid = band & (kpos >= 0)[None, :]
        s = jnp.where(valid, s, -jnp.inf)
        m = jnp.maximum(jnp.max(s, axis=-1, keepdims=True), sink)
        p = jnp.exp(s - m)
        p = p / (jnp.sum(p, axis=-1, keepdims=True) + jnp.exp(sink - m))
        return jnp.einsum('bhgqk,bkhd->bqhgd', p.astype(vi.dtype), vi)

    out = lax.map(one, (qb, kw, vw, jnp.arange(nb)))
    return out.transpose(1, 0, 2, 3, 4, 5).reshape(B, S, Hq * D)


def odd_mixer(h, positions, w_qkv, b_qkv, sinks, w_o):
    B, S, _ = h.shape
    qkv = h @ w_qkv + b_qkv
    nq = S_HEADS * S_HD
    nk = S_KV_HEADS * S_HD
    q = rope(qkv[..., :nq].reshape(B, S, S_HEADS, S_HD), positions)
    k = rope(qkv[..., nq:nq + nk].reshape(B, S, S_KV_HEADS, S_HD), positions)
    v = qkv[..., nq + nk:].reshape(B, S, S_KV_HEADS, S_HD)
    return swa_sinks(q, k, v, sinks) @ w_o


def moe(h, w_router, router_bias, w_gate_up, w_down, ws_gate_up, ws_down):
    B, S, D = h.shape
    T = B * S
    TK = T * TOP_K
    hf = h.reshape(T, D)
    scores = jax.nn.sigmoid(jnp.matmul(hf, w_router, preferred_element_type=jnp.float32))
    biased = scores + router_bias.astype(jnp.float32)
    per_group = N_EXPERTS // N_GROUPS
    grp_score = jnp.sum(lax.top_k(biased.reshape(T, N_GROUPS, per_group), 2)[0], axis=-1)
    _, top_groups = lax.top_k(grp_score, TOPK_GROUPS)
    gmask = jnp.sum(jax.nn.one_hot(top_groups, N_GROUPS, dtype=jnp.float32), axis=1) > 0
    masked = jnp.where(jnp.repeat(gmask, per_group, axis=-1), biased, -jnp.inf)
    _, idx = lax.top_k(masked, TOP_K)
    w = jnp.take_along_axis(scores, idx, axis=-1)
    w = w / jnp.sum(w, axis=-1, keepdims=True) * ROUTED_SCALE
    flat_e = idx.reshape(-1)
    order = jnp.argsort(flat_e)
    se = flat_e[order]
    stok = (jnp.arange(TK, dtype=jnp.int32) // TOP_K)[order]
    sw = w.reshape(-1)[order]
    gs = jnp.bincount(flat_e, length=N_EXPERTS).astype(jnp.int32)
    start = jnp.cumsum(gs) - gs
    pgs = ((gs + E_BLOCK - 1) // E_BLOCK) * E_BLOCK
    pend = jnp.cumsum(pgs)
    pstart = pend - pgs
    dest = pstart[se] + jnp.arange(TK, dtype=jnp.int32) - start[se]
    NPAD = TK + N_EXPERTS * E_BLOCK
    nblk = NPAD // E_BLOCK
    src = jnp.full((NPAD,), T, jnp.int32).at[dest].set(stok)
    wbuf = jnp.zeros((NPAD,), jnp.float32).at[dest].set(sw)
    xbuf = jnp.take(hf, src, axis=0, mode='fill', fill_value=0)
    block_e = jnp.minimum(jnp.searchsorted(pend, jnp.arange(nblk) * E_BLOCK, side='right'),
                          N_EXPERTS - 1)

    def expert_block(args):
        xb, e = args
        gu = xb @ w_gate_up[e]
        return (jax.nn.silu(gu[:, :E_FF]) * gu[:, E_FF:]) @ w_down[e]

    ybuf = lax.map(expert_block, (xbuf.reshape(nblk, E_BLOCK, D), block_e)).reshape(NPAD, D)
    ybuf = ybuf * wbuf[:, None].astype(ybuf.dtype)
    routed = jnp.zeros((T, D), ybuf.dtype).at[src].add(ybuf, mode='drop')
    sg = hf @ ws_gate_up
    shared = (jax.nn.silu(sg[:, :SHARED_FF]) * sg[:, SHARED_FF:]) @ ws_down
    return (routed + shared).reshape(B, S, D)


def setup_inputs(seed: int = 0) -> dict:
    key = jax.random.key(seed)
    ks = iter(jax.random.split(key, 40))
    f32 = jnp.float32
    D = D_MODEL

    def nrm(shape, scale):
        return jax.random.normal(next(ks), shape, f32) * scale

    def gain(shape):
        return 1.0 + 0.02 * jax.random.normal(next(ks), shape, f32)

    x = nrm((BATCH, SEQ, D), 1.0)
    c = nrm((BATCH, D), 1.0)
    offset = jax.random.randint(next(ks), (BATCH, 1), 0, 1024, dtype=jnp.int32)
    positions = offset + jnp.arange(SEQ, dtype=jnp.int32)[None, :]

    w_ada = nrm((DEPTH, D, 6 * D), 0.5 * D ** -0.5)
    b_ada = nrm((DEPTH, 6 * D), 0.02)

    a_w_in = nrm((N_EVEN, D, IN_DIM), D ** -0.5)
    b_i = nrm((N_EVEN, M_HEADS), 0.1)
    b_f = 3.0 + 0.5 * jax.random.normal(next(ks), (N_EVEN, M_HEADS), f32)
    a_b_if = jnp.concatenate([b_i, b_f], axis=-1)
    a_mlstm_norm = gain((N_EVEN, M_HEADS * M_V))
    a_q_norm = gain((N_EVEN, A_Q_LORA))
    a_kv_norm = gain((N_EVEN, A_KV_LORA))
    a_w_uq = nrm((N_EVEN, A_Q_LORA, A_HEADS * (A_NOPE + A_ROPE)), A_Q_LORA ** -0.5)
    a_w_ukv = nrm((N_EVEN, A_KV_LORA, A_HEADS * (A_NOPE + A_V)), A_KV_LORA ** -0.5)
    a_w_out = nrm((N_EVEN, MIX_OUT, D), MIX_OUT ** -0.5)

    s_w_qkv = nrm((N_ODD, D, SWA_QKV), D ** -0.5)
    s_b_qkv = nrm((N_ODD, SWA_QKV), 0.02)
    s_sinks = nrm((N_ODD, S_HEADS), 1.0)
    s_w_o = nrm((N_ODD, S_HEADS * S_HD, D), (S_HEADS * S_HD) ** -0.5)

    e_w_router = nrm((DEPTH, D, N_EXPERTS), D ** -0.5)
    e_router_bias = nrm((DEPTH, N_EXPERTS), 0.01)
    e_w_gate_up = nrm((DEPTH, N_EXPERTS, D, 2 * E_FF), D ** -0.5)
    e_w_down = nrm((DEPTH, N_EXPERTS, E_FF, D), E_FF ** -0.5)
    e_ws_gate_up = nrm((DEPTH, D, 2 * SHARED_FF), D ** -0.5)
    e_ws_down = nrm((DEPTH, SHARED_FF, D), SHARED_FF ** -0.5)

    final_norm = gain((D,))
    return {
        "x": x, "c": c, "positions": positions,
        "w_ada": w_ada, "b_ada": b_ada,
        "a_w_in": a_w_in, "a_b_if": a_b_if, "a_mlstm_norm": a_mlstm_norm,
        "a_q_norm": a_q_norm, "a_kv_norm": a_kv_norm, "a_w_uq": a_w_uq,
        "a_w_ukv": a_w_ukv, "a_w_out": a_w_out,
        "s_w_qkv": s_w_qkv, "s_b_qkv": s_b_qkv, "s_sinks": s_sinks, "s_w_o": s_w_o,
        "e_w_router": e_w_router, "e_router_bias": e_router_bias,
        "e_w_gate_up": e_w_gate_up, "e_w_down": e_w_down,
        "e_ws_gate_up": e_ws_gate_up, "e_ws_down": e_ws_down,
        "final_norm": final_norm,
    }


def reference(x, c, positions, w_ada, b_ada,
              a_w_in, a_b_if, a_mlstm_norm, a_q_norm, a_kv_norm, a_w_uq, a_w_ukv, a_w_out,
              s_w_qkv, s_b_qkv, s_sinks, s_w_o,
              e_w_router, e_router_bias, e_w_gate_up, e_w_down, e_ws_gate_up, e_ws_down,
              final_norm):
    c_act = jax.nn.silu(c)
    for layer in range(DEPTH):
        mod = c_act @ w_ada[layer] + b_ada[layer]
        sh1, sc1, g1, sh2, sc2, g2 = jnp.split(mod, 6, axis=-1)
        h = modulate(x, sh1, sc1)
        if layer % 2 == 0:
            e = layer // 2
            mix = even_mixer(h, positions, a_w_in[e], a_b_if[e], a_mlstm_norm[e], a_q_norm[e],
                             a_kv_norm[e], a_w_uq[e], a_w_ukv[e], a_w_out[e])
        else:
            o = layer // 2
            mix = odd_mixer(h, positions, s_w_qkv[o], s_b_qkv[o], s_sinks[o], s_w_o[o])
        x = x + g1[:, None, :] * mix
        h = modulate(x, sh2, sc2)
        x = x + g2[:, None, :] * moe(h, e_w_router[layer], e_router_bias[layer],
                                     e_w_gate_up[layer], e_w_down[layer],
                                     e_ws_gate_up[layer], e_ws_down[layer])
    return rms_norm(x, final_norm)
```

```python
import functools

import jax
import jax.numpy as jnp
from jax import lax
from jax.experimental import pallas as pl
from jax.experimental.pallas import tpu as pltpu

F32 = jnp.float32
BF16 = jnp.bfloat16
I32 = jnp.int32
HIGHEST = lax.Precision.HIGHEST

EPS = 1e-6
ROPE_THETA = 10000.0
ROPE_DIM = 64

M_HEADS = 4
M_QK = 128
M_V = 256
GATE_SOFTCAP = 15.0
MLSTM_CHUNK = 128

A_HEADS = 8
A_NOPE = 128
A_ROPE = 64
A_V = 128
A_Q_LORA = 768
A_KV_LORA = 512
A_QK_PAD = 256

S_HEADS = 32
S_KV_HEADS = 4
S_HD = 64
S_BLOCK = 128

N_EXPERTS = 64
TOP_K = 8
N_GROUPS = 8
TOPK_GROUPS = 4
E_FF = 256
SHARED_FF = 256
ROUTED_SCALE = 2.5
E_BLOCK = 256

LANES = 128
VMEM_LIMIT_BYTES = 48 * 1024 * 1024


def _params(*semantics):
    return pltpu.CompilerParams(dimension_semantics=semantics, vmem_limit_bytes=VMEM_LIMIT_BYTES)


def _dot(a, b, precision=None):
    return jnp.dot(a, b, preferred_element_type=F32, precision=precision)


def _dot_nt(a, b):
    return lax.dot_general(a, b, (((1,), (1,)), ((), ())), preferred_element_type=F32)


def _dot_tn(a, b):
    return lax.dot_general(a, b, (((0,), (0,)), ((), ())), preferred_element_type=F32)


def _modulated_norm(x, shift, scale):
    y = x * lax.rsqrt(jnp.mean(x * x, axis=-1, keepdims=True) + EPS)
    return y * (1.0 + scale) + shift


def _rope_lanes(x, cos, sin_signed):
    lane = lax.broadcasted_iota(I32, x.shape, 1)
    first_half = (lane % ROPE_DIM) < (ROPE_DIM // 2)
    swapped = jnp.where(first_half, pltpu.roll(x, LANES - ROPE_DIM // 2, 1), pltpu.roll(x, ROPE_DIM // 2, 1))
    return x * cos + swapped * sin_signed


def _ada_kernel(c_ref, w_ref, b_ref, o_ref):
    c = c_ref[...]
    c_act = c * jax.nn.sigmoid(c)
    o_ref[...] = jnp.sum(c_act * w_ref[...], axis=0, keepdims=True) + b_ref[...]


def _ada(c, w_ada, b_ada):
    depth, d, n = w_ada.shape
    tn = 1024
    return pl.pallas_call(
        _ada_kernel,
        out_shape=jax.ShapeDtypeStruct((depth, 1, n), F32),
        grid=(depth, n // tn),
        in_specs=[pl.BlockSpec((d, 1), lambda l, j: (0, 0)),
                  pl.BlockSpec((None, d, tn), lambda l, j: (l, 0, j)),
                  pl.BlockSpec((None, 1, tn), lambda l, j: (l, 0, j))],
        out_specs=pl.BlockSpec((None, 1, tn), lambda l, j: (l, 0, j)),
        compiler_params=_params("parallel", "parallel"),
        name="ada_mod",
    )(c.reshape(d, 1), w_ada, b_ada.reshape(depth, 1, n))


def _rope_table_kernel(pos_ref, inv_ref, sign_ref, cos_ref, sin_ref):
    ang = pos_ref[...].astype(F32) * inv_ref[...]
    cos_ref[...] = jnp.cos(ang)
    sin_ref[...] = jnp.sin(ang) * sign_ref[...]


def _rope_tables(positions):
    t = positions.shape[0]
    half = ROPE_DIM // 2
    inv_freq = jnp.power(ROPE_THETA, -jnp.arange(half, dtype=F32) / half)
    inv = jnp.tile(inv_freq, LANES // half).reshape(1, LANES)
    sign = jnp.tile(jnp.concatenate([-jnp.ones((half,), F32), jnp.ones((half,), F32)]), LANES // ROPE_DIM)
    tm = min(t, 2048)
    return pl.pallas_call(
        _rope_table_kernel,
        out_shape=(jax.ShapeDtypeStruct((t, LANES), F32), jax.ShapeDtypeStruct((t, LANES), F32)),
        grid=(t // tm,),
        in_specs=[pl.BlockSpec((tm, 1), lambda i: (i, 0)),
                  pl.BlockSpec((1, LANES), lambda i: (0, 0)),
                  pl.BlockSpec((1, LANES), lambda i: (0, 0))],
        out_specs=(pl.BlockSpec((tm, LANES), lambda i: (i, 0)), pl.BlockSpec((tm, LANES), lambda i: (i, 0))),
        compiler_params=_params("parallel"),
        name="rope_tables",
    )(positions.reshape(t, 1), inv, sign.reshape(1, LANES))


def _norm_matmul_kernel(x_ref, sh_ref, sc_ref, w_ref, b_ref, o_ref, h_scr):
    @pl.when(pl.program_id(1) == 0)
    def _():
        h_scr[...] = _modulated_norm(x_ref[...], sh_ref[...], sc_ref[...]).astype(BF16)

    o_ref[...] = (_dot(h_scr[...], w_ref[...]) + b_ref[...]).astype(o_ref.dtype)


def _norm_matmul(x, shift, scale, w, bias, out_dtype, name):
    t, d = x.shape
    n = w.shape[1]
    tm = min(t, 1024)
    tn = 512 if n % 512 == 0 else 256
    return pl.pallas_call(
        _norm_matmul_kernel,
        out_shape=jax.ShapeDtypeStruct((t, n), out_dtype),
        grid=(t // tm, n // tn),
        in_specs=[pl.BlockSpec((tm, d), lambda i, j: (i, 0)),
                  pl.BlockSpec((1, d), lambda i, j: (0, 0)),
                  pl.BlockSpec((1, d), lambda i, j: (0, 0)),
                  pl.BlockSpec((d, tn), lambda i, j: (0, j)),
                  pl.BlockSpec((1, tn), lambda i, j: (0, j))],
        out_specs=pl.BlockSpec((tm, tn), lambda i, j: (i, j)),
        scratch_shapes=[pltpu.VMEM((tm, d), BF16)],
        compiler_params=_params("parallel", "arbitrary"),
        name=name,
    )(x, shift, scale, w, bias)


def _mlstm_kernel(q_ref, k_ref, v_ref, o_ref, g_ref, gb_ref, gain_ref, out_ref, c_scr, n_scr, m_scr):
    L = MLSTM_CHUNK
    tm = q_ref.shape[0]

    @pl.when(pl.program_id(0) == 0)
    def _():
        c_scr[...] = jnp.zeros_like(c_scr)
        n_scr[...] = jnp.zeros_like(n_scr)
        m_scr[...] = jnp.zeros_like(m_scr)

    capped = GATE_SOFTCAP * jnp.tanh((g_ref[...] + gb_ref[...]) / GATE_SOFTCAP)
    log_sig = jnp.minimum(capped, 0.0) - jnp.log1p(jnp.exp(-jnp.abs(capped)))
    lane = lax.broadcasted_iota(I32, capped.shape, 1)
    gate = jnp.where(lane < M_HEADS, capped, log_sig)
    r = lax.broadcasted_iota(I32, (tm, tm), 0)
    c = lax.broadcasted_iota(I32, (tm, tm), 1)
    chunk_tril = jnp.where(((r // L) == (c // L)) & (c <= r), 1.0, 0.0).astype(F32)
    cum = _dot(chunk_tril, gate, precision=HIGHEST)
    gate_rows = gate.T
    cum_rows = cum.T
    rr = lax.broadcasted_iota(I32, (L, L), 0)
    cc = lax.broadcasted_iota(I32, (L, L), 1)
    causal = cc <= rr

    for ci in range(tm // L):
        sl = slice(ci * L, (ci + 1) * L)
        for h in range(M_HEADS):
            qk = slice(h * M_QK, (h + 1) * M_QK)
            vv = slice(h * M_V, (h + 1) * M_V)
            qc = q_ref[sl, qk]
            kc = k_ref[sl, qk].astype(F32) * (M_QK ** -0.5)
            kcb = kc.astype(BF16)
            vc = v_ref[sl, vv]
            ig_col = gate[sl, h:h + 1]
            b_col = cum[sl, M_HEADS + h:M_HEADS + h + 1]
            ig_row = gate_rows[h:h + 1, sl]
            b_row = cum_rows[M_HEADS + h:M_HEADS + h + 1, sl]
            m_prev = m_scr[h:h + 1, 0:1]

            dm = jnp.where(causal, b_col - b_row + ig_row, -jnp.inf)
            inter = b_col + m_prev
            m_row = jnp.maximum(inter, jnp.max(dm, axis=-1, keepdims=True))
            w_intra = jnp.exp(dm - m_row)
            w_inter = jnp.exp(inter - m_row)
            s = _dot_nt(qc, kcb) * w_intra
            c_state = c_scr[h]
            n_state = n_scr[h]
            num = _dot(s.astype(BF16), vc) + w_inter * _dot_nt(qc, c_state.astype(BF16))
            den = (jnp.sum(s, axis=-1, keepdims=True)
                   + w_inter * jnp.sum(qc.astype(F32) * n_state, axis=-1, keepdims=True))
            hh = num / jnp.maximum(jnp.abs(den), jnp.exp(-m_row))

            b_last = b_col[L - 1:L, :]
            g_row = b_last - b_row + ig_row
            g_col = b_last - b_col + ig_col
            m_new = jnp.maximum(b_last + m_prev, jnp.max(g_row, axis=-1, keepdims=True))
            ws_col = jnp.exp(g_col - m_new)
            decay = jnp.exp(b_last + m_prev - m_new)
            vw = (vc.astype(F32) * ws_col).astype(BF16)
            c_scr[h] = decay * c_state + _dot_tn(vw, kcb)
            n_scr[h] = decay * n_state + jnp.sum(kc * ws_col, axis=0, keepdims=True)
            m_scr[h:h + 1, :] = jnp.broadcast_to(m_new, (1, LANES))

            y = hh * lax.rsqrt(jnp.mean(hh * hh, axis=-1, keepdims=True) + EPS) * gain_ref[:, vv]
            y = y * jax.nn.sigmoid(o_ref[sl, vv].astype(F32))
            out_ref[sl, vv] = y.astype(out_ref.dtype)


def _mlstm(proj_a, proj_b, gate_col_block, gate_bias, gain):
    t = proj_a.shape[0]
    tm = min(t, 512)
    nq = M_HEADS * M_QK
    nv = M_HEADS * M_V
    return pl.pallas_call(
        _mlstm_kernel,
        out_shape=jax.ShapeDtypeStruct((t, nv), BF16),
        grid=(t // tm,),
        in_specs=[pl.BlockSpec((tm, nq), lambda i: (i, 0)),
                  pl.BlockSpec((tm, nq), lambda i: (i, 1)),
                  pl.BlockSpec((tm, nv), lambda i: (i, 1)),
                  pl.BlockSpec((tm, nv), lambda i: (i, 2)),
                  pl.BlockSpec((tm, LANES), lambda i: (i, gate_col_block)),
                  pl.BlockSpec((1, LANES), lambda i: (0, 0)),
                  pl.BlockSpec((1, nv), lambda i: (0, 0))],
        out_specs=pl.BlockSpec((tm, nv), lambda i: (i, 0)),
        scratch_shapes=[pltpu.VMEM((M_HEADS, M_V, M_QK), F32),
                        pltpu.VMEM((M_HEADS, 1, M_QK), F32),
                        pltpu.VMEM((8, LANES), F32)],
        compiler_params=_params("arbitrary"),
        name="mlstm",
    )(proj_a, proj_a, proj_a, proj_a, proj_b, gate_bias, gain)


def _mla_up_kernel(pb_ref, qn_ref, kvn_ref, wq_ref, wkv_ref, cos_ref, sin_ref,
                   q_ref, k_ref, v_ref, cq_scr, ckv_scr, kr_scr):
    @pl.when(pl.program_id(1) == 0)
    def _():
        cq = pb_ref[:, 0:A_Q_LORA]
        cq_scr[...] = (cq * lax.rsqrt(jnp.mean(cq * cq, axis=-1, keepdims=True) + EPS) * qn_ref[...]).astype(BF16)
        ckv = pb_ref[:, A_Q_LORA:A_Q_LORA + A_KV_LORA]
        ckv_scr[...] = (ckv * lax.rsqrt(jnp.mean(ckv * ckv, axis=-1, keepdims=True) + EPS)
                        * kvn_ref[...]).astype(BF16)
        kr = pb_ref[:, A_Q_LORA + A_KV_LORA:A_Q_LORA + A_KV_LORA + LANES]
        kr_scr[...] = _rope_lanes(kr, cos_ref[...], sin_ref[...]).astype(BF16)

    scale = (A_NOPE + A_ROPE) ** -0.5
    qh = _dot(cq_scr[...], wq_ref[...])
    q_pe = _rope_lanes(qh[:, A_NOPE:], cos_ref[...], sin_ref[...])
    q_ref[...] = (jnp.concatenate([qh[:, :A_NOPE], q_pe], axis=1) * scale).astype(q_ref.dtype)
    kvh = _dot(ckv_scr[...], wkv_ref[...])
    k_ref[...] = jnp.concatenate([kvh[:, :A_NOPE].astype(BF16), kr_scr[...]], axis=1)
    v_ref[...] = kvh[:, A_NOPE:].astype(v_ref.dtype)


def _mla_up(proj_b, q_norm, kv_norm, wq, wkv, cos, sin):
    t, nb = proj_b.shape
    tm = min(t, 1024)
    return pl.pallas_call(
        _mla_up_kernel,
        out_shape=(jax.ShapeDtypeStruct((A_HEADS, t, A_QK_PAD), BF16),
                   jax.ShapeDtypeStruct((A_HEADS, t, A_QK_PAD), BF16),
                   jax.ShapeDtypeStruct((A_HEADS, t, A_V), BF16)),
        grid=(t // tm, A_HEADS),
        in_specs=[pl.BlockSpec((tm, nb), lambda i, h: (i, 0)),
                  pl.BlockSpec((1, A_Q_LORA), lambda i, h: (0, 0)),
                  pl.BlockSpec((1, A_KV_LORA), lambda i, h: (0, 0)),
                  pl.BlockSpec((A_Q_LORA, A_QK_PAD), lambda i, h: (0, h)),
                  pl.BlockSpec((A_KV_LORA, A_NOPE + A_V), lambda i, h: (0, h)),
                  pl.BlockSpec((tm, LANES), lambda i, h: (i, 0)),
                  pl.BlockSpec((tm, LANES), lambda i, h: (i, 0))],
        out_specs=(pl.BlockSpec((None, tm, A_QK_PAD), lambda i, h: (h, i, 0)),
                   pl.BlockSpec((None, tm, A_QK_PAD), lambda i, h: (h, i, 0)),
                   pl.BlockSpec((None, tm, A_V), lambda i, h: (h, i, 0))),
        scratch_shapes=[pltpu.VMEM((tm, A_Q_LORA), BF16),
                        pltpu.VMEM((tm, A_KV_LORA), BF16),
                        pltpu.VMEM((tm, LANES), BF16)],
        compiler_params=_params("parallel", "arbitrary"),
        name="mla_up",
    )(proj_b, q_norm, kv_norm, wq, wkv, cos, sin)


def _mla_flash_kernel(q_ref, k_ref, v_ref, o_ref, *, tq, tk):
    i = pl.program_id(1)
    q = q_ref[...]
    row = i * tq + lax.broadcasted_iota(I32, (tq, tk), 0)
    col = lax.broadcasted_iota(I32, (tq, tk), 1)

    def body(j, carry):
        m, l, acc = carry
        start = pl.multiple_of(j * tk, tk)
        s = _dot_nt(q, k_ref[pl.ds(start, tk), :])
        s = jnp.where(col + start <= row, s, -jnp.inf)
        m_new = jnp.maximum(m, jnp.max(s, axis=-1, keepdims=True))
        alpha = jnp.exp(m - m_new)
        p = jnp.exp(s - m_new)
        l = alpha * l + jnp.sum(p, axis=-1, keepdims=True)
        acc = alpha * acc + _dot(p.astype(BF16), v_ref[pl.ds(start, tk), :])
        return m_new, l, acc

    n_kv = ((i + 1) * tq + tk - 1) // tk
    init = (jnp.full((tq, 1), -jnp.inf, F32), jnp.zeros((tq, 1), F32), jnp.zeros((tq, A_V), F32))
    _, l, acc = lax.fori_loop(0, n_kv, body, init)
    o_ref[...] = (acc / l).astype(o_ref.dtype)


def _mla_flash(q, k, v):
    _, t, _ = q.shape
    tq = min(t, 256)
    tk = min(t, 512)
    return pl.pallas_call(
        functools.partial(_mla_flash_kernel, tq=tq, tk=tk),
        out_shape=jax.ShapeDtypeStruct((t, A_HEADS * A_V), BF16),
        grid=(A_HEADS, t // tq),
        in_specs=[pl.BlockSpec((None, tq, A_QK_PAD), lambda h, i: (h, i, 0)),
                  pl.BlockSpec((None, t, A_QK_PAD), lambda h, i: (h, 0, 0)),
                  pl.BlockSpec((None, t, A_V), lambda h, i: (h, 0, 0))],
        out_specs=pl.BlockSpec((tq, A_V), lambda h, i: (i, h)),
        compiler_params=_params("parallel", "arbitrary"),
        name="mla_flash",
    )(q, k, v)


def _proj_residual_kernel(*refs, n_lhs):
    lhs = refs[:n_lhs]
    ws = refs[n_lhs:2 * n_lhs]
    x_ref, g_ref, o_ref = refs[2 * n_lhs:]
    acc = _dot(lhs[0][...], ws[0][...])
    for a, w in zip(lhs[1:], ws[1:]):
        acc = acc + _dot(a[...], w[...])
    o_ref[...] = x_ref[...] + g_ref[...] * acc


def _proj_residual(lhs_list, w_list, x, gate, name):
    t, d = x.shape
    tm = min(t, 1024)
    tn = 512
    n_lhs = len(lhs_list)
    in_specs = ([pl.BlockSpec((tm, a.shape[1]), lambda i, j: (i, 0)) for a in lhs_list]
                + [pl.BlockSpec((w.shape[0], tn), lambda i, j: (0, j)) for w in w_list]
                + [pl.BlockSpec((tm, tn), lambda i, j: (i, j)), pl.BlockSpec((1, tn), lambda i, j: (0, j))])
    return pl.pallas_call(
        functools.partial(_proj_residual_kernel, n_lhs=n_lhs),
        out_shape=jax.ShapeDtypeStruct((t, d), F32),
        grid=(t // tm, d // tn),
        in_specs=in_specs,
        out_specs=pl.BlockSpec((tm, tn), lambda i, j: (i, j)),
        compiler_params=_params("parallel", "parallel"),
        name=name,
    )(*lhs_list, *w_list, x, gate)


def _swa_kernel(q_ref, kc_ref, kp_ref, vc_ref, vp_ref, cosc_ref, sinc_ref, cosp_ref, sinp_ref, sink_ref, o_ref):
    i = pl.program_id(0)
    qb = S_BLOCK
    group = S_HEADS // S_KV_HEADS
    pairs = group // 2
    cos_c, sin_c = cosc_ref[...], sinc_ref[...]
    cos_w = jnp.concatenate([cosp_ref[...], cos_c], axis=0)
    sin_w = jnp.concatenate([sinp_ref[...], sin_c], axis=0)
    kw = jnp.concatenate([kp_ref[...], kc_ref[...]], axis=0).astype(F32)
    vw = jnp.concatenate([vp_ref[...], vc_ref[...]], axis=0)

    r = lax.broadcasted_iota(I32, (pairs * qb, 2 * qb), 0) % qb
    c = lax.broadcasted_iota(I32, (pairs * qb, 2 * qb), 1)
    dist = qb + r - c
    valid = (dist >= 0) & (dist < S_BLOCK) & ((c >= qb) | (i > 0))
    lane = lax.broadcasted_iota(I32, (2 * qb, LANES), 1)
    low = lane < S_HD

    for g in range(S_KV_HEADS):
        col = slice((g // 2) * LANES, (g // 2 + 1) * LANES)
        k_pair = _rope_lanes(kw[:, col], cos_w, sin_w)
        v_pair = vw[:, col]
        keep = low if g % 2 == 0 else jnp.logical_not(low)
        k_own = jnp.where(keep, k_pair, 0.0)
        v_own = jnp.where(keep, v_pair.astype(F32), 0.0)
        k_other = pltpu.roll(k_own, S_HD, 1)
        v_other = pltpu.roll(v_own, S_HD, 1)
        k_lo, k_hi = (k_own, k_other) if g % 2 == 0 else (k_other, k_own)
        v_lo, v_hi = (v_own, v_other) if g % 2 == 0 else (v_other, v_own)
        kd = jnp.concatenate([k_lo, k_hi], axis=0).astype(BF16)
        vd = jnp.concatenate([v_lo, v_hi], axis=0).astype(BF16)
        qs = []
        for p in range(pairs):
            qcol = slice((g * pairs + p) * LANES, (g * pairs + p + 1) * LANES)
            qs.append(_rope_lanes(q_ref[:, qcol].astype(F32), cos_c, sin_c) * (S_HD ** -0.5))
        qg = jnp.concatenate(qs, axis=0).astype(BF16)
        s = _dot_nt(qg, kd)
        probs = []
        for half in range(2):
            sh = jnp.where(valid, s[:, half * 2 * qb:(half + 1) * 2 * qb], -jnp.inf)
            sink = sink_ref[g, half]
            m = jnp.maximum(jnp.max(sh, axis=-1, keepdims=True), sink)
            e = jnp.exp(sh - m)
            den = jnp.sum(e, axis=-1, keepdims=True) + jnp.exp(sink - m)
            probs.append((e / den).astype(BF16))
        o = _dot(jnp.concatenate(probs, axis=1), vd)
        for p in range(pairs):
            qcol = slice((g * pairs + p) * LANES, (g * pairs + p + 1) * LANES)
            o_ref[:, qcol] = o[p * qb:(p + 1) * qb].astype(o_ref.dtype)


def _swa(qkv, cos, sin, sinks):
    t = qkv.shape[0]
    qb = S_BLOCK
    nq = S_HEADS * S_HD
    nk = S_KV_HEADS * S_HD
    group = S_HEADS // S_KV_HEADS
    pairs = group // 2
    sink_cols = jnp.repeat(sinks.astype(F32).reshape(S_KV_HEADS, pairs, 2).transpose(0, 2, 1), qb, axis=-1)
    sink_cols = sink_cols.reshape(S_KV_HEADS, 2, pairs * qb, 1)
    k_blk = nq // nk
    prev = lambda i: jnp.maximum(i - 1, 0)
    return pl.pallas_call(
        _swa_kernel,
        out_shape=jax.ShapeDtypeStruct((t, nq), BF16),
        grid=(t // qb,),
        in_specs=[pl.BlockSpec((qb, nq), lambda i: (i, 0)),
                  pl.BlockSpec((qb, nk), lambda i: (i, k_blk)),
                  pl.BlockSpec((qb, nk), lambda i: (prev(i), k_blk)),
                  pl.BlockSpec((qb, nk), lambda i: (i, k_blk + 1)),
                  pl.BlockSpec((qb, nk), lambda i: (prev(i), k_blk + 1)),
                  pl.BlockSpec((qb, LANES), lambda i: (i, 0)),
                  pl.BlockSpec((qb, LANES), lambda i: (i, 0)),
                  pl.BlockSpec((qb, LANES), lambda i: (prev(i), 0)),
                  pl.BlockSpec((qb, LANES), lambda i: (prev(i), 0)),
                  pl.BlockSpec((S_KV_HEADS, 2, pairs * qb, 1), lambda i: (0, 0, 0, 0))],
        out_specs=pl.BlockSpec((qb, nq), lambda i: (i, 0)),
        compiler_params=_params("parallel"),
        name="swa",
    )(qkv, qkv, qkv, qkv, qkv, cos, sin, cos, sin, sink_cols)


def _expert_of_row(p):
    per_group = N_EXPERTS // N_GROUPS
    return (p % N_GROUPS) * per_group + p // N_GROUPS


def _moe_pre_kernel(x_ref, sh_ref, sc_ref, wr_ref, rb_ref, hf_ref, idx_ref, wt_ref, rank_ref, cnt_ref, carry_scr):
    tm = x_ref.shape[0]
    per_group = N_EXPERTS // N_GROUPS

    @pl.when(pl.program_id(0) == 0)
    def _():
        carry_scr[...] = jnp.zeros_like(carry_scr)

    h = _modulated_norm(x_ref[...], sh_ref[...], sc_ref[...])
    hf_ref[...] = h
    logits = _dot(h, wr_ref[...], precision=HIGHEST)
    scores = jax.nn.sigmoid(logits.T[:N_EXPERTS, :])
    biased = scores + rb_ref[...]

    members = [biased[j * N_GROUPS:(j + 1) * N_GROUPS, :] for j in range(per_group)]
    m1 = members[0]
    for a in members[1:]:
        m1 = jnp.maximum(m1, a)
    first = jnp.full(m1.shape, per_group, I32)
    for j in reversed(range(per_group)):
        first = jnp.where(members[j] == m1, j, first)
    m2 = jnp.full(m1.shape, -jnp.inf, F32)
    for j in range(per_group):
        m2 = jnp.maximum(m2, jnp.where(first == j, -jnp.inf, members[j]))
    group_score = m1 + m2

    g_iota = lax.broadcasted_iota(I32, group_score.shape, 0).astype(F32)
    g_sel = jnp.zeros(group_score.shape, F32)
    for _ in range(TOPK_GROUPS):
        best = jnp.max(group_score, axis=0, keepdims=True)
        gi = jnp.min(jnp.where(group_score == best, g_iota, float(N_GROUPS)), axis=0, keepdims=True)
        hit = g_iota == gi
        g_sel = jnp.where(hit, 1.0, g_sel)
        group_score = jnp.where(hit, -jnp.inf, group_score)
    masked = jnp.concatenate([jnp.where(g_sel > 0.5, a, -jnp.inf) for a in members], axis=0)

    e_iota = _expert_of_row(lax.broadcasted_iota(I32, masked.shape, 0)).astype(F32)
    sel = jnp.zeros(masked.shape, F32)
    idx_rows, w_rows = [], []
    for _ in range(TOP_K):
        best = jnp.max(masked, axis=0, keepdims=True)
        ei = jnp.min(jnp.where(masked == best, e_iota, float(N_EXPERTS)), axis=0, keepdims=True)
        hit = e_iota == ei
        idx_rows.append(ei)
        w_rows.append(jnp.sum(jnp.where(hit, scores, 0.0), axis=0, keepdims=True))
        sel = jnp.where(hit, 1.0, sel)
        masked = jnp.where(hit, -jnp.inf, masked)
    idx = jnp.concatenate(idx_rows, axis=0).astype(I32)
    wts = jnp.concatenate(w_rows, axis=0)
    wts = wts / jnp.sum(wts, axis=0, keepdims=True) * ROUTED_SCALE

    r = lax.broadcasted_iota(I32, (tm, tm), 0)
    c = lax.broadcasted_iota(I32, (tm, tm), 1)
    before = jnp.where(r < c, 1.0, 0.0).astype(BF16)
    rank_excl = carry_scr[:, 0:1] + _dot(sel.astype(BF16), before)
    rank_rows = [jnp.sum(jnp.where(e_iota == idx_rows[k], rank_excl, 0.0), axis=0, keepdims=True)
                 for k in range(TOP_K)]
    carry_scr[...] = carry_scr[...] + jnp.sum(sel, axis=1, keepdims=True)

    idx_ref[...] = idx
    wt_ref[...] = wts
    rank_ref[...] = jnp.concatenate(rank_rows, axis=0).astype(I32)
    cnt_ref[...] = carry_scr[...]


def _moe_pre(x, shift, scale, w_router, router_bias):
    t, d = x.shape
    tm = min(t, 256)
    rows = jnp.arange(N_EXPERTS)
    perm = _expert_of_row(rows)
    wr = jnp.zeros((d, LANES), F32).at[:, :N_EXPERTS].set(w_router[:, perm])
    rb = router_bias.astype(F32)[perm].reshape(N_EXPERTS, 1)
    tok = lambda n, dt: jax.ShapeDtypeStruct((TOP_K, t), dt)
    return pl.pallas_call(
        _moe_pre_kernel,
        out_shape=(jax.ShapeDtypeStruct((t, d), F32), tok(t, I32), tok(t, F32), tok(t, I32),
                   jax.ShapeDtypeStruct((N_EXPERTS, LANES), F32)),
        grid=(t // tm,),
        in_specs=[pl.BlockSpec((tm, d), lambda i: (i, 0)),
                  pl.BlockSpec((1, d), lambda i: (0, 0)),
                  pl.BlockSpec((1, d), lambda i: (0, 0)),
                  pl.BlockSpec((d, LANES), lambda i: (0, 0)),
                  pl.BlockSpec((N_EXPERTS, 1), lambda i: (0, 0))],
        out_specs=(pl.BlockSpec((tm, d), lambda i: (i, 0)),
                   pl.BlockSpec((TOP_K, tm), lambda i: (0, i)),
                   pl.BlockSpec((TOP_K, tm), lambda i: (0, i)),
                   pl.BlockSpec((TOP_K, tm), lambda i: (0, i)),
                   pl.BlockSpec((N_EXPERTS, LANES), lambda i: (0, 0))),
        scratch_shapes=[pltpu.VMEM((N_EXPERTS, LANES), F32)],
        compiler_params=_params("arbitrary"),
        name="moe_router",
    )(x, shift, scale, wr, rb)


def _moe_dest_kernel(idx_ref, rank_ref, cnt_ref, dest_ref, be_ref, nused_ref):
    cnt = cnt_ref[...]
    padded = jnp.floor((cnt + (E_BLOCK - 1)) * (1.0 / E_BLOCK)) * E_BLOCK
    e_i = _expert_of_row(lax.broadcasted_iota(I32, (N_EXPERTS, N_EXPERTS), 0))
    e_j = _expert_of_row(lax.broadcasted_iota(I32, (N_EXPERTS, N_EXPERTS), 1))
    earlier = jnp.where(e_j < e_i, 1.0, 0.0).astype(F32)
    pstart = _dot(earlier, padded, precision=HIGHEST)
    pend = pstart + padded

    tm = idx_ref.shape[1]
    e_col = _expert_of_row(lax.broadcasted_iota(I32, (N_EXPERTS, tm), 0))
    rows = []
    for k in range(TOP_K):
        hit = e_col == idx_ref[k:k + 1, :]
        rows.append(jnp.sum(jnp.where(hit, pstart[:, 0:1], 0.0), axis=0, keepdims=True))
    dest_ref[...] = jnp.concatenate(rows, axis=0).astype(I32) + rank_ref[...]

    nb = be_ref.shape[1]
    block_start = (lax.broadcasted_iota(I32, (N_EXPERTS, nb), 1) * E_BLOCK).astype(F32)
    ended = jnp.sum(jnp.where(pend[:, 0:1] <= block_start, 1.0, 0.0), axis=0, keepdims=True)
    be_ref[...] = jnp.minimum(ended, float(N_EXPERTS - 1)).astype(I32)
    nused_ref[...] = (jnp.sum(padded, axis=0, keepdims=True) * (1.0 / E_BLOCK)).astype(I32)


def _moe_dest(idx_t, rank_t, counts, nblk):
    t = idx_t.shape[1]
    tm = min(t, 2048)
    nb = ((nblk + LANES - 1) // LANES) * LANES
    return pl.pallas_call(
        _moe_dest_kernel,
        out_shape=(jax.ShapeDtypeStruct((TOP_K, t), I32),
                   jax.ShapeDtypeStruct((1, nb), I32),
                   jax.ShapeDtypeStruct((1, LANES), I32)),
        grid=(t // tm,),
        in_specs=[pl.BlockSpec((TOP_K, tm), lambda i: (0, i)),
                  pl.BlockSpec((TOP_K, tm), lambda i: (0, i)),
                  pl.BlockSpec((N_EXPERTS, LANES), lambda i: (0, 0))],
        out_specs=(pl.BlockSpec((TOP_K, tm), lambda i: (0, i)),
                   pl.BlockSpec((1, nb), lambda i: (0, 0)),
                   pl.BlockSpec((1, LANES), lambda i: (0, 0))),
        compiler_params=_params("arbitrary"),
        name="moe_dest",
    )(idx_t, rank_t, counts)


def _dispatch_kernel(dest_ref, hf_hbm, xinit_hbm, xbuf_hbm, dest_smem, idx_sem, row_sem):
    del xinit_hbm
    i = pl.program_id(0)
    tm = dest_ref.shape[1]
    load = pltpu.make_async_copy(dest_ref, dest_smem, idx_sem)
    load.start()
    load.wait()

    def row_copy(src_row, dst_row):
        return pltpu.make_async_copy(hf_hbm.at[pl.ds(src_row, 1)], xbuf_hbm.at[pl.ds(dst_row, 1)], row_sem)

    def issue(tok, carry):
        for k in range(TOP_K):
            row_copy(i * tm + tok, dest_smem[k, tok]).start()
        return carry

    lax.fori_loop(0, tm, issue, 0)

    def drain(tok, carry):
        for k in range(TOP_K):
            row_copy(0, 0).wait()
        return carry

    lax.fori_loop(0, tm, drain, 0)


def _dispatch(hf, dest_t, npad):
    t, d = hf.shape
    tm = min(t, 1024)
    xinit = jnp.zeros((npad, d), F32)
    return pl.pallas_call(
        _dispatch_kernel,
        out_shape=jax.ShapeDtypeStruct((npad, d), F32),
        grid=(t // tm,),
        in_specs=[pl.BlockSpec((TOP_K, tm), lambda i: (0, i)),
                  pl.BlockSpec(memory_space=pl.ANY),
                  pl.BlockSpec(memory_space=pl.ANY)],
        out_specs=pl.BlockSpec(memory_space=pl.ANY),
        scratch_shapes=[pltpu.SMEM((TOP_K, tm), I32), pltpu.SemaphoreType.DMA, pltpu.SemaphoreType.DMA],
        input_output_aliases={2: 0},
        compiler_params=_params("arbitrary"),
        name="moe_dispatch",
    )(dest_t, hf, xinit)


def _expert_kernel(be_ref, nused_ref, x_ref, wgu_ref, wd_ref, y_ref, wgu_scr, wd_scr):
    b = pl.program_id(0)
    e = be_ref[b]
    e_prev = be_ref[jnp.maximum(b - 1, 0)]

    @pl.when((b == 0) | (e != e_prev))
    def _():
        wgu_scr[...] = wgu_ref[...].astype(BF16)
        wd_scr[...] = wd_ref[...].astype(BF16)

    @pl.when(b < nused_ref[0])
    def _():
        gu = _dot(x_ref[...].astype(BF16), wgu_scr[...])
        gate = gu[:, :E_FF]
        act = gate * jax.nn.sigmoid(gate) * gu[:, E_FF:]
        y_ref[...] = _dot(act.astype(BF16), wd_scr[...])

    @pl.when(b >= nused_ref[0])
    def _():
        y_ref[...] = jnp.zeros_like(y_ref)


def _experts(xbuf, block_e, nused, w_gate_up, w_down):
    npad, d = xbuf.shape
    nblk = npad // E_BLOCK
    ff2 = w_gate_up.shape[2]
    last = lambda b, nu: jnp.minimum(b, nu[0] - 1)
    return pl.pallas_call(
        _expert_kernel,
        out_shape=jax.ShapeDtypeStruct((npad, d), F32),
        grid_spec=pltpu.PrefetchScalarGridSpec(
            num_scalar_prefetch=2,
            grid=(nblk,),
            in_specs=[pl.BlockSpec((E_BLOCK, d), lambda b, be, nu: (last(b, nu), 0)),
                      pl.BlockSpec((None, d, ff2), lambda b, be, nu: (be[b], 0, 0)),
                      pl.BlockSpec((None, ff2 // 2, d), lambda b, be, nu: (be[b], 0, 0))],
            out_specs=pl.BlockSpec((E_BLOCK, d), lambda b, be, nu: (b, 0)),
            scratch_shapes=[pltpu.VMEM((d, ff2), BF16), pltpu.VMEM((ff2 // 2, d), BF16)]),
        compiler_params=_params("arbitrary"),
        name="moe_experts",
    )(block_e, nused, xbuf, w_gate_up, w_down)


def _shared_kernel(x_ref, wgu_ref, wd_ref, y_ref):
    gu = _dot(x_ref[...].astype(BF16), wgu_ref[...])
    gate = gu[:, :SHARED_FF]
    act = gate * jax.nn.sigmoid(gate) * gu[:, SHARED_FF:]
    y_ref[...] = _dot(act.astype(BF16), wd_ref[...])


def _shared_expert(hf, wgu, wd):
    t, d = hf.shape
    tm = min(t, 512)
    return pl.pallas_call(
        _shared_kernel,
        out_shape=jax.ShapeDtypeStruct((t, d), F32),
        grid=(t // tm,),
        in_specs=[pl.BlockSpec((tm, d), lambda i: (i, 0)),
                  pl.BlockSpec(wgu.shape, lambda i: (0, 0)),
                  pl.BlockSpec(wd.shape, lambda i: (0, 0))],
        out_specs=pl.BlockSpec((tm, d), lambda i: (i, 0)),
        compiler_params=_params("parallel"),
        name="moe_shared",
    )(hf, wgu, wd)


def _combine_kernel(dest_ref, w_ref, x_ref, sh_ref, g_ref, ybuf_hbm, o_ref, rows_scr, dest_smem, idx_sem, row_sem):
    tm = x_ref.shape[0]
    load = pltpu.make_async_copy(dest_ref, dest_smem, idx_sem)
    load.start()
    load.wait()

    def row_copy(src_row, k, tok):
        return pltpu.make_async_copy(ybuf_hbm.at[pl.ds(src_row, 1)], rows_scr.at[k, pl.ds(tok, 1)], row_sem)

    def issue(tok, carry):
        for k in range(TOP_K):
            row_copy(dest_smem[k, tok], k, tok).start()
        return carry

    lax.fori_loop(0, tm, issue, 0)

    def drain(tok, carry):
        for k in range(TOP_K):
            row_copy(0, k, tok).wait()
        return carry

    lax.fori_loop(0, tm, drain, 0)

    routed = w_ref[:, 0:1] * rows_scr[0]
    for k in range(1, TOP_K):
        routed = routed + w_ref[:, k:k + 1] * rows_scr[k]
    o_ref[...] = x_ref[...] + g_ref[...] * (routed + sh_ref[...])


def _combine(dest_t, w_tok, x, shared, gate, ybuf):
    t, d = x.shape
    tm = min(t, 128)
    return pl.pallas_call(
        _combine_kernel,
        out_shape=jax.ShapeDtypeStruct((t, d), F32),
        grid=(t // tm,),
        in_specs=[pl.BlockSpec((TOP_K, tm), lambda i: (0, i)),
                  pl.BlockSpec((tm, TOP_K), lambda i: (i, 0)),
                  pl.BlockSpec((tm, d), lambda i: (i, 0)),
                  pl.BlockSpec((tm, d), lambda i: (i, 0)),
                  pl.BlockSpec((1, d), lambda i: (0, 0)),
                  pl.BlockSpec(memory_space=pl.ANY)],
        out_specs=pl.BlockSpec((tm, d), lambda i: (i, 0)),
        scratch_shapes=[pltpu.VMEM((TOP_K, tm, d), F32), pltpu.SMEM((TOP_K, tm), I32),
                        pltpu.SemaphoreType.DMA, pltpu.SemaphoreType.DMA],
        compiler_params=_params("arbitrary"),
        name="moe_combine",
    )(dest_t, w_tok, x, shared, gate, ybuf)


def _moe_layer(x, shift, scale, gate, w_router, router_bias, w_gate_up, w_down, ws_gate_up, ws_down):
    t, _ = x.shape
    npad = t * TOP_K + N_EXPERTS * E_BLOCK
    nblk = npad // E_BLOCK
    hf, idx_t, wts_t, rank_t, counts = _moe_pre(x, shift, scale, w_router, router_bias)
    dest_t, block_e, nused = _moe_dest(idx_t, rank_t, counts, nblk)
    xbuf = _dispatch(hf, dest_t, npad)
    ybuf = _experts(xbuf, block_e.reshape(-1), nused.reshape(-1)[:1], w_gate_up, w_down)
    shared = _shared_expert(hf, ws_gate_up.astype(BF16), ws_down.astype(BF16))
    return _combine(dest_t, wts_t.T, x, shared, gate, ybuf)


def _final_norm_kernel(x_ref, g_ref, o_ref):
    x = x_ref[...]
    o_ref[...] = x * lax.rsqrt(jnp.mean(x * x, axis=-1, keepdims=True) + EPS) * g_ref[...]


def _final_norm(x, gain):
    t, d = x.shape
    tm = min(t, 512)
    return pl.pallas_call(
        _final_norm_kernel,
        out_shape=jax.ShapeDtypeStruct((t, d), F32),
        grid=(t // tm,),
        in_specs=[pl.BlockSpec((tm, d), lambda i: (i, 0)), pl.BlockSpec((1, d), lambda i: (0, 0))],
        out_specs=pl.BlockSpec((tm, d), lambda i: (i, 0)),
        compiler_params=_params("parallel"),
        name="final_norm",
    )(x, gain.reshape(1, d))


def _even_layer(x, mod, cos, sin, w_in, b_if, mlstm_norm, q_norm, kv_norm, w_uq, w_ukv, w_out):
    t, d = x.shape
    sh1, sc1, g1 = mod[0], mod[1], mod[2]
    nq, nv = M_HEADS * M_QK, M_HEADS * M_V
    o = 0
    cols = []
    for sz in (nq, nq, nv, nv, M_HEADS, M_HEADS, A_Q_LORA, A_KV_LORA, A_ROPE):
        cols.append(w_in[:, o:o + sz])
        o += sz
    mq, mk, mv, mo, mi, mf, cq, ckv, kr = cols
    w_a = jnp.concatenate([mq, mk, mv, mo], axis=1).astype(BF16)
    zeros = lambda n: jnp.zeros((d, n), F32)
    w_b = jnp.concatenate([cq, ckv, kr, zeros(LANES - A_ROPE), mi, mf, zeros(LANES - 2 * M_HEADS)], axis=1).astype(BF16)
    gate_col_block = (A_Q_LORA + A_KV_LORA + LANES) // LANES
    proj_a = _norm_matmul(x, sh1, sc1, w_a, jnp.zeros((1, w_a.shape[1]), F32), BF16, "even_in_a")
    proj_b = _norm_matmul(x, sh1, sc1, w_b, jnp.zeros((1, w_b.shape[1]), F32), F32, "even_in_b")
    gate_bias = jnp.zeros((1, LANES), F32).at[0, :2 * M_HEADS].set(b_if.astype(F32))
    hm = _mlstm(proj_a, proj_b, gate_col_block, gate_bias, mlstm_norm.astype(F32).reshape(1, nv))

    qk = A_NOPE + A_ROPE
    wq = jnp.pad(w_uq.reshape(A_Q_LORA, A_HEADS, qk), ((0, 0), (0, 0), (0, A_QK_PAD - qk)))
    wq = wq.reshape(A_Q_LORA, A_HEADS * A_QK_PAD).astype(BF16)
    q, k, v = _mla_up(proj_b, q_norm.astype(F32).reshape(1, -1), kv_norm.astype(F32).reshape(1, -1),
                      wq, w_ukv.astype(BF16), cos, sin)
    ha = _mla_flash(q, k, v)
    w_out = w_out.astype(BF16)
    return _proj_residual([hm, ha], [w_out[:nv], w_out[nv:]], x, g1, "even_out")


def _odd_layer(x, mod, cos, sin, w_qkv, b_qkv, sinks, w_o):
    sh1, sc1, g1 = mod[0], mod[1], mod[2]
    qkv = _norm_matmul(x, sh1, sc1, w_qkv.astype(BF16), b_qkv.astype(F32).reshape(1, -1), BF16, "odd_qkv")
    o = _swa(qkv, cos, sin, sinks)
    return _proj_residual([o], [w_o.astype(BF16)], x, g1, "odd_out")


def kernel(x, c, positions, w_ada, b_ada, a_w_in, a_b_if, a_mlstm_norm, a_q_norm, a_kv_norm, a_w_uq, a_w_ukv,
           a_w_out, s_w_qkv, s_b_qkv, s_sinks, s_w_o, e_w_router, e_router_bias, e_w_gate_up, e_w_down,
           e_ws_gate_up, e_ws_down, final_norm):
    batch, t, d = x.shape
    assert batch == 1, "kernels are written for a single sequence"
    depth = w_ada.shape[0]
    xs = x.reshape(t, d)
    mods = _ada(c, w_ada, b_ada).reshape(depth, 6, 1, d)
    cos, sin = _rope_tables(positions.reshape(t))
    for layer in range(depth):
        mod = mods[layer]
        if layer % 2 == 0:
            e = layer // 2
            xs = _even_layer(xs, mod, cos, sin, a_w_in[e], a_b_if[e], a_mlstm_norm[e], a_q_norm[e], a_kv_norm[e],
                             a_w_uq[e], a_w_ukv[e], a_w_out[e])
        else:
            o = layer // 2
            xs = _odd_layer(xs, mod, cos, sin, s_w_qkv[o], s_b_qkv[o], s_sinks[o], s_w_o[o])
        xs = _moe_layer(xs, mod[3], mod[4], mod[5], e_w_router[layer], e_router_bias[layer],
                        e_w_gate_up[layer], e_w_down[layer], e_ws_gate_up[layer], e_ws_down[layer])
    return _final_norm(xs, final_norm.astype(F32)).reshape(batch, t, d)
```

```python
import functools

import jax
import jax.numpy as jnp
from jax import lax
from jax.experimental import pallas as pl
from jax.experimental.pallas import tpu as pltpu

F32 = jnp.float32
BF16 = jnp.bfloat16
I32 = jnp.int32
U32 = jnp.uint32
HIGHEST = lax.Precision.HIGHEST

EPS = 1e-6
ROPE_THETA = 10000.0
ROPE_DIM = 64

M_HEADS = 4
M_QK = 128
M_V = 256
GATE_SOFTCAP = 15.0
MLSTM_CHUNK = 128

A_HEADS = 8
A_NOPE = 128
A_ROPE = 64
A_V = 128
A_Q_LORA = 768
A_KV_LORA = 512
A_QK_PAD = 256

S_HEADS = 32
S_KV_HEADS = 4
S_HD = 64
S_BLOCK = 128

N_EXPERTS = 64
TOP_K = 8
N_GROUPS = 8
TOPK_GROUPS = 4
E_FF = 256
SHARED_FF = 256
ROUTED_SCALE = 2.5
E_BLOCK = 256

LANES = 128
VMEM_LIMIT_BYTES = 48 * 1024 * 1024
FLASH_VMEM_LIMIT_BYTES = 56 * 1024 * 1024
LOG2_E = 1.4426950408889634


def _params(*semantics):
    return pltpu.CompilerParams(dimension_semantics=semantics, vmem_limit_bytes=VMEM_LIMIT_BYTES)


def _dot(a, b, precision=None):
    return jnp.dot(a, b, preferred_element_type=F32, precision=precision)


def _dot_nt(a, b):
    return lax.dot_general(a, b, (((1,), (1,)), ((), ())), preferred_element_type=F32)


def _dot_tn(a, b):
    return lax.dot_general(a, b, (((0,), (0,)), ((), ())), preferred_element_type=F32)


def _modulated_norm(x, shift, scale):
    y = x * lax.rsqrt(jnp.mean(x * x, axis=-1, keepdims=True) + EPS)
    return y * (1.0 + scale) + shift


def _rope_lanes(x, cos, sin_signed):
    lane = lax.broadcasted_iota(I32, x.shape, 1)
    first_half = (lane % ROPE_DIM) < (ROPE_DIM // 2)
    swapped = jnp.where(first_half, pltpu.roll(x, LANES - ROPE_DIM // 2, 1), pltpu.roll(x, ROPE_DIM // 2, 1))
    return x * cos + swapped * sin_signed


def _pack_bf16_halves(x):
    n = x.shape[1] // 2
    lo = lax.bitcast_convert_type(x[:, :n].astype(BF16).astype(F32), U32)
    hi = lax.bitcast_convert_type(x[:, n:].astype(BF16).astype(F32), U32)
    return (hi & jnp.uint32(0xFFFF0000)) | (lo >> 16)


def _unpack_bf16_halves(w):
    lo = lax.bitcast_convert_type(w << 16, F32)
    hi = lax.bitcast_convert_type(w & jnp.uint32(0xFFFF0000), F32)
    return lo, hi


def _ada_kernel(c_ref, w_ref, b_ref, o_ref):
    c = c_ref[...]
    c_act = c * jax.nn.sigmoid(c)
    o_ref[...] = jnp.sum(c_act * w_ref[...], axis=0, keepdims=True) + b_ref[...]


def _ada(c, w_ada, b_ada):
    depth, d, n = w_ada.shape
    tn = 1024
    return pl.pallas_call(
        _ada_kernel,
        out_shape=jax.ShapeDtypeStruct((depth, 1, n), F32),
        grid=(depth, n // tn),
        in_specs=[pl.BlockSpec((d, 1), lambda l, j: (0, 0)),
                  pl.BlockSpec((None, d, tn), lambda l, j: (l, 0, j)),
                  pl.BlockSpec((None, 1, tn), lambda l, j: (l, 0, j))],
        out_specs=pl.BlockSpec((None, 1, tn), lambda l, j: (l, 0, j)),
        compiler_params=_params("parallel", "parallel"),
        name="ada_mod",
    )(c.reshape(d, 1), w_ada, b_ada.reshape(depth, 1, n))


def _rope_table_kernel(pos_ref, inv_ref, sign_ref, cos_ref, sin_ref):
    ang = pos_ref[...].astype(F32) * inv_ref[...]
    cos_ref[...] = jnp.cos(ang)
    sin_ref[...] = jnp.sin(ang) * sign_ref[...]


def _rope_tables(positions):
    t = positions.shape[0]
    half = ROPE_DIM // 2
    inv_freq = jnp.power(ROPE_THETA, -jnp.arange(half, dtype=F32) / half)
    inv = jnp.tile(inv_freq, LANES // half).reshape(1, LANES)
    sign = jnp.tile(jnp.concatenate([-jnp.ones((half,), F32), jnp.ones((half,), F32)]), LANES // ROPE_DIM)
    tm = min(t, 2048)
    return pl.pallas_call(
        _rope_table_kernel,
        out_shape=(jax.ShapeDtypeStruct((t, LANES), F32), jax.ShapeDtypeStruct((t, LANES), F32)),
        grid=(t // tm,),
        in_specs=[pl.BlockSpec((tm, 1), lambda i: (i, 0)),
                  pl.BlockSpec((1, LANES), lambda i: (0, 0)),
                  pl.BlockSpec((1, LANES), lambda i: (0, 0))],
        out_specs=(pl.BlockSpec((tm, LANES), lambda i: (i, 0)), pl.BlockSpec((tm, LANES), lambda i: (i, 0))),
        compiler_params=_params("parallel"),
        name="rope_tables",
    )(positions.reshape(t, 1), inv, sign.reshape(1, LANES))


def _norm_matmul_kernel(x_ref, sh_ref, sc_ref, w_ref, b_ref, o_ref, h_scr):
    @pl.when(pl.program_id(1) == 0)
    def _():
        h_scr[...] = _modulated_norm(x_ref[...], sh_ref[...], sc_ref[...]).astype(BF16)

    o_ref[...] = (_dot(h_scr[...], w_ref[...]) + b_ref[...]).astype(o_ref.dtype)


def _norm_matmul(x, shift, scale, w, bias, out_dtype, name):
    t, d = x.shape
    n = w.shape[1]
    tm = min(t, 1024)
    tn = 512 if n % 512 == 0 else 256
    return pl.pallas_call(
        _norm_matmul_kernel,
        out_shape=jax.ShapeDtypeStruct((t, n), out_dtype),
        grid=(t // tm, n // tn),
        in_specs=[pl.BlockSpec((tm, d), lambda i, j: (i, 0)),
                  pl.BlockSpec((1, d), lambda i, j: (0, 0)),
                  pl.BlockSpec((1, d), lambda i, j: (0, 0)),
                  pl.BlockSpec((d, tn), lambda i, j: (0, j)),
                  pl.BlockSpec((1, tn), lambda i, j: (0, j))],
        out_specs=pl.BlockSpec((tm, tn), lambda i, j: (i, j)),
        scratch_shapes=[pltpu.VMEM((tm, d), BF16)],
        compiler_params=_params("parallel", "arbitrary"),
        name=name,
    )(x, shift, scale, w, bias)


def _mlstm_kernel(q_ref, k_ref, v_ref, o_ref, g_ref, gb_ref, gain_ref, out_ref, c_scr, n_scr, m_scr):
    L = MLSTM_CHUNK
    tm = q_ref.shape[0]

    @pl.when(pl.program_id(0) == 0)
    def _():
        c_scr[...] = jnp.zeros_like(c_scr)
        n_scr[...] = jnp.zeros_like(n_scr)
        m_scr[...] = jnp.zeros_like(m_scr)

    capped = GATE_SOFTCAP * jnp.tanh((g_ref[...] + gb_ref[...]) / GATE_SOFTCAP)
    log_sig = jnp.minimum(capped, 0.0) - jnp.log1p(jnp.exp(-jnp.abs(capped)))
    lane = lax.broadcasted_iota(I32, capped.shape, 1)
    gate = jnp.where(lane < M_HEADS, capped, log_sig)
    r = lax.broadcasted_iota(I32, (tm, tm), 0)
    c = lax.broadcasted_iota(I32, (tm, tm), 1)
    chunk_tril = jnp.where(((r // L) == (c // L)) & (c <= r), 1.0, 0.0).astype(F32)
    cum = _dot(chunk_tril, gate, precision=HIGHEST)
    gate_rows = gate.T
    cum_rows = cum.T
    rr = lax.broadcasted_iota(I32, (L, L), 0)
    cc = lax.broadcasted_iota(I32, (L, L), 1)
    causal = cc <= rr

    for ci in range(tm // L):
        sl = slice(ci * L, (ci + 1) * L)
        for h in range(M_HEADS):
            qk = slice(h * M_QK, (h + 1) * M_QK)
            vv = slice(h * M_V, (h + 1) * M_V)
            qc = q_ref[sl, qk]
            kc = k_ref[sl, qk].astype(F32) * (M_QK ** -0.5)
            kcb = kc.astype(BF16)
            vc = v_ref[sl, vv]
            ig_col = gate[sl, h:h + 1]
            b_col = cum[sl, M_HEADS + h:M_HEADS + h + 1]
            ig_row = gate_rows[h:h + 1, sl]
            b_row = cum_rows[M_HEADS + h:M_HEADS + h + 1, sl]
            m_prev = m_scr[h:h + 1, 0:1]

            dm = jnp.where(causal, b_col - b_row + ig_row, -jnp.inf)
            inter = b_col + m_prev
            m_row = jnp.maximum(inter, jnp.max(dm, axis=-1, keepdims=True))
            w_intra = jnp.exp(dm - m_row)
            w_inter = jnp.exp(inter - m_row)
            s = _dot_nt(qc, kcb) * w_intra
            c_state = c_scr[h]
            n_state = n_scr[h]
            num = _dot(s.astype(BF16), vc) + w_inter * _dot_nt(qc, c_state.astype(BF16))
            den = (jnp.sum(s, axis=-1, keepdims=True)
                   + w_inter * jnp.sum(qc.astype(F32) * n_state, axis=-1, keepdims=True))
            hh = num / jnp.maximum(jnp.abs(den), jnp.exp(-m_row))

            b_last = b_col[L - 1:L, :]
            g_row = b_last - b_row + ig_row
            g_col = b_last - b_col + ig_col
            m_new = jnp.maximum(b_last + m_prev, jnp.max(g_row, axis=-1, keepdims=True))
            ws_col = jnp.exp(g_col - m_new)
            decay = jnp.exp(b_last + m_prev - m_new)
            vw = (vc.astype(F32) * ws_col).astype(BF16)
            c_scr[h] = decay * c_state + _dot_tn(vw, kcb)
            n_scr[h] = decay * n_state + jnp.sum(kc * ws_col, axis=0, keepdims=True)
            m_scr[h:h + 1, :] = jnp.broadcast_to(m_new, (1, LANES))

            y = hh * lax.rsqrt(jnp.mean(hh * hh, axis=-1, keepdims=True) + EPS) * gain_ref[:, vv]
            y = y * jax.nn.sigmoid(o_ref[sl, vv].astype(F32))
            out_ref[sl, vv] = y.astype(out_ref.dtype)


def _mlstm(proj_a, proj_b, gate_col_block, gate_bias, gain):
    t = proj_a.shape[0]
    tm = min(t, 512)
    nq = M_HEADS * M_QK
    nv = M_HEADS * M_V
    return pl.pallas_call(
        _mlstm_kernel,
        out_shape=jax.ShapeDtypeStruct((t, nv), BF16),
        grid=(t // tm,),
        in_specs=[pl.BlockSpec((tm, nq), lambda i: (i, 0)),
                  pl.BlockSpec((tm, nq), lambda i: (i, 1)),
                  pl.BlockSpec((tm, nv), lambda i: (i, 1)),
                  pl.BlockSpec((tm, nv), lambda i: (i, 2)),
                  pl.BlockSpec((tm, LANES), lambda i: (i, gate_col_block)),
                  pl.BlockSpec((1, LANES), lambda i: (0, 0)),
                  pl.BlockSpec((1, nv), lambda i: (0, 0))],
        out_specs=pl.BlockSpec((tm, nv), lambda i: (i, 0)),
        scratch_shapes=[pltpu.VMEM((M_HEADS, M_V, M_QK), F32),
                        pltpu.VMEM((M_HEADS, 1, M_QK), F32),
                        pltpu.VMEM((8, LANES), F32)],
        compiler_params=_params("arbitrary"),
        name="mlstm",
    )(proj_a, proj_a, proj_a, proj_a, proj_b, gate_bias, gain)


def _mla_up_kernel(pb_ref, qn_ref, kvn_ref, wq_ref, wkv_ref, cos_ref, sin_ref,
                   q_ref, k_ref, v_ref, cq_scr, ckv_scr, kr_scr):
    @pl.when(pl.program_id(1) == 0)
    def _():
        cq = pb_ref[:, 0:A_Q_LORA]
        cq_scr[...] = (cq * lax.rsqrt(jnp.mean(cq * cq, axis=-1, keepdims=True) + EPS) * qn_ref[...]).astype(BF16)
        ckv = pb_ref[:, A_Q_LORA:A_Q_LORA + A_KV_LORA]
        ckv_scr[...] = (ckv * lax.rsqrt(jnp.mean(ckv * ckv, axis=-1, keepdims=True) + EPS)
                        * kvn_ref[...]).astype(BF16)
        kr = pb_ref[:, A_Q_LORA + A_KV_LORA:A_Q_LORA + A_KV_LORA + LANES]
        kr_scr[...] = _rope_lanes(kr, cos_ref[...], sin_ref[...]).astype(BF16)

    scale = (A_NOPE + A_ROPE) ** -0.5 * LOG2_E
    qh = _dot(cq_scr[...], wq_ref[...])
    q_pe = _rope_lanes(qh[:, A_NOPE:], cos_ref[...], sin_ref[...])
    q_ref[...] = (jnp.concatenate([qh[:, :A_NOPE], q_pe], axis=1) * scale).astype(q_ref.dtype)
    kvh = _dot(ckv_scr[...], wkv_ref[...])
    k_ref[...] = jnp.concatenate([kvh[:, :A_NOPE].astype(BF16), kr_scr[...]], axis=1)
    v_ref[...] = kvh[:, A_NOPE:].astype(v_ref.dtype)


def _mla_up(proj_b, q_norm, kv_norm, wq, wkv, cos, sin):
    t, nb = proj_b.shape
    tm = min(t, 1024)
    return pl.pallas_call(
        _mla_up_kernel,
        out_shape=(jax.ShapeDtypeStruct((A_HEADS, t, A_QK_PAD), BF16),
                   jax.ShapeDtypeStruct((A_HEADS, t, A_QK_PAD), BF16),
                   jax.ShapeDtypeStruct((A_HEADS, t, A_V), BF16)),
        grid=(t // tm, A_HEADS),
        in_specs=[pl.BlockSpec((tm, nb), lambda i, h: (i, 0)),
                  pl.BlockSpec((1, A_Q_LORA), lambda i, h: (0, 0)),
                  pl.BlockSpec((1, A_KV_LORA), lambda i, h: (0, 0)),
                  pl.BlockSpec((A_Q_LORA, A_QK_PAD), lambda i, h: (0, h)),
                  pl.BlockSpec((A_KV_LORA, A_NOPE + A_V), lambda i, h: (0, h)),
                  pl.BlockSpec((tm, LANES), lambda i, h: (i, 0)),
                  pl.BlockSpec((tm, LANES), lambda i, h: (i, 0))],
        out_specs=(pl.BlockSpec((None, tm, A_QK_PAD), lambda i, h: (h, i, 0)),
                   pl.BlockSpec((None, tm, A_QK_PAD), lambda i, h: (h, i, 0)),
                   pl.BlockSpec((None, tm, A_V), lambda i, h: (h, i, 0))),
        scratch_shapes=[pltpu.VMEM((tm, A_Q_LORA), BF16),
                        pltpu.VMEM((tm, A_KV_LORA), BF16),
                        pltpu.VMEM((tm, LANES), BF16)],
        compiler_params=_params("parallel", "arbitrary"),
        name="mla_up",
    )(proj_b, q_norm, kv_norm, wq, wkv, cos, sin)


def _mla_flash_kernel(q_ref, k_ref, v_ref, o_ref, m_scr, l_scr, acc_scr, *, tq, tkb):
    i = pl.program_id(1)
    q = q_ref[...]
    m_scr[...] = jnp.full(m_scr.shape, -jnp.inf, F32)
    l_scr[...] = jnp.zeros(l_scr.shape, F32)
    acc_scr[...] = jnp.zeros(acc_scr.shape, F32)

    def tile(start, width, masked):
        s = _dot_nt(q, k_ref[pl.ds(start, width), :])
        if masked:
            row = lax.broadcasted_iota(I32, (tq, width), 0)
            col = lax.broadcasted_iota(I32, (tq, width), 1)
            s = jnp.where(col <= row, s, -jnp.inf)
        m = m_scr[...]
        m_new = jnp.maximum(m, jnp.broadcast_to(jnp.max(s, axis=-1, keepdims=True), m.shape))
        alpha = jnp.exp2(m - m_new)
        p = jnp.exp2(s - jnp.concatenate([m_new] * (width // LANES), axis=1))
        lane_sums = p[:, 0:LANES]
        for j in range(1, width // LANES):
            lane_sums = lane_sums + p[:, j * LANES:(j + 1) * LANES]
        l_scr[...] = alpha * l_scr[...] + lane_sums
        acc_scr[...] = alpha * acc_scr[...] + _dot(p.astype(BF16), v_ref[pl.ds(start, width), :])
        m_scr[...] = m_new

    per = tkb // tq
    n_big = i // per

    def body(j, carry):
        tile(pl.multiple_of(j * tkb, tkb), tkb, False)
        return carry

    lax.fori_loop(0, n_big, body, 0)
    rem = i - n_big * per
    for r in range(1, per):
        @pl.when(rem >= r)
        def _():
            tile(pl.multiple_of(n_big * tkb + (r - 1) * tq, tq), tq, False)
    tile(pl.multiple_of(i * tq, tq), tq, True)
    o_ref[...] = (acc_scr[...] / jnp.sum(l_scr[...], axis=-1, keepdims=True)).astype(o_ref.dtype)


def _mla_flash(q, k, v):
    _, t, _ = q.shape
    tq = min(t, 1024)
    tkb = min(t, 2048)
    return pl.pallas_call(
        functools.partial(_mla_flash_kernel, tq=tq, tkb=tkb),
        out_shape=jax.ShapeDtypeStruct((t, A_HEADS * A_V), BF16),
        grid=(A_HEADS, t // tq),
        in_specs=[pl.BlockSpec((None, tq, A_QK_PAD), lambda h, i: (h, i, 0)),
                  pl.BlockSpec((None, t, A_QK_PAD), lambda h, i: (h, 0, 0)),
                  pl.BlockSpec((None, t, A_V), lambda h, i: (h, 0, 0))],
        out_specs=pl.BlockSpec((tq, A_V), lambda h, i: (i, h)),
        scratch_shapes=[pltpu.VMEM((tq, LANES), F32), pltpu.VMEM((tq, LANES), F32), pltpu.VMEM((tq, A_V), F32)],
        compiler_params=pltpu.CompilerParams(dimension_semantics=("parallel", "arbitrary"),
                                             vmem_limit_bytes=FLASH_VMEM_LIMIT_BYTES),
        name="mla_flash",
    )(q, k, v)


def _proj_residual_kernel(*refs, n_lhs):
    lhs = refs[:n_lhs]
    ws = refs[n_lhs:2 * n_lhs]
    x_ref, g_ref, o_ref = refs[2 * n_lhs:]
    acc = _dot(lhs[0][...], ws[0][...])
    for a, w in zip(lhs[1:], ws[1:]):
        acc = acc + _dot(a[...], w[...])
    o_ref[...] = x_ref[...] + g_ref[...] * acc


def _proj_residual(lhs_list, w_list, x, gate, name):
    t, d = x.shape
    tm = min(t, 1024)
    tn = 512
    n_lhs = len(lhs_list)
    in_specs = ([pl.BlockSpec((tm, a.shape[1]), lambda i, j: (i, 0)) for a in lhs_list]
                + [pl.BlockSpec((w.shape[0], tn), lambda i, j: (0, j)) for w in w_list]
                + [pl.BlockSpec((tm, tn), lambda i, j: (i, j)), pl.BlockSpec((1, tn), lambda i, j: (0, j))])
    return pl.pallas_call(
        functools.partial(_proj_residual_kernel, n_lhs=n_lhs),
        out_shape=jax.ShapeDtypeStruct((t, d), F32),
        grid=(t // tm, d // tn),
        in_specs=in_specs,
        out_specs=pl.BlockSpec((tm, tn), lambda i, j: (i, j)),
        compiler_params=_params("parallel", "parallel"),
        name=name,
    )(*lhs_list, *w_list, x, gate)


def _swa_kernel(q_ref, kc_ref, kp_ref, vc_ref, vp_ref, cosc_ref, sinc_ref, cosp_ref, sinp_ref, sink_ref, o_ref):
    i = pl.program_id(0)
    qb = S_BLOCK
    group = S_HEADS // S_KV_HEADS
    pairs = group // 2
    cos_c, sin_c = cosc_ref[...], sinc_ref[...]
    cos_w = jnp.concatenate([cosp_ref[...], cos_c], axis=0)
    sin_w = jnp.concatenate([sinp_ref[...], sin_c], axis=0)
    kw = jnp.concatenate([kp_ref[...], kc_ref[...]], axis=0).astype(F32)
    vw = jnp.concatenate([vp_ref[...], vc_ref[...]], axis=0)

    r = lax.broadcasted_iota(I32, (pairs * qb, 2 * qb), 0) % qb
    c = lax.broadcasted_iota(I32, (pairs * qb, 2 * qb), 1)
    dist = qb + r - c
    valid = (dist >= 0) & (dist < S_BLOCK) & ((c >= qb) | (i > 0))
    lane = lax.broadcasted_iota(I32, (2 * qb, LANES), 1)
    low = lane < S_HD

    for g in range(S_KV_HEADS):
        col = slice((g // 2) * LANES, (g // 2 + 1) * LANES)
        k_pair = _rope_lanes(kw[:, col], cos_w, sin_w)
        v_pair = vw[:, col]
        keep = low if g % 2 == 0 else jnp.logical_not(low)
        k_own = jnp.where(keep, k_pair, 0.0)
        v_own = jnp.where(keep, v_pair.astype(F32), 0.0)
        k_other = pltpu.roll(k_own, S_HD, 1)
        v_other = pltpu.roll(v_own, S_HD, 1)
        k_lo, k_hi = (k_own, k_other) if g % 2 == 0 else (k_other, k_own)
        v_lo, v_hi = (v_own, v_other) if g % 2 == 0 else (v_other, v_own)
        kd = jnp.concatenate([k_lo, k_hi], axis=0).astype(BF16)
        vd = jnp.concatenate([v_lo, v_hi], axis=0).astype(BF16)
        qs = []
        for p in range(pairs):
            qcol = slice((g * pairs + p) * LANES, (g * pairs + p + 1) * LANES)
            qs.append(_rope_lanes(q_ref[:, qcol].astype(F32), cos_c, sin_c) * (S_HD ** -0.5))
        qg = jnp.concatenate(qs, axis=0).astype(BF16)
        s = _dot_nt(qg, kd)
        probs = []
        for half in range(2):
            sh = jnp.where(valid, s[:, half * 2 * qb:(half + 1) * 2 * qb], -jnp.inf)
            sink = sink_ref[g, half]
            m = jnp.maximum(jnp.max(sh, axis=-1, keepdims=True), sink)
            e = jnp.exp(sh - m)
            den = jnp.sum(e, axis=-1, keepdims=True) + jnp.exp(sink - m)
            probs.append((e / den).astype(BF16))
        o = _dot(jnp.concatenate(probs, axis=1), vd)
        for p in range(pairs):
            qcol = slice((g * pairs + p) * LANES, (g * pairs + p + 1) * LANES)
            o_ref[:, qcol] = o[p * qb:(p + 1) * qb].astype(o_ref.dtype)


def _swa(qkv, cos, sin, sinks):
    t = qkv.shape[0]
    qb = S_BLOCK
    nq = S_HEADS * S_HD
    nk = S_KV_HEADS * S_HD
    group = S_HEADS // S_KV_HEADS
    pairs = group // 2
    sink_cols = jnp.repeat(sinks.astype(F32).reshape(S_KV_HEADS, pairs, 2).transpose(0, 2, 1), qb, axis=-1)
    sink_cols = sink_cols.reshape(S_KV_HEADS, 2, pairs * qb, 1)
    k_blk = nq // nk
    prev = lambda i: jnp.maximum(i - 1, 0)
    return pl.pallas_call(
        _swa_kernel,
        out_shape=jax.ShapeDtypeStruct((t, nq), BF16),
        grid=(t // qb,),
        in_specs=[pl.BlockSpec((qb, nq), lambda i: (i, 0)),
                  pl.BlockSpec((qb, nk), lambda i: (i, k_blk)),
                  pl.BlockSpec((qb, nk), lambda i: (prev(i), k_blk)),
                  pl.BlockSpec((qb, nk), lambda i: (i, k_blk + 1)),
                  pl.BlockSpec((qb, nk), lambda i: (prev(i), k_blk + 1)),
                  pl.BlockSpec((qb, LANES), lambda i: (i, 0)),
                  pl.BlockSpec((qb, LANES), lambda i: (i, 0)),
                  pl.BlockSpec((qb, LANES), lambda i: (prev(i), 0)),
                  pl.BlockSpec((qb, LANES), lambda i: (prev(i), 0)),
                  pl.BlockSpec((S_KV_HEADS, 2, pairs * qb, 1), lambda i: (0, 0, 0, 0))],
        out_specs=pl.BlockSpec((qb, nq), lambda i: (i, 0)),
        compiler_params=_params("parallel"),
        name="swa",
    )(qkv, qkv, qkv, qkv, qkv, cos, sin, cos, sin, sink_cols)


def _expert_of_row(p):
    per_group = N_EXPERTS // N_GROUPS
    return (p % N_GROUPS) * per_group + p // N_GROUPS


def _moe_pre_kernel(x_ref, sh_ref, sc_ref, wr_ref, rb_ref, hf_ref, idx_ref, wt_ref, rank_ref, cnt_ref, carry_scr):
    tm = x_ref.shape[0]
    per_group = N_EXPERTS // N_GROUPS

    @pl.when(pl.program_id(0) == 0)
    def _():
        carry_scr[...] = jnp.zeros_like(carry_scr)

    h = _modulated_norm(x_ref[...], sh_ref[...], sc_ref[...])
    hf_ref[...] = _pack_bf16_halves(h)
    logits = _dot(h, wr_ref[...], precision=HIGHEST)
    scores = jax.nn.sigmoid(logits.T[:N_EXPERTS, :])
    biased = scores + rb_ref[...]

    members = [biased[j * N_GROUPS:(j + 1) * N_GROUPS, :] for j in range(per_group)]
    m1 = members[0]
    for a in members[1:]:
        m1 = jnp.maximum(m1, a)
    first = jnp.full(m1.shape, per_group, I32)
    for j in reversed(range(per_group)):
        first = jnp.where(members[j] == m1, j, first)
    m2 = jnp.full(m1.shape, -jnp.inf, F32)
    for j in range(per_group):
        m2 = jnp.maximum(m2, jnp.where(first == j, -jnp.inf, members[j]))
    group_score = m1 + m2

    g_iota = lax.broadcasted_iota(I32, group_score.shape, 0).astype(F32)
    g_sel = jnp.zeros(group_score.shape, F32)
    for _ in range(TOPK_GROUPS):
        best = jnp.max(group_score, axis=0, keepdims=True)
        gi = jnp.min(jnp.where(group_score == best, g_iota, float(N_GROUPS)), axis=0, keepdims=True)
        hit = g_iota == gi
        g_sel = jnp.where(hit, 1.0, g_sel)
        group_score = jnp.where(hit, -jnp.inf, group_score)
    masked = jnp.concatenate([jnp.where(g_sel > 0.5, a, -jnp.inf) for a in members], axis=0)

    e_iota = _expert_of_row(lax.broadcasted_iota(I32, masked.shape, 0)).astype(F32)
    sel = jnp.zeros(masked.shape, F32)
    idx_rows, w_rows = [], []
    for _ in range(TOP_K):
        best = jnp.max(masked, axis=0, keepdims=True)
        ei = jnp.min(jnp.where(masked == best, e_iota, float(N_EXPERTS)), axis=0, keepdims=True)
        hit = e_iota == ei
        idx_rows.append(ei)
        w_rows.append(jnp.sum(jnp.where(hit, scores, 0.0), axis=0, keepdims=True))
        sel = jnp.where(hit, 1.0, sel)
        masked = jnp.where(hit, -jnp.inf, masked)
    idx = jnp.concatenate(idx_rows, axis=0).astype(I32)
    wts = jnp.concatenate(w_rows, axis=0)
    wts = wts / jnp.sum(wts, axis=0, keepdims=True) * ROUTED_SCALE

    r = lax.broadcasted_iota(I32, (tm, tm), 0)
    c = lax.broadcasted_iota(I32, (tm, tm), 1)
    before = jnp.where(r < c, 1.0, 0.0).astype(BF16)
    rank_excl = carry_scr[:, 0:1] + _dot(sel.astype(BF16), before)
    rank_rows = [jnp.sum(jnp.where(e_iota == idx_rows[k], rank_excl, 0.0), axis=0, keepdims=True)
                 for k in range(TOP_K)]
    carry_scr[...] = carry_scr[...] + jnp.sum(sel, axis=1, keepdims=True)

    idx_ref[...] = idx
    wt_ref[...] = wts
    rank_ref[...] = jnp.concatenate(rank_rows, axis=0).astype(I32)
    cnt_ref[...] = carry_scr[...]


def _moe_pre(x, shift, scale, w_router, router_bias):
    t, d = x.shape
    tm = min(t, 256)
    rows = jnp.arange(N_EXPERTS)
    perm = _expert_of_row(rows)
    wr = jnp.zeros((d, LANES), F32).at[:, :N_EXPERTS].set(w_router[:, perm])
    rb = router_bias.astype(F32)[perm].reshape(N_EXPERTS, 1)
    tok = lambda n, dt: jax.ShapeDtypeStruct((TOP_K, t), dt)
    return pl.pallas_call(
        _moe_pre_kernel,
        out_shape=(jax.ShapeDtypeStruct((t, d // 2), U32), tok(t, I32), tok(t, F32), tok(t, I32),
                   jax.ShapeDtypeStruct((N_EXPERTS, LANES), F32)),
        grid=(t // tm,),
        in_specs=[pl.BlockSpec((tm, d), lambda i: (i, 0)),
                  pl.BlockSpec((1, d), lambda i: (0, 0)),
                  pl.BlockSpec((1, d), lambda i: (0, 0)),
                  pl.BlockSpec((d, LANES), lambda i: (0, 0)),
                  pl.BlockSpec((N_EXPERTS, 1), lambda i: (0, 0))],
        out_specs=(pl.BlockSpec((tm, d // 2), lambda i: (i, 0)),
                   pl.BlockSpec((TOP_K, tm), lambda i: (0, i)),
                   pl.BlockSpec((TOP_K, tm), lambda i: (0, i)),
                   pl.BlockSpec((TOP_K, tm), lambda i: (0, i)),
                   pl.BlockSpec((N_EXPERTS, LANES), lambda i: (0, 0))),
        scratch_shapes=[pltpu.VMEM((N_EXPERTS, LANES), F32)],
        compiler_params=_params("arbitrary"),
        name="moe_router",
    )(x, shift, scale, wr, rb)


def _moe_dest_kernel(idx_ref, rank_ref, cnt_ref, dest_ref, be_ref, nused_ref, tail_ref):
    cnt = cnt_ref[...]
    padded = jnp.floor((cnt + (E_BLOCK - 1)) * (1.0 / E_BLOCK)) * E_BLOCK
    e_i = _expert_of_row(lax.broadcasted_iota(I32, (N_EXPERTS, N_EXPERTS), 0))
    e_j = _expert_of_row(lax.broadcasted_iota(I32, (N_EXPERTS, N_EXPERTS), 1))
    earlier = jnp.where(e_j < e_i, 1.0, 0.0).astype(F32)
    pstart = _dot(earlier, padded, precision=HIGHEST)
    pend = pstart + padded

    tm = idx_ref.shape[1]
    e_col = _expert_of_row(lax.broadcasted_iota(I32, (N_EXPERTS, tm), 0))
    rows = []
    for k in range(TOP_K):
        hit = e_col == idx_ref[k:k + 1, :]
        rows.append(jnp.sum(jnp.where(hit, pstart[:, 0:1], 0.0), axis=0, keepdims=True))
    dest_ref[...] = jnp.concatenate(rows, axis=0).astype(I32) + rank_ref[...]

    nb = be_ref.shape[1]
    block_start = (lax.broadcasted_iota(I32, (N_EXPERTS, nb), 1) * E_BLOCK).astype(F32)
    ended = jnp.sum(jnp.where(pend[:, 0:1] <= block_start, 1.0, 0.0), axis=0, keepdims=True)
    be_ref[...] = jnp.minimum(ended, float(N_EXPERTS - 1)).astype(I32)
    nused_ref[...] = (jnp.sum(padded, axis=0, keepdims=True) * (1.0 / E_BLOCK)).astype(I32)
    e_row = _expert_of_row(lax.broadcasted_iota(I32, (N_EXPERTS, LANES), 0))
    lane = lax.broadcasted_iota(I32, (N_EXPERTS, LANES), 1)
    end_lane = jnp.sum(jnp.where(e_row <= lane, padded, 0.0), axis=0, keepdims=True)
    own_lane = jnp.sum(jnp.where(e_row == lane, padded, 0.0), axis=0, keepdims=True)
    tail_ref[...] = jnp.where(own_lane > 0.0, end_lane - E_BLOCK, -1.0).astype(I32)


def _moe_dest(idx_t, rank_t, counts, nblk):
    t = idx_t.shape[1]
    tm = min(t, 2048)
    nb = ((nblk + LANES - 1) // LANES) * LANES
    return pl.pallas_call(
        _moe_dest_kernel,
        out_shape=(jax.ShapeDtypeStruct((TOP_K, t), I32),
                   jax.ShapeDtypeStruct((1, nb), I32),
                   jax.ShapeDtypeStruct((1, LANES), I32),
                   jax.ShapeDtypeStruct((1, LANES), I32)),
        grid=(t // tm,),
        in_specs=[pl.BlockSpec((TOP_K, tm), lambda i: (0, i)),
                  pl.BlockSpec((TOP_K, tm), lambda i: (0, i)),
                  pl.BlockSpec((N_EXPERTS, LANES), lambda i: (0, 0))],
        out_specs=(pl.BlockSpec((TOP_K, tm), lambda i: (0, i)),
                   pl.BlockSpec((1, nb), lambda i: (0, 0)),
                   pl.BlockSpec((1, LANES), lambda i: (0, 0)),
                   pl.BlockSpec((1, LANES), lambda i: (0, 0))),
        compiler_params=_params("arbitrary"),
        name="moe_dest",
    )(idx_t, rank_t, counts)


def _dispatch_kernel(tail_ref, nused_ref, dest_ref, hf_ref, xbuf_hbm, dest_smem, zero_scr, idx_sem, zero_sem, row_sem):
    tm = dest_ref.shape[1]
    nblk = xbuf_hbm.shape[0] // E_BLOCK

    @pl.when(pl.program_id(0) == 0)
    def _():
        zero_scr[...] = jnp.zeros_like(zero_scr)

        def zero_copy(row):
            return pltpu.make_async_copy(zero_scr, xbuf_hbm.at[pl.ds(pl.multiple_of(row, E_BLOCK), E_BLOCK)], zero_sem)

        def start(e, carry):
            @pl.when(tail_ref[e] >= 0)
            def _():
                zero_copy(tail_ref[e]).start()
            return carry

        def wait(e, carry):
            @pl.when(tail_ref[e] >= 0)
            def _():
                zero_copy(0).wait()
            return carry

        def start_unused(b, carry):
            zero_copy(b * E_BLOCK).start()
            return carry

        def wait_unused(b, carry):
            zero_copy(0).wait()
            return carry

        lax.fori_loop(0, N_EXPERTS, start, 0)
        lax.fori_loop(nused_ref[0], nblk, start_unused, 0)
        lax.fori_loop(0, N_EXPERTS, wait, 0)
        lax.fori_loop(nused_ref[0], nblk, wait_unused, 0)

    load = pltpu.make_async_copy(dest_ref, dest_smem, idx_sem)
    load.start()
    load.wait()

    def issue(tok, carry):
        for k in range(TOP_K):
            pltpu.make_async_copy(hf_ref.at[pl.ds(tok, 1)], xbuf_hbm.at[pl.ds(dest_smem[k, tok], 1)], row_sem).start()
        return carry

    lax.fori_loop(0, tm, issue, 0)
    for k in range(TOP_K):
        pltpu.make_async_copy(hf_ref, xbuf_hbm.at[pl.ds(0, tm)], row_sem).wait()


def _dispatch(hf, dest_t, tail_start, nused, npad):
    t, dw = hf.shape
    tm = min(t, 1024)
    return pl.pallas_call(
        _dispatch_kernel,
        out_shape=jax.ShapeDtypeStruct((npad, dw), U32),
        grid_spec=pltpu.PrefetchScalarGridSpec(
            num_scalar_prefetch=2,
            grid=(t // tm,),
            in_specs=[pl.BlockSpec((TOP_K, tm), lambda i, tail, nu: (0, i)),
                      pl.BlockSpec((tm, dw), lambda i, tail, nu: (i, 0))],
            out_specs=pl.BlockSpec(memory_space=pl.ANY),
            scratch_shapes=[pltpu.SMEM((TOP_K, tm), I32), pltpu.VMEM((E_BLOCK, dw), U32),
                            pltpu.SemaphoreType.DMA, pltpu.SemaphoreType.DMA, pltpu.SemaphoreType.DMA]),
        compiler_params=_params("arbitrary"),
        name="moe_dispatch",
    )(tail_start, nused, dest_t, hf)


def _expert_kernel(be_ref, nused_ref, x_ref, wgu_ref, wd_ref, y_ref, wgu_scr, wd_scr):
    b = pl.program_id(0)
    e = be_ref[b]
    e_prev = be_ref[jnp.maximum(b - 1, 0)]

    @pl.when((b == 0) | (e != e_prev))
    def _():
        wgu_scr[...] = wgu_ref[...].astype(BF16)
        wd_scr[...] = wd_ref[...].astype(BF16)

    @pl.when(b < nused_ref[0])
    def _():
        y_ref[...] = _pack_bf16_halves(_gated_ffn(x_ref[...], wgu_scr, wd_scr))

    @pl.when(b >= nused_ref[0])
    def _():
        y_ref[...] = jnp.zeros_like(y_ref)


def _gated_ffn(x_packed, wgu_ref, wd_ref):
    x_lo, x_hi = _unpack_bf16_halves(x_packed)
    half = x_packed.shape[1]
    gu = _dot(x_lo.astype(BF16), wgu_ref[0:half, :]) + _dot(x_hi.astype(BF16), wgu_ref[half:2 * half, :])
    ff = gu.shape[1] // 2
    gate = gu[:, :ff]
    act = gate * jax.nn.sigmoid(gate) * gu[:, ff:]
    return _dot(act.astype(BF16), wd_ref[...])


def _experts(xbuf, block_e, nused, w_gate_up, w_down):
    npad, dw = xbuf.shape
    d = 2 * dw
    nblk = npad // E_BLOCK
    ff2 = w_gate_up.shape[2]
    last = lambda b, nu: jnp.minimum(b, nu[0] - 1)
    return pl.pallas_call(
        _expert_kernel,
        out_shape=jax.ShapeDtypeStruct((npad, dw), U32),
        grid_spec=pltpu.PrefetchScalarGridSpec(
            num_scalar_prefetch=2,
            grid=(nblk,),
            in_specs=[pl.BlockSpec((E_BLOCK, dw), lambda b, be, nu: (last(b, nu), 0)),
                      pl.BlockSpec((None, d, ff2), lambda b, be, nu: (be[b], 0, 0)),
                      pl.BlockSpec((None, ff2 // 2, d), lambda b, be, nu: (be[b], 0, 0))],
            out_specs=pl.BlockSpec((E_BLOCK, dw), lambda b, be, nu: (b, 0)),
            scratch_shapes=[pltpu.VMEM((d, ff2), BF16), pltpu.VMEM((ff2 // 2, d), BF16)]),
        compiler_params=_params("arbitrary"),
        name="moe_experts",
    )(block_e, nused, xbuf, w_gate_up, w_down)


def _shared_kernel(x_ref, wgu_ref, wd_ref, y_ref):
    y_ref[...] = _gated_ffn(x_ref[...], wgu_ref, wd_ref)


def _shared_expert(hf, wgu, wd):
    t, dw = hf.shape
    tm = min(t, 512)
    return pl.pallas_call(
        _shared_kernel,
        out_shape=jax.ShapeDtypeStruct((t, 2 * dw), F32),
        grid=(t // tm,),
        in_specs=[pl.BlockSpec((tm, dw), lambda i: (i, 0)),
                  pl.BlockSpec(wgu.shape, lambda i: (0, 0)),
                  pl.BlockSpec(wd.shape, lambda i: (0, 0))],
        out_specs=pl.BlockSpec((tm, 2 * dw), lambda i: (i, 0)),
        compiler_params=_params("parallel"),
        name="moe_shared",
    )(hf, wgu, wd)


def _combine_kernel(dest_ref, w_ref, x_ref, sh_ref, g_ref, ybuf_hbm, o_ref, rows_scr, dest_smem, idx_sem, row_sem):
    tm = x_ref.shape[0]
    load = pltpu.make_async_copy(dest_ref, dest_smem, idx_sem)
    load.start()
    load.wait()

    def issue(tok, carry):
        for k in range(TOP_K):
            pltpu.make_async_copy(ybuf_hbm.at[pl.ds(dest_smem[k, tok], 1)], rows_scr.at[k, pl.ds(tok, 1)],
                                  row_sem).start()
        return carry

    lax.fori_loop(0, tm, issue, 0)
    for k in range(TOP_K):
        pltpu.make_async_copy(ybuf_hbm.at[pl.ds(0, tm)], rows_scr.at[k], row_sem).wait()

    half = rows_scr.shape[2]
    routed_lo = jnp.zeros((tm, half), F32)
    routed_hi = jnp.zeros((tm, half), F32)
    for k in range(TOP_K):
        lo, hi = _unpack_bf16_halves(rows_scr[k])
        routed_lo = routed_lo + w_ref[:, k:k + 1] * lo
        routed_hi = routed_hi + w_ref[:, k:k + 1] * hi
    o_ref[:, :half] = x_ref[:, :half] + g_ref[:, :half] * (routed_lo + sh_ref[:, :half])
    o_ref[:, half:] = x_ref[:, half:] + g_ref[:, half:] * (routed_hi + sh_ref[:, half:])


def _combine(dest_t, w_tok, x, shared, gate, ybuf):
    t, d = x.shape
    tm = min(t, 256)
    return pl.pallas_call(
        _combine_kernel,
        out_shape=jax.ShapeDtypeStruct((t, d), F32),
        grid=(t // tm,),
        in_specs=[pl.BlockSpec((TOP_K, tm), lambda i: (0, i)),
                  pl.BlockSpec((tm, TOP_K), lambda i: (i, 0)),
                  pl.BlockSpec((tm, d), lambda i: (i, 0)),
                  pl.BlockSpec((tm, d), lambda i: (i, 0)),
                  pl.BlockSpec((1, d), lambda i: (0, 0)),
                  pl.BlockSpec(memory_space=pl.ANY)],
        out_specs=pl.BlockSpec((tm, d), lambda i: (i, 0)),
        scratch_shapes=[pltpu.VMEM((TOP_K, tm, d // 2), U32), pltpu.SMEM((TOP_K, tm), I32),
                        pltpu.SemaphoreType.DMA, pltpu.SemaphoreType.DMA],
        compiler_params=_params("arbitrary"),
        name="moe_combine",
    )(dest_t, w_tok, x, shared, gate, ybuf)


def _moe_layer(x, shift, scale, gate, w_router, router_bias, w_gate_up, w_down, ws_gate_up, ws_down):
    t, _ = x.shape
    npad = t * TOP_K + N_EXPERTS * E_BLOCK
    nblk = npad // E_BLOCK
    hf, idx_t, wts_t, rank_t, counts = _moe_pre(x, shift, scale, w_router, router_bias)
    dest_t, block_e, nused, tail_start = _moe_dest(idx_t, rank_t, counts, nblk)
    nused = nused.reshape(-1)[:1]
    xbuf = _dispatch(hf, dest_t, tail_start.reshape(-1), nused, npad)
    ybuf = _experts(xbuf, block_e.reshape(-1), nused, w_gate_up, w_down)
    shared = _shared_expert(hf, ws_gate_up.astype(BF16), ws_down.astype(BF16))
    return _combine(dest_t, wts_t.T, x, shared, gate, ybuf)


def _final_norm_kernel(x_ref, g_ref, o_ref):
    x = x_ref[...]
    o_ref[...] = x * lax.rsqrt(jnp.mean(x * x, axis=-1, keepdims=True) + EPS) * g_ref[...]


def _final_norm(x, gain):
    t, d = x.shape
    tm = min(t, 512)
    return pl.pallas_call(
        _final_norm_kernel,
        out_shape=jax.ShapeDtypeStruct((t, d), F32),
        grid=(t // tm,),
        in_specs=[pl.BlockSpec((tm, d), lambda i: (i, 0)), pl.BlockSpec((1, d), lambda i: (0, 0))],
        out_specs=pl.BlockSpec((tm, d), lambda i: (i, 0)),
        compiler_params=_params("parallel"),
        name="final_norm",
    )(x, gain.reshape(1, d))


def _even_layer(x, mod, cos, sin, w_in, b_if, mlstm_norm, q_norm, kv_norm, w_uq, w_ukv, w_out):
    t, d = x.shape
    sh1, sc1, g1 = mod[0], mod[1], mod[2]
    nq, nv = M_HEADS * M_QK, M_HEADS * M_V
    o = 0
    cols = []
    for sz in (nq, nq, nv, nv, M_HEADS, M_HEADS, A_Q_LORA, A_KV_LORA, A_ROPE):
        cols.append(w_in[:, o:o + sz])
        o += sz
    mq, mk, mv, mo, mi, mf, cq, ckv, kr = cols
    w_a = jnp.concatenate([mq, mk, mv, mo], axis=1).astype(BF16)
    zeros = lambda n: jnp.zeros((d, n), F32)
    w_b = jnp.concatenate([cq, ckv, kr, zeros(LANES - A_ROPE), mi, mf, zeros(LANES - 2 * M_HEADS)], axis=1).astype(BF16)
    gate_col_block = (A_Q_LORA + A_KV_LORA + LANES) // LANES
    proj_a = _norm_matmul(x, sh1, sc1, w_a, jnp.zeros((1, w_a.shape[1]), F32), BF16, "even_in_a")
    proj_b = _norm_matmul(x, sh1, sc1, w_b, jnp.zeros((1, w_b.shape[1]), F32), F32, "even_in_b")
    gate_bias = jnp.zeros((1, LANES), F32).at[0, :2 * M_HEADS].set(b_if.astype(F32))
    hm = _mlstm(proj_a, proj_b, gate_col_block, gate_bias, mlstm_norm.astype(F32).reshape(1, nv))

    qk = A_NOPE + A_ROPE
    wq = jnp.pad(w_uq.reshape(A_Q_LORA, A_HEADS, qk), ((0, 0), (0, 0), (0, A_QK_PAD - qk)))
    wq = wq.reshape(A_Q_LORA, A_HEADS * A_QK_PAD).astype(BF16)
    q, k, v = _mla_up(proj_b, q_norm.astype(F32).reshape(1, -1), kv_norm.astype(F32).reshape(1, -1),
                      wq, w_ukv.astype(BF16), cos, sin)
    ha = _mla_flash(q, k, v)
    w_out = w_out.astype(BF16)
    return _proj_residual([hm, ha], [w_out[:nv], w_out[nv:]], x, g1, "even_out")


def _odd_layer(x, mod, cos, sin, w_qkv, b_qkv, sinks, w_o):
    sh1, sc1, g1 = mod[0], mod[1], mod[2]
    qkv = _norm_matmul(x, sh1, sc1, w_qkv.astype(BF16), b_qkv.astype(F32).reshape(1, -1), BF16, "odd_qkv")
    o = _swa(qkv, cos, sin, sinks)
    return _proj_residual([o], [w_o.astype(BF16)], x, g1, "odd_out")


def kernel(x, c, positions, w_ada, b_ada, a_w_in, a_b_if, a_mlstm_norm, a_q_norm, a_kv_norm, a_w_uq, a_w_ukv,
           a_w_out, s_w_qkv, s_b_qkv, s_sinks, s_w_o, e_w_router, e_router_bias, e_w_gate_up, e_w_down,
           e_ws_gate_up, e_ws_down, final_norm):
    batch, t, d = x.shape
    assert batch == 1, "kernels are written for a single sequence"
    depth = w_ada.shape[0]
    xs = x.reshape(t, d)
    mods = _ada(c, w_ada, b_ada).reshape(depth, 6, 1, d)
    cos, sin = _rope_tables(positions.reshape(t))
    for layer in range(depth):
        mod = mods[layer]
        if layer % 2 == 0:
            e = layer // 2
            xs = _even_layer(xs, mod, cos, sin, a_w_in[e], a_b_if[e], a_mlstm_norm[e], a_q_norm[e], a_kv_norm[e],
                             a_w_uq[e], a_w_ukv[e], a_w_out[e])
        else:
            o = layer // 2
            xs = _odd_layer(xs, mod, cos, sin, s_w_qkv[o], s_b_qkv[o], s_sinks[o], s_w_o[o])
        xs = _moe_layer(xs, mod[3], mod[4], mod[5], e_w_router[layer], e_router_bias[layer],
                        e_w_gate_up[layer], e_w_down[layer], e_ws_gate_up[layer], e_ws_down[layer])
    return _final_norm(xs, final_norm.astype(F32)).reshape(batch, t, d)
```

```python
import functools

import jax
import jax.numpy as jnp
from jax import lax
from jax.experimental import pallas as pl
from jax.experimental.pallas import tpu as pltpu

F32 = jnp.float32
BF16 = jnp.bfloat16
I32 = jnp.int32
U32 = jnp.uint32
HIGHEST = lax.Precision.HIGHEST

EPS = 1e-6
ROPE_THETA = 10000.0
ROPE_DIM = 64

M_HEADS = 4
M_QK = 128
M_V = 256
GATE_SOFTCAP = 15.0
MLSTM_CHUNK = 128

A_HEADS = 8
A_NOPE = 128
A_ROPE = 64
A_V = 128
A_Q_LORA = 768
A_KV_LORA = 512
A_QK_PAD = 256

S_HEADS = 32
S_KV_HEADS = 4
S_HD = 64
S_BLOCK = 128

N_EXPERTS = 64
TOP_K = 8
N_GROUPS = 8
TOPK_GROUPS = 4
E_FF = 256
SHARED_FF = 256
ROUTED_SCALE = 2.5
E_BLOCK = 256

LANES = 128
VMEM_LIMIT_BYTES = 48 * 1024 * 1024
FLASH_VMEM_LIMIT_BYTES = 56 * 1024 * 1024
LOG2_E = 1.4426950408889634


def _params(*semantics):
    return pltpu.CompilerParams(dimension_semantics=semantics, vmem_limit_bytes=VMEM_LIMIT_BYTES)


def _dot(a, b, precision=None):
    return jnp.dot(a, b, preferred_element_type=F32, precision=precision)


def _dot_nt(a, b):
    return lax.dot_general(a, b, (((1,), (1,)), ((), ())), preferred_element_type=F32)


def _dot_tn(a, b):
    return lax.dot_general(a, b, (((0,), (0,)), ((), ())), preferred_element_type=F32)


def _modulated_norm(x, shift, scale):
    y = x * lax.rsqrt(jnp.mean(x * x, axis=-1, keepdims=True) + EPS)
    return y * (1.0 + scale) + shift


def _rope_lanes(x, cos, sin_signed):
    lane = lax.broadcasted_iota(I32, x.shape, 1)
    first_half = (lane % ROPE_DIM) < (ROPE_DIM // 2)
    swapped = jnp.where(first_half, pltpu.roll(x, LANES - ROPE_DIM // 2, 1), pltpu.roll(x, ROPE_DIM // 2, 1))
    return x * cos + swapped * sin_signed


def _pack_bf16_halves(x):
    n = x.shape[1] // 2
    lo = lax.bitcast_convert_type(x[:, :n].astype(BF16).astype(F32), U32)
    hi = lax.bitcast_convert_type(x[:, n:].astype(BF16).astype(F32), U32)
    return (hi & jnp.uint32(0xFFFF0000)) | (lo >> 16)


def _unpack_bf16_halves(w):
    lo = lax.bitcast_convert_type(w << 16, F32)
    hi = lax.bitcast_convert_type(w & jnp.uint32(0xFFFF0000), F32)
    return lo, hi


ROW_TILE = 8


def _load_rows(ref):
    m = ref.shape[0] // ROW_TILE
    return jnp.concatenate([ref[pl.ds(s, m, stride=ROW_TILE), :] for s in range(ROW_TILE)], axis=1)


def _store_rows(ref, x):
    m = x.shape[0]
    for s in range(ROW_TILE):
        ref[pl.ds(s, m, stride=ROW_TILE), :] = x[:, s * LANES:(s + 1) * LANES]


def _ada_kernel(c_ref, w_ref, b_ref, o_ref):
    c = c_ref[...]
    c_act = c * jax.nn.sigmoid(c)
    o_ref[...] = jnp.sum(c_act * w_ref[...], axis=0, keepdims=True) + b_ref[...]


def _ada(c, w_ada, b_ada):
    depth, d, n = w_ada.shape
    tn = 1024
    return pl.pallas_call(
        _ada_kernel,
        out_shape=jax.ShapeDtypeStruct((depth, 1, n), F32),
        grid=(depth, n // tn),
        in_specs=[pl.BlockSpec((d, 1), lambda l, j: (0, 0)),
                  pl.BlockSpec((None, d, tn), lambda l, j: (l, 0, j)),
                  pl.BlockSpec((None, 1, tn), lambda l, j: (l, 0, j))],
        out_specs=pl.BlockSpec((None, 1, tn), lambda l, j: (l, 0, j)),
        compiler_params=_params("parallel", "parallel"),
        name="ada_mod",
    )(c.reshape(d, 1), w_ada, b_ada.reshape(depth, 1, n))


def _rope_table_kernel(pos_ref, inv_ref, sign_ref, cos_ref, sin_ref):
    ang = pos_ref[...].astype(F32) * inv_ref[...]
    cos_ref[...] = jnp.cos(ang)
    sin_ref[...] = jnp.sin(ang) * sign_ref[...]


def _rope_tables(positions):
    t = positions.shape[0]
    half = ROPE_DIM // 2
    inv_freq = jnp.power(ROPE_THETA, -jnp.arange(half, dtype=F32) / half)
    inv = jnp.tile(inv_freq, LANES // half).reshape(1, LANES)
    sign = jnp.tile(jnp.concatenate([-jnp.ones((half,), F32), jnp.ones((half,), F32)]), LANES // ROPE_DIM)
    tm = min(t, 2048)
    return pl.pallas_call(
        _rope_table_kernel,
        out_shape=(jax.ShapeDtypeStruct((t, LANES), F32), jax.ShapeDtypeStruct((t, LANES), F32)),
        grid=(t // tm,),
        in_specs=[pl.BlockSpec((tm, 1), lambda i: (i, 0)),
                  pl.BlockSpec((1, LANES), lambda i: (0, 0)),
                  pl.BlockSpec((1, LANES), lambda i: (0, 0))],
        out_specs=(pl.BlockSpec((tm, LANES), lambda i: (i, 0)), pl.BlockSpec((tm, LANES), lambda i: (i, 0))),
        compiler_params=_params("parallel"),
        name="rope_tables",
    )(positions.reshape(t, 1), inv, sign.reshape(1, LANES))


def _norm_matmul_kernel(x_ref, sh_ref, sc_ref, w_ref, b_ref, o_ref, h_scr):
    @pl.when(pl.program_id(1) == 0)
    def _():
        h_scr[...] = _modulated_norm(x_ref[...], sh_ref[...], sc_ref[...]).astype(BF16)

    o_ref[...] = (_dot(h_scr[...], w_ref[...]) + b_ref[...]).astype(o_ref.dtype)


def _norm_matmul(x, shift, scale, w, bias, out_dtype, name):
    t, d = x.shape
    n = w.shape[1]
    tm = min(t, 1024)
    tn = 512 if n % 512 == 0 else 256
    return pl.pallas_call(
        _norm_matmul_kernel,
        out_shape=jax.ShapeDtypeStruct((t, n), out_dtype),
        grid=(t // tm, n // tn),
        in_specs=[pl.BlockSpec((tm, d), lambda i, j: (i, 0)),
                  pl.BlockSpec((1, d), lambda i, j: (0, 0)),
                  pl.BlockSpec((1, d), lambda i, j: (0, 0)),
                  pl.BlockSpec((d, tn), lambda i, j: (0, j)),
                  pl.BlockSpec((1, tn), lambda i, j: (0, j))],
        out_specs=pl.BlockSpec((tm, tn), lambda i, j: (i, j)),
        scratch_shapes=[pltpu.VMEM((tm, d), BF16)],
        compiler_params=_params("parallel", "arbitrary"),
        name=name,
    )(x, shift, scale, w, bias)


def _mlstm_kernel(q_ref, k_ref, v_ref, o_ref, g_ref, gb_ref, gain_ref, out_ref, c_scr, n_scr, m_scr):
    L = MLSTM_CHUNK
    tm = q_ref.shape[0]

    @pl.when(pl.program_id(0) == 0)
    def _():
        c_scr[...] = jnp.zeros_like(c_scr)
        n_scr[...] = jnp.zeros_like(n_scr)
        m_scr[...] = jnp.zeros_like(m_scr)

    capped = GATE_SOFTCAP * jnp.tanh((g_ref[...] + gb_ref[...]) / GATE_SOFTCAP)
    log_sig = jnp.minimum(capped, 0.0) - jnp.log1p(jnp.exp(-jnp.abs(capped)))
    lane = lax.broadcasted_iota(I32, capped.shape, 1)
    gate = jnp.where(lane < M_HEADS, capped, log_sig)
    r = lax.broadcasted_iota(I32, (tm, tm), 0)
    c = lax.broadcasted_iota(I32, (tm, tm), 1)
    chunk_tril = jnp.where(((r // L) == (c // L)) & (c <= r), 1.0, 0.0).astype(F32)
    cum = _dot(chunk_tril, gate, precision=HIGHEST)
    gate_rows = gate.T
    cum_rows = cum.T
    rr = lax.broadcasted_iota(I32, (L, L), 0)
    cc = lax.broadcasted_iota(I32, (L, L), 1)
    causal = cc <= rr

    for ci in range(tm // L):
        sl = slice(ci * L, (ci + 1) * L)
        for h in range(M_HEADS):
            qk = slice(h * M_QK, (h + 1) * M_QK)
            vv = slice(h * M_V, (h + 1) * M_V)
            qc = q_ref[sl, qk]
            kc = k_ref[sl, qk].astype(F32) * (M_QK ** -0.5)
            kcb = kc.astype(BF16)
            vc = v_ref[sl, vv]
            ig_col = gate[sl, h:h + 1]
            b_col = cum[sl, M_HEADS + h:M_HEADS + h + 1]
            ig_row = gate_rows[h:h + 1, sl]
            b_row = cum_rows[M_HEADS + h:M_HEADS + h + 1, sl]
            m_prev = m_scr[h:h + 1, 0:1]

            dm = jnp.where(causal, b_col - b_row + ig_row, -jnp.inf)
            inter = b_col + m_prev
            m_row = jnp.maximum(inter, jnp.max(dm, axis=-1, keepdims=True))
            w_intra = jnp.exp(dm - m_row)
            w_inter = jnp.exp(inter - m_row)
            s = _dot_nt(qc, kcb) * w_intra
            c_state = c_scr[h]
            n_state = n_scr[h]
            num = _dot(s.astype(BF16), vc) + w_inter * _dot_nt(qc, c_state.astype(BF16))
            den = (jnp.sum(s, axis=-1, keepdims=True)
                   + w_inter * jnp.sum(qc.astype(F32) * n_state, axis=-1, keepdims=True))
            hh = num / jnp.maximum(jnp.abs(den), jnp.exp(-m_row))

            b_last = b_col[L - 1:L, :]
            g_row = b_last - b_row + ig_row
            g_col = b_last - b_col + ig_col
            m_new = jnp.maximum(b_last + m_prev, jnp.max(g_row, axis=-1, keepdims=True))
            ws_col = jnp.exp(g_col - m_new)
            decay = jnp.exp(b_last + m_prev - m_new)
            vw = (vc.astype(F32) * ws_col).astype(BF16)
            c_scr[h] = decay * c_state + _dot_tn(vw, kcb)
            n_scr[h] = decay * n_state + jnp.sum(kc * ws_col, axis=0, keepdims=True)
            m_scr[h:h + 1, :] = jnp.broadcast_to(m_new, (1, LANES))

            y = hh * lax.rsqrt(jnp.mean(hh * hh, axis=-1, keepdims=True) + EPS) * gain_ref[:, vv]
            y = y * jax.nn.sigmoid(o_ref[sl, vv].astype(F32))
            out_ref[sl, vv] = y.astype(out_ref.dtype)


def _mlstm(proj_a, proj_b, gate_col_block, gate_bias, gain):
    t = proj_a.shape[0]
    tm = min(t, 512)
    nq = M_HEADS * M_QK
    nv = M_HEADS * M_V
    return pl.pallas_call(
        _mlstm_kernel,
        out_shape=jax.ShapeDtypeStruct((t, nv), BF16),
        grid=(t // tm,),
        in_specs=[pl.BlockSpec((tm, nq), lambda i: (i, 0)),
                  pl.BlockSpec((tm, nq), lambda i: (i, 1)),
                  pl.BlockSpec((tm, nv), lambda i: (i, 1)),
                  pl.BlockSpec((tm, nv), lambda i: (i, 2)),
                  pl.BlockSpec((tm, LANES), lambda i: (i, gate_col_block)),
                  pl.BlockSpec((1, LANES), lambda i: (0, 0)),
                  pl.BlockSpec((1, nv), lambda i: (0, 0))],
        out_specs=pl.BlockSpec((tm, nv), lambda i: (i, 0)),
        scratch_shapes=[pltpu.VMEM((M_HEADS, M_V, M_QK), F32),
                        pltpu.VMEM((M_HEADS, 1, M_QK), F32),
                        pltpu.VMEM((8, LANES), F32)],
        compiler_params=_params("arbitrary"),
        name="mlstm",
    )(proj_a, proj_a, proj_a, proj_a, proj_b, gate_bias, gain)


def _mla_up_kernel(pb_ref, qn_ref, kvn_ref, wq_ref, wkv_ref, cos_ref, sin_ref,
                   q_ref, k_ref, v_ref, cq_scr, ckv_scr, kr_scr):
    @pl.when(pl.program_id(1) == 0)
    def _():
        cq = pb_ref[:, 0:A_Q_LORA]
        cq_scr[...] = (cq * lax.rsqrt(jnp.mean(cq * cq, axis=-1, keepdims=True) + EPS) * qn_ref[...]).astype(BF16)
        ckv = pb_ref[:, A_Q_LORA:A_Q_LORA + A_KV_LORA]
        ckv_scr[...] = (ckv * lax.rsqrt(jnp.mean(ckv * ckv, axis=-1, keepdims=True) + EPS)
                        * kvn_ref[...]).astype(BF16)
        kr = pb_ref[:, A_Q_LORA + A_KV_LORA:A_Q_LORA + A_KV_LORA + LANES]
        kr_scr[...] = _rope_lanes(kr, cos_ref[...], sin_ref[...]).astype(BF16)

    scale = (A_NOPE + A_ROPE) ** -0.5 * LOG2_E
    qh = _dot(cq_scr[...], wq_ref[...])
    q_pe = _rope_lanes(qh[:, A_NOPE:], cos_ref[...], sin_ref[...])
    q_ref[...] = (jnp.concatenate([qh[:, :A_NOPE], q_pe], axis=1) * scale).astype(q_ref.dtype)
    kvh = _dot(ckv_scr[...], wkv_ref[...])
    k_ref[...] = jnp.concatenate([kvh[:, :A_NOPE].astype(BF16), kr_scr[...]], axis=1)
    v_ref[...] = kvh[:, A_NOPE:].astype(v_ref.dtype)


def _mla_up(proj_b, q_norm, kv_norm, wq, wkv, cos, sin):
    t, nb = proj_b.shape
    tm = min(t, 1024)
    return pl.pallas_call(
        _mla_up_kernel,
        out_shape=(jax.ShapeDtypeStruct((A_HEADS, t, A_QK_PAD), BF16),
                   jax.ShapeDtypeStruct((A_HEADS, t, A_QK_PAD), BF16),
                   jax.ShapeDtypeStruct((A_HEADS, t, A_V), BF16)),
        grid=(t // tm, A_HEADS),
        in_specs=[pl.BlockSpec((tm, nb), lambda i, h: (i, 0)),
                  pl.BlockSpec((1, A_Q_LORA), lambda i, h: (0, 0)),
                  pl.BlockSpec((1, A_KV_LORA), lambda i, h: (0, 0)),
                  pl.BlockSpec((A_Q_LORA, A_QK_PAD), lambda i, h: (0, h)),
                  pl.BlockSpec((A_KV_LORA, A_NOPE + A_V), lambda i, h: (0, h)),
                  pl.BlockSpec((tm, LANES), lambda i, h: (i, 0)),
                  pl.BlockSpec((tm, LANES), lambda i, h: (i, 0))],
        out_specs=(pl.BlockSpec((None, tm, A_QK_PAD), lambda i, h: (h, i, 0)),
                   pl.BlockSpec((None, tm, A_QK_PAD), lambda i, h: (h, i, 0)),
                   pl.BlockSpec((None, tm, A_V), lambda i, h: (h, i, 0))),
        scratch_shapes=[pltpu.VMEM((tm, A_Q_LORA), BF16),
                        pltpu.VMEM((tm, A_KV_LORA), BF16),
                        pltpu.VMEM((tm, LANES), BF16)],
        compiler_params=_params("parallel", "arbitrary"),
        name="mla_up",
    )(proj_b, q_norm, kv_norm, wq, wkv, cos, sin)


def _mla_flash_kernel(q_ref, k_ref, v_ref, o_ref, m_scr, l_scr, acc_scr, *, tq, tkb):
    i = pl.program_id(1)
    q = q_ref[...]
    m_scr[...] = jnp.full(m_scr.shape, -jnp.inf, F32)
    l_scr[...] = jnp.zeros(l_scr.shape, F32)
    acc_scr[...] = jnp.zeros(acc_scr.shape, F32)

    def tile(start, width, masked):
        s = _dot_nt(q, k_ref[pl.ds(start, width), :])
        if masked:
            row = lax.broadcasted_iota(I32, (tq, width), 0)
            col = lax.broadcasted_iota(I32, (tq, width), 1)
            s = jnp.where(col <= row, s, -jnp.inf)
        m = m_scr[...]
        m_new = jnp.maximum(m, jnp.broadcast_to(jnp.max(s, axis=-1, keepdims=True), m.shape))
        alpha = jnp.exp2(m - m_new)
        p = jnp.exp2(s - jnp.concatenate([m_new] * (width // LANES), axis=1))
        lane_sums = p[:, 0:LANES]
        for j in range(1, width // LANES):
            lane_sums = lane_sums + p[:, j * LANES:(j + 1) * LANES]
        l_scr[...] = alpha * l_scr[...] + lane_sums
        acc_scr[...] = alpha * acc_scr[...] + _dot(p.astype(BF16), v_ref[pl.ds(start, width), :])
        m_scr[...] = m_new

    per = tkb // tq
    n_big = i // per

    def body(j, carry):
        tile(pl.multiple_of(j * tkb, tkb), tkb, False)
        return carry

    lax.fori_loop(0, n_big, body, 0)
    rem = i - n_big * per
    for r in range(1, per):
        @pl.when(rem >= r)
        def _():
            tile(pl.multiple_of(n_big * tkb + (r - 1) * tq, tq), tq, False)
    tile(pl.multiple_of(i * tq, tq), tq, True)
    o_ref[...] = (acc_scr[...] / jnp.sum(l_scr[...], axis=-1, keepdims=True)).astype(o_ref.dtype)


def _mla_flash(q, k, v):
    _, t, _ = q.shape
    tq = min(t, 1024)
    tkb = min(t, 2048)
    return pl.pallas_call(
        functools.partial(_mla_flash_kernel, tq=tq, tkb=tkb),
        out_shape=jax.ShapeDtypeStruct((t, A_HEADS * A_V), BF16),
        grid=(A_HEADS, t // tq),
        in_specs=[pl.BlockSpec((None, tq, A_QK_PAD), lambda h, i: (h, i, 0)),
                  pl.BlockSpec((None, t, A_QK_PAD), lambda h, i: (h, 0, 0)),
                  pl.BlockSpec((None, t, A_V), lambda h, i: (h, 0, 0))],
        out_specs=pl.BlockSpec((tq, A_V), lambda h, i: (i, h)),
        scratch_shapes=[pltpu.VMEM((tq, LANES), F32), pltpu.VMEM((tq, LANES), F32), pltpu.VMEM((tq, A_V), F32)],
        compiler_params=pltpu.CompilerParams(dimension_semantics=("parallel", "arbitrary"),
                                             vmem_limit_bytes=FLASH_VMEM_LIMIT_BYTES),
        name="mla_flash",
    )(q, k, v)


def _proj_residual_kernel(*refs, n_lhs):
    lhs = refs[:n_lhs]
    ws = refs[n_lhs:2 * n_lhs]
    x_ref, g_ref, o_ref = refs[2 * n_lhs:]
    acc = _dot(lhs[0][...], ws[0][...])
    for a, w in zip(lhs[1:], ws[1:]):
        acc = acc + _dot(a[...], w[...])
    o_ref[...] = x_ref[...] + g_ref[...] * acc


def _proj_residual(lhs_list, w_list, x, gate, name):
    t, d = x.shape
    tm = min(t, 1024)
    tn = 512
    n_lhs = len(lhs_list)
    in_specs = ([pl.BlockSpec((tm, a.shape[1]), lambda i, j: (i, 0)) for a in lhs_list]
                + [pl.BlockSpec((w.shape[0], tn), lambda i, j: (0, j)) for w in w_list]
                + [pl.BlockSpec((tm, tn), lambda i, j: (i, j)), pl.BlockSpec((1, tn), lambda i, j: (0, j))])
    return pl.pallas_call(
        functools.partial(_proj_residual_kernel, n_lhs=n_lhs),
        out_shape=jax.ShapeDtypeStruct((t, d), F32),
        grid=(t // tm, d // tn),
        in_specs=in_specs,
        out_specs=pl.BlockSpec((tm, tn), lambda i, j: (i, j)),
        compiler_params=_params("parallel", "parallel"),
        name=name,
    )(*lhs_list, *w_list, x, gate)


def _swa_kernel(q_ref, kc_ref, kp_ref, vc_ref, vp_ref, cosc_ref, sinc_ref, cosp_ref, sinp_ref, sink_ref, o_ref):
    i = pl.program_id(0)
    qb = S_BLOCK
    group = S_HEADS // S_KV_HEADS
    pairs = group // 2
    cos_c, sin_c = cosc_ref[...], sinc_ref[...]
    cos_w = jnp.concatenate([cosp_ref[...], cos_c], axis=0)
    sin_w = jnp.concatenate([sinp_ref[...], sin_c], axis=0)
    kw = jnp.concatenate([kp_ref[...], kc_ref[...]], axis=0).astype(F32)
    vw = jnp.concatenate([vp_ref[...], vc_ref[...]], axis=0)

    r = lax.broadcasted_iota(I32, (pairs * qb, 2 * qb), 0) % qb
    c = lax.broadcasted_iota(I32, (pairs * qb, 2 * qb), 1)
    dist = qb + r - c
    valid = (dist >= 0) & (dist < S_BLOCK) & ((c >= qb) | (i > 0))
    lane = lax.broadcasted_iota(I32, (2 * qb, LANES), 1)
    low = lane < S_HD
    low_rows = lax.broadcasted_iota(I32, (pairs * qb, LANES), 1) < S_HD
    ones_d = jnp.concatenate([jnp.where(low, 1.0, 0.0), jnp.where(low, 0.0, 1.0)], axis=0).astype(BF16)

    for g in range(S_KV_HEADS):
        col = slice((g // 2) * LANES, (g // 2 + 1) * LANES)
        k_pair = _rope_lanes(kw[:, col], cos_w, sin_w)
        v_pair = vw[:, col]
        keep = low if g % 2 == 0 else jnp.logical_not(low)
        k_own = jnp.where(keep, k_pair, 0.0)
        v_own = jnp.where(keep, v_pair.astype(F32), 0.0)
        k_other = pltpu.roll(k_own, S_HD, 1)
        v_other = pltpu.roll(v_own, S_HD, 1)
        k_lo, k_hi = (k_own, k_other) if g % 2 == 0 else (k_other, k_own)
        v_lo, v_hi = (v_own, v_other) if g % 2 == 0 else (v_other, v_own)
        kd = jnp.concatenate([k_lo, k_hi], axis=0).astype(BF16)
        vd = jnp.concatenate([v_lo, v_hi], axis=0).astype(BF16)
        qs = []
        for p in range(pairs):
            qcol = slice((g * pairs + p) * LANES, (g * pairs + p + 1) * LANES)
            qs.append(_rope_lanes(q_ref[:, qcol].astype(F32), cos_c, sin_c) * (S_HD ** -0.5))
        qg = jnp.concatenate(qs, axis=0).astype(BF16)
        s = _dot_nt(qg, kd)
        exps, sink_terms = [], []
        for half in range(2):
            sh = jnp.where(valid, s[:, half * 2 * qb:(half + 1) * 2 * qb], -jnp.inf)
            sink = sink_ref[g, half]
            m = jnp.maximum(jnp.broadcast_to(jnp.max(sh, axis=-1, keepdims=True), sink.shape), sink)
            exps.append(jnp.exp(sh - jnp.concatenate([m, m], axis=1)).astype(BF16))
            sink_terms.append(jnp.exp(sink - m))
        e_all = jnp.concatenate(exps, axis=1)
        den = _dot(e_all, ones_d) + jnp.where(low_rows, sink_terms[0], sink_terms[1])
        o = _dot(e_all, vd) / den
        for p in range(pairs):
            qcol = slice((g * pairs + p) * LANES, (g * pairs + p + 1) * LANES)
            o_ref[:, qcol] = o[p * qb:(p + 1) * qb].astype(o_ref.dtype)


def _swa(qkv, cos, sin, sinks):
    t = qkv.shape[0]
    qb = S_BLOCK
    nq = S_HEADS * S_HD
    nk = S_KV_HEADS * S_HD
    group = S_HEADS // S_KV_HEADS
    pairs = group // 2
    sink_cols = jnp.repeat(sinks.astype(F32).reshape(S_KV_HEADS, pairs, 2).transpose(0, 2, 1), qb, axis=-1)
    sink_cols = jnp.broadcast_to(sink_cols.reshape(S_KV_HEADS, 2, pairs * qb, 1), (S_KV_HEADS, 2, pairs * qb, LANES))
    k_blk = nq // nk
    prev = lambda i: jnp.maximum(i - 1, 0)
    return pl.pallas_call(
        _swa_kernel,
        out_shape=jax.ShapeDtypeStruct((t, nq), BF16),
        grid=(t // qb,),
        in_specs=[pl.BlockSpec((qb, nq), lambda i: (i, 0)),
                  pl.BlockSpec((qb, nk), lambda i: (i, k_blk)),
                  pl.BlockSpec((qb, nk), lambda i: (prev(i), k_blk)),
                  pl.BlockSpec((qb, nk), lambda i: (i, k_blk + 1)),
                  pl.BlockSpec((qb, nk), lambda i: (prev(i), k_blk + 1)),
                  pl.BlockSpec((qb, LANES), lambda i: (i, 0)),
                  pl.BlockSpec((qb, LANES), lambda i: (i, 0)),
                  pl.BlockSpec((qb, LANES), lambda i: (prev(i), 0)),
                  pl.BlockSpec((qb, LANES), lambda i: (prev(i), 0)),
                  pl.BlockSpec((S_KV_HEADS, 2, pairs * qb, LANES), lambda i: (0, 0, 0, 0))],
        out_specs=pl.BlockSpec((qb, nq), lambda i: (i, 0)),
        compiler_params=_params("parallel"),
        name="swa",
    )(qkv, qkv, qkv, qkv, qkv, cos, sin, cos, sin, sink_cols)


def _expert_of_row(p):
    per_group = N_EXPERTS // N_GROUPS
    return (p % N_GROUPS) * per_group + p // N_GROUPS


def _moe_pre_kernel(x_ref, sh_ref, sc_ref, wr_ref, rb_ref, hf_ref, idx_ref, wt_ref, rank_ref, cnt_ref, carry_scr):
    tm = x_ref.shape[0]
    per_group = N_EXPERTS // N_GROUPS

    @pl.when(pl.program_id(0) == 0)
    def _():
        carry_scr[...] = jnp.zeros_like(carry_scr)

    h = _modulated_norm(x_ref[...], sh_ref[...], sc_ref[...])
    _store_rows(hf_ref, _pack_bf16_halves(h))
    h_hi = h.astype(BF16)
    h_lo = (h - h_hi.astype(F32)).astype(BF16)
    logits = _dot(h_hi, wr_ref[0]) + (_dot(h_hi, wr_ref[1]) + _dot(h_lo, wr_ref[0]))
    scores = jax.nn.sigmoid(logits.T[:N_EXPERTS, :])
    biased = scores + rb_ref[...]

    members = [biased[j * N_GROUPS:(j + 1) * N_GROUPS, :] for j in range(per_group)]
    m1 = members[0]
    for a in members[1:]:
        m1 = jnp.maximum(m1, a)
    first = jnp.full(m1.shape, per_group, I32)
    for j in reversed(range(per_group)):
        first = jnp.where(members[j] == m1, j, first)
    m2 = jnp.full(m1.shape, -jnp.inf, F32)
    for j in range(per_group):
        m2 = jnp.maximum(m2, jnp.where(first == j, -jnp.inf, members[j]))
    group_score = m1 + m2

    g_iota = lax.broadcasted_iota(I32, group_score.shape, 0).astype(F32)
    g_sel = jnp.zeros(group_score.shape, F32)
    for _ in range(TOPK_GROUPS):
        best = jnp.max(group_score, axis=0, keepdims=True)
        gi = jnp.min(jnp.where(group_score == best, g_iota, float(N_GROUPS)), axis=0, keepdims=True)
        hit = g_iota == gi
        g_sel = jnp.where(hit, 1.0, g_sel)
        group_score = jnp.where(hit, -jnp.inf, group_score)
    masked = jnp.concatenate([jnp.where(g_sel > 0.5, a, -jnp.inf) for a in members], axis=0)

    e_iota = _expert_of_row(lax.broadcasted_iota(I32, masked.shape, 0)).astype(F32)
    sel = jnp.zeros(masked.shape, F32)
    idx_rows, w_rows = [], []
    for _ in range(TOP_K):
        best = jnp.max(masked, axis=0, keepdims=True)
        ei = jnp.min(jnp.where(masked == best, e_iota, float(N_EXPERTS)), axis=0, keepdims=True)
        hit = e_iota == ei
        idx_rows.append(ei)
        w_rows.append(jnp.sum(jnp.where(hit, scores, 0.0), axis=0, keepdims=True))
        sel = jnp.where(hit, 1.0, sel)
        masked = jnp.where(hit, -jnp.inf, masked)
    idx = jnp.concatenate(idx_rows, axis=0).astype(I32)
    wts = jnp.concatenate(w_rows, axis=0)
    wts = wts / jnp.sum(wts, axis=0, keepdims=True) * ROUTED_SCALE

    r = lax.broadcasted_iota(I32, (tm, tm), 0)
    c = lax.broadcasted_iota(I32, (tm, tm), 1)
    before = jnp.where(r < c, 1.0, 0.0).astype(BF16)
    rank_excl = carry_scr[:, 0:1] + _dot(sel.astype(BF16), before)
    rank_rows = [jnp.sum(jnp.where(e_iota == idx_rows[k], rank_excl, 0.0), axis=0, keepdims=True)
                 for k in range(TOP_K)]
    carry_scr[...] = carry_scr[...] + jnp.sum(sel, axis=1, keepdims=True)

    idx_ref[...] = idx
    wt_ref[...] = wts
    rank_ref[...] = jnp.concatenate(rank_rows, axis=0).astype(I32)
    cnt_ref[...] = carry_scr[...]


def _moe_pre(x, shift, scale, w_router, router_bias):
    t, d = x.shape
    tm = min(t, 256)
    rows = jnp.arange(N_EXPERTS)
    perm = _expert_of_row(rows)
    wr = jnp.zeros((d, LANES), F32).at[:, :N_EXPERTS].set(w_router[:, perm])
    wr_hi = wr.astype(BF16)
    wr = jnp.stack([wr_hi, (wr - wr_hi.astype(F32)).astype(BF16)])
    rb = router_bias.astype(F32)[perm].reshape(N_EXPERTS, 1)
    tok = lambda n, dt: jax.ShapeDtypeStruct((TOP_K, t), dt)
    return pl.pallas_call(
        _moe_pre_kernel,
        out_shape=(jax.ShapeDtypeStruct((t * ROW_TILE, LANES), U32), tok(t, I32), tok(t, F32), tok(t, I32),
                   jax.ShapeDtypeStruct((N_EXPERTS, LANES), F32)),
        grid=(t // tm,),
        in_specs=[pl.BlockSpec((tm, d), lambda i: (i, 0)),
                  pl.BlockSpec((1, d), lambda i: (0, 0)),
                  pl.BlockSpec((1, d), lambda i: (0, 0)),
                  pl.BlockSpec((2, d, LANES), lambda i: (0, 0, 0)),
                  pl.BlockSpec((N_EXPERTS, 1), lambda i: (0, 0))],
        out_specs=(pl.BlockSpec((tm * ROW_TILE, LANES), lambda i: (i, 0)),
                   pl.BlockSpec((TOP_K, tm), lambda i: (0, i)),
                   pl.BlockSpec((TOP_K, tm), lambda i: (0, i)),
                   pl.BlockSpec((TOP_K, tm), lambda i: (0, i)),
                   pl.BlockSpec((N_EXPERTS, LANES), lambda i: (0, 0))),
        scratch_shapes=[pltpu.VMEM((N_EXPERTS, LANES), F32)],
        compiler_params=_params("arbitrary"),
        name="moe_router",
    )(x, shift, scale, wr, rb)


def _moe_dest_kernel(idx_ref, rank_ref, cnt_ref, dest_ref, be_ref, nused_ref, tail_ref):
    cnt = cnt_ref[...]
    padded = jnp.floor((cnt + (E_BLOCK - 1)) * (1.0 / E_BLOCK)) * E_BLOCK
    e_i = _expert_of_row(lax.broadcasted_iota(I32, (N_EXPERTS, N_EXPERTS), 0))
    e_j = _expert_of_row(lax.broadcasted_iota(I32, (N_EXPERTS, N_EXPERTS), 1))
    earlier = jnp.where(e_j < e_i, 1.0, 0.0).astype(F32)
    pstart = _dot(earlier, padded, precision=HIGHEST)
    pend = pstart + padded

    tm = idx_ref.shape[1]
    e_col = _expert_of_row(lax.broadcasted_iota(I32, (N_EXPERTS, tm), 0))
    rows = []
    for k in range(TOP_K):
        hit = e_col == idx_ref[k:k + 1, :]
        rows.append(jnp.sum(jnp.where(hit, pstart[:, 0:1], 0.0), axis=0, keepdims=True))
    dest_ref[...] = jnp.concatenate(rows, axis=0).astype(I32) + rank_ref[...]

    nb = be_ref.shape[1]
    block_start = (lax.broadcasted_iota(I32, (N_EXPERTS, nb), 1) * E_BLOCK).astype(F32)
    ended = jnp.sum(jnp.where(pend[:, 0:1] <= block_start, 1.0, 0.0), axis=0, keepdims=True)
    be_ref[...] = jnp.minimum(ended, float(N_EXPERTS - 1)).astype(I32)
    nused_ref[...] = (jnp.sum(padded, axis=0, keepdims=True) * (1.0 / E_BLOCK)).astype(I32)
    e_row = _expert_of_row(lax.broadcasted_iota(I32, (N_EXPERTS, LANES), 0))
    lane = lax.broadcasted_iota(I32, (N_EXPERTS, LANES), 1)
    end_lane = jnp.sum(jnp.where(e_row <= lane, padded, 0.0), axis=0, keepdims=True)
    own_lane = jnp.sum(jnp.where(e_row == lane, padded, 0.0), axis=0, keepdims=True)
    tail_ref[...] = jnp.where(own_lane > 0.0, end_lane - E_BLOCK, -1.0).astype(I32)


def _moe_dest(idx_t, rank_t, counts, nblk):
    t = idx_t.shape[1]
    tm = min(t, 2048)
    nb = ((nblk + LANES - 1) // LANES) * LANES
    return pl.pallas_call(
        _moe_dest_kernel,
        out_shape=(jax.ShapeDtypeStruct((TOP_K, t), I32),
                   jax.ShapeDtypeStruct((1, nb), I32),
                   jax.ShapeDtypeStruct((1, LANES), I32),
                   jax.ShapeDtypeStruct((1, LANES), I32)),
        grid=(t // tm,),
        in_specs=[pl.BlockSpec((TOP_K, tm), lambda i: (0, i)),
                  pl.BlockSpec((TOP_K, tm), lambda i: (0, i)),
                  pl.BlockSpec((N_EXPERTS, LANES), lambda i: (0, 0))],
        out_specs=(pl.BlockSpec((TOP_K, tm), lambda i: (0, i)),
                   pl.BlockSpec((1, nb), lambda i: (0, 0)),
                   pl.BlockSpec((1, LANES), lambda i: (0, 0)),
                   pl.BlockSpec((1, LANES), lambda i: (0, 0))),
        compiler_params=_params("arbitrary"),
        name="moe_dest",
    )(idx_t, rank_t, counts)


def _dispatch_kernel(tail_ref, nused_ref, dest_ref, hf_ref, xbuf_hbm, dest_smem, zero_scr, idx_sem, zero_sem, row_sem):
    tm = dest_ref.shape[1]
    block_sublanes = E_BLOCK * ROW_TILE
    nblk = xbuf_hbm.shape[0] // block_sublanes

    @pl.when(pl.program_id(0) == 0)
    def _():
        zero_scr[...] = jnp.zeros_like(zero_scr)

        def zero_copy(row):
            start = pl.multiple_of(row * ROW_TILE, block_sublanes)
            return pltpu.make_async_copy(zero_scr, xbuf_hbm.at[pl.ds(start, block_sublanes)], zero_sem)

        def start(e, carry):
            @pl.when(tail_ref[e] >= 0)
            def _():
                zero_copy(tail_ref[e]).start()
            return carry

        def wait(e, carry):
            @pl.when(tail_ref[e] >= 0)
            def _():
                zero_copy(0).wait()
            return carry

        def start_unused(b, carry):
            zero_copy(b * E_BLOCK).start()
            return carry

        def wait_unused(b, carry):
            zero_copy(0).wait()
            return carry

        lax.fori_loop(0, N_EXPERTS, start, 0)
        lax.fori_loop(nused_ref[0], nblk, start_unused, 0)
        lax.fori_loop(0, N_EXPERTS, wait, 0)
        lax.fori_loop(nused_ref[0], nblk, wait_unused, 0)

    load = pltpu.make_async_copy(dest_ref, dest_smem, idx_sem)
    load.start()
    load.wait()

    def row(ref, r):
        return ref.at[pl.ds(pl.multiple_of(r * ROW_TILE, ROW_TILE), ROW_TILE)]

    def issue(tok, carry):
        for k in range(TOP_K):
            pltpu.make_async_copy(row(hf_ref, tok), row(xbuf_hbm, dest_smem[k, tok]), row_sem).start()
        return carry

    lax.fori_loop(0, tm, issue, 0)
    for k in range(TOP_K):
        pltpu.make_async_copy(hf_ref, xbuf_hbm.at[pl.ds(0, tm * ROW_TILE)], row_sem).wait()


def _dispatch(hf, dest_t, tail_start, nused, npad):
    t = hf.shape[0] // ROW_TILE
    tm = min(t, 1024)
    return pl.pallas_call(
        _dispatch_kernel,
        out_shape=jax.ShapeDtypeStruct((npad * ROW_TILE, LANES), U32),
        grid_spec=pltpu.PrefetchScalarGridSpec(
            num_scalar_prefetch=2,
            grid=(t // tm,),
            in_specs=[pl.BlockSpec((TOP_K, tm), lambda i, tail, nu: (0, i)),
                      pl.BlockSpec((tm * ROW_TILE, LANES), lambda i, tail, nu: (i, 0))],
            out_specs=pl.BlockSpec(memory_space=pl.ANY),
            scratch_shapes=[pltpu.SMEM((TOP_K, tm), I32), pltpu.VMEM((E_BLOCK * ROW_TILE, LANES), U32),
                            pltpu.SemaphoreType.DMA, pltpu.SemaphoreType.DMA, pltpu.SemaphoreType.DMA]),
        compiler_params=_params("arbitrary"),
        name="moe_dispatch",
    )(tail_start, nused, dest_t, hf)


def _expert_kernel(be_ref, nused_ref, x_ref, wgu_ref, wd_ref, y_ref, wgu_scr, wd_scr):
    b = pl.program_id(0)
    e = be_ref[b]
    e_prev = be_ref[jnp.maximum(b - 1, 0)]

    @pl.when((b == 0) | (e != e_prev))
    def _():
        wgu_scr[...] = wgu_ref[...].astype(BF16)
        wd_scr[...] = wd_ref[...].astype(BF16)

    @pl.when(b < nused_ref[0])
    def _():
        _store_rows(y_ref, _pack_bf16_halves(_gated_ffn(_load_rows(x_ref), wgu_scr, wd_scr)))

    @pl.when(b >= nused_ref[0])
    def _():
        y_ref[...] = jnp.zeros_like(y_ref)


def _gated_ffn(x_packed, wgu_ref, wd_ref):
    x_lo, x_hi = _unpack_bf16_halves(x_packed)
    half = x_packed.shape[1]
    gu = _dot(x_lo.astype(BF16), wgu_ref[0:half, :]) + _dot(x_hi.astype(BF16), wgu_ref[half:2 * half, :])
    ff = gu.shape[1] // 2
    gate = gu[:, :ff]
    act = gate * jax.nn.sigmoid(gate) * gu[:, ff:]
    return _dot(act.astype(BF16), wd_ref[...])


def _experts(xbuf, block_e, nused, w_gate_up, w_down, layer):
    nblk = xbuf.shape[0] // (E_BLOCK * ROW_TILE)
    _, _, d, ff2 = w_gate_up.shape
    last = lambda b, nu: jnp.minimum(b, nu[0] - 1)
    return pl.pallas_call(
        _expert_kernel,
        out_shape=jax.ShapeDtypeStruct(xbuf.shape, U32),
        grid_spec=pltpu.PrefetchScalarGridSpec(
            num_scalar_prefetch=2,
            grid=(nblk,),
            in_specs=[pl.BlockSpec((E_BLOCK * ROW_TILE, LANES), lambda b, be, nu: (last(b, nu), 0)),
                      pl.BlockSpec((None, None, d, ff2), lambda b, be, nu: (layer, be[b], 0, 0)),
                      pl.BlockSpec((None, None, ff2 // 2, d), lambda b, be, nu: (layer, be[b], 0, 0))],
            out_specs=pl.BlockSpec((E_BLOCK * ROW_TILE, LANES), lambda b, be, nu: (b, 0)),
            scratch_shapes=[pltpu.VMEM((d, ff2), BF16), pltpu.VMEM((ff2 // 2, d), BF16)]),
        compiler_params=_params("arbitrary"),
        name="moe_experts",
    )(block_e, nused, xbuf, w_gate_up, w_down)


def _shared_kernel(x_ref, wgu_ref, wd_ref, y_ref):
    y_ref[...] = _gated_ffn(_load_rows(x_ref), wgu_ref, wd_ref)


def _shared_expert(hf, wgu, wd):
    t = hf.shape[0] // ROW_TILE
    d = wgu.shape[0]
    tm = min(t, 512)
    return pl.pallas_call(
        _shared_kernel,
        out_shape=jax.ShapeDtypeStruct((t, d), F32),
        grid=(t // tm,),
        in_specs=[pl.BlockSpec((tm * ROW_TILE, LANES), lambda i: (i, 0)),
                  pl.BlockSpec(wgu.shape, lambda i: (0, 0)),
                  pl.BlockSpec(wd.shape, lambda i: (0, 0))],
        out_specs=pl.BlockSpec((tm, d), lambda i: (i, 0)),
        compiler_params=_params("parallel"),
        name="moe_shared",
    )(hf, wgu, wd)


def _combine_kernel(dest_ref, w_ref, x_ref, sh_ref, g_ref, fg_ref, ybuf_hbm, o_ref, rows_scr, dest_smem, idx_sem,
                    row_sem, *, final_norm):
    tm = x_ref.shape[0]
    load = pltpu.make_async_copy(dest_ref, dest_smem, idx_sem)
    load.start()
    load.wait()

    def tile(r):
        return pl.ds(pl.multiple_of(r * ROW_TILE, ROW_TILE), ROW_TILE)

    def issue(tok, carry):
        for k in range(TOP_K):
            pltpu.make_async_copy(ybuf_hbm.at[tile(dest_smem[k, tok])], rows_scr.at[k, tile(tok)], row_sem).start()
        return carry

    lax.fori_loop(0, tm, issue, 0)
    for k in range(TOP_K):
        pltpu.make_async_copy(ybuf_hbm.at[pl.ds(0, tm * ROW_TILE)], rows_scr.at[k], row_sem).wait()

    half = ROW_TILE * LANES
    weights = [jnp.broadcast_to(w_ref[:, k:k + 1], (tm, LANES)) for k in range(TOP_K)]
    for s in range(ROW_TILE):
        routed_lo = jnp.zeros((tm, LANES), F32)
        routed_hi = jnp.zeros((tm, LANES), F32)
        for k in range(TOP_K):
            lo, hi = _unpack_bf16_halves(rows_scr[k, pl.ds(s, tm, stride=ROW_TILE), :])
            routed_lo = routed_lo + weights[k] * lo
            routed_hi = routed_hi + weights[k] * hi
        cl = slice(s * LANES, (s + 1) * LANES)
        ch = slice(half + s * LANES, half + (s + 1) * LANES)
        o_ref[:, cl] = x_ref[:, cl] + g_ref[:, cl] * (routed_lo + sh_ref[:, cl])
        o_ref[:, ch] = x_ref[:, ch] + g_ref[:, ch] * (routed_hi + sh_ref[:, ch])
    if final_norm:
        y = o_ref[...]
        o_ref[...] = y * lax.rsqrt(jnp.mean(y * y, axis=-1, keepdims=True) + EPS) * fg_ref[...]


def _combine(dest_t, w_tok, x, shared, gate, final_gain, ybuf, final_norm):
    t, d = x.shape
    tm = min(t, 256)
    return pl.pallas_call(
        functools.partial(_combine_kernel, final_norm=final_norm),
        out_shape=jax.ShapeDtypeStruct((t, d), F32),
        grid=(t // tm,),
        in_specs=[pl.BlockSpec((TOP_K, tm), lambda i: (0, i)),
                  pl.BlockSpec((tm, TOP_K), lambda i: (i, 0)),
                  pl.BlockSpec((tm, d), lambda i: (i, 0)),
                  pl.BlockSpec((tm, d), lambda i: (i, 0)),
                  pl.BlockSpec((1, d), lambda i: (0, 0)),
                  pl.BlockSpec((1, d), lambda i: (0, 0)),
                  pl.BlockSpec(memory_space=pl.ANY)],
        out_specs=pl.BlockSpec((tm, d), lambda i: (i, 0)),
        scratch_shapes=[pltpu.VMEM((TOP_K, tm * ROW_TILE, LANES), U32), pltpu.SMEM((TOP_K, tm), I32),
                        pltpu.SemaphoreType.DMA, pltpu.SemaphoreType.DMA],
        compiler_params=_params("arbitrary"),
        name="moe_combine",
    )(dest_t, w_tok, x, shared, gate, final_gain, ybuf)


def _moe_layer(x, shift, scale, gate, layer, w_router, router_bias, w_gate_up, w_down, ws_gate_up, ws_down,
               final_gain, final_norm):
    t, d = x.shape
    assert d == 2 * ROW_TILE * LANES, "a packed row must be exactly one (8, 128) tile of 32-bit words"
    npad = t * TOP_K + N_EXPERTS * E_BLOCK
    nblk = npad // E_BLOCK
    hf, idx_t, wts_t, rank_t, counts = _moe_pre(x, shift, scale, w_router, router_bias)
    dest_t, block_e, nused, tail_start = _moe_dest(idx_t, rank_t, counts, nblk)
    nused = nused.reshape(-1)[:1]
    xbuf = _dispatch(hf, dest_t, tail_start.reshape(-1), nused, npad)
    ybuf = _experts(xbuf, block_e.reshape(-1), nused, w_gate_up, w_down, layer)
    shared = _shared_expert(hf, ws_gate_up.astype(BF16), ws_down.astype(BF16))
    return _combine(dest_t, wts_t.T, x, shared, gate, final_gain, ybuf, final_norm)


def _even_layer(x, mod, cos, sin, w_in, b_if, mlstm_norm, q_norm, kv_norm, w_uq, w_ukv, w_out):
    t, d = x.shape
    sh1, sc1, g1 = mod[0], mod[1], mod[2]
    nq, nv = M_HEADS * M_QK, M_HEADS * M_V
    o = 0
    cols = []
    for sz in (nq, nq, nv, nv, M_HEADS, M_HEADS, A_Q_LORA, A_KV_LORA, A_ROPE):
        cols.append(w_in[:, o:o + sz])
        o += sz
    mq, mk, mv, mo, mi, mf, cq, ckv, kr = cols
    w_a = jnp.concatenate([mq, mk, mv, mo], axis=1).astype(BF16)
    zeros = lambda n: jnp.zeros((d, n), F32)
    w_b = jnp.concatenate([cq, ckv, kr, zeros(LANES - A_ROPE), mi, mf, zeros(LANES - 2 * M_HEADS)], axis=1).astype(BF16)
    gate_col_block = (A_Q_LORA + A_KV_LORA + LANES) // LANES
    proj_a = _norm_matmul(x, sh1, sc1, w_a, jnp.zeros((1, w_a.shape[1]), F32), BF16, "even_in_a")
    proj_b = _norm_matmul(x, sh1, sc1, w_b, jnp.zeros((1, w_b.shape[1]), F32), F32, "even_in_b")
    gate_bias = jnp.zeros((1, LANES), F32).at[0, :2 * M_HEADS].set(b_if.astype(F32))
    hm = _mlstm(proj_a, proj_b, gate_col_block, gate_bias, mlstm_norm.astype(F32).reshape(1, nv))

    qk = A_NOPE + A_ROPE
    wq = jnp.pad(w_uq.reshape(A_Q_LORA, A_HEADS, qk), ((0, 0), (0, 0), (0, A_QK_PAD - qk)))
    wq = wq.reshape(A_Q_LORA, A_HEADS * A_QK_PAD).astype(BF16)
    q, k, v = _mla_up(proj_b, q_norm.astype(F32).reshape(1, -1), kv_norm.astype(F32).reshape(1, -1),
                      wq, w_ukv.astype(BF16), cos, sin)
    ha = _mla_flash(q, k, v)
    w_out = w_out.astype(BF16)
    return _proj_residual([hm, ha], [w_out[:nv], w_out[nv:]], x, g1, "even_out")


def _odd_layer(x, mod, cos, sin, w_qkv, b_qkv, sinks, w_o):
    sh1, sc1, g1 = mod[0], mod[1], mod[2]
    qkv = _norm_matmul(x, sh1, sc1, w_qkv.astype(BF16), b_qkv.astype(F32).reshape(1, -1), BF16, "odd_qkv")
    o = _swa(qkv, cos, sin, sinks)
    return _proj_residual([o], [w_o.astype(BF16)], x, g1, "odd_out")


def kernel(x, c, positions, w_ada, b_ada, a_w_in, a_b_if, a_mlstm_norm, a_q_norm, a_kv_norm, a_w_uq, a_w_ukv,
           a_w_out, s_w_qkv, s_b_qkv, s_sinks, s_w_o, e_w_router, e_router_bias, e_w_gate_up, e_w_down,
           e_ws_gate_up, e_ws_down, final_norm):
    batch, t, d = x.shape
    assert batch == 1, "kernels are written for a single sequence"
    depth = w_ada.shape[0]
    xs = x.reshape(t, d)
    mods = _ada(c, w_ada, b_ada).reshape(depth, 6, 1, d)
    cos, sin = _rope_tables(positions.reshape(t))
    final_gain = final_norm.astype(F32).reshape(1, d)
    for layer in range(depth):
        mod = mods[layer]
        if layer % 2 == 0:
            e = layer // 2
            xs = _even_layer(xs, mod, cos, sin, a_w_in[e], a_b_if[e], a_mlstm_norm[e], a_q_norm[e], a_kv_norm[e],
                             a_w_uq[e], a_w_ukv[e], a_w_out[e])
        else:
            o = layer // 2
            xs = _odd_layer(xs, mod, cos, sin, s_w_qkv[o], s_b_qkv[o], s_sinks[o], s_w_o[o])
        xs = _moe_layer(xs, mod[3], mod[4], mod[5], layer, e_w_router[layer], e_router_bias[layer],
                        e_w_gate_up, e_w_down, e_ws_gate_up[layer], e_ws_down[layer],
                        final_gain, final_norm=(layer == depth - 1))
    return xs.reshape(batch, t, d)
```

```python
import functools

import jax
import jax.numpy as jnp
from jax import lax
from jax.experimental import pallas as pl
from jax.experimental.pallas import tpu as pltpu

F32 = jnp.float32
BF16 = jnp.bfloat16
I32 = jnp.int32
U32 = jnp.uint32
HIGHEST = lax.Precision.HIGHEST

EPS = 1e-6
ROPE_THETA = 10000.0
ROPE_DIM = 64

M_HEADS = 4
M_QK = 128
M_V = 256
GATE_SOFTCAP = 15.0
MLSTM_CHUNK = 128

A_HEADS = 8
A_NOPE = 128
A_ROPE = 64
A_V = 128
A_Q_LORA = 768
A_KV_LORA = 512
A_QK_PAD = 256

S_HEADS = 32
S_KV_HEADS = 4
S_HD = 64
S_BLOCK = 128

N_EXPERTS = 64
TOP_K = 8
N_GROUPS = 8
TOPK_GROUPS = 4
E_FF = 256
SHARED_FF = 256
ROUTED_SCALE = 2.5
E_BLOCK = 512

LANES = 128
VMEM_LIMIT_BYTES = 48 * 1024 * 1024
FLASH_VMEM_LIMIT_BYTES = 56 * 1024 * 1024
LOG2_E = 1.4426950408889634


def _params(*semantics):
    return pltpu.CompilerParams(dimension_semantics=semantics, vmem_limit_bytes=VMEM_LIMIT_BYTES)


def _dot(a, b, precision=None):
    return jnp.dot(a, b, preferred_element_type=F32, precision=precision)


def _dot_nt(a, b):
    return lax.dot_general(a, b, (((1,), (1,)), ((), ())), preferred_element_type=F32)


def _dot_tn(a, b):
    return lax.dot_general(a, b, (((0,), (0,)), ((), ())), preferred_element_type=F32)


def _modulated_norm(x, shift, scale):
    y = x * lax.rsqrt(jnp.mean(x * x, axis=-1, keepdims=True) + EPS)
    return y * (1.0 + scale) + shift


def _rope_lanes(x, cos, sin_signed):
    lane = lax.broadcasted_iota(I32, x.shape, 1)
    first_half = (lane % ROPE_DIM) < (ROPE_DIM // 2)
    swapped = jnp.where(first_half, pltpu.roll(x, LANES - ROPE_DIM // 2, 1), pltpu.roll(x, ROPE_DIM // 2, 1))
    return x * cos + swapped * sin_signed


def _pack_bf16_halves(x):
    n = x.shape[1] // 2
    lo = lax.bitcast_convert_type(x[:, :n].astype(BF16).astype(F32), U32)
    hi = lax.bitcast_convert_type(x[:, n:].astype(BF16).astype(F32), U32)
    return (hi & jnp.uint32(0xFFFF0000)) | (lo >> 16)


def _unpack_bf16_halves(w):
    lo = lax.bitcast_convert_type(w << 16, F32)
    hi = lax.bitcast_convert_type(w & jnp.uint32(0xFFFF0000), F32)
    return lo, hi


ROW_TILE = 8


def _load_rows(ref):
    m = ref.shape[0] // ROW_TILE
    return jnp.concatenate([ref[pl.ds(s, m, stride=ROW_TILE), :] for s in range(ROW_TILE)], axis=1)


def _store_rows(ref, x):
    m = x.shape[0]
    for s in range(ROW_TILE):
        ref[pl.ds(s, m, stride=ROW_TILE), :] = x[:, s * LANES:(s + 1) * LANES]


def _ada_kernel(c_ref, w_ref, b_ref, o_ref):
    c = c_ref[...]
    c_act = c * jax.nn.sigmoid(c)
    o_ref[...] = jnp.sum(c_act * w_ref[...], axis=0, keepdims=True) + b_ref[...]


def _ada(c, w_ada, b_ada):
    depth, d, n = w_ada.shape
    tn = 1024
    return pl.pallas_call(
        _ada_kernel,
        out_shape=jax.ShapeDtypeStruct((depth, 1, n), F32),
        grid=(depth, n // tn),
        in_specs=[pl.BlockSpec((d, 1), lambda l, j: (0, 0)),
                  pl.BlockSpec((None, d, tn), lambda l, j: (l, 0, j)),
                  pl.BlockSpec((None, 1, tn), lambda l, j: (l, 0, j))],
        out_specs=pl.BlockSpec((None, 1, tn), lambda l, j: (l, 0, j)),
        compiler_params=_params("parallel", "parallel"),
        name="ada_mod",
    )(c.reshape(d, 1), w_ada, b_ada.reshape(depth, 1, n))


def _rope_table_kernel(pos_ref, inv_ref, sign_ref, cos_ref, sin_ref):
    ang = pos_ref[...].astype(F32) * inv_ref[...]
    cos_ref[...] = jnp.cos(ang)
    sin_ref[...] = jnp.sin(ang) * sign_ref[...]


def _rope_tables(positions):
    t = positions.shape[0]
    half = ROPE_DIM // 2
    inv_freq = jnp.power(ROPE_THETA, -jnp.arange(half, dtype=F32) / half)
    inv = jnp.tile(inv_freq, LANES // half).reshape(1, LANES)
    sign = jnp.tile(jnp.concatenate([-jnp.ones((half,), F32), jnp.ones((half,), F32)]), LANES // ROPE_DIM)
    tm = min(t, 2048)
    return pl.pallas_call(
        _rope_table_kernel,
        out_shape=(jax.ShapeDtypeStruct((t, LANES), F32), jax.ShapeDtypeStruct((t, LANES), F32)),
        grid=(t // tm,),
        in_specs=[pl.BlockSpec((tm, 1), lambda i: (i, 0)),
                  pl.BlockSpec((1, LANES), lambda i: (0, 0)),
                  pl.BlockSpec((1, LANES), lambda i: (0, 0))],
        out_specs=(pl.BlockSpec((tm, LANES), lambda i: (i, 0)), pl.BlockSpec((tm, LANES), lambda i: (i, 0))),
        compiler_params=_params("parallel"),
        name="rope_tables",
    )(positions.reshape(t, 1), inv, sign.reshape(1, LANES))


def _norm_matmul_kernel(x_ref, sh_ref, sc_ref, w_ref, b_ref, o_ref, h_scr):
    @pl.when(pl.program_id(1) == 0)
    def _():
        h_scr[...] = _modulated_norm(x_ref[...], sh_ref[...], sc_ref[...]).astype(BF16)

    o_ref[...] = (_dot(h_scr[...], w_ref[...]) + b_ref[...]).astype(o_ref.dtype)


def _norm_matmul(x, shift, scale, w, bias, out_dtype, name):
    t, d = x.shape
    n = w.shape[1]
    tm = min(t, 1024)
    tn = 512 if n % 512 == 0 else 256
    return pl.pallas_call(
        _norm_matmul_kernel,
        out_shape=jax.ShapeDtypeStruct((t, n), out_dtype),
        grid=(t // tm, n // tn),
        in_specs=[pl.BlockSpec((tm, d), lambda i, j: (i, 0)),
                  pl.BlockSpec((1, d), lambda i, j: (0, 0)),
                  pl.BlockSpec((1, d), lambda i, j: (0, 0)),
                  pl.BlockSpec((d, tn), lambda i, j: (0, j)),
                  pl.BlockSpec((1, tn), lambda i, j: (0, j))],
        out_specs=pl.BlockSpec((tm, tn), lambda i, j: (i, j)),
        scratch_shapes=[pltpu.VMEM((tm, d), BF16)],
        compiler_params=_params("parallel", "arbitrary"),
        name=name,
    )(x, shift, scale, w, bias)


def _mlstm_kernel(q_ref, k_ref, v_ref, o_ref, g_ref, gb_ref, gain_ref, out_ref, c_scr, n_scr, m_scr):
    L = MLSTM_CHUNK
    tm = q_ref.shape[0]

    @pl.when(pl.program_id(0) == 0)
    def _():
        c_scr[...] = jnp.zeros_like(c_scr)
        n_scr[...] = jnp.zeros_like(n_scr)
        m_scr[...] = jnp.zeros_like(m_scr)

    capped = GATE_SOFTCAP * jnp.tanh((g_ref[...] + gb_ref[...]) / GATE_SOFTCAP)
    log_sig = jnp.minimum(capped, 0.0) - jnp.log1p(jnp.exp(-jnp.abs(capped)))
    lane = lax.broadcasted_iota(I32, capped.shape, 1)
    gate = jnp.where(lane < M_HEADS, capped, log_sig)
    r = lax.broadcasted_iota(I32, (tm, tm), 0)
    c = lax.broadcasted_iota(I32, (tm, tm), 1)
    chunk_tril = jnp.where(((r // L) == (c // L)) & (c <= r), 1.0, 0.0).astype(F32)
    cum = _dot(chunk_tril, gate, precision=HIGHEST)
    gate_rows = gate.T
    cum_rows = cum.T
    rr = lax.broadcasted_iota(I32, (L, L), 0)
    cc = lax.broadcasted_iota(I32, (L, L), 1)
    causal = cc <= rr

    for ci in range(tm // L):
        sl = slice(ci * L, (ci + 1) * L)
        for h in range(M_HEADS):
            qk = slice(h * M_QK, (h + 1) * M_QK)
            vv = slice(h * M_V, (h + 1) * M_V)
            qc = q_ref[sl, qk]
            kc = k_ref[sl, qk].astype(F32) * (M_QK ** -0.5)
            kcb = kc.astype(BF16)
            vc = v_ref[sl, vv]
            ig_col = gate[sl, h:h + 1]
            b_col = cum[sl, M_HEADS + h:M_HEADS + h + 1]
            ig_row = gate_rows[h:h + 1, sl]
            b_row = cum_rows[M_HEADS + h:M_HEADS + h + 1, sl]
            m_prev = m_scr[h:h + 1, 0:1]

            dm = jnp.where(causal, b_col - b_row + ig_row, -jnp.inf)
            inter = b_col + m_prev
            m_row = jnp.maximum(inter, jnp.max(dm, axis=-1, keepdims=True))
            w_intra = jnp.exp(dm - m_row)
            w_inter = jnp.exp(inter - m_row)
            s = _dot_nt(qc, kcb) * w_intra
            c_state = c_scr[h]
            n_state = n_scr[h]
            num = _dot(s.astype(BF16), vc) + w_inter * _dot_nt(qc, c_state.astype(BF16))
            den = (jnp.sum(s, axis=-1, keepdims=True)
                   + w_inter * jnp.sum(qc.astype(F32) * n_state, axis=-1, keepdims=True))
            hh = num / jnp.maximum(jnp.abs(den), jnp.exp(-m_row))

            b_last = b_col[L - 1:L, :]
            g_row = b_last - b_row + ig_row
            g_col = b_last - b_col + ig_col
            m_new = jnp.maximum(b_last + m_prev, jnp.max(g_row, axis=-1, keepdims=True))
            ws_col = jnp.exp(g_col - m_new)
            decay = jnp.exp(b_last + m_prev - m_new)
            vw = (vc.astype(F32) * ws_col).astype(BF16)
            c_scr[h] = decay * c_state + _dot_tn(vw, kcb)
            n_scr[h] = decay * n_state + jnp.sum(kc * ws_col, axis=0, keepdims=True)
            m_scr[h:h + 1, :] = jnp.broadcast_to(m_new, (1, LANES))

            y = hh * lax.rsqrt(jnp.mean(hh * hh, axis=-1, keepdims=True) + EPS) * gain_ref[:, vv]
            y = y * jax.nn.sigmoid(o_ref[sl, vv].astype(F32))
            out_ref[sl, vv] = y.astype(out_ref.dtype)


def _mlstm(proj_a, proj_b, gate_col_block, gate_bias, gain):
    t = proj_a.shape[0]
    tm = min(t, 512)
    nq = M_HEADS * M_QK
    nv = M_HEADS * M_V
    return pl.pallas_call(
        _mlstm_kernel,
        out_shape=jax.ShapeDtypeStruct((t, nv), BF16),
        grid=(t // tm,),
        in_specs=[pl.BlockSpec((tm, nq), lambda i: (i, 0)),
                  pl.BlockSpec((tm, nq), lambda i: (i, 1)),
                  pl.BlockSpec((tm, nv), lambda i: (i, 1)),
                  pl.BlockSpec((tm, nv), lambda i: (i, 2)),
                  pl.BlockSpec((tm, LANES), lambda i: (i, gate_col_block)),
                  pl.BlockSpec((1, LANES), lambda i: (0, 0)),
                  pl.BlockSpec((1, nv), lambda i: (0, 0))],
        out_specs=pl.BlockSpec((tm, nv), lambda i: (i, 0)),
        scratch_shapes=[pltpu.VMEM((M_HEADS, M_V, M_QK), F32),
                        pltpu.VMEM((M_HEADS, 1, M_QK), F32),
                        pltpu.VMEM((8, LANES), F32)],
        compiler_params=_params("arbitrary"),
        name="mlstm",
    )(proj_a, proj_a, proj_a, proj_a, proj_b, gate_bias, gain)


def _mla_up_kernel(pb_ref, qn_ref, kvn_ref, wq_ref, wkv_ref, cos_ref, sin_ref,
                   q_ref, k_ref, v_ref, cq_scr, ckv_scr, kr_scr):
    @pl.when(pl.program_id(1) == 0)
    def _():
        cq = pb_ref[:, 0:A_Q_LORA]
        cq_scr[...] = (cq * lax.rsqrt(jnp.mean(cq * cq, axis=-1, keepdims=True) + EPS) * qn_ref[...]).astype(BF16)
        ckv = pb_ref[:, A_Q_LORA:A_Q_LORA + A_KV_LORA]
        ckv_scr[...] = (ckv * lax.rsqrt(jnp.mean(ckv * ckv, axis=-1, keepdims=True) + EPS)
                        * kvn_ref[...]).astype(BF16)
        kr = pb_ref[:, A_Q_LORA + A_KV_LORA:A_Q_LORA + A_KV_LORA + LANES]
        kr_scr[...] = _rope_lanes(kr, cos_ref[...], sin_ref[...]).astype(BF16)

    scale = (A_NOPE + A_ROPE) ** -0.5 * LOG2_E
    qh = _dot(cq_scr[...], wq_ref[...])
    q_pe = _rope_lanes(qh[:, A_NOPE:], cos_ref[...], sin_ref[...])
    q_ref[...] = (jnp.concatenate([qh[:, :A_NOPE], q_pe], axis=1) * scale).astype(q_ref.dtype)
    kvh = _dot(ckv_scr[...], wkv_ref[...])
    k_ref[...] = jnp.concatenate([kvh[:, :A_NOPE].astype(BF16), kr_scr[...]], axis=1)
    v_ref[...] = kvh[:, A_NOPE:].astype(v_ref.dtype)


def _mla_up(proj_b, q_norm, kv_norm, wq, wkv, cos, sin):
    t, nb = proj_b.shape
    tm = min(t, 1024)
    return pl.pallas_call(
        _mla_up_kernel,
        out_shape=(jax.ShapeDtypeStruct((A_HEADS, t, A_QK_PAD), BF16),
                   jax.ShapeDtypeStruct((A_HEADS, t, A_QK_PAD), BF16),
                   jax.ShapeDtypeStruct((A_HEADS, t, A_V), BF16)),
        grid=(t // tm, A_HEADS),
        in_specs=[pl.BlockSpec((tm, nb), lambda i, h: (i, 0)),
                  pl.BlockSpec((1, A_Q_LORA), lambda i, h: (0, 0)),
                  pl.BlockSpec((1, A_KV_LORA), lambda i, h: (0, 0)),
                  pl.BlockSpec((A_Q_LORA, A_QK_PAD), lambda i, h: (0, h)),
                  pl.BlockSpec((A_KV_LORA, A_NOPE + A_V), lambda i, h: (0, h)),
                  pl.BlockSpec((tm, LANES), lambda i, h: (i, 0)),
                  pl.BlockSpec((tm, LANES), lambda i, h: (i, 0))],
        out_specs=(pl.BlockSpec((None, tm, A_QK_PAD), lambda i, h: (h, i, 0)),
                   pl.BlockSpec((None, tm, A_QK_PAD), lambda i, h: (h, i, 0)),
                   pl.BlockSpec((None, tm, A_V), lambda i, h: (h, i, 0))),
        scratch_shapes=[pltpu.VMEM((tm, A_Q_LORA), BF16),
                        pltpu.VMEM((tm, A_KV_LORA), BF16),
                        pltpu.VMEM((tm, LANES), BF16)],
        compiler_params=_params("parallel", "arbitrary"),
        name="mla_up",
    )(proj_b, q_norm, kv_norm, wq, wkv, cos, sin)


def _mla_flash_kernel(q_ref, k_ref, v_ref, o_ref, m_scr, l_scr, acc_scr, *, tq, tkb):
    i = pl.program_id(1)
    q = q_ref[...]
    m_scr[...] = jnp.full(m_scr.shape, -jnp.inf, F32)
    l_scr[...] = jnp.zeros(l_scr.shape, F32)
    acc_scr[...] = jnp.zeros(acc_scr.shape, F32)

    def tile(start, width, masked):
        s = _dot_nt(q, k_ref[pl.ds(start, width), :])
        if masked:
            row = lax.broadcasted_iota(I32, (tq, width), 0)
            col = lax.broadcasted_iota(I32, (tq, width), 1)
            s = jnp.where(col <= row, s, -jnp.inf)
        m = m_scr[...]
        m_new = jnp.maximum(m, jnp.broadcast_to(jnp.max(s, axis=-1, keepdims=True), m.shape))
        alpha = jnp.exp2(m - m_new)
        p = jnp.exp2(s - jnp.concatenate([m_new] * (width // LANES), axis=1))
        lane_sums = p[:, 0:LANES]
        for j in range(1, width // LANES):
            lane_sums = lane_sums + p[:, j * LANES:(j + 1) * LANES]
        l_scr[...] = alpha * l_scr[...] + lane_sums
        acc_scr[...] = alpha * acc_scr[...] + _dot(p.astype(BF16), v_ref[pl.ds(start, width), :])
        m_scr[...] = m_new

    per = tkb // tq
    n_big = i // per

    def body(j, carry):
        tile(pl.multiple_of(j * tkb, tkb), tkb, False)
        return carry

    lax.fori_loop(0, n_big, body, 0)
    rem = i - n_big * per
    for r in range(1, per):
        @pl.when(rem >= r)
        def _():
            tile(pl.multiple_of(n_big * tkb + (r - 1) * tq, tq), tq, False)
    tile(pl.multiple_of(i * tq, tq), tq, True)
    o_ref[...] = (acc_scr[...] / jnp.sum(l_scr[...], axis=-1, keepdims=True)).astype(o_ref.dtype)


def _mla_flash(q, k, v):
    _, t, _ = q.shape
    tq = min(t, 1024)
    tkb = min(t, 2048)
    return pl.pallas_call(
        functools.partial(_mla_flash_kernel, tq=tq, tkb=tkb),
        out_shape=jax.ShapeDtypeStruct((t, A_HEADS * A_V), BF16),
        grid=(A_HEADS, t // tq),
        in_specs=[pl.BlockSpec((None, tq, A_QK_PAD), lambda h, i: (h, i, 0)),
                  pl.BlockSpec((None, t, A_QK_PAD), lambda h, i: (h, 0, 0)),
                  pl.BlockSpec((None, t, A_V), lambda h, i: (h, 0, 0))],
        out_specs=pl.BlockSpec((tq, A_V), lambda h, i: (i, h)),
        scratch_shapes=[pltpu.VMEM((tq, LANES), F32), pltpu.VMEM((tq, LANES), F32), pltpu.VMEM((tq, A_V), F32)],
        compiler_params=pltpu.CompilerParams(dimension_semantics=("parallel", "arbitrary"),
                                             vmem_limit_bytes=FLASH_VMEM_LIMIT_BYTES),
        name="mla_flash",
    )(q, k, v)


def _proj_residual_kernel(*refs, n_lhs):
    lhs = refs[:n_lhs]
    ws = refs[n_lhs:2 * n_lhs]
    x_ref, g_ref, o_ref = refs[2 * n_lhs:]
    acc = _dot(lhs[0][...], ws[0][...])
    for a, w in zip(lhs[1:], ws[1:]):
        acc = acc + _dot(a[...], w[...])
    o_ref[...] = x_ref[...] + g_ref[...] * acc


def _proj_residual(lhs_list, w_list, x, gate, name):
    t, d = x.shape
    tm = min(t, 1024)
    tn = 512
    n_lhs = len(lhs_list)
    in_specs = ([pl.BlockSpec((tm, a.shape[1]), lambda i, j: (i, 0)) for a in lhs_list]
                + [pl.BlockSpec((w.shape[0], tn), lambda i, j: (0, j)) for w in w_list]
                + [pl.BlockSpec((tm, tn), lambda i, j: (i, j)), pl.BlockSpec((1, tn), lambda i, j: (0, j))])
    return pl.pallas_call(
        functools.partial(_proj_residual_kernel, n_lhs=n_lhs),
        out_shape=jax.ShapeDtypeStruct((t, d), F32),
        grid=(t // tm, d // tn),
        in_specs=in_specs,
        out_specs=pl.BlockSpec((tm, tn), lambda i, j: (i, j)),
        compiler_params=_params("parallel", "parallel"),
        name=name,
    )(*lhs_list, *w_list, x, gate)


def _swa_kernel(q_ref, kc_ref, kp_ref, vc_ref, vp_ref, cosc_ref, sinc_ref, cosp_ref, sinp_ref, sink_ref, o_ref):
    i = pl.program_id(0)
    qb = S_BLOCK
    group = S_HEADS // S_KV_HEADS
    pairs = group // 2
    cos_c, sin_c = cosc_ref[...], sinc_ref[...]
    cos_w = jnp.concatenate([cosp_ref[...], cos_c], axis=0)
    sin_w = jnp.concatenate([sinp_ref[...], sin_c], axis=0)
    kw = jnp.concatenate([kp_ref[...], kc_ref[...]], axis=0).astype(F32)
    vw = jnp.concatenate([vp_ref[...], vc_ref[...]], axis=0)

    r = lax.broadcasted_iota(I32, (pairs * qb, 2 * qb), 0) % qb
    c = lax.broadcasted_iota(I32, (pairs * qb, 2 * qb), 1)
    dist = qb + r - c
    valid = (dist >= 0) & (dist < S_BLOCK) & ((c >= qb) | (i > 0))
    lane = lax.broadcasted_iota(I32, (2 * qb, LANES), 1)
    low = lane < S_HD
    low_rows = lax.broadcasted_iota(I32, (pairs * qb, LANES), 1) < S_HD
    ones_d = jnp.concatenate([jnp.where(low, 1.0, 0.0), jnp.where(low, 0.0, 1.0)], axis=0).astype(BF16)

    for g in range(S_KV_HEADS):
        col = slice((g // 2) * LANES, (g // 2 + 1) * LANES)
        k_pair = _rope_lanes(kw[:, col], cos_w, sin_w)
        v_pair = vw[:, col]
        keep = low if g % 2 == 0 else jnp.logical_not(low)
        k_own = jnp.where(keep, k_pair, 0.0)
        v_own = jnp.where(keep, v_pair.astype(F32), 0.0)
        k_other = pltpu.roll(k_own, S_HD, 1)
        v_other = pltpu.roll(v_own, S_HD, 1)
        k_lo, k_hi = (k_own, k_other) if g % 2 == 0 else (k_other, k_own)
        v_lo, v_hi = (v_own, v_other) if g % 2 == 0 else (v_other, v_own)
        kd = jnp.concatenate([k_lo, k_hi], axis=0).astype(BF16)
        vd = jnp.concatenate([v_lo, v_hi], axis=0).astype(BF16)
        qs = []
        for p in range(pairs):
            qcol = slice((g * pairs + p) * LANES, (g * pairs + p + 1) * LANES)
            qs.append(_rope_lanes(q_ref[:, qcol].astype(F32), cos_c, sin_c) * (S_HD ** -0.5))
        qg = jnp.concatenate(qs, axis=0).astype(BF16)
        s = _dot_nt(qg, kd)
        exps, sink_terms = [], []
        for half in range(2):
            sh = jnp.where(valid, s[:, half * 2 * qb:(half + 1) * 2 * qb], -jnp.inf)
            sink = sink_ref[g, half]
            m = jnp.maximum(jnp.broadcast_to(jnp.max(sh, axis=-1, keepdims=True), sink.shape), sink)
            exps.append(jnp.exp(sh - jnp.concatenate([m, m], axis=1)).astype(BF16))
            sink_terms.append(jnp.exp(sink - m))
        e_all = jnp.concatenate(exps, axis=1)
        den = _dot(e_all, ones_d) + jnp.where(low_rows, sink_terms[0], sink_terms[1])
        o = _dot(e_all, vd) / den
        for p in range(pairs):
            qcol = slice((g * pairs + p) * LANES, (g * pairs + p + 1) * LANES)
            o_ref[:, qcol] = o[p * qb:(p + 1) * qb].astype(o_ref.dtype)


def _swa(qkv, cos, sin, sinks):
    t = qkv.shape[0]
    qb = S_BLOCK
    nq = S_HEADS * S_HD
    nk = S_KV_HEADS * S_HD
    group = S_HEADS // S_KV_HEADS
    pairs = group // 2
    sink_cols = jnp.repeat(sinks.astype(F32).reshape(S_KV_HEADS, pairs, 2).transpose(0, 2, 1), qb, axis=-1)
    sink_cols = jnp.broadcast_to(sink_cols.reshape(S_KV_HEADS, 2, pairs * qb, 1), (S_KV_HEADS, 2, pairs * qb, LANES))
    k_blk = nq // nk
    prev = lambda i: jnp.maximum(i - 1, 0)
    return pl.pallas_call(
        _swa_kernel,
        out_shape=jax.ShapeDtypeStruct((t, nq), BF16),
        grid=(t // qb,),
        in_specs=[pl.BlockSpec((qb, nq), lambda i: (i, 0)),
                  pl.BlockSpec((qb, nk), lambda i: (i, k_blk)),
                  pl.BlockSpec((qb, nk), lambda i: (prev(i), k_blk)),
                  pl.BlockSpec((qb, nk), lambda i: (i, k_blk + 1)),
                  pl.BlockSpec((qb, nk), lambda i: (prev(i), k_blk + 1)),
                  pl.BlockSpec((qb, LANES), lambda i: (i, 0)),
                  pl.BlockSpec((qb, LANES), lambda i: (i, 0)),
                  pl.BlockSpec((qb, LANES), lambda i: (prev(i), 0)),
                  pl.BlockSpec((qb, LANES), lambda i: (prev(i), 0)),
                  pl.BlockSpec((S_KV_HEADS, 2, pairs * qb, LANES), lambda i: (0, 0, 0, 0))],
        out_specs=pl.BlockSpec((qb, nq), lambda i: (i, 0)),
        compiler_params=_params("parallel"),
        name="swa",
    )(qkv, qkv, qkv, qkv, qkv, cos, sin, cos, sin, sink_cols)


def _expert_of_row(p):
    per_group = N_EXPERTS // N_GROUPS
    return (p % N_GROUPS) * per_group + p // N_GROUPS


def _moe_pre_kernel(x_ref, sh_ref, sc_ref, wr_ref, rb_ref, hf_ref, idx_ref, wt_ref, rank_ref, cnt_ref, carry_scr):
    tm = x_ref.shape[0]
    per_group = N_EXPERTS // N_GROUPS

    @pl.when(pl.program_id(0) == 0)
    def _():
        carry_scr[...] = jnp.zeros_like(carry_scr)

    h = _modulated_norm(x_ref[...], sh_ref[...], sc_ref[...])
    _store_rows(hf_ref, _pack_bf16_halves(h))
    h_hi = h.astype(BF16)
    h_lo = (h - h_hi.astype(F32)).astype(BF16)
    logits = _dot(h_hi, wr_ref[0]) + (_dot(h_hi, wr_ref[1]) + _dot(h_lo, wr_ref[0]))
    scores = jax.nn.sigmoid(logits.T[:N_EXPERTS, :])
    biased = scores + rb_ref[...]

    members = [biased[j * N_GROUPS:(j + 1) * N_GROUPS, :] for j in range(per_group)]
    m1 = members[0]
    for a in members[1:]:
        m1 = jnp.maximum(m1, a)
    first = jnp.full(m1.shape, per_group, I32)
    for j in reversed(range(per_group)):
        first = jnp.where(members[j] == m1, j, first)
    m2 = jnp.full(m1.shape, -jnp.inf, F32)
    for j in range(per_group):
        m2 = jnp.maximum(m2, jnp.where(first == j, -jnp.inf, members[j]))
    group_score = m1 + m2

    g_iota = lax.broadcasted_iota(I32, group_score.shape, 0).astype(F32)
    g_sel = jnp.zeros(group_score.shape, F32)
    for _ in range(TOPK_GROUPS):
        best = jnp.max(group_score, axis=0, keepdims=True)
        gi = jnp.min(jnp.where(group_score == best, g_iota, float(N_GROUPS)), axis=0, keepdims=True)
        hit = g_iota == gi
        g_sel = jnp.where(hit, 1.0, g_sel)
        group_score = jnp.where(hit, -jnp.inf, group_score)
    masked = jnp.concatenate([jnp.where(g_sel > 0.5, a, -jnp.inf) for a in members], axis=0)

    e_iota = _expert_of_row(lax.broadcasted_iota(I32, masked.shape, 0)).astype(F32)
    sel = jnp.zeros(masked.shape, F32)
    idx_rows, w_rows = [], []
    for _ in range(TOP_K):
        best = jnp.max(masked, axis=0, keepdims=True)
        ei = jnp.min(jnp.where(masked == best, e_iota, float(N_EXPERTS)), axis=0, keepdims=True)
        hit = e_iota == ei
        idx_rows.append(ei)
        w_rows.append(jnp.sum(jnp.where(hit, scores, 0.0), axis=0, keepdims=True))
        sel = jnp.where(hit, 1.0, sel)
        masked = jnp.where(hit, -jnp.inf, masked)
    idx = jnp.concatenate(idx_rows, axis=0).astype(I32)
    wts = jnp.concatenate(w_rows, axis=0)
    wts = wts / jnp.sum(wts, axis=0, keepdims=True) * ROUTED_SCALE

    r = lax.broadcasted_iota(I32, (tm, tm), 0)
    c = lax.broadcasted_iota(I32, (tm, tm), 1)
    before = jnp.where(r < c, 1.0, 0.0).astype(BF16)
    rank_excl = carry_scr[:, 0:1] + _dot(sel.astype(BF16), before)
    rank_rows = [jnp.sum(jnp.where(e_iota == idx_rows[k], rank_excl, 0.0), axis=0, keepdims=True)
                 for k in range(TOP_K)]
    carry_scr[...] = carry_scr[...] + jnp.sum(sel, axis=1, keepdims=True)

    idx_ref[...] = idx
    wt_ref[...] = wts
    rank_ref[...] = jnp.concatenate(rank_rows, axis=0).astype(I32)
    cnt_ref[...] = carry_scr[...]


def _moe_pre(x, shift, scale, w_router, router_bias):
    t, d = x.shape
    tm = min(t, 256)
    rows = jnp.arange(N_EXPERTS)
    perm = _expert_of_row(rows)
    wr = jnp.zeros((d, LANES), F32).at[:, :N_EXPERTS].set(w_router[:, perm])
    wr_hi = wr.astype(BF16)
    wr = jnp.stack([wr_hi, (wr - wr_hi.astype(F32)).astype(BF16)])
    rb = router_bias.astype(F32)[perm].reshape(N_EXPERTS, 1)
    tok = lambda n, dt: jax.ShapeDtypeStruct((TOP_K, t), dt)
    return pl.pallas_call(
        _moe_pre_kernel,
        out_shape=(jax.ShapeDtypeStruct((t * ROW_TILE, LANES), U32), tok(t, I32), tok(t, F32), tok(t, I32),
                   jax.ShapeDtypeStruct((N_EXPERTS, LANES), F32)),
        grid=(t // tm,),
        in_specs=[pl.BlockSpec((tm, d), lambda i: (i, 0)),
                  pl.BlockSpec((1, d), lambda i: (0, 0)),
                  pl.BlockSpec((1, d), lambda i: (0, 0)),
                  pl.BlockSpec((2, d, LANES), lambda i: (0, 0, 0)),
                  pl.BlockSpec((N_EXPERTS, 1), lambda i: (0, 0))],
        out_specs=(pl.BlockSpec((tm * ROW_TILE, LANES), lambda i: (i, 0)),
                   pl.BlockSpec((TOP_K, tm), lambda i: (0, i)),
                   pl.BlockSpec((TOP_K, tm), lambda i: (0, i)),
                   pl.BlockSpec((TOP_K, tm), lambda i: (0, i)),
                   pl.BlockSpec((N_EXPERTS, LANES), lambda i: (0, 0))),
        scratch_shapes=[pltpu.VMEM((N_EXPERTS, LANES), F32)],
        compiler_params=_params("arbitrary"),
        name="moe_router",
    )(x, shift, scale, wr, rb)


def _moe_dest_kernel(idx_ref, rank_ref, cnt_ref, dest_ref, be_ref, nused_ref, tail_ref):
    cnt = cnt_ref[...]
    padded = jnp.floor((cnt + (E_BLOCK - 1)) * (1.0 / E_BLOCK)) * E_BLOCK
    e_i = _expert_of_row(lax.broadcasted_iota(I32, (N_EXPERTS, N_EXPERTS), 0))
    e_j = _expert_of_row(lax.broadcasted_iota(I32, (N_EXPERTS, N_EXPERTS), 1))
    earlier = jnp.where(e_j < e_i, 1.0, 0.0).astype(F32)
    pstart = _dot(earlier, padded, precision=HIGHEST)
    pend = pstart + padded

    tm = idx_ref.shape[1]
    e_col = _expert_of_row(lax.broadcasted_iota(I32, (N_EXPERTS, tm), 0))
    rows = []
    for k in range(TOP_K):
        hit = e_col == idx_ref[k:k + 1, :]
        rows.append(jnp.sum(jnp.where(hit, pstart[:, 0:1], 0.0), axis=0, keepdims=True))
    dest_ref[...] = jnp.concatenate(rows, axis=0).astype(I32) + rank_ref[...]

    nb = be_ref.shape[1]
    block_start = (lax.broadcasted_iota(I32, (N_EXPERTS, nb), 1) * E_BLOCK).astype(F32)
    ended = jnp.sum(jnp.where(pend[:, 0:1] <= block_start, 1.0, 0.0), axis=0, keepdims=True)
    be_ref[...] = jnp.minimum(ended, float(N_EXPERTS - 1)).astype(I32)
    nused_ref[...] = (jnp.sum(padded, axis=0, keepdims=True) * (1.0 / E_BLOCK)).astype(I32)
    e_row = _expert_of_row(lax.broadcasted_iota(I32, (N_EXPERTS, LANES), 0))
    lane = lax.broadcasted_iota(I32, (N_EXPERTS, LANES), 1)
    end_lane = jnp.sum(jnp.where(e_row <= lane, padded, 0.0), axis=0, keepdims=True)
    own_lane = jnp.sum(jnp.where(e_row == lane, padded, 0.0), axis=0, keepdims=True)
    tail_ref[...] = jnp.where(own_lane > 0.0, end_lane - E_BLOCK, -1.0).astype(I32)


def _moe_dest(idx_t, rank_t, counts, nblk):
    t = idx_t.shape[1]
    tm = min(t, 2048)
    nb = ((nblk + LANES - 1) // LANES) * LANES
    return pl.pallas_call(
        _moe_dest_kernel,
        out_shape=(jax.ShapeDtypeStruct((TOP_K, t), I32),
                   jax.ShapeDtypeStruct((1, nb), I32),
                   jax.ShapeDtypeStruct((1, LANES), I32),
                   jax.ShapeDtypeStruct((1, LANES), I32)),
        grid=(t // tm,),
        in_specs=[pl.BlockSpec((TOP_K, tm), lambda i: (0, i)),
                  pl.BlockSpec((TOP_K, tm), lambda i: (0, i)),
                  pl.BlockSpec((N_EXPERTS, LANES), lambda i: (0, 0))],
        out_specs=(pl.BlockSpec((TOP_K, tm), lambda i: (0, i)),
                   pl.BlockSpec((1, nb), lambda i: (0, 0)),
                   pl.BlockSpec((1, LANES), lambda i: (0, 0)),
                   pl.BlockSpec((1, LANES), lambda i: (0, 0))),
        compiler_params=_params("arbitrary"),
        name="moe_dest",
    )(idx_t, rank_t, counts)


def _dispatch_kernel(tail_ref, nused_ref, dest_ref, hf_ref, xbuf_hbm, dest_smem, zero_scr, idx_sem, zero_sem, row_sem):
    tm = dest_ref.shape[1]
    block_sublanes = E_BLOCK * ROW_TILE
    nblk = xbuf_hbm.shape[0] // block_sublanes

    @pl.when(pl.program_id(0) == 0)
    def _():
        zero_scr[...] = jnp.zeros_like(zero_scr)

        def zero_copy(row):
            start = pl.multiple_of(row * ROW_TILE, block_sublanes)
            return pltpu.make_async_copy(zero_scr, xbuf_hbm.at[pl.ds(start, block_sublanes)], zero_sem)

        def start(e, carry):
            @pl.when(tail_ref[e] >= 0)
            def _():
                zero_copy(tail_ref[e]).start()
            return carry

        def wait(e, carry):
            @pl.when(tail_ref[e] >= 0)
            def _():
                zero_copy(0).wait()
            return carry

        def start_unused(b, carry):
            zero_copy(b * E_BLOCK).start()
            return carry

        def wait_unused(b, carry):
            zero_copy(0).wait()
            return carry

        lax.fori_loop(0, N_EXPERTS, start, 0)
        lax.fori_loop(nused_ref[0], nblk, start_unused, 0)
        lax.fori_loop(0, N_EXPERTS, wait, 0)
        lax.fori_loop(nused_ref[0], nblk, wait_unused, 0)

    load = pltpu.make_async_copy(dest_ref, dest_smem, idx_sem)
    load.start()
    load.wait()

    def row(ref, r):
        return ref.at[pl.ds(pl.multiple_of(r * ROW_TILE, ROW_TILE), ROW_TILE)]

    def issue(tok, carry):
        for k in range(TOP_K):
            pltpu.make_async_copy(row(hf_ref, tok), row(xbuf_hbm, dest_smem[k, tok]), row_sem).start(priority=k % 2)
        return carry

    lax.fori_loop(0, tm, issue, 0)
    for k in range(TOP_K):
        pltpu.make_async_copy(hf_ref, xbuf_hbm.at[pl.ds(0, tm * ROW_TILE)], row_sem).wait()


def _dispatch(hf, dest_t, tail_start, nused, npad):
    t = hf.shape[0] // ROW_TILE
    tm = min(t, 1024)
    return pl.pallas_call(
        _dispatch_kernel,
        out_shape=jax.ShapeDtypeStruct((npad * ROW_TILE, LANES), U32),
        grid_spec=pltpu.PrefetchScalarGridSpec(
            num_scalar_prefetch=2,
            grid=(t // tm,),
            in_specs=[pl.BlockSpec((TOP_K, tm), lambda i, tail, nu: (0, i)),
                      pl.BlockSpec((tm * ROW_TILE, LANES), lambda i, tail, nu: (i, 0))],
            out_specs=pl.BlockSpec(memory_space=pl.ANY),
            scratch_shapes=[pltpu.SMEM((TOP_K, tm), I32), pltpu.VMEM((E_BLOCK * ROW_TILE, LANES), U32),
                            pltpu.SemaphoreType.DMA, pltpu.SemaphoreType.DMA, pltpu.SemaphoreType.DMA]),
        compiler_params=_params("arbitrary"),
        name="moe_dispatch",
    )(tail_start, nused, dest_t, hf)


def _expert_kernel(be_ref, nused_ref, x_ref, wgu_ref, wd_ref, y_ref, wgu_scr, wd_scr):
    b = pl.program_id(0)
    e = be_ref[b]
    e_prev = be_ref[jnp.maximum(b - 1, 0)]

    @pl.when((b == 0) | (e != e_prev))
    def _():
        wgu_scr[...] = wgu_ref[...].astype(BF16)
        wd_scr[...] = wd_ref[...].astype(BF16)

    @pl.when(b < nused_ref[0])
    def _():
        _store_rows(y_ref, _pack_bf16_halves(_gated_ffn(_load_rows(x_ref), wgu_scr, wd_scr)))

    @pl.when(b >= nused_ref[0])
    def _():
        y_ref[...] = jnp.zeros_like(y_ref)


def _gated_ffn(x_packed, wgu_ref, wd_ref):
    x_lo, x_hi = _unpack_bf16_halves(x_packed)
    half = x_packed.shape[1]
    gu = _dot(x_lo.astype(BF16), wgu_ref[0:half, :]) + _dot(x_hi.astype(BF16), wgu_ref[half:2 * half, :])
    ff = gu.shape[1] // 2
    gate = gu[:, :ff]
    act = gate * jax.nn.sigmoid(gate) * gu[:, ff:]
    return _dot(act.astype(BF16), wd_ref[...])


def _experts(xbuf, block_e, nused, w_gate_up, w_down, layer):
    nblk = xbuf.shape[0] // (E_BLOCK * ROW_TILE)
    _, _, d, ff2 = w_gate_up.shape
    last = lambda b, nu: jnp.minimum(b, nu[0] - 1)
    return pl.pallas_call(
        _expert_kernel,
        out_shape=jax.ShapeDtypeStruct(xbuf.shape, U32),
        grid_spec=pltpu.PrefetchScalarGridSpec(
            num_scalar_prefetch=2,
            grid=(nblk,),
            in_specs=[pl.BlockSpec((E_BLOCK * ROW_TILE, LANES), lambda b, be, nu: (last(b, nu), 0)),
                      pl.BlockSpec((None, None, d, ff2), lambda b, be, nu: (layer, be[b], 0, 0)),
                      pl.BlockSpec((None, None, ff2 // 2, d), lambda b, be, nu: (layer, be[b], 0, 0))],
            out_specs=pl.BlockSpec((E_BLOCK * ROW_TILE, LANES), lambda b, be, nu: (b, 0)),
            scratch_shapes=[pltpu.VMEM((d, ff2), BF16), pltpu.VMEM((ff2 // 2, d), BF16)]),
        compiler_params=_params("arbitrary"),
        name="moe_experts",
    )(block_e, nused, xbuf, w_gate_up, w_down)


def _shared_kernel(x_ref, wgu_ref, wd_ref, y_ref):
    y_ref[...] = _gated_ffn(_load_rows(x_ref), wgu_ref, wd_ref)


def _shared_expert(hf, wgu, wd):
    t = hf.shape[0] // ROW_TILE
    d = wgu.shape[0]
    tm = min(t, 512)
    return pl.pallas_call(
        _shared_kernel,
        out_shape=jax.ShapeDtypeStruct((t, d), F32),
        grid=(t // tm,),
        in_specs=[pl.BlockSpec((tm * ROW_TILE, LANES), lambda i: (i, 0)),
                  pl.BlockSpec(wgu.shape, lambda i: (0, 0)),
                  pl.BlockSpec(wd.shape, lambda i: (0, 0))],
        out_specs=pl.BlockSpec((tm, d), lambda i: (i, 0)),
        compiler_params=_params("parallel"),
        name="moe_shared",
    )(hf, wgu, wd)


def _combine_kernel(dest_ref, dest_next_ref, w_ref, x_ref, sh_ref, g_ref, fg_ref, ybuf_hbm, o_ref, rows_scr, dest_smem,
                    idx_sem, row_sems, *, final_norm):
    i = pl.program_id(0)
    tm = x_ref.shape[0]
    slot = i % 2

    def tile(r):
        return pl.ds(pl.multiple_of(r * ROW_TILE, ROW_TILE), ROW_TILE)

    def start_gather(indices_ref, dst_slot):
        load = pltpu.make_async_copy(indices_ref, dest_smem, idx_sem)
        load.start()
        load.wait()

        def issue(tok, carry):
            for k in range(TOP_K):
                pltpu.make_async_copy(ybuf_hbm.at[tile(dest_smem[k, tok])], rows_scr.at[dst_slot, k, tile(tok)],
                                      row_sems.at[dst_slot]).start(priority=k % 2)
            return carry

        lax.fori_loop(0, tm, issue, 0)

    @pl.when(i == 0)
    def _():
        start_gather(dest_ref, 0)

    @pl.when(i + 1 < pl.num_programs(0))
    def _():
        start_gather(dest_next_ref, 1 - slot)

    for k in range(TOP_K):
        pltpu.make_async_copy(ybuf_hbm.at[pl.ds(0, tm * ROW_TILE)], rows_scr.at[slot, k], row_sems.at[slot]).wait()

    half = ROW_TILE * LANES
    weights = [jnp.broadcast_to(w_ref[:, k:k + 1], (tm, LANES)) for k in range(TOP_K)]
    for s in range(ROW_TILE):
        routed_lo = jnp.zeros((tm, LANES), F32)
        routed_hi = jnp.zeros((tm, LANES), F32)
        for k in range(TOP_K):
            lo, hi = _unpack_bf16_halves(rows_scr[slot, k, pl.ds(s, tm, stride=ROW_TILE), :])
            routed_lo = routed_lo + weights[k] * lo
            routed_hi = routed_hi + weights[k] * hi
        cl = slice(s * LANES, (s + 1) * LANES)
        ch = slice(half + s * LANES, half + (s + 1) * LANES)
        o_ref[:, cl] = x_ref[:, cl] + g_ref[:, cl] * (routed_lo + sh_ref[:, cl])
        o_ref[:, ch] = x_ref[:, ch] + g_ref[:, ch] * (routed_hi + sh_ref[:, ch])
    if final_norm:
        y = o_ref[...]
        o_ref[...] = y * lax.rsqrt(jnp.mean(y * y, axis=-1, keepdims=True) + EPS) * fg_ref[...]


def _combine(dest_t, w_tok, x, shared, gate, final_gain, ybuf, final_norm):
    t, d = x.shape
    tm = min(t, 256)
    n_tiles = t // tm
    return pl.pallas_call(
        functools.partial(_combine_kernel, final_norm=final_norm),
        out_shape=jax.ShapeDtypeStruct((t, d), F32),
        grid=(n_tiles,),
        in_specs=[pl.BlockSpec((TOP_K, tm), lambda i: (0, i)),
                  pl.BlockSpec((TOP_K, tm), lambda i: (0, jnp.minimum(i + 1, n_tiles - 1))),
                  pl.BlockSpec((tm, TOP_K), lambda i: (i, 0)),
                  pl.BlockSpec((tm, d), lambda i: (i, 0)),
                  pl.BlockSpec((tm, d), lambda i: (i, 0)),
                  pl.BlockSpec((1, d), lambda i: (0, 0)),
                  pl.BlockSpec((1, d), lambda i: (0, 0)),
                  pl.BlockSpec(memory_space=pl.ANY)],
        out_specs=pl.BlockSpec((tm, d), lambda i: (i, 0)),
        scratch_shapes=[pltpu.VMEM((2, TOP_K, tm * ROW_TILE, LANES), U32), pltpu.SMEM((TOP_K, tm), I32),
                        pltpu.SemaphoreType.DMA, pltpu.SemaphoreType.DMA((2,))],
        compiler_params=_params("arbitrary"),
        name="moe_combine",
    )(dest_t, dest_t, w_tok, x, shared, gate, final_gain, ybuf)


def _moe_layer(x, shift, scale, gate, layer, w_router, router_bias, w_gate_up, w_down, ws_gate_up, ws_down,
               final_gain, final_norm):
    t, d = x.shape
    assert d == 2 * ROW_TILE * LANES, "a packed row must be exactly one (8, 128) tile of 32-bit words"
    npad = t * TOP_K + N_EXPERTS * E_BLOCK
    nblk = npad // E_BLOCK
    hf, idx_t, wts_t, rank_t, counts = _moe_pre(x, shift, scale, w_router, router_bias)
    dest_t, block_e, nused, tail_start = _moe_dest(idx_t, rank_t, counts, nblk)
    nused = nused.reshape(-1)[:1]
    xbuf = _dispatch(hf, dest_t, tail_start.reshape(-1), nused, npad)
    ybuf = _experts(xbuf, block_e.reshape(-1), nused, w_gate_up, w_down, layer)
    shared = _shared_expert(hf, ws_gate_up.astype(BF16), ws_down.astype(BF16))
    return _combine(dest_t, wts_t.T, x, shared, gate, final_gain, ybuf, final_norm)


def _even_layer(x, mod, cos, sin, w_in, b_if, mlstm_norm, q_norm, kv_norm, w_uq, w_ukv, w_out):
    t, d = x.shape
    sh1, sc1, g1 = mod[0], mod[1], mod[2]
    nq, nv = M_HEADS * M_QK, M_HEADS * M_V
    o = 0
    cols = []
    for sz in (nq, nq, nv, nv, M_HEADS, M_HEADS, A_Q_LORA, A_KV_LORA, A_ROPE):
        cols.append(w_in[:, o:o + sz])
        o += sz
    mq, mk, mv, mo, mi, mf, cq, ckv, kr = cols
    w_a = jnp.concatenate([mq, mk, mv, mo], axis=1).astype(BF16)
    zeros = lambda n: jnp.zeros((d, n), F32)
    w_b = jnp.concatenate([cq, ckv, kr, zeros(LANES - A_ROPE), mi, mf, zeros(LANES - 2 * M_HEADS)], axis=1).astype(BF16)
    gate_col_block = (A_Q_LORA + A_KV_LORA + LANES) // LANES
    proj_a = _norm_matmul(x, sh1, sc1, w_a, jnp.zeros((1, w_a.shape[1]), F32), BF16, "even_in_a")
    proj_b = _norm_matmul(x, sh1, sc1, w_b, jnp.zeros((1, w_b.shape[1]), F32), F32, "even_in_b")
    gate_bias = jnp.zeros((1, LANES), F32).at[0, :2 * M_HEADS].set(b_if.astype(F32))
    hm = _mlstm(proj_a, proj_b, gate_col_block, gate_bias, mlstm_norm.astype(F32).reshape(1, nv))

    qk = A_NOPE + A_ROPE
    wq = jnp.pad(w_uq.reshape(A_Q_LORA, A_HEADS, qk), ((0, 0), (0, 0), (0, A_QK_PAD - qk)))
    wq = wq.reshape(A_Q_LORA, A_HEADS * A_QK_PAD).astype(BF16)
    q, k, v = _mla_up(proj_b, q_norm.astype(F32).reshape(1, -1), kv_norm.astype(F32).reshape(1, -1),
                      wq, w_ukv.astype(BF16), cos, sin)
    ha = _mla_flash(q, k, v)
    w_out = w_out.astype(BF16)
    return _proj_residual([hm, ha], [w_out[:nv], w_out[nv:]], x, g1, "even_out")


def _odd_layer(x, mod, cos, sin, w_qkv, b_qkv, sinks, w_o):
    sh1, sc1, g1 = mod[0], mod[1], mod[2]
    qkv = _norm_matmul(x, sh1, sc1, w_qkv.astype(BF16), b_qkv.astype(F32).reshape(1, -1), BF16, "odd_qkv")
    o = _swa(qkv, cos, sin, sinks)
    return _proj_residual([o], [w_o.astype(BF16)], x, g1, "odd_out")


def kernel(x, c, positions, w_ada, b_ada, a_w_in, a_b_if, a_mlstm_norm, a_q_norm, a_kv_norm, a_w_uq, a_w_ukv,
           a_w_out, s_w_qkv, s_b_qkv, s_sinks, s_w_o, e_w_router, e_router_bias, e_w_gate_up, e_w_down,
           e_ws_gate_up, e_ws_down, final_norm):
    batch, t, d = x.shape
    assert batch == 1, "kernels are written for a single sequence"
    depth = w_ada.shape[0]
    xs = x.reshape(t, d)
    mods = _ada(c, w_ada, b_ada).reshape(depth, 6, 1, d)
    cos, sin = _rope_tables(positions.reshape(t))
    final_gain = final_norm.astype(F32).reshape(1, d)
    for layer in range(depth):
        mod = mods[layer]
        if layer % 2 == 0:
            e = layer // 2
            xs = _even_layer(xs, mod, cos, sin, a_w_in[e], a_b_if[e], a_mlstm_norm[e], a_q_norm[e], a_kv_norm[e],
                             a_w_uq[e], a_w_ukv[e], a_w_out[e])
        else:
            o = layer // 2
            xs = _odd_layer(xs, mod, cos, sin, s_w_qkv[o], s_b_qkv[o], s_sinks[o], s_w_o[o])
        xs = _moe_layer(xs, mod[3], mod[4], mod[5], layer, e_w_router[layer], e_router_bias[layer],
                        e_w_gate_up, e_w_down, e_ws_gate_up[layer], e_ws_down[layer],
                        final_gain, final_norm=(layer == depth - 1))
    return xs.reshape(batch, t, d)
```

```python
import functools

import jax
import jax.numpy as jnp
from jax import lax
from jax.experimental import pallas as pl
from jax.experimental.pallas import tpu as pltpu

F32 = jnp.float32
BF16 = jnp.bfloat16
I32 = jnp.int32
U32 = jnp.uint32
HIGHEST = lax.Precision.HIGHEST

EPS = 1e-6
ROPE_THETA = 10000.0
ROPE_DIM = 64

M_HEADS = 4
M_QK = 128
M_V = 256
GATE_SOFTCAP = 15.0
MLSTM_CHUNK = 128

A_HEADS = 8
A_NOPE = 128
A_ROPE = 64
A_V = 128
A_Q_LORA = 768
A_KV_LORA = 512
A_QK_PAD = 256

S_HEADS = 32
S_KV_HEADS = 4
S_HD = 64
S_BLOCK = 128

N_EXPERTS = 64
TOP_K = 8
N_GROUPS = 8
TOPK_GROUPS = 4
E_FF = 256
SHARED_FF = 256
ROUTED_SCALE = 2.5
E_BLOCK = 512

LANES = 128
VMEM_LIMIT_BYTES = 48 * 1024 * 1024
FLASH_VMEM_LIMIT_BYTES = 56 * 1024 * 1024
LOG2_E = 1.4426950408889634


def _params(*semantics):
    return pltpu.CompilerParams(dimension_semantics=semantics, vmem_limit_bytes=VMEM_LIMIT_BYTES)


def _dot(a, b, precision=None):
    return jnp.dot(a, b, preferred_element_type=F32, precision=precision)


def _dot_nt(a, b):
    return lax.dot_general(a, b, (((1,), (1,)), ((), ())), preferred_element_type=F32)


def _dot_tn(a, b):
    return lax.dot_general(a, b, (((0,), (0,)), ((), ())), preferred_element_type=F32)


def _modulated_norm(x, shift, scale):
    y = x * lax.rsqrt(jnp.mean(x * x, axis=-1, keepdims=True) + EPS)
    return y * (1.0 + scale) + shift


def _rope_lanes(x, cos, sin_signed):
    lane = lax.broadcasted_iota(I32, x.shape, 1)
    first_half = (lane % ROPE_DIM) < (ROPE_DIM // 2)
    swapped = jnp.where(first_half, pltpu.roll(x, LANES - ROPE_DIM // 2, 1), pltpu.roll(x, ROPE_DIM // 2, 1))
    return x * cos + swapped * sin_signed


def _pack_bf16_halves(x):
    n = x.shape[1] // 2
    lo = lax.bitcast_convert_type(x[:, :n].astype(BF16).astype(F32), U32)
    hi = lax.bitcast_convert_type(x[:, n:].astype(BF16).astype(F32), U32)
    return (hi & jnp.uint32(0xFFFF0000)) | (lo >> 16)


def _unpack_bf16_halves(w):
    lo = lax.bitcast_convert_type(w << 16, F32)
    hi = lax.bitcast_convert_type(w & jnp.uint32(0xFFFF0000), F32)
    return lo, hi


ROW_TILE = 8


def _load_rows(ref):
    m = ref.shape[0] // ROW_TILE
    return jnp.concatenate([ref[pl.ds(s, m, stride=ROW_TILE), :] for s in range(ROW_TILE)], axis=1)


def _store_rows(ref, x):
    m = x.shape[0]
    for s in range(ROW_TILE):
        ref[pl.ds(s, m, stride=ROW_TILE), :] = x[:, s * LANES:(s + 1) * LANES]


def _ada_kernel(c_ref, w_ref, b_ref, o_ref):
    c = c_ref[...]
    c_act = c * jax.nn.sigmoid(c)
    o_ref[...] = jnp.sum(c_act * w_ref[...], axis=0, keepdims=True) + b_ref[...]


def _ada(c, w_ada, b_ada):
    depth, d, n = w_ada.shape
    tn = 1024
    return pl.pallas_call(
        _ada_kernel,
        out_shape=jax.ShapeDtypeStruct((depth, 1, n), F32),
        grid=(depth, n // tn),
        in_specs=[pl.BlockSpec((d, 1), lambda l, j: (0, 0)),
                  pl.BlockSpec((None, d, tn), lambda l, j: (l, 0, j)),
                  pl.BlockSpec((None, 1, tn), lambda l, j: (l, 0, j))],
        out_specs=pl.BlockSpec((None, 1, tn), lambda l, j: (l, 0, j)),
        compiler_params=_params("parallel", "parallel"),
        name="ada_mod",
    )(c.reshape(d, 1), w_ada, b_ada.reshape(depth, 1, n))


def _rope_table_kernel(pos_ref, inv_ref, sign_ref, cos_ref, sin_ref):
    ang = pos_ref[...].astype(F32) * inv_ref[...]
    cos_ref[...] = jnp.cos(ang)
    sin_ref[...] = jnp.sin(ang) * sign_ref[...]


def _rope_tables(positions):
    t = positions.shape[0]
    half = ROPE_DIM // 2
    inv_freq = jnp.power(ROPE_THETA, -jnp.arange(half, dtype=F32) / half)
    inv = jnp.tile(inv_freq, LANES // half).reshape(1, LANES)
    sign = jnp.tile(jnp.concatenate([-jnp.ones((half,), F32), jnp.ones((half,), F32)]), LANES // ROPE_DIM)
    tm = min(t, 2048)
    return pl.pallas_call(
        _rope_table_kernel,
        out_shape=(jax.ShapeDtypeStruct((t, LANES), F32), jax.ShapeDtypeStruct((t, LANES), F32)),
        grid=(t // tm,),
        in_specs=[pl.BlockSpec((tm, 1), lambda i: (i, 0)),
                  pl.BlockSpec((1, LANES), lambda i: (0, 0)),
                  pl.BlockSpec((1, LANES), lambda i: (0, 0))],
        out_specs=(pl.BlockSpec((tm, LANES), lambda i: (i, 0)), pl.BlockSpec((tm, LANES), lambda i: (i, 0))),
        compiler_params=_params("parallel"),
        name="rope_tables",
    )(positions.reshape(t, 1), inv, sign.reshape(1, LANES))


def _norm_matmul_kernel(x_ref, sh_ref, sc_ref, w_ref, b_ref, o_ref, h_scr):
    @pl.when(pl.program_id(1) == 0)
    def _():
        h_scr[...] = _modulated_norm(x_ref[...], sh_ref[...], sc_ref[...]).astype(BF16)

    o_ref[...] = (_dot(h_scr[...], w_ref[...]) + b_ref[...]).astype(o_ref.dtype)


def _col_tile(n, cap):
    return max(c for c in range(LANES, min(n, cap) + 1, LANES) if n % c == 0)


def _norm_matmul(x, shift, scale, w, bias, out_dtype, name):
    t, d = x.shape
    n = w.shape[1]
    tm = min(t, 1024)
    tn = _col_tile(n, 1280)
    return pl.pallas_call(
        _norm_matmul_kernel,
        out_shape=jax.ShapeDtypeStruct((t, n), out_dtype),
        grid=(t // tm, n // tn),
        in_specs=[pl.BlockSpec((tm, d), lambda i, j: (i, 0)),
                  pl.BlockSpec((1, d), lambda i, j: (0, 0)),
                  pl.BlockSpec((1, d), lambda i, j: (0, 0)),
                  pl.BlockSpec((d, tn), lambda i, j: (0, j)),
                  pl.BlockSpec((1, tn), lambda i, j: (0, j))],
        out_specs=pl.BlockSpec((tm, tn), lambda i, j: (i, j)),
        scratch_shapes=[pltpu.VMEM((tm, d), BF16)],
        compiler_params=_params("parallel", "arbitrary"),
        name=name,
    )(x, shift, scale, w, bias)


def _mlstm_kernel(q_ref, k_ref, v_ref, o_ref, g_ref, gb_ref, gain_ref, out_ref, c_scr, n_scr, m_scr):
    L = MLSTM_CHUNK
    tm = q_ref.shape[0]

    @pl.when(pl.program_id(0) == 0)
    def _():
        c_scr[...] = jnp.zeros_like(c_scr)
        n_scr[...] = jnp.zeros_like(n_scr)
        m_scr[...] = jnp.zeros_like(m_scr)

    capped = GATE_SOFTCAP * jnp.tanh((g_ref[...] + gb_ref[...]) / GATE_SOFTCAP)
    log_sig = jnp.minimum(capped, 0.0) - jnp.log1p(jnp.exp(-jnp.abs(capped)))
    lane = lax.broadcasted_iota(I32, capped.shape, 1)
    gate = jnp.where(lane < M_HEADS, capped, log_sig)
    r = lax.broadcasted_iota(I32, (tm, tm), 0)
    c = lax.broadcasted_iota(I32, (tm, tm), 1)
    chunk_tril = jnp.where(((r // L) == (c // L)) & (c <= r), 1.0, 0.0).astype(F32)
    cum = _dot(chunk_tril, gate, precision=HIGHEST)
    gate_rows = gate.T
    cum_rows = cum.T
    rr = lax.broadcasted_iota(I32, (L, L), 0)
    cc = lax.broadcasted_iota(I32, (L, L), 1)
    causal = cc <= rr

    states = [(c_scr[h], n_scr[h], m_scr[h:h + 1, 0:1]) for h in range(M_HEADS)]
    for ci in range(tm // L):
        sl = slice(ci * L, (ci + 1) * L)
        for h in range(M_HEADS):
            qk = slice(h * M_QK, (h + 1) * M_QK)
            vv = slice(h * M_V, (h + 1) * M_V)
            c_state, n_state, m_prev = states[h]
            qc = q_ref[sl, qk]
            kc = k_ref[sl, qk].astype(F32) * (M_QK ** -0.5)
            kcb = kc.astype(BF16)
            vc = v_ref[sl, vv]
            ig_col = gate[sl, h:h + 1]
            b_col = cum[sl, M_HEADS + h:M_HEADS + h + 1]
            ig_row = gate_rows[h:h + 1, sl]
            b_row = cum_rows[M_HEADS + h:M_HEADS + h + 1, sl]

            dm = jnp.where(causal, b_col - b_row + ig_row, -jnp.inf)
            inter = b_col + m_prev
            m_row = jnp.maximum(inter, jnp.max(dm, axis=-1, keepdims=True))
            w_intra = jnp.exp(dm - m_row)
            w_inter = jnp.exp(inter - m_row)
            s = _dot_nt(qc, kcb) * w_intra
            num = _dot(s.astype(BF16), vc) + w_inter * _dot_nt(qc, c_state.astype(BF16))
            den = (jnp.sum(s, axis=-1, keepdims=True)
                   + w_inter * jnp.sum(qc.astype(F32) * n_state, axis=-1, keepdims=True))
            hh = num / jnp.maximum(jnp.abs(den), jnp.exp(-m_row))

            b_last = b_col[L - 1:L, :]
            g_row = b_last - b_row + ig_row
            g_col = b_last - b_col + ig_col
            m_new = jnp.maximum(b_last + m_prev, jnp.max(g_row, axis=-1, keepdims=True))
            ws_col = jnp.exp(g_col - m_new)
            decay = jnp.exp(b_last + m_prev - m_new)
            vw = (vc.astype(F32) * ws_col).astype(BF16)
            states[h] = (decay * c_state + _dot_tn(vw, kcb),
                         decay * n_state + jnp.sum(kc * ws_col, axis=0, keepdims=True),
                         m_new)

            y = hh * lax.rsqrt(jnp.mean(hh * hh, axis=-1, keepdims=True) + EPS) * gain_ref[:, vv]
            y = y * jax.nn.sigmoid(o_ref[sl, vv].astype(F32))
            out_ref[sl, vv] = y.astype(out_ref.dtype)
    for h in range(M_HEADS):
        c_scr[h], n_scr[h] = states[h][0], states[h][1]
        m_scr[h:h + 1, :] = jnp.broadcast_to(states[h][2], (1, LANES))


def _mlstm(proj_a, proj_b, gate_col_block, gate_bias, gain):
    t = proj_a.shape[0]
    tm = min(t, 512)
    nq = M_HEADS * M_QK
    nv = M_HEADS * M_V
    return pl.pallas_call(
        _mlstm_kernel,
        out_shape=jax.ShapeDtypeStruct((t, nv), BF16),
        grid=(t // tm,),
        in_specs=[pl.BlockSpec((tm, nq), lambda i: (i, 0)),
                  pl.BlockSpec((tm, nq), lambda i: (i, 1)),
                  pl.BlockSpec((tm, nv), lambda i: (i, 1)),
                  pl.BlockSpec((tm, nv), lambda i: (i, 2)),
                  pl.BlockSpec((tm, LANES), lambda i: (i, gate_col_block)),
                  pl.BlockSpec((1, LANES), lambda i: (0, 0)),
                  pl.BlockSpec((1, nv), lambda i: (0, 0))],
        out_specs=pl.BlockSpec((tm, nv), lambda i: (i, 0)),
        scratch_shapes=[pltpu.VMEM((M_HEADS, M_V, M_QK), F32),
                        pltpu.VMEM((M_HEADS, 1, M_QK), F32),
                        pltpu.VMEM((8, LANES), F32)],
        compiler_params=_params("arbitrary"),
        name="mlstm",
    )(proj_a, proj_a, proj_a, proj_a, proj_b, gate_bias, gain)


def _mla_up_kernel(pb_ref, qn_ref, kvn_ref, wq_ref, wkv_ref, cos_ref, sin_ref, q_ref, k_ref, v_ref):
    cos, sin = cos_ref[...], sin_ref[...]
    cq = pb_ref[:, 0:A_Q_LORA]
    cq = (cq * lax.rsqrt(jnp.mean(cq * cq, axis=-1, keepdims=True) + EPS) * qn_ref[...]).astype(BF16)
    ckv = pb_ref[:, A_Q_LORA:A_Q_LORA + A_KV_LORA]
    ckv = (ckv * lax.rsqrt(jnp.mean(ckv * ckv, axis=-1, keepdims=True) + EPS) * kvn_ref[...]).astype(BF16)
    k_pe = _rope_lanes(pb_ref[:, A_Q_LORA + A_KV_LORA:A_Q_LORA + A_KV_LORA + LANES], cos, sin).astype(BF16)
    scale = (A_NOPE + A_ROPE) ** -0.5 * LOG2_E
    for h in range(A_HEADS):
        qh = _dot(cq, wq_ref[:, h * A_QK_PAD:(h + 1) * A_QK_PAD])
        q_pe = _rope_lanes(qh[:, A_NOPE:], cos, sin)
        q_ref[h] = (jnp.concatenate([qh[:, :A_NOPE], q_pe], axis=1) * scale).astype(q_ref.dtype)
        kvh = _dot(ckv, wkv_ref[:, h * (A_NOPE + A_V):(h + 1) * (A_NOPE + A_V)])
        k_ref[h] = jnp.concatenate([kvh[:, :A_NOPE].astype(BF16), k_pe], axis=1)
        v_ref[h] = kvh[:, A_NOPE:].astype(v_ref.dtype)


def _mla_up(proj_b, q_norm, kv_norm, wq, wkv, cos, sin):
    t, nb = proj_b.shape
    tm = min(t, 512)
    return pl.pallas_call(
        _mla_up_kernel,
        out_shape=(jax.ShapeDtypeStruct((A_HEADS, t, A_QK_PAD), BF16),
                   jax.ShapeDtypeStruct((A_HEADS, t, A_QK_PAD), BF16),
                   jax.ShapeDtypeStruct((A_HEADS, t, A_V), BF16)),
        grid=(t // tm,),
        in_specs=[pl.BlockSpec((tm, nb), lambda i: (i, 0)),
                  pl.BlockSpec((1, A_Q_LORA), lambda i: (0, 0)),
                  pl.BlockSpec((1, A_KV_LORA), lambda i: (0, 0)),
                  pl.BlockSpec(wq.shape, lambda i: (0, 0)),
                  pl.BlockSpec(wkv.shape, lambda i: (0, 0)),
                  pl.BlockSpec((tm, LANES), lambda i: (i, 0)),
                  pl.BlockSpec((tm, LANES), lambda i: (i, 0))],
        out_specs=(pl.BlockSpec((A_HEADS, tm, A_QK_PAD), lambda i: (0, i, 0)),
                   pl.BlockSpec((A_HEADS, tm, A_QK_PAD), lambda i: (0, i, 0)),
                   pl.BlockSpec((A_HEADS, tm, A_V), lambda i: (0, i, 0))),
        compiler_params=_params("parallel"),
        name="mla_up",
    )(proj_b, q_norm, kv_norm, wq, wkv, cos, sin)


def _mla_flash_kernel(q_ref, k_ref, v_ref, o_ref, m_scr, l_scr, acc_scr, *, tq, tkb):
    i = pl.program_id(1)
    q = q_ref[...]
    m_scr[...] = jnp.full(m_scr.shape, -jnp.inf, F32)
    l_scr[...] = jnp.zeros(l_scr.shape, F32)
    acc_scr[...] = jnp.zeros(acc_scr.shape, F32)

    def tile(start, width, masked):
        s = _dot_nt(q, k_ref[pl.ds(start, width), :])
        if masked:
            row = lax.broadcasted_iota(I32, (tq, width), 0)
            col = lax.broadcasted_iota(I32, (tq, width), 1)
            s = jnp.where(col <= row, s, -jnp.inf)
        m = m_scr[...]
        m_new = jnp.maximum(m, jnp.broadcast_to(jnp.max(s, axis=-1, keepdims=True), m.shape))
        alpha = jnp.exp2(m - m_new)
        p = jnp.exp2(s - jnp.concatenate([m_new] * (width // LANES), axis=1))
        lane_sums = p[:, 0:LANES]
        for j in range(1, width // LANES):
            lane_sums = lane_sums + p[:, j * LANES:(j + 1) * LANES]
        l_scr[...] = alpha * l_scr[...] + lane_sums
        acc_scr[...] = alpha * acc_scr[...] + _dot(p.astype(BF16), v_ref[pl.ds(start, width), :])
        m_scr[...] = m_new

    per = tkb // tq
    n_big = i // per

    def body(j, carry):
        tile(pl.multiple_of(j * tkb, tkb), tkb, False)
        return carry

    lax.fori_loop(0, n_big, body, 0)
    rem = i - n_big * per
    for r in range(1, per):
        @pl.when(rem >= r)
        def _():
            tile(pl.multiple_of(n_big * tkb + (r - 1) * tq, tq), tq, False)
    tile(pl.multiple_of(i * tq, tq), tq, True)
    o_ref[...] = (acc_scr[...] / jnp.sum(l_scr[...], axis=-1, keepdims=True)).astype(o_ref.dtype)


def _mla_flash(q, k, v):
    _, t, _ = q.shape
    tq = min(t, 1024)
    tkb = min(t, 2048)
    return pl.pallas_call(
        functools.partial(_mla_flash_kernel, tq=tq, tkb=tkb),
        out_shape=jax.ShapeDtypeStruct((t, A_HEADS * A_V), BF16),
        grid=(A_HEADS, t // tq),
        in_specs=[pl.BlockSpec((None, tq, A_QK_PAD), lambda h, i: (h, i, 0)),
                  pl.BlockSpec((None, t, A_QK_PAD), lambda h, i: (h, 0, 0)),
                  pl.BlockSpec((None, t, A_V), lambda h, i: (h, 0, 0))],
        out_specs=pl.BlockSpec((tq, A_V), lambda h, i: (i, h)),
        scratch_shapes=[pltpu.VMEM((tq, LANES), F32), pltpu.VMEM((tq, LANES), F32), pltpu.VMEM((tq, A_V), F32)],
        compiler_params=pltpu.CompilerParams(dimension_semantics=("parallel", "arbitrary"),
                                             vmem_limit_bytes=FLASH_VMEM_LIMIT_BYTES),
        name="mla_flash",
    )(q, k, v)


def _proj_residual_kernel(*refs, n_lhs):
    lhs = refs[:n_lhs]
    ws = refs[n_lhs:2 * n_lhs]
    x_ref, g_ref, o_ref = refs[2 * n_lhs:]
    acc = _dot(lhs[0][...], ws[0][...])
    for a, w in zip(lhs[1:], ws[1:]):
        acc = acc + _dot(a[...], w[...])
    o_ref[...] = x_ref[...] + g_ref[...] * acc


def _proj_residual(lhs_list, w_list, x, gate, name):
    t, d = x.shape
    tm = min(t, 1024)
    tn = _col_tile(d, 1024)
    n_lhs = len(lhs_list)
    in_specs = ([pl.BlockSpec((tm, a.shape[1]), lambda i, j: (i, 0)) for a in lhs_list]
                + [pl.BlockSpec((w.shape[0], tn), lambda i, j: (0, j)) for w in w_list]
                + [pl.BlockSpec((tm, tn), lambda i, j: (i, j)), pl.BlockSpec((1, tn), lambda i, j: (0, j))])
    return pl.pallas_call(
        functools.partial(_proj_residual_kernel, n_lhs=n_lhs),
        out_shape=jax.ShapeDtypeStruct((t, d), F32),
        grid=(t // tm, d // tn),
        in_specs=in_specs,
        out_specs=pl.BlockSpec((tm, tn), lambda i, j: (i, j)),
        compiler_params=_params("parallel", "parallel"),
        name=name,
    )(*lhs_list, *w_list, x, gate)


def _swa_kernel(q_ref, kc_ref, kp_ref, vc_ref, vp_ref, cosc_ref, sinc_ref, cosp_ref, sinp_ref, sink_ref, o_ref):
    i = pl.program_id(0)
    qb = S_BLOCK
    group = S_HEADS // S_KV_HEADS
    pairs = group // 2
    cos_c, sin_c = cosc_ref[...], sinc_ref[...]
    cos_w = jnp.concatenate([cosp_ref[...], cos_c], axis=0)
    sin_w = jnp.concatenate([sinp_ref[...], sin_c], axis=0)
    kw = jnp.concatenate([kp_ref[...], kc_ref[...]], axis=0).astype(F32)
    vw = jnp.concatenate([vp_ref[...], vc_ref[...]], axis=0)

    r = lax.broadcasted_iota(I32, (pairs * qb, 2 * qb), 0) % qb
    c = lax.broadcasted_iota(I32, (pairs * qb, 2 * qb), 1)
    dist = qb + r - c
    valid = (dist >= 0) & (dist < S_BLOCK) & ((c >= qb) | (i > 0))
    lane = lax.broadcasted_iota(I32, (2 * qb, LANES), 1)
    low = lane < S_HD
    low_rows = lax.broadcasted_iota(I32, (pairs * qb, LANES), 1) < S_HD
    ones_d = jnp.concatenate([jnp.where(low, 1.0, 0.0), jnp.where(low, 0.0, 1.0)], axis=0).astype(BF16)

    for g in range(S_KV_HEADS):
        col = slice((g // 2) * LANES, (g // 2 + 1) * LANES)
        k_pair = _rope_lanes(kw[:, col], cos_w, sin_w)
        v_pair = vw[:, col]
        keep = low if g % 2 == 0 else jnp.logical_not(low)
        k_own = jnp.where(keep, k_pair, 0.0)
        v_own = jnp.where(keep, v_pair.astype(F32), 0.0)
        k_other = pltpu.roll(k_own, S_HD, 1)
        v_other = pltpu.roll(v_own, S_HD, 1)
        k_lo, k_hi = (k_own, k_other) if g % 2 == 0 else (k_other, k_own)
        v_lo, v_hi = (v_own, v_other) if g % 2 == 0 else (v_other, v_own)
        kd = jnp.concatenate([k_lo, k_hi], axis=0).astype(BF16)
        vd = jnp.concatenate([v_lo, v_hi], axis=0).astype(BF16)
        qs = []
        for p in range(pairs):
            qcol = slice((g * pairs + p) * LANES, (g * pairs + p + 1) * LANES)
            qs.append(_rope_lanes(q_ref[:, qcol].astype(F32), cos_c, sin_c) * (S_HD ** -0.5))
        qg = jnp.concatenate(qs, axis=0).astype(BF16)
        s = _dot_nt(qg, kd)
        exps, sink_terms = [], []
        for half in range(2):
            sh = jnp.where(valid, s[:, half * 2 * qb:(half + 1) * 2 * qb], -jnp.inf)
            sink = sink_ref[g, half]
            m = jnp.maximum(jnp.broadcast_to(jnp.max(sh, axis=-1, keepdims=True), sink.shape), sink)
            exps.append(jnp.exp(sh - jnp.concatenate([m, m], axis=1)).astype(BF16))
            sink_terms.append(jnp.exp(sink - m))
        e_all = jnp.concatenate(exps, axis=1)
        den = _dot(e_all, ones_d) + jnp.where(low_rows, sink_terms[0], sink_terms[1])
        o = _dot(e_all, vd) / den
        for p in range(pairs):
            qcol = slice((g * pairs + p) * LANES, (g * pairs + p + 1) * LANES)
            o_ref[:, qcol] = o[p * qb:(p + 1) * qb].astype(o_ref.dtype)


def _swa(qkv, cos, sin, sinks):
    t = qkv.shape[0]
    qb = S_BLOCK
    nq = S_HEADS * S_HD
    nk = S_KV_HEADS * S_HD
    group = S_HEADS // S_KV_HEADS
    pairs = group // 2
    sink_cols = jnp.repeat(sinks.astype(F32).reshape(S_KV_HEADS, pairs, 2).transpose(0, 2, 1), qb, axis=-1)
    sink_cols = jnp.broadcast_to(sink_cols.reshape(S_KV_HEADS, 2, pairs * qb, 1), (S_KV_HEADS, 2, pairs * qb, LANES))
    k_blk = nq // nk
    prev = lambda i: jnp.maximum(i - 1, 0)
    return pl.pallas_call(
        _swa_kernel,
        out_shape=jax.ShapeDtypeStruct((t, nq), BF16),
        grid=(t // qb,),
        in_specs=[pl.BlockSpec((qb, nq), lambda i: (i, 0)),
                  pl.BlockSpec((qb, nk), lambda i: (i, k_blk)),
                  pl.BlockSpec((qb, nk), lambda i: (prev(i), k_blk)),
                  pl.BlockSpec((qb, nk), lambda i: (i, k_blk + 1)),
                  pl.BlockSpec((qb, nk), lambda i: (prev(i), k_blk + 1)),
                  pl.BlockSpec((qb, LANES), lambda i: (i, 0)),
                  pl.BlockSpec((qb, LANES), lambda i: (i, 0)),
                  pl.BlockSpec((qb, LANES), lambda i: (prev(i), 0)),
                  pl.BlockSpec((qb, LANES), lambda i: (prev(i), 0)),
                  pl.BlockSpec((S_KV_HEADS, 2, pairs * qb, LANES), lambda i: (0, 0, 0, 0))],
        out_specs=pl.BlockSpec((qb, nq), lambda i: (i, 0)),
        compiler_params=_params("parallel"),
        name="swa",
    )(qkv, qkv, qkv, qkv, qkv, cos, sin, cos, sin, sink_cols)


def _expert_of_row(p):
    per_group = N_EXPERTS // N_GROUPS
    return (p % N_GROUPS) * per_group + p // N_GROUPS


def _moe_pre_kernel(x_ref, sh_ref, sc_ref, wr_ref, rb_ref, hf_ref, idx_ref, wt_ref, rank_ref, cnt_ref, carry_scr):
    tm = x_ref.shape[0]
    per_group = N_EXPERTS // N_GROUPS

    @pl.when(pl.program_id(0) == 0)
    def _():
        carry_scr[...] = jnp.zeros_like(carry_scr)

    h = _modulated_norm(x_ref[...], sh_ref[...], sc_ref[...])
    _store_rows(hf_ref, _pack_bf16_halves(h))
    h_hi = h.astype(BF16)
    h_lo = (h - h_hi.astype(F32)).astype(BF16)
    logits = _dot(h_hi, wr_ref[0]) + (_dot(h_hi, wr_ref[1]) + _dot(h_lo, wr_ref[0]))
    scores = jax.nn.sigmoid(logits.T[:N_EXPERTS, :])
    biased = scores + rb_ref[...]

    members = [biased[j * N_GROUPS:(j + 1) * N_GROUPS, :] for j in range(per_group)]
    m1 = members[0]
    for a in members[1:]:
        m1 = jnp.maximum(m1, a)
    first = jnp.full(m1.shape, per_group, I32)
    for j in reversed(range(per_group)):
        first = jnp.where(members[j] == m1, j, first)
    m2 = jnp.full(m1.shape, -jnp.inf, F32)
    for j in range(per_group):
        m2 = jnp.maximum(m2, jnp.where(first == j, -jnp.inf, members[j]))
    group_score = m1 + m2

    g_iota = lax.broadcasted_iota(I32, group_score.shape, 0).astype(F32)
    g_sel = jnp.zeros(group_score.shape, F32)
    for _ in range(TOPK_GROUPS):
        best = jnp.max(group_score, axis=0, keepdims=True)
        gi = jnp.min(jnp.where(group_score == best, g_iota, float(N_GROUPS)), axis=0, keepdims=True)
        hit = g_iota == gi
        g_sel = jnp.where(hit, 1.0, g_sel)
        group_score = jnp.where(hit, -jnp.inf, group_score)
    masked = jnp.concatenate([jnp.where(g_sel > 0.5, a, -jnp.inf) for a in members], axis=0)

    e_iota = _expert_of_row(lax.broadcasted_iota(I32, masked.shape, 0)).astype(F32)
    sel = jnp.zeros(masked.shape, F32)
    idx_rows, w_rows = [], []
    for _ in range(TOP_K):
        best = jnp.max(masked, axis=0, keepdims=True)
        ei = jnp.min(jnp.where(masked == best, e_iota, float(N_EXPERTS)), axis=0, keepdims=True)
        hit = e_iota == ei
        idx_rows.append(ei)
        w_rows.append(jnp.sum(jnp.where(hit, scores, 0.0), axis=0, keepdims=True))
        sel = jnp.where(hit, 1.0, sel)
        masked = jnp.where(hit, -jnp.inf, masked)
    idx = jnp.concatenate(idx_rows, axis=0).astype(I32)
    wts = jnp.concatenate(w_rows, axis=0)
    wts = wts / jnp.sum(wts, axis=0, keepdims=True) * ROUTED_SCALE

    r = lax.broadcasted_iota(I32, (tm, tm), 0)
    c = lax.broadcasted_iota(I32, (tm, tm), 1)
    before = jnp.where(r < c, 1.0, 0.0).astype(BF16)
    rank_excl = carry_scr[:, 0:1] + _dot(sel.astype(BF16), before)
    rank_rows = [jnp.sum(jnp.where(e_iota == idx_rows[k], rank_excl, 0.0), axis=0, keepdims=True)
                 for k in range(TOP_K)]
    carry_scr[...] = carry_scr[...] + jnp.sum(sel, axis=1, keepdims=True)

    idx_ref[...] = idx
    wt_ref[...] = wts
    rank_ref[...] = jnp.concatenate(rank_rows, axis=0).astype(I32)
    cnt_ref[...] = carry_scr[...]


def _moe_pre(x, shift, scale, w_router, router_bias):
    t, d = x.shape
    tm = min(t, 512)
    rows = jnp.arange(N_EXPERTS)
    perm = _expert_of_row(rows)
    wr = jnp.zeros((d, LANES), F32).at[:, :N_EXPERTS].set(w_router[:, perm])
    wr_hi = wr.astype(BF16)
    wr = jnp.stack([wr_hi, (wr - wr_hi.astype(F32)).astype(BF16)])
    rb = router_bias.astype(F32)[perm].reshape(N_EXPERTS, 1)
    tok = lambda n, dt: jax.ShapeDtypeStruct((TOP_K, t), dt)
    return pl.pallas_call(
        _moe_pre_kernel,
        out_shape=(jax.ShapeDtypeStruct((t * ROW_TILE, LANES), U32), tok(t, I32), tok(t, F32), tok(t, I32),
                   jax.ShapeDtypeStruct((N_EXPERTS, LANES), F32)),
        grid=(t // tm,),
        in_specs=[pl.BlockSpec((tm, d), lambda i: (i, 0)),
                  pl.BlockSpec((1, d), lambda i: (0, 0)),
                  pl.BlockSpec((1, d), lambda i: (0, 0)),
                  pl.BlockSpec((2, d, LANES), lambda i: (0, 0, 0)),
                  pl.BlockSpec((N_EXPERTS, 1), lambda i: (0, 0))],
        out_specs=(pl.BlockSpec((tm * ROW_TILE, LANES), lambda i: (i, 0)),
                   pl.BlockSpec((TOP_K, tm), lambda i: (0, i)),
                   pl.BlockSpec((TOP_K, tm), lambda i: (0, i)),
                   pl.BlockSpec((TOP_K, tm), lambda i: (0, i)),
                   pl.BlockSpec((N_EXPERTS, LANES), lambda i: (0, 0))),
        scratch_shapes=[pltpu.VMEM((N_EXPERTS, LANES), F32)],
        compiler_params=_params("arbitrary"),
        name="moe_router",
    )(x, shift, scale, wr, rb)


def _moe_dest_kernel(idx_ref, rank_ref, cnt_ref, dest_ref, be_ref, nused_ref, tail_ref):
    cnt = cnt_ref[...]
    padded = jnp.floor((cnt + (E_BLOCK - 1)) * (1.0 / E_BLOCK)) * E_BLOCK
    e_i = _expert_of_row(lax.broadcasted_iota(I32, (N_EXPERTS, N_EXPERTS), 0))
    e_j = _expert_of_row(lax.broadcasted_iota(I32, (N_EXPERTS, N_EXPERTS), 1))
    earlier = jnp.where(e_j < e_i, 1.0, 0.0).astype(F32)
    pstart = _dot(earlier, padded, precision=HIGHEST)
    pend = pstart + padded

    tm = idx_ref.shape[1]
    e_col = _expert_of_row(lax.broadcasted_iota(I32, (N_EXPERTS, tm), 0))
    rows = []
    for k in range(TOP_K):
        hit = e_col == idx_ref[k:k + 1, :]
        rows.append(jnp.sum(jnp.where(hit, pstart[:, 0:1], 0.0), axis=0, keepdims=True))
    dest_ref[...] = jnp.concatenate(rows, axis=0).astype(I32) + rank_ref[...]

    nb = be_ref.shape[1]
    block_start = (lax.broadcasted_iota(I32, (N_EXPERTS, nb), 1) * E_BLOCK).astype(F32)
    ended = jnp.sum(jnp.where(pend[:, 0:1] <= block_start, 1.0, 0.0), axis=0, keepdims=True)
    be_ref[...] = jnp.minimum(ended, float(N_EXPERTS - 1)).astype(I32)
    nused_ref[...] = (jnp.sum(padded, axis=0, keepdims=True) * (1.0 / E_BLOCK)).astype(I32)
    e_row = _expert_of_row(lax.broadcasted_iota(I32, (N_EXPERTS, LANES), 0))
    lane = lax.broadcasted_iota(I32, (N_EXPERTS, LANES), 1)
    end_lane = jnp.sum(jnp.where(e_row <= lane, padded, 0.0), axis=0, keepdims=True)
    own_lane = jnp.sum(jnp.where(e_row == lane, padded, 0.0), axis=0, keepdims=True)
    tail_ref[...] = jnp.where(own_lane > 0.0, end_lane - E_BLOCK, -1.0).astype(I32)


def _moe_dest(idx_t, rank_t, counts, nblk):
    t = idx_t.shape[1]
    tm = min(t, 2048)
    nb = ((nblk + LANES - 1) // LANES) * LANES
    return pl.pallas_call(
        _moe_dest_kernel,
        out_shape=(jax.ShapeDtypeStruct((TOP_K, t), I32),
                   jax.ShapeDtypeStruct((1, nb), I32),
                   jax.ShapeDtypeStruct((1, LANES), I32),
                   jax.ShapeDtypeStruct((1, LANES), I32)),
        grid=(t // tm,),
        in_specs=[pl.BlockSpec((TOP_K, tm), lambda i: (0, i)),
                  pl.BlockSpec((TOP_K, tm), lambda i: (0, i)),
                  pl.BlockSpec((N_EXPERTS, LANES), lambda i: (0, 0))],
        out_specs=(pl.BlockSpec((TOP_K, tm), lambda i: (0, i)),
                   pl.BlockSpec((1, nb), lambda i: (0, 0)),
                   pl.BlockSpec((1, LANES), lambda i: (0, 0)),
                   pl.BlockSpec((1, LANES), lambda i: (0, 0))),
        compiler_params=_params("arbitrary"),
        name="moe_dest",
    )(idx_t, rank_t, counts)


def _dispatch_kernel(tail_ref, nused_ref, dest_ref, hf_ref, xbuf_hbm, dest_smem, zero_scr, idx_sem, zero_sem, row_sem):
    tm = dest_ref.shape[1]
    block_sublanes = E_BLOCK * ROW_TILE
    nblk = xbuf_hbm.shape[0] // block_sublanes

    @pl.when(pl.program_id(0) == 0)
    def _():
        zero_scr[...] = jnp.zeros_like(zero_scr)

        def zero_copy(row):
            start = pl.multiple_of(row * ROW_TILE, block_sublanes)
            return pltpu.make_async_copy(zero_scr, xbuf_hbm.at[pl.ds(start, block_sublanes)], zero_sem)

        def start(e, carry):
            @pl.when(tail_ref[e] >= 0)
            def _():
                zero_copy(tail_ref[e]).start()
            return carry

        def wait(e, carry):
            @pl.when(tail_ref[e] >= 0)
            def _():
                zero_copy(0).wait()
            return carry

        def start_unused(b, carry):
            zero_copy(b * E_BLOCK).start()
            return carry

        def wait_unused(b, carry):
            zero_copy(0).wait()
            return carry

        lax.fori_loop(0, N_EXPERTS, start, 0)
        lax.fori_loop(nused_ref[0], nblk, start_unused, 0)
        lax.fori_loop(0, N_EXPERTS, wait, 0)
        lax.fori_loop(nused_ref[0], nblk, wait_unused, 0)

    load = pltpu.make_async_copy(dest_ref, dest_smem, idx_sem)
    load.start()
    load.wait()

    def row(ref, r):
        return ref.at[pl.ds(pl.multiple_of(r * ROW_TILE, ROW_TILE), ROW_TILE)]

    def issue(tok, carry):
        for k in range(TOP_K):
            pltpu.make_async_copy(row(hf_ref, tok), row(xbuf_hbm, dest_smem[k, tok]), row_sem).start(priority=k % 2)
        return carry

    lax.fori_loop(0, tm, issue, 0)
    for k in range(TOP_K):
        pltpu.make_async_copy(hf_ref, xbuf_hbm.at[pl.ds(0, tm * ROW_TILE)], row_sem).wait()


def _dispatch(hf, dest_t, tail_start, nused, npad):
    t = hf.shape[0] // ROW_TILE
    tm = min(t, 1024)
    return pl.pallas_call(
        _dispatch_kernel,
        out_shape=jax.ShapeDtypeStruct((npad * ROW_TILE, LANES), U32),
        grid_spec=pltpu.PrefetchScalarGridSpec(
            num_scalar_prefetch=2,
            grid=(t // tm,),
            in_specs=[pl.BlockSpec((TOP_K, tm), lambda i, tail, nu: (0, i)),
                      pl.BlockSpec((tm * ROW_TILE, LANES), lambda i, tail, nu: (i, 0))],
            out_specs=pl.BlockSpec(memory_space=pl.ANY),
            scratch_shapes=[pltpu.SMEM((TOP_K, tm), I32), pltpu.VMEM((E_BLOCK * ROW_TILE, LANES), U32),
                            pltpu.SemaphoreType.DMA, pltpu.SemaphoreType.DMA, pltpu.SemaphoreType.DMA]),
        compiler_params=_params("arbitrary"),
        name="moe_dispatch",
    )(tail_start, nused, dest_t, hf)


def _expert_kernel(be_ref, nused_ref, x_ref, wgu_ref, wd_ref, y_ref, wgu_scr, wd_scr):
    b = pl.program_id(0)
    e = be_ref[b]
    e_prev = be_ref[jnp.maximum(b - 1, 0)]

    @pl.when((b == 0) | (e != e_prev))
    def _():
        wgu_scr[...] = wgu_ref[...].astype(BF16)
        wd_scr[...] = wd_ref[...].astype(BF16)

    @pl.when(b < nused_ref[0])
    def _():
        _store_rows(y_ref, _pack_bf16_halves(_gated_ffn(_load_rows(x_ref), wgu_scr, wd_scr)))

    @pl.when(b >= nused_ref[0])
    def _():
        y_ref[...] = jnp.zeros_like(y_ref)


def _gated_ffn(x_packed, wgu_ref, wd_ref):
    x_lo, x_hi = _unpack_bf16_halves(x_packed)
    half = x_packed.shape[1]
    gu = _dot(x_lo.astype(BF16), wgu_ref[0:half, :]) + _dot(x_hi.astype(BF16), wgu_ref[half:2 * half, :])
    ff = gu.shape[1] // 2
    gate = gu[:, :ff]
    act = gate * jax.nn.sigmoid(gate) * gu[:, ff:]
    return _dot(act.astype(BF16), wd_ref[...])


def _experts(xbuf, block_e, nused, w_gate_up, w_down, layer):
    nblk = xbuf.shape[0] // (E_BLOCK * ROW_TILE)
    _, _, d, ff2 = w_gate_up.shape
    last = lambda b, nu: jnp.minimum(b, nu[0] - 1)
    return pl.pallas_call(
        _expert_kernel,
        out_shape=jax.ShapeDtypeStruct(xbuf.shape, U32),
        grid_spec=pltpu.PrefetchScalarGridSpec(
            num_scalar_prefetch=2,
            grid=(nblk,),
            in_specs=[pl.BlockSpec((E_BLOCK * ROW_TILE, LANES), lambda b, be, nu: (last(b, nu), 0)),
                      pl.BlockSpec((None, None, d, ff2), lambda b, be, nu: (layer, be[b], 0, 0)),
                      pl.BlockSpec((None, None, ff2 // 2, d), lambda b, be, nu: (layer, be[b], 0, 0))],
            out_specs=pl.BlockSpec((E_BLOCK * ROW_TILE, LANES), lambda b, be, nu: (b, 0)),
            scratch_shapes=[pltpu.VMEM((d, ff2), BF16), pltpu.VMEM((ff2 // 2, d), BF16)]),
        compiler_params=_params("arbitrary"),
        name="moe_experts",
    )(block_e, nused, xbuf, w_gate_up, w_down)


def _shared_kernel(x_ref, wgu_ref, wd_ref, y_ref):
    y_ref[...] = _gated_ffn(_load_rows(x_ref), wgu_ref, wd_ref)


def _shared_expert(hf, wgu, wd):
    t = hf.shape[0] // ROW_TILE
    d = wgu.shape[0]
    tm = min(t, 1024)
    return pl.pallas_call(
        _shared_kernel,
        out_shape=jax.ShapeDtypeStruct((t, d), F32),
        grid=(t // tm,),
        in_specs=[pl.BlockSpec((tm * ROW_TILE, LANES), lambda i: (i, 0)),
                  pl.BlockSpec(wgu.shape, lambda i: (0, 0)),
                  pl.BlockSpec(wd.shape, lambda i: (0, 0))],
        out_specs=pl.BlockSpec((tm, d), lambda i: (i, 0)),
        compiler_params=_params("parallel"),
        name="moe_shared",
    )(hf, wgu, wd)


def _combine_kernel(dest_ref, dest_next_ref, w_ref, x_ref, sh_ref, g_ref, fg_ref, ybuf_hbm, o_ref, rows_scr, dest_smem,
                    idx_sem, row_sems, *, final_norm):
    i = pl.program_id(0)
    tm = x_ref.shape[0]
    slot = i % 2

    def tile(r):
        return pl.ds(pl.multiple_of(r * ROW_TILE, ROW_TILE), ROW_TILE)

    def start_gather(indices_ref, dst_slot):
        load = pltpu.make_async_copy(indices_ref, dest_smem, idx_sem)
        load.start()
        load.wait()

        def issue(tok, carry):
            for k in range(TOP_K):
                pltpu.make_async_copy(ybuf_hbm.at[tile(dest_smem[k, tok])], rows_scr.at[dst_slot, k, tile(tok)],
                                      row_sems.at[dst_slot]).start(priority=k % 2)
            return carry

        lax.fori_loop(0, tm, issue, 0)

    @pl.when(i == 0)
    def _():
        start_gather(dest_ref, 0)

    @pl.when(i + 1 < pl.num_programs(0))
    def _():
        start_gather(dest_next_ref, 1 - slot)

    for k in range(TOP_K):
        pltpu.make_async_copy(ybuf_hbm.at[pl.ds(0, tm * ROW_TILE)], rows_scr.at[slot, k], row_sems.at[slot]).wait()

    half = ROW_TILE * LANES
    weights = [jnp.broadcast_to(w_ref[:, k:k + 1], (tm, LANES)) for k in range(TOP_K)]
    for s in range(ROW_TILE):
        routed_lo = jnp.zeros((tm, LANES), F32)
        routed_hi = jnp.zeros((tm, LANES), F32)
        for k in range(TOP_K):
            lo, hi = _unpack_bf16_halves(rows_scr[slot, k, pl.ds(s, tm, stride=ROW_TILE), :])
            routed_lo = routed_lo + weights[k] * lo
            routed_hi = routed_hi + weights[k] * hi
        cl = slice(s * LANES, (s + 1) * LANES)
        ch = slice(half + s * LANES, half + (s + 1) * LANES)
        o_ref[:, cl] = x_ref[:, cl] + g_ref[:, cl] * (routed_lo + sh_ref[:, cl])
        o_ref[:, ch] = x_ref[:, ch] + g_ref[:, ch] * (routed_hi + sh_ref[:, ch])
    if final_norm:
        y = o_ref[...]
        o_ref[...] = y * lax.rsqrt(jnp.mean(y * y, axis=-1, keepdims=True) + EPS) * fg_ref[...]


def _combine(dest_t, w_tok, x, shared, gate, final_gain, ybuf, final_norm):
    t, d = x.shape
    tm = min(t, 256)
    n_tiles = t // tm
    return pl.pallas_call(
        functools.partial(_combine_kernel, final_norm=final_norm),
        out_shape=jax.ShapeDtypeStruct((t, d), F32),
        grid=(n_tiles,),
        in_specs=[pl.BlockSpec((TOP_K, tm), lambda i: (0, i)),
                  pl.BlockSpec((TOP_K, tm), lambda i: (0, jnp.minimum(i + 1, n_tiles - 1))),
                  pl.BlockSpec((tm, TOP_K), lambda i: (i, 0)),
                  pl.BlockSpec((tm, d), lambda i: (i, 0)),
                  pl.BlockSpec((tm, d), lambda i: (i, 0)),
                  pl.BlockSpec((1, d), lambda i: (0, 0)),
                  pl.BlockSpec((1, d), lambda i: (0, 0)),
                  pl.BlockSpec(memory_space=pl.ANY)],
        out_specs=pl.BlockSpec((tm, d), lambda i: (i, 0)),
        scratch_shapes=[pltpu.VMEM((2, TOP_K, tm * ROW_TILE, LANES), U32), pltpu.SMEM((TOP_K, tm), I32),
                        pltpu.SemaphoreType.DMA, pltpu.SemaphoreType.DMA((2,))],
        compiler_params=_params("arbitrary"),
        name="moe_combine",
    )(dest_t, dest_t, w_tok, x, shared, gate, final_gain, ybuf)


def _moe_layer(x, shift, scale, gate, layer, w_router, router_bias, w_gate_up, w_down, ws_gate_up, ws_down,
               final_gain, final_norm):
    t, d = x.shape
    assert d == 2 * ROW_TILE * LANES, "a packed row must be exactly one (8, 128) tile of 32-bit words"
    npad = t * TOP_K + N_EXPERTS * E_BLOCK
    nblk = npad // E_BLOCK
    hf, idx_t, wts_t, rank_t, counts = _moe_pre(x, shift, scale, w_router, router_bias)
    dest_t, block_e, nused, tail_start = _moe_dest(idx_t, rank_t, counts, nblk)
    nused = nused.reshape(-1)[:1]
    xbuf = _dispatch(hf, dest_t, tail_start.reshape(-1), nused, npad)
    ybuf = _experts(xbuf, block_e.reshape(-1), nused, w_gate_up, w_down, layer)
    shared = _shared_expert(hf, ws_gate_up.astype(BF16), ws_down.astype(BF16))
    return _combine(dest_t, wts_t.T, x, shared, gate, final_gain, ybuf, final_norm)


def _even_layer(x, mod, cos, sin, w_in, b_if, mlstm_norm, q_norm, kv_norm, w_uq, w_ukv, w_out):
    t, d = x.shape
    sh1, sc1, g1 = mod[0], mod[1], mod[2]
    nq, nv = M_HEADS * M_QK, M_HEADS * M_V
    o = 0
    cols = []
    for sz in (nq, nq, nv, nv, M_HEADS, M_HEADS, A_Q_LORA, A_KV_LORA, A_ROPE):
        cols.append(w_in[:, o:o + sz])
        o += sz
    mq, mk, mv, mo, mi, mf, cq, ckv, kr = cols
    w_a = jnp.concatenate([mq, mk, mv, mo], axis=1).astype(BF16)
    zeros = lambda n: jnp.zeros((d, n), F32)
    w_b = jnp.concatenate([cq, ckv, kr, zeros(LANES - A_ROPE), mi, mf, zeros(LANES - 2 * M_HEADS)], axis=1).astype(BF16)
    gate_col_block = (A_Q_LORA + A_KV_LORA + LANES) // LANES
    proj_a = _norm_matmul(x, sh1, sc1, w_a, jnp.zeros((1, w_a.shape[1]), F32), BF16, "even_in_a")
    proj_b = _norm_matmul(x, sh1, sc1, w_b, jnp.zeros((1, w_b.shape[1]), F32), F32, "even_in_b")
    gate_bias = jnp.zeros((1, LANES), F32).at[0, :2 * M_HEADS].set(b_if.astype(F32))
    hm = _mlstm(proj_a, proj_b, gate_col_block, gate_bias, mlstm_norm.astype(F32).reshape(1, nv))

    qk = A_NOPE + A_ROPE
    wq = jnp.pad(w_uq.reshape(A_Q_LORA, A_HEADS, qk), ((0, 0), (0, 0), (0, A_QK_PAD - qk)))
    wq = wq.reshape(A_Q_LORA, A_HEADS * A_QK_PAD).astype(BF16)
    q, k, v = _mla_up(proj_b, q_norm.astype(F32).reshape(1, -1), kv_norm.astype(F32).reshape(1, -1),
                      wq, w_ukv.astype(BF16), cos, sin)
    ha = _mla_flash(q, k, v)
    w_out = w_out.astype(BF16)
    return _proj_residual([hm, ha], [w_out[:nv], w_out[nv:]], x, g1, "even_out")


def _odd_layer(x, mod, cos, sin, w_qkv, b_qkv, sinks, w_o):
    sh1, sc1, g1 = mod[0], mod[1], mod[2]
    qkv = _norm_matmul(x, sh1, sc1, w_qkv.astype(BF16), b_qkv.astype(F32).reshape(1, -1), BF16, "odd_qkv")
    o = _swa(qkv, cos, sin, sinks)
    return _proj_residual([o], [w_o.astype(BF16)], x, g1, "odd_out")


def kernel(x, c, positions, w_ada, b_ada, a_w_in, a_b_if, a_mlstm_norm, a_q_norm, a_kv_norm, a_w_uq, a_w_ukv,
           a_w_out, s_w_qkv, s_b_qkv, s_sinks, s_w_o, e_w_router, e_router_bias, e_w_gate_up, e_w_down,
           e_ws_gate_up, e_ws_down, final_norm):
    batch, t, d = x.shape
    assert batch == 1, "kernels are written for a single sequence"
    depth = w_ada.shape[0]
    xs = x.reshape(t, d)
    mods = _ada(c, w_ada, b_ada).reshape(depth, 6, 1, d)
    cos, sin = _rope_tables(positions.reshape(t))
    final_gain = final_norm.astype(F32).reshape(1, d)
    for layer in range(depth):
        mod = mods[layer]
        if layer % 2 == 0:
            e = layer // 2
            xs = _even_layer(xs, mod, cos, sin, a_w_in[e], a_b_if[e], a_mlstm_norm[e], a_q_norm[e], a_kv_norm[e],
                             a_w_uq[e], a_w_ukv[e], a_w_out[e])
        else:
            o = layer // 2
            xs = _odd_layer(xs, mod, cos, sin, s_w_qkv[o], s_b_qkv[o], s_sinks[o], s_w_o[o])
        xs = _moe_layer(xs, mod[3], mod[4], mod[5], layer, e_w_router[layer], e_router_bias[layer],
                        e_w_gate_up, e_w_down, e_ws_gate_up[layer], e_ws_down[layer],
                        final_gain, final_norm=(layer == depth - 1))
    return xs.reshape(batch, t, d)
```

```python
import functools

import jax
import jax.numpy as jnp
from jax import lax
from jax.experimental import pallas as pl
from jax.experimental.pallas import tpu as pltpu

F32 = jnp.float32
BF16 = jnp.bfloat16
I32 = jnp.int32
U32 = jnp.uint32
HIGHEST = lax.Precision.HIGHEST

EPS = 1e-6
ROPE_THETA = 10000.0
ROPE_DIM = 64

M_HEADS = 4
M_QK = 128
M_V = 256
GATE_SOFTCAP = 15.0
MLSTM_CHUNK = 128

A_HEADS = 8
A_NOPE = 128
A_ROPE = 64
A_V = 128
A_Q_LORA = 768
A_KV_LORA = 512
A_QK_PAD = 256

S_HEADS = 32
S_KV_HEADS = 4
S_HD = 64
S_BLOCK = 128

N_EXPERTS = 64
TOP_K = 8
N_GROUPS = 8
TOPK_GROUPS = 4
E_FF = 256
SHARED_FF = 256
ROUTED_SCALE = 2.5
E_BLOCK = 512

LANES = 128
VMEM_LIMIT_BYTES = 48 * 1024 * 1024
FLASH_VMEM_LIMIT_BYTES = 56 * 1024 * 1024
LOG2_E = 1.4426950408889634


def _params(*semantics):
    return pltpu.CompilerParams(dimension_semantics=semantics, vmem_limit_bytes=VMEM_LIMIT_BYTES)


def _dot(a, b, precision=None):
    return jnp.dot(a, b, preferred_element_type=F32, precision=precision)


def _dot_nt(a, b):
    return lax.dot_general(a, b, (((1,), (1,)), ((), ())), preferred_element_type=F32)


def _dot_tn(a, b):
    return lax.dot_general(a, b, (((0,), (0,)), ((), ())), preferred_element_type=F32)


def _modulated_norm(x, shift, scale):
    y = x * lax.rsqrt(jnp.mean(x * x, axis=-1, keepdims=True) + EPS)
    return y * (1.0 + scale) + shift


def _rope_lanes(x, cos, sin_signed):
    lane = lax.broadcasted_iota(I32, x.shape, 1)
    first_half = (lane % ROPE_DIM) < (ROPE_DIM // 2)
    swapped = jnp.where(first_half, pltpu.roll(x, LANES - ROPE_DIM // 2, 1), pltpu.roll(x, ROPE_DIM // 2, 1))
    return x * cos + swapped * sin_signed


def _pack_bf16_halves(x):
    n = x.shape[1] // 2
    lo = lax.bitcast_convert_type(x[:, :n].astype(BF16).astype(F32), U32)
    hi = lax.bitcast_convert_type(x[:, n:].astype(BF16).astype(F32), U32)
    return (hi & jnp.uint32(0xFFFF0000)) | (lo >> 16)


def _unpack_bf16_halves(w):
    lo = lax.bitcast_convert_type(w << 16, F32)
    hi = lax.bitcast_convert_type(w & jnp.uint32(0xFFFF0000), F32)
    return lo, hi


ROW_TILE = 8


def _load_rows(ref):
    m = ref.shape[0] // ROW_TILE
    return jnp.concatenate([ref[pl.ds(s, m, stride=ROW_TILE), :] for s in range(ROW_TILE)], axis=1)


def _store_rows(ref, x):
    m = x.shape[0]
    for s in range(ROW_TILE):
        ref[pl.ds(s, m, stride=ROW_TILE), :] = x[:, s * LANES:(s + 1) * LANES]


def _ada_kernel(c_ref, w_ref, b_ref, o_ref):
    c = c_ref[...]
    c_act = c * jax.nn.sigmoid(c)
    o_ref[...] = jnp.sum(c_act * w_ref[...], axis=0, keepdims=True) + b_ref[...]


def _ada(c, w_ada, b_ada):
    depth, d, n = w_ada.shape
    tn = 1024
    return pl.pallas_call(
        _ada_kernel,
        out_shape=jax.ShapeDtypeStruct((depth, 1, n), F32),
        grid=(depth, n // tn),
        in_specs=[pl.BlockSpec((d, 1), lambda l, j: (0, 0)),
                  pl.BlockSpec((None, d, tn), lambda l, j: (l, 0, j)),
                  pl.BlockSpec((None, 1, tn), lambda l, j: (l, 0, j))],
        out_specs=pl.BlockSpec((None, 1, tn), lambda l, j: (l, 0, j)),
        compiler_params=_params("parallel", "parallel"),
        name="ada_mod",
    )(c.reshape(d, 1), w_ada, b_ada.reshape(depth, 1, n))


def _rope_table_kernel(pos_ref, inv_ref, sign_ref, cos_ref, sin_ref):
    ang = pos_ref[...].astype(F32) * inv_ref[...]
    cos_ref[...] = jnp.cos(ang)
    sin_ref[...] = jnp.sin(ang) * sign_ref[...]


def _rope_tables(positions):
    t = positions.shape[0]
    half = ROPE_DIM // 2
    inv_freq = jnp.power(ROPE_THETA, -jnp.arange(half, dtype=F32) / half)
    inv = jnp.tile(inv_freq, LANES // half).reshape(1, LANES)
    sign = jnp.tile(jnp.concatenate([-jnp.ones((half,), F32), jnp.ones((half,), F32)]), LANES // ROPE_DIM)
    tm = min(t, 2048)
    return pl.pallas_call(
        _rope_table_kernel,
        out_shape=(jax.ShapeDtypeStruct((t, LANES), F32), jax.ShapeDtypeStruct((t, LANES), F32)),
        grid=(t // tm,),
        in_specs=[pl.BlockSpec((tm, 1), lambda i: (i, 0)),
                  pl.BlockSpec((1, LANES), lambda i: (0, 0)),
                  pl.BlockSpec((1, LANES), lambda i: (0, 0))],
        out_specs=(pl.BlockSpec((tm, LANES), lambda i: (i, 0)), pl.BlockSpec((tm, LANES), lambda i: (i, 0))),
        compiler_params=_params("parallel"),
        name="rope_tables",
    )(positions.reshape(t, 1), inv, sign.reshape(1, LANES))


def _norm_matmul_kernel(x_ref, sh_ref, sc_ref, w_ref, b_ref, o_ref, h_scr):
    @pl.when(pl.program_id(1) == 0)
    def _():
        h_scr[...] = _modulated_norm(x_ref[...], sh_ref[...], sc_ref[...]).astype(BF16)

    o_ref[...] = (_dot(h_scr[...], w_ref[...]) + b_ref[...]).astype(o_ref.dtype)


def _col_tile(n, cap):
    return max(c for c in range(LANES, min(n, cap) + 1, LANES) if n % c == 0)


def _norm_matmul(x, shift, scale, w, bias, out_dtype, name):
    t, d = x.shape
    n = w.shape[1]
    tm = min(t, 1024)
    tn = _col_tile(n, 1280)
    return pl.pallas_call(
        _norm_matmul_kernel,
        out_shape=jax.ShapeDtypeStruct((t, n), out_dtype),
        grid=(t // tm, n // tn),
        in_specs=[pl.BlockSpec((tm, d), lambda i, j: (i, 0)),
                  pl.BlockSpec((1, d), lambda i, j: (0, 0)),
                  pl.BlockSpec((1, d), lambda i, j: (0, 0)),
                  pl.BlockSpec((d, tn), lambda i, j: (0, j)),
                  pl.BlockSpec((1, tn), lambda i, j: (0, j))],
        out_specs=pl.BlockSpec((tm, tn), lambda i, j: (i, j)),
        scratch_shapes=[pltpu.VMEM((tm, d), BF16)],
        compiler_params=_params("parallel", "arbitrary"),
        name=name,
    )(x, shift, scale, w, bias)


def _mlstm_kernel(q_ref, k_ref, v_ref, o_ref, g_ref, gb_ref, gain_ref, out_ref, c_scr, n_scr, m_scr):
    L = MLSTM_CHUNK
    tm = q_ref.shape[0]

    @pl.when(pl.program_id(0) == 0)
    def _():
        c_scr[...] = jnp.zeros_like(c_scr)
        n_scr[...] = jnp.zeros_like(n_scr)
        m_scr[...] = jnp.zeros_like(m_scr)

    capped = GATE_SOFTCAP * jnp.tanh((g_ref[...] + gb_ref[...]) / GATE_SOFTCAP)
    log_sig = jnp.minimum(capped, 0.0) - jnp.log1p(jnp.exp(-jnp.abs(capped)))
    lane = lax.broadcasted_iota(I32, capped.shape, 1)
    gate = jnp.where(lane < M_HEADS, capped, log_sig)
    r = lax.broadcasted_iota(I32, (tm, tm), 0)
    c = lax.broadcasted_iota(I32, (tm, tm), 1)
    chunk_tril = jnp.where(((r // L) == (c // L)) & (c <= r), 1.0, 0.0).astype(F32)
    cum = _dot(chunk_tril, gate, precision=HIGHEST)
    gate_rows = gate.T
    cum_rows = cum.T
    rr = lax.broadcasted_iota(I32, (L, L), 0)
    cc = lax.broadcasted_iota(I32, (L, L), 1)
    causal = cc <= rr

    states = [(c_scr[h], n_scr[h], m_scr[h:h + 1, 0:1]) for h in range(M_HEADS)]
    for ci in range(tm // L):
        sl = slice(ci * L, (ci + 1) * L)
        for h in range(M_HEADS):
            qk = slice(h * M_QK, (h + 1) * M_QK)
            vv = slice(h * M_V, (h + 1) * M_V)
            c_state, n_state, m_prev = states[h]
            qc = q_ref[sl, qk]
            kc = k_ref[sl, qk].astype(F32) * (M_QK ** -0.5)
            kcb = kc.astype(BF16)
            vc = v_ref[sl, vv]
            ig_col = gate[sl, h:h + 1]
            b_col = cum[sl, M_HEADS + h:M_HEADS + h + 1]
            ig_row = gate_rows[h:h + 1, sl]
            b_row = cum_rows[M_HEADS + h:M_HEADS + h + 1, sl]

            dm = jnp.where(causal, b_col - b_row + ig_row, -jnp.inf)
            inter = b_col + m_prev
            m_row = jnp.maximum(inter, jnp.max(dm, axis=-1, keepdims=True))
            w_intra = jnp.exp(dm - m_row)
            w_inter = jnp.exp(inter - m_row)
            s = _dot_nt(qc, kcb) * w_intra
            num = _dot(s.astype(BF16), vc) + w_inter * _dot_nt(qc, c_state.astype(BF16))
            den = (jnp.sum(s, axis=-1, keepdims=True)
                   + w_inter * jnp.sum(qc.astype(F32) * n_state, axis=-1, keepdims=True))
            hh = num / jnp.maximum(jnp.abs(den), jnp.exp(-m_row))

            b_last = b_col[L - 1:L, :]
            g_row = b_last - b_row + ig_row
            g_col = b_last - b_col + ig_col
            m_new = jnp.maximum(b_last + m_prev, jnp.max(g_row, axis=-1, keepdims=True))
            ws_col = jnp.exp(g_col - m_new)
            decay = jnp.exp(b_last + m_prev - m_new)
            vw = (vc.astype(F32) * ws_col).astype(BF16)
            states[h] = (decay * c_state + _dot_tn(vw, kcb),
                         decay * n_state + jnp.sum(kc * ws_col, axis=0, keepdims=True),
                         m_new)

            y = hh * lax.rsqrt(jnp.mean(hh * hh, axis=-1, keepdims=True) + EPS) * gain_ref[:, vv]
            y = y * jax.nn.sigmoid(o_ref[sl, vv].astype(F32))
            out_ref[sl, vv] = y.astype(out_ref.dtype)
    for h in range(M_HEADS):
        c_scr[h], n_scr[h] = states[h][0], states[h][1]
        m_scr[h:h + 1, :] = jnp.broadcast_to(states[h][2], (1, LANES))


def _mlstm(proj_a, proj_b, gate_col_block, gate_bias, gain):
    t = proj_a.shape[0]
    tm = min(t, 512)
    nq = M_HEADS * M_QK
    nv = M_HEADS * M_V
    return pl.pallas_call(
        _mlstm_kernel,
        out_shape=jax.ShapeDtypeStruct((t, nv), BF16),
        grid=(t // tm,),
        in_specs=[pl.BlockSpec((tm, nq), lambda i: (i, 0)),
                  pl.BlockSpec((tm, nq), lambda i: (i, 1)),
                  pl.BlockSpec((tm, nv), lambda i: (i, 1)),
                  pl.BlockSpec((tm, nv), lambda i: (i, 2)),
                  pl.BlockSpec((tm, LANES), lambda i: (i, gate_col_block)),
                  pl.BlockSpec((1, LANES), lambda i: (0, 0)),
                  pl.BlockSpec((1, nv), lambda i: (0, 0))],
        out_specs=pl.BlockSpec((tm, nv), lambda i: (i, 0)),
        scratch_shapes=[pltpu.VMEM((M_HEADS, M_V, M_QK), F32),
                        pltpu.VMEM((M_HEADS, 1, M_QK), F32),
                        pltpu.VMEM((8, LANES), F32)],
        compiler_params=_params("arbitrary"),
        name="mlstm",
    )(proj_a, proj_a, proj_a, proj_a, proj_b, gate_bias, gain)


def _mla_up_kernel(pb_ref, qn_ref, kvn_ref, wq_ref, wkv_ref, cos_ref, sin_ref, q_ref, k_ref, v_ref):
    cos, sin = cos_ref[...], sin_ref[...]
    cq = pb_ref[:, 0:A_Q_LORA]
    cq = (cq * lax.rsqrt(jnp.mean(cq * cq, axis=-1, keepdims=True) + EPS) * qn_ref[...]).astype(BF16)
    ckv = pb_ref[:, A_Q_LORA:A_Q_LORA + A_KV_LORA]
    ckv = (ckv * lax.rsqrt(jnp.mean(ckv * ckv, axis=-1, keepdims=True) + EPS) * kvn_ref[...]).astype(BF16)
    k_pe = _rope_lanes(pb_ref[:, A_Q_LORA + A_KV_LORA:A_Q_LORA + A_KV_LORA + LANES], cos, sin).astype(BF16)
    scale = (A_NOPE + A_ROPE) ** -0.5 * LOG2_E
    for h in range(A_HEADS):
        qh = _dot(cq, wq_ref[:, h * A_QK_PAD:(h + 1) * A_QK_PAD])
        q_pe = _rope_lanes(qh[:, A_NOPE:], cos, sin)
        q_ref[h] = (jnp.concatenate([qh[:, :A_NOPE], q_pe], axis=1) * scale).astype(q_ref.dtype)
        kvh = _dot(ckv, wkv_ref[:, h * (A_NOPE + A_V):(h + 1) * (A_NOPE + A_V)])
        k_ref[h] = jnp.concatenate([kvh[:, :A_NOPE].astype(BF16), k_pe], axis=1)
        v_ref[h] = kvh[:, A_NOPE:].astype(v_ref.dtype)


def _mla_up(proj_b, q_norm, kv_norm, wq, wkv, cos, sin):
    t, nb = proj_b.shape
    tm = min(t, 512)
    return pl.pallas_call(
        _mla_up_kernel,
        out_shape=(jax.ShapeDtypeStruct((A_HEADS, t, A_QK_PAD), BF16),
                   jax.ShapeDtypeStruct((A_HEADS, t, A_QK_PAD), BF16),
                   jax.ShapeDtypeStruct((A_HEADS, t, A_V), BF16)),
        grid=(t // tm,),
        in_specs=[pl.BlockSpec((tm, nb), lambda i: (i, 0)),
                  pl.BlockSpec((1, A_Q_LORA), lambda i: (0, 0)),
                  pl.BlockSpec((1, A_KV_LORA), lambda i: (0, 0)),
                  pl.BlockSpec(wq.shape, lambda i: (0, 0)),
                  pl.BlockSpec(wkv.shape, lambda i: (0, 0)),
                  pl.BlockSpec((tm, LANES), lambda i: (i, 0)),
                  pl.BlockSpec((tm, LANES), lambda i: (i, 0))],
        out_specs=(pl.BlockSpec((A_HEADS, tm, A_QK_PAD), lambda i: (0, i, 0)),
                   pl.BlockSpec((A_HEADS, tm, A_QK_PAD), lambda i: (0, i, 0)),
                   pl.BlockSpec((A_HEADS, tm, A_V), lambda i: (0, i, 0))),
        compiler_params=_params("parallel"),
        name="mla_up",
    )(proj_b, q_norm, kv_norm, wq, wkv, cos, sin)


def _mla_flash_kernel(q_ref, k_ref, v_ref, o_ref, m_scr, l_scr, acc_scr, *, tq, tkb):
    i = pl.program_id(1)
    q = q_ref[...]
    m_scr[...] = jnp.full(m_scr.shape, -jnp.inf, F32)
    l_scr[...] = jnp.zeros(l_scr.shape, F32)
    acc_scr[...] = jnp.zeros(acc_scr.shape, F32)

    def tile(start, width, masked):
        s = _dot_nt(q, k_ref[pl.ds(start, width), :])
        if masked:
            row = lax.broadcasted_iota(I32, (tq, width), 0)
            col = lax.broadcasted_iota(I32, (tq, width), 1)
            s = jnp.where(col <= row, s, -jnp.inf)
        m = m_scr[...]
        m_new = jnp.maximum(m, jnp.broadcast_to(jnp.max(s, axis=-1, keepdims=True), m.shape))
        alpha = jnp.exp2(m - m_new)
        p = jnp.exp2(s - jnp.concatenate([m_new] * (width // LANES), axis=1))
        lane_sums = p[:, 0:LANES]
        for j in range(1, width // LANES):
            lane_sums = lane_sums + p[:, j * LANES:(j + 1) * LANES]
        l_scr[...] = alpha * l_scr[...] + lane_sums
        acc_scr[...] = alpha * acc_scr[...] + _dot(p.astype(BF16), v_ref[pl.ds(start, width), :])
        m_scr[...] = m_new

    per = tkb // tq
    n_big = i // per

    def body(j, carry):
        tile(pl.multiple_of(j * tkb, tkb), tkb, False)
        return carry

    lax.fori_loop(0, n_big, body, 0)
    rem = i - n_big * per
    for r in range(1, per):
        @pl.when(rem >= r)
        def _():
            tile(pl.multiple_of(n_big * tkb + (r - 1) * tq, tq), tq, False)
    tile(pl.multiple_of(i * tq, tq), tq, True)
    o_ref[...] = (acc_scr[...] / jnp.sum(l_scr[...], axis=-1, keepdims=True)).astype(o_ref.dtype)


def _mla_flash(q, k, v):
    _, t, _ = q.shape
    tq = min(t, 1024)
    tkb = min(t, 2048)
    return pl.pallas_call(
        functools.partial(_mla_flash_kernel, tq=tq, tkb=tkb),
        out_shape=jax.ShapeDtypeStruct((t, A_HEADS * A_V), BF16),
        grid=(A_HEADS, t // tq),
        in_specs=[pl.BlockSpec((None, tq, A_QK_PAD), lambda h, i: (h, i, 0)),
                  pl.BlockSpec((None, t, A_QK_PAD), lambda h, i: (h, 0, 0)),
                  pl.BlockSpec((None, t, A_V), lambda h, i: (h, 0, 0))],
        out_specs=pl.BlockSpec((tq, A_V), lambda h, i: (i, h)),
        scratch_shapes=[pltpu.VMEM((tq, LANES), F32), pltpu.VMEM((tq, LANES), F32), pltpu.VMEM((tq, A_V), F32)],
        compiler_params=pltpu.CompilerParams(dimension_semantics=("parallel", "arbitrary"),
                                             vmem_limit_bytes=FLASH_VMEM_LIMIT_BYTES),
        name="mla_flash",
    )(q, k, v)


def _proj_residual_kernel(*refs, n_lhs):
    lhs = refs[:n_lhs]
    ws = refs[n_lhs:2 * n_lhs]
    x_ref, g_ref, o_ref = refs[2 * n_lhs:]
    acc = _dot(lhs[0][...], ws[0][...])
    for a, w in zip(lhs[1:], ws[1:]):
        acc = acc + _dot(a[...], w[...])
    o_ref[...] = x_ref[...] + g_ref[...] * acc


def _proj_residual(lhs_list, w_list, x, gate, name):
    t, d = x.shape
    tm = min(t, 1024)
    tn = _col_tile(d, 1024)
    n_lhs = len(lhs_list)
    in_specs = ([pl.BlockSpec((tm, a.shape[1]), lambda i, j: (i, 0)) for a in lhs_list]
                + [pl.BlockSpec((w.shape[0], tn), lambda i, j: (0, j)) for w in w_list]
                + [pl.BlockSpec((tm, tn), lambda i, j: (i, j)), pl.BlockSpec((1, tn), lambda i, j: (0, j))])
    return pl.pallas_call(
        functools.partial(_proj_residual_kernel, n_lhs=n_lhs),
        out_shape=jax.ShapeDtypeStruct((t, d), F32),
        grid=(t // tm, d // tn),
        in_specs=in_specs,
        out_specs=pl.BlockSpec((tm, tn), lambda i, j: (i, j)),
        compiler_params=_params("parallel", "parallel"),
        name=name,
    )(*lhs_list, *w_list, x, gate)


def _swa_kernel(q_ref, kc_ref, kp_ref, vc_ref, vp_ref, cosc_ref, sinc_ref, cosp_ref, sinp_ref, sink_ref, o_ref):
    i = pl.program_id(0)
    qb = S_BLOCK
    group = S_HEADS // S_KV_HEADS
    pairs = group // 2
    cos_c, sin_c = cosc_ref[...], sinc_ref[...]
    cos_w = jnp.concatenate([cosp_ref[...], cos_c], axis=0)
    sin_w = jnp.concatenate([sinp_ref[...], sin_c], axis=0)
    kw = jnp.concatenate([kp_ref[...], kc_ref[...]], axis=0).astype(F32)
    vw = jnp.concatenate([vp_ref[...], vc_ref[...]], axis=0)

    r = lax.broadcasted_iota(I32, (pairs * qb, 2 * qb), 0) % qb
    c = lax.broadcasted_iota(I32, (pairs * qb, 2 * qb), 1)
    dist = qb + r - c
    valid = (dist >= 0) & (dist < S_BLOCK) & ((c >= qb) | (i > 0))
    lane = lax.broadcasted_iota(I32, (2 * qb, LANES), 1)
    low = lane < S_HD
    low_rows = lax.broadcasted_iota(I32, (pairs * qb, LANES), 1) < S_HD
    ones_d = jnp.concatenate([jnp.where(low, 1.0, 0.0), jnp.where(low, 0.0, 1.0)], axis=0).astype(BF16)

    for g in range(S_KV_HEADS):
        col = slice((g // 2) * LANES, (g // 2 + 1) * LANES)
        k_pair = _rope_lanes(kw[:, col], cos_w, sin_w)
        v_pair = vw[:, col]
        keep = low if g % 2 == 0 else jnp.logical_not(low)
        k_own = jnp.where(keep, k_pair, 0.0)
        v_own = jnp.where(keep, v_pair.astype(F32), 0.0)
        k_other = pltpu.roll(k_own, S_HD, 1)
        v_other = pltpu.roll(v_own, S_HD, 1)
        k_lo, k_hi = (k_own, k_other) if g % 2 == 0 else (k_other, k_own)
        v_lo, v_hi = (v_own, v_other) if g % 2 == 0 else (v_other, v_own)
        kd = jnp.concatenate([k_lo, k_hi], axis=0).astype(BF16)
        vd = jnp.concatenate([v_lo, v_hi], axis=0).astype(BF16)
        qs = []
        for p in range(pairs):
            qcol = slice((g * pairs + p) * LANES, (g * pairs + p + 1) * LANES)
            qs.append(_rope_lanes(q_ref[:, qcol].astype(F32), cos_c, sin_c) * (S_HD ** -0.5))
        qg = jnp.concatenate(qs, axis=0).astype(BF16)
        s = _dot_nt(qg, kd)
        exps, sink_terms = [], []
        for half in range(2):
            sh = jnp.where(valid, s[:, half * 2 * qb:(half + 1) * 2 * qb], -jnp.inf)
            sink = sink_ref[g, half]
            m = jnp.maximum(jnp.broadcast_to(jnp.max(sh, axis=-1, keepdims=True), sink.shape), sink)
            exps.append(jnp.exp(sh - jnp.concatenate([m, m], axis=1)).astype(BF16))
            sink_terms.append(jnp.exp(sink - m))
        e_all = jnp.concatenate(exps, axis=1)
        den = _dot(e_all, ones_d) + jnp.where(low_rows, sink_terms[0], sink_terms[1])
        o = _dot(e_all, vd) / den
        for p in range(pairs):
            qcol = slice((g * pairs + p) * LANES, (g * pairs + p + 1) * LANES)
            o_ref[:, qcol] = o[p * qb:(p + 1) * qb].astype(o_ref.dtype)


def _swa(qkv, cos, sin, sinks):
    t = qkv.shape[0]
    qb = S_BLOCK
    nq = S_HEADS * S_HD
    nk = S_KV_HEADS * S_HD
    group = S_HEADS // S_KV_HEADS
    pairs = group // 2
    sink_cols = jnp.repeat(sinks.astype(F32).reshape(S_KV_HEADS, pairs, 2).transpose(0, 2, 1), qb, axis=-1)
    sink_cols = jnp.broadcast_to(sink_cols.reshape(S_KV_HEADS, 2, pairs * qb, 1), (S_KV_HEADS, 2, pairs * qb, LANES))
    k_blk = nq // nk
    prev = lambda i: jnp.maximum(i - 1, 0)
    return pl.pallas_call(
        _swa_kernel,
        out_shape=jax.ShapeDtypeStruct((t, nq), BF16),
        grid=(t // qb,),
        in_specs=[pl.BlockSpec((qb, nq), lambda i: (i, 0)),
                  pl.BlockSpec((qb, nk), lambda i: (i, k_blk)),
                  pl.BlockSpec((qb, nk), lambda i: (prev(i), k_blk)),
                  pl.BlockSpec((qb, nk), lambda i: (i, k_blk + 1)),
                  pl.BlockSpec((qb, nk), lambda i: (prev(i), k_blk + 1)),
                  pl.BlockSpec((qb, LANES), lambda i: (i, 0)),
                  pl.BlockSpec((qb, LANES), lambda i: (i, 0)),
                  pl.BlockSpec((qb, LANES), lambda i: (prev(i), 0)),
                  pl.BlockSpec((qb, LANES), lambda i: (prev(i), 0)),
                  pl.BlockSpec((S_KV_HEADS, 2, pairs * qb, LANES), lambda i: (0, 0, 0, 0))],
        out_specs=pl.BlockSpec((qb, nq), lambda i: (i, 0)),
        compiler_params=_params("parallel"),
        name="swa",
    )(qkv, qkv, qkv, qkv, qkv, cos, sin, cos, sin, sink_cols)


def _expert_of_row(p):
    per_group = N_EXPERTS // N_GROUPS
    return (p % N_GROUPS) * per_group + p // N_GROUPS


def _moe_pre_kernel(x_ref, sh_ref, sc_ref, wr_ref, rb_ref, hf_ref, idx_ref, wt_ref, rank_ref, cnt_ref, carry_scr):
    tm = x_ref.shape[0]
    per_group = N_EXPERTS // N_GROUPS

    @pl.when(pl.program_id(0) == 0)
    def _():
        carry_scr[...] = jnp.zeros_like(carry_scr)

    h = _modulated_norm(x_ref[...], sh_ref[...], sc_ref[...])
    _store_rows(hf_ref, _pack_bf16_halves(h))
    h_hi = h.astype(BF16)
    h_lo = (h - h_hi.astype(F32)).astype(BF16)
    logits = _dot(h_hi, wr_ref[0]) + (_dot(h_hi, wr_ref[1]) + _dot(h_lo, wr_ref[0]))
    scores = jax.nn.sigmoid(logits.T[:N_EXPERTS, :])
    biased = scores + rb_ref[...]

    members = [biased[j * N_GROUPS:(j + 1) * N_GROUPS, :] for j in range(per_group)]
    m1 = members[0]
    for a in members[1:]:
        m1 = jnp.maximum(m1, a)
    first = jnp.full(m1.shape, per_group, I32)
    for j in reversed(range(per_group)):
        first = jnp.where(members[j] == m1, j, first)
    m2 = jnp.full(m1.shape, -jnp.inf, F32)
    for j in range(per_group):
        m2 = jnp.maximum(m2, jnp.where(first == j, -jnp.inf, members[j]))
    group_score = m1 + m2

    g_iota = lax.broadcasted_iota(I32, group_score.shape, 0).astype(F32)
    g_sel = jnp.zeros(group_score.shape, F32)
    for _ in range(TOPK_GROUPS):
        best = jnp.max(group_score, axis=0, keepdims=True)
        gi = jnp.min(jnp.where(group_score == best, g_iota, float(N_GROUPS)), axis=0, keepdims=True)
        hit = g_iota == gi
        g_sel = jnp.where(hit, 1.0, g_sel)
        group_score = jnp.where(hit, -jnp.inf, group_score)
    masked = jnp.concatenate([jnp.where(g_sel > 0.5, a, -jnp.inf) for a in members], axis=0)

    e_iota = _expert_of_row(lax.broadcasted_iota(I32, masked.shape, 0)).astype(F32)
    sel = jnp.zeros(masked.shape, F32)
    idx_rows, w_rows = [], []
    for _ in range(TOP_K):
        best = jnp.max(masked, axis=0, keepdims=True)
        ei = jnp.min(jnp.where(masked == best, e_iota, float(N_EXPERTS)), axis=0, keepdims=True)
        hit = e_iota == ei
        idx_rows.append(ei)
        w_rows.append(jnp.sum(jnp.where(hit, scores, 0.0), axis=0, keepdims=True))
        sel = jnp.where(hit, 1.0, sel)
        masked = jnp.where(hit, -jnp.inf, masked)
    idx = jnp.concatenate(idx_rows, axis=0).astype(I32)
    wts = jnp.concatenate(w_rows, axis=0)
    wts = wts / jnp.sum(wts, axis=0, keepdims=True) * ROUTED_SCALE

    r = lax.broadcasted_iota(I32, (tm, tm), 0)
    c = lax.broadcasted_iota(I32, (tm, tm), 1)
    before = jnp.where(r < c, 1.0, 0.0).astype(BF16)
    rank_excl = carry_scr[:, 0:1] + _dot(sel.astype(BF16), before)
    rank_rows = [jnp.sum(jnp.where(e_iota == idx_rows[k], rank_excl, 0.0), axis=0, keepdims=True)
                 for k in range(TOP_K)]
    carry_scr[...] = carry_scr[...] + jnp.sum(sel, axis=1, keepdims=True)

    idx_ref[...] = idx
    wt_ref[...] = wts
    rank_ref[...] = jnp.concatenate(rank_rows, axis=0).astype(I32)
    cnt_ref[...] = carry_scr[...]


def _moe_pre(x, shift, scale, w_router, router_bias):
    t, d = x.shape
    tm = min(t, 512)
    rows = jnp.arange(N_EXPERTS)
    perm = _expert_of_row(rows)
    wr = jnp.zeros((d, LANES), F32).at[:, :N_EXPERTS].set(w_router[:, perm])
    wr_hi = wr.astype(BF16)
    wr = jnp.stack([wr_hi, (wr - wr_hi.astype(F32)).astype(BF16)])
    rb = router_bias.astype(F32)[perm].reshape(N_EXPERTS, 1)
    tok = lambda n, dt: jax.ShapeDtypeStruct((TOP_K, t), dt)
    return pl.pallas_call(
        _moe_pre_kernel,
        out_shape=(jax.ShapeDtypeStruct((t * ROW_TILE, LANES), U32), tok(t, I32), tok(t, F32), tok(t, I32),
                   jax.ShapeDtypeStruct((N_EXPERTS, LANES), F32)),
        grid=(t // tm,),
        in_specs=[pl.BlockSpec((tm, d), lambda i: (i, 0)),
                  pl.BlockSpec((1, d), lambda i: (0, 0)),
                  pl.BlockSpec((1, d), lambda i: (0, 0)),
                  pl.BlockSpec((2, d, LANES), lambda i: (0, 0, 0)),
                  pl.BlockSpec((N_EXPERTS, 1), lambda i: (0, 0))],
        out_specs=(pl.BlockSpec((tm * ROW_TILE, LANES), lambda i: (i, 0)),
                   pl.BlockSpec((TOP_K, tm), lambda i: (0, i)),
                   pl.BlockSpec((TOP_K, tm), lambda i: (0, i)),
                   pl.BlockSpec((TOP_K, tm), lambda i: (0, i)),
                   pl.BlockSpec((N_EXPERTS, LANES), lambda i: (0, 0))),
        scratch_shapes=[pltpu.VMEM((N_EXPERTS, LANES), F32)],
        compiler_params=_params("arbitrary"),
        name="moe_router",
    )(x, shift, scale, wr, rb)


def _moe_dest_kernel(idx_ref, rank_ref, cnt_ref, dest_ref, be_ref, nused_ref, tail_ref):
    cnt = cnt_ref[...]
    padded = jnp.floor((cnt + (E_BLOCK - 1)) * (1.0 / E_BLOCK)) * E_BLOCK
    e_i = _expert_of_row(lax.broadcasted_iota(I32, (N_EXPERTS, N_EXPERTS), 0))
    e_j = _expert_of_row(lax.broadcasted_iota(I32, (N_EXPERTS, N_EXPERTS), 1))
    earlier = jnp.where(e_j < e_i, 1.0, 0.0).astype(F32)
    pstart = _dot(earlier, padded, precision=HIGHEST)
    pend = pstart + padded

    tm = idx_ref.shape[1]
    e_col = _expert_of_row(lax.broadcasted_iota(I32, (N_EXPERTS, tm), 0))
    rows = []
    for k in range(TOP_K):
        hit = e_col == idx_ref[k:k + 1, :]
        rows.append(jnp.sum(jnp.where(hit, pstart[:, 0:1], 0.0), axis=0, keepdims=True))
    dest_ref[...] = jnp.concatenate(rows, axis=0).astype(I32) + rank_ref[...]

    nb = be_ref.shape[1]
    block_start = (lax.broadcasted_iota(I32, (N_EXPERTS, nb), 1) * E_BLOCK).astype(F32)
    ended = jnp.sum(jnp.where(pend[:, 0:1] <= block_start, 1.0, 0.0), axis=0, keepdims=True)
    be_ref[...] = jnp.minimum(ended, float(N_EXPERTS - 1)).astype(I32)
    nused_ref[...] = (jnp.sum(padded, axis=0, keepdims=True) * (1.0 / E_BLOCK)).astype(I32)
    e_row = _expert_of_row(lax.broadcasted_iota(I32, (N_EXPERTS, LANES), 0))
    lane = lax.broadcasted_iota(I32, (N_EXPERTS, LANES), 1)
    end_lane = jnp.sum(jnp.where(e_row <= lane, padded, 0.0), axis=0, keepdims=True)
    own_lane = jnp.sum(jnp.where(e_row == lane, padded, 0.0), axis=0, keepdims=True)
    tail_ref[...] = jnp.where(own_lane > 0.0, end_lane - E_BLOCK, -1.0).astype(I32)


def _moe_dest(idx_t, rank_t, counts, nblk):
    t = idx_t.shape[1]
    tm = min(t, 2048)
    nb = ((nblk + LANES - 1) // LANES) * LANES
    return pl.pallas_call(
        _moe_dest_kernel,
        out_shape=(jax.ShapeDtypeStruct((TOP_K, t), I32),
                   jax.ShapeDtypeStruct((1, nb), I32),
                   jax.ShapeDtypeStruct((1, LANES), I32),
                   jax.ShapeDtypeStruct((1, LANES), I32)),
        grid=(t // tm,),
        in_specs=[pl.BlockSpec((TOP_K, tm), lambda i: (0, i)),
                  pl.BlockSpec((TOP_K, tm), lambda i: (0, i)),
                  pl.BlockSpec((N_EXPERTS, LANES), lambda i: (0, 0))],
        out_specs=(pl.BlockSpec((TOP_K, tm), lambda i: (0, i)),
                   pl.BlockSpec((1, nb), lambda i: (0, 0)),
                   pl.BlockSpec((1, LANES), lambda i: (0, 0)),
                   pl.BlockSpec((1, LANES), lambda i: (0, 0))),
        compiler_params=_params("arbitrary"),
        name="moe_dest",
    )(idx_t, rank_t, counts)


def _dispatch_kernel(tail_ref, nused_ref, dest_ref, hf_ref, xbuf_hbm, dest_smem, zero_scr, idx_sem, zero_sem, row_sem):
    tm = dest_ref.shape[1]
    block_sublanes = E_BLOCK * ROW_TILE
    nblk = xbuf_hbm.shape[0] // block_sublanes

    @pl.when(pl.program_id(0) == 0)
    def _():
        zero_scr[...] = jnp.zeros_like(zero_scr)

        def zero_copy(row):
            start = pl.multiple_of(row * ROW_TILE, block_sublanes)
            return pltpu.make_async_copy(zero_scr, xbuf_hbm.at[pl.ds(start, block_sublanes)], zero_sem)

        def start(e, carry):
            @pl.when(tail_ref[e] >= 0)
            def _():
                zero_copy(tail_ref[e]).start()
            return carry

        def wait(e, carry):
            @pl.when(tail_ref[e] >= 0)
            def _():
                zero_copy(0).wait()
            return carry

        def start_unused(b, carry):
            zero_copy(b * E_BLOCK).start()
            return carry

        def wait_unused(b, carry):
            zero_copy(0).wait()
            return carry

        lax.fori_loop(0, N_EXPERTS, start, 0)
        lax.fori_loop(nused_ref[0], nblk, start_unused, 0)
        lax.fori_loop(0, N_EXPERTS, wait, 0)
        lax.fori_loop(nused_ref[0], nblk, wait_unused, 0)

    load = pltpu.make_async_copy(dest_ref, dest_smem, idx_sem)
    load.start()
    load.wait()

    def row(ref, r):
        return ref.at[pl.ds(pl.multiple_of(r * ROW_TILE, ROW_TILE), ROW_TILE)]

    def issue(tok, carry):
        for k in range(TOP_K):
            pltpu.make_async_copy(row(hf_ref, tok), row(xbuf_hbm, dest_smem[k, tok]), row_sem).start(priority=k % 2)
        return carry

    lax.fori_loop(0, tm, issue, 0)
    for k in range(TOP_K):
        pltpu.make_async_copy(hf_ref, xbuf_hbm.at[pl.ds(0, tm * ROW_TILE)], row_sem).wait()


def _dispatch(hf, dest_t, tail_start, nused, npad):
    t = hf.shape[0] // ROW_TILE
    tm = min(t, 1024)
    return pl.pallas_call(
        _dispatch_kernel,
        out_shape=jax.ShapeDtypeStruct((npad * ROW_TILE, LANES), U32),
        grid_spec=pltpu.PrefetchScalarGridSpec(
            num_scalar_prefetch=2,
            grid=(t // tm,),
            in_specs=[pl.BlockSpec((TOP_K, tm), lambda i, tail, nu: (0, i)),
                      pl.BlockSpec((tm * ROW_TILE, LANES), lambda i, tail, nu: (i, 0))],
            out_specs=pl.BlockSpec(memory_space=pl.ANY),
            scratch_shapes=[pltpu.SMEM((TOP_K, tm), I32), pltpu.VMEM((E_BLOCK * ROW_TILE, LANES), U32),
                            pltpu.SemaphoreType.DMA, pltpu.SemaphoreType.DMA, pltpu.SemaphoreType.DMA]),
        compiler_params=_params("arbitrary"),
        name="moe_dispatch",
    )(tail_start, nused, dest_t, hf)


def _expert_kernel(be_ref, nused_ref, x_ref, wgu_ref, wd_ref, y_ref, wgu_scr, wd_scr):
    b = pl.program_id(0)
    e = be_ref[b]
    e_prev = be_ref[jnp.maximum(b - 1, 0)]

    @pl.when((b == 0) | (e != e_prev))
    def _():
        wgu_scr[...] = wgu_ref[...].astype(BF16)
        wd_scr[...] = wd_ref[...].astype(BF16)

    @pl.when(b < nused_ref[0])
    def _():
        _store_rows(y_ref, _pack_bf16_halves(_gated_ffn(_load_rows(x_ref), wgu_scr, wd_scr)))

    @pl.when(b >= nused_ref[0])
    def _():
        y_ref[...] = jnp.zeros_like(y_ref)


def _gated_ffn(x_packed, wgu_ref, wd_ref):
    x_lo, x_hi = _unpack_bf16_halves(x_packed)
    half = x_packed.shape[1]
    gu = _dot(x_lo.astype(BF16), wgu_ref[0:half, :]) + _dot(x_hi.astype(BF16), wgu_ref[half:2 * half, :])
    ff = gu.shape[1] // 2
    gate = gu[:, :ff]
    act = gate * jax.nn.sigmoid(gate) * gu[:, ff:]
    return _dot(act.astype(BF16), wd_ref[...])


def _experts(xbuf, block_e, nused, w_gate_up, w_down, layer):
    nblk = xbuf.shape[0] // (E_BLOCK * ROW_TILE)
    _, _, d, ff2 = w_gate_up.shape
    last = lambda b, nu: jnp.minimum(b, nu[0] - 1)
    return pl.pallas_call(
        _expert_kernel,
        out_shape=jax.ShapeDtypeStruct(xbuf.shape, U32),
        grid_spec=pltpu.PrefetchScalarGridSpec(
            num_scalar_prefetch=2,
            grid=(nblk,),
            in_specs=[pl.BlockSpec((E_BLOCK * ROW_TILE, LANES), lambda b, be, nu: (last(b, nu), 0)),
                      pl.BlockSpec((None, None, d, ff2), lambda b, be, nu: (layer, be[b], 0, 0)),
                      pl.BlockSpec((None, None, ff2 // 2, d), lambda b, be, nu: (layer, be[b], 0, 0))],
            out_specs=pl.BlockSpec((E_BLOCK * ROW_TILE, LANES), lambda b, be, nu: (b, 0)),
            scratch_shapes=[pltpu.VMEM((d, ff2), BF16), pltpu.VMEM((ff2 // 2, d), BF16)]),
        compiler_params=_params("arbitrary"),
        name="moe_experts",
    )(block_e, nused, xbuf, w_gate_up, w_down)


def _shared_kernel(x_ref, wgu_ref, wd_ref, y_ref):
    y_ref[...] = _gated_ffn(_load_rows(x_ref), wgu_ref, wd_ref)


def _shared_expert(hf, wgu, wd):
    t = hf.shape[0] // ROW_TILE
    d = wgu.shape[0]
    tm = min(t, 1024)
    return pl.pallas_call(
        _shared_kernel,
        out_shape=jax.ShapeDtypeStruct((t, d), F32),
        grid=(t // tm,),
        in_specs=[pl.BlockSpec((tm * ROW_TILE, LANES), lambda i: (i, 0)),
                  pl.BlockSpec(wgu.shape, lambda i: (0, 0)),
                  pl.BlockSpec(wd.shape, lambda i: (0, 0))],
        out_specs=pl.BlockSpec((tm, d), lambda i: (i, 0)),
        compiler_params=_params("parallel"),
        name="moe_shared",
    )(hf, wgu, wd)


def _combine_kernel(dest_ref, dest_next_ref, w_ref, x_ref, sh_ref, g_ref, fg_ref, ybuf_hbm, o_ref, rows_a, rows_b,
                    dest_smem, idx_sem, row_sems, *, final_norm):
    i = pl.program_id(0)
    tm = x_ref.shape[0]

    def tile(r):
        return pl.ds(pl.multiple_of(r * ROW_TILE, ROW_TILE), ROW_TILE)

    def load_indices(indices_ref):
        load = pltpu.make_async_copy(indices_ref, dest_smem, idx_sem)
        load.start()
        load.wait()

    def row_copy(tok, k, rows, sem):
        pltpu.make_async_copy(ybuf_hbm.at[tile(dest_smem[k, tok])], rows.at[k, tile(tok)], sem).start(priority=k % 2)

    def wait_rows(rows, sem):
        for k in range(TOP_K):
            pltpu.make_async_copy(ybuf_hbm.at[pl.ds(0, tm * ROW_TILE)], rows.at[k], sem).wait()

    @pl.when(i == 0)
    def _():
        load_indices(dest_ref)

        def issue(tok, carry):
            for k in range(TOP_K):
                row_copy(tok, k, rows_a, row_sems.at[0])
            return carry

        lax.fori_loop(0, tm, issue, 0)

    load_indices(dest_next_ref)
    half = ROW_TILE * LANES
    per_chunk = tm // ROW_TILE

    def step(cur, cur_sem, nxt, nxt_sem):
        wait_rows(cur, cur_sem)
        weights = [jnp.broadcast_to(w_ref[:, k:k + 1], (tm, LANES)) for k in range(TOP_K)]
        for s in range(ROW_TILE):
            for tok in range(s * per_chunk, (s + 1) * per_chunk):
                for k in range(TOP_K):
                    row_copy(tok, k, nxt, nxt_sem)
            routed_lo = jnp.zeros((tm, LANES), F32)
            routed_hi = jnp.zeros((tm, LANES), F32)
            for k in range(TOP_K):
                lo, hi = _unpack_bf16_halves(cur[k, pl.ds(s, tm, stride=ROW_TILE), :])
                routed_lo = routed_lo + weights[k] * lo
                routed_hi = routed_hi + weights[k] * hi
            cl = slice(s * LANES, (s + 1) * LANES)
            ch = slice(half + s * LANES, half + (s + 1) * LANES)
            o_ref[:, cl] = x_ref[:, cl] + g_ref[:, cl] * (routed_lo + sh_ref[:, cl])
            o_ref[:, ch] = x_ref[:, ch] + g_ref[:, ch] * (routed_hi + sh_ref[:, ch])
        if final_norm:
            y = o_ref[...]
            o_ref[...] = y * lax.rsqrt(jnp.mean(y * y, axis=-1, keepdims=True) + EPS) * fg_ref[...]

        @pl.when(i + 1 == pl.num_programs(0))
        def _():
            wait_rows(nxt, nxt_sem)

    @pl.when(i % 2 == 0)
    def _():
        step(rows_a, row_sems.at[0], rows_b, row_sems.at[1])

    @pl.when(i % 2 == 1)
    def _():
        step(rows_b, row_sems.at[1], rows_a, row_sems.at[0])


def _combine(dest_t, w_tok, x, shared, gate, final_gain, ybuf, final_norm):
    t, d = x.shape
    tm = min(t, 256)
    n_tiles = t // tm
    return pl.pallas_call(
        functools.partial(_combine_kernel, final_norm=final_norm),
        out_shape=jax.ShapeDtypeStruct((t, d), F32),
        grid=(n_tiles,),
        in_specs=[pl.BlockSpec((TOP_K, tm), lambda i: (0, i)),
                  pl.BlockSpec((TOP_K, tm), lambda i: (0, jnp.minimum(i + 1, n_tiles - 1))),
                  pl.BlockSpec((tm, TOP_K), lambda i: (i, 0)),
                  pl.BlockSpec((tm, d), lambda i: (i, 0)),
                  pl.BlockSpec((tm, d), lambda i: (i, 0)),
                  pl.BlockSpec((1, d), lambda i: (0, 0)),
                  pl.BlockSpec((1, d), lambda i: (0, 0)),
                  pl.BlockSpec(memory_space=pl.ANY)],
        out_specs=pl.BlockSpec((tm, d), lambda i: (i, 0)),
        scratch_shapes=[pltpu.VMEM((TOP_K, tm * ROW_TILE, LANES), U32), pltpu.VMEM((TOP_K, tm * ROW_TILE, LANES), U32),
                        pltpu.SMEM((TOP_K, tm), I32), pltpu.SemaphoreType.DMA, pltpu.SemaphoreType.DMA((2,))],
        compiler_params=_params("arbitrary"),
        name="moe_combine",
    )(dest_t, dest_t, w_tok, x, shared, gate, final_gain, ybuf)


def _moe_layer(x, shift, scale, gate, layer, w_router, router_bias, w_gate_up, w_down, ws_gate_up, ws_down,
               final_gain, final_norm):
    t, d = x.shape
    assert d == 2 * ROW_TILE * LANES, "a packed row must be exactly one (8, 128) tile of 32-bit words"
    npad = t * TOP_K + N_EXPERTS * E_BLOCK
    nblk = npad // E_BLOCK
    hf, idx_t, wts_t, rank_t, counts = _moe_pre(x, shift, scale, w_router, router_bias)
    dest_t, block_e, nused, tail_start = _moe_dest(idx_t, rank_t, counts, nblk)
    nused = nused.reshape(-1)[:1]
    xbuf = _dispatch(hf, dest_t, tail_start.reshape(-1), nused, npad)
    ybuf = _experts(xbuf, block_e.reshape(-1), nused, w_gate_up, w_down, layer)
    shared = _shared_expert(hf, ws_gate_up.astype(BF16), ws_down.astype(BF16))
    return _combine(dest_t, wts_t.T, x, shared, gate, final_gain, ybuf, final_norm)


def _even_layer(x, mod, cos, sin, w_in, b_if, mlstm_norm, q_norm, kv_norm, w_uq, w_ukv, w_out):
    t, d = x.shape
    sh1, sc1, g1 = mod[0], mod[1], mod[2]
    nq, nv = M_HEADS * M_QK, M_HEADS * M_V
    o = 0
    cols = []
    for sz in (nq, nq, nv, nv, M_HEADS, M_HEADS, A_Q_LORA, A_KV_LORA, A_ROPE):
        cols.append(w_in[:, o:o + sz])
        o += sz
    mq, mk, mv, mo, mi, mf, cq, ckv, kr = cols
    w_a = jnp.concatenate([mq, mk, mv, mo], axis=1).astype(BF16)
    zeros = lambda n: jnp.zeros((d, n), F32)
    w_b = jnp.concatenate([cq, ckv, kr, zeros(LANES - A_ROPE), mi, mf, zeros(LANES - 2 * M_HEADS)], axis=1).astype(BF16)
    gate_col_block = (A_Q_LORA + A_KV_LORA + LANES) // LANES
    proj_a = _norm_matmul(x, sh1, sc1, w_a, jnp.zeros((1, w_a.shape[1]), F32), BF16, "even_in_a")
    proj_b = _norm_matmul(x, sh1, sc1, w_b, jnp.zeros((1, w_b.shape[1]), F32), F32, "even_in_b")
    gate_bias = jnp.zeros((1, LANES), F32).at[0, :2 * M_HEADS].set(b_if.astype(F32))
    hm = _mlstm(proj_a, proj_b, gate_col_block, gate_bias, mlstm_norm.astype(F32).reshape(1, nv))

    qk = A_NOPE + A_ROPE
    wq = jnp.pad(w_uq.reshape(A_Q_LORA, A_HEADS, qk), ((0, 0), (0, 0), (0, A_QK_PAD - qk)))
    wq = wq.reshape(A_Q_LORA, A_HEADS * A_QK_PAD).astype(BF16)
    q, k, v = _mla_up(proj_b, q_norm.astype(F32).reshape(1, -1), kv_norm.astype(F32).reshape(1, -1),
                      wq, w_ukv.astype(BF16), cos, sin)
    ha = _mla_flash(q, k, v)
    w_out = w_out.astype(BF16)
    return _proj_residual([hm, ha], [w_out[:nv], w_out[nv:]], x, g1, "even_out")


def _odd_layer(x, mod, cos, sin, w_qkv, b_qkv, sinks, w_o):
    sh1, sc1, g1 = mod[0], mod[1], mod[2]
    qkv = _norm_matmul(x, sh1, sc1, w_qkv.astype(BF16), b_qkv.astype(F32).reshape(1, -1), BF16, "odd_qkv")
    o = _swa(qkv, cos, sin, sinks)
    return _proj_residual([o], [w_o.astype(BF16)], x, g1, "odd_out")


def kernel(x, c, positions, w_ada, b_ada, a_w_in, a_b_if, a_mlstm_norm, a_q_norm, a_kv_norm, a_w_uq, a_w_ukv,
           a_w_out, s_w_qkv, s_b_qkv, s_sinks, s_w_o, e_w_router, e_router_bias, e_w_gate_up, e_w_down,
           e_ws_gate_up, e_ws_down, final_norm):
    batch, t, d = x.shape
    assert batch == 1, "kernels are written for a single sequence"
    depth = w_ada.shape[0]
    xs = x.reshape(t, d)
    mods = _ada(c, w_ada, b_ada).reshape(depth, 6, 1, d)
    cos, sin = _rope_tables(positions.reshape(t))
    final_gain = final_norm.astype(F32).reshape(1, d)
    for layer in range(depth):
        mod = mods[layer]
        if layer % 2 == 0:
            e = layer // 2
            xs = _even_layer(xs, mod, cos, sin, a_w_in[e], a_b_if[e], a_mlstm_norm[e], a_q_norm[e], a_kv_norm[e],
                             a_w_uq[e], a_w_ukv[e], a_w_out[e])
        else:
            o = layer // 2
            xs = _odd_layer(xs, mod, cos, sin, s_w_qkv[o], s_b_qkv[o], s_sinks[o], s_w_o[o])
        xs = _moe_layer(xs, mod[3], mod[4], mod[5], layer, e_w_router[layer], e_router_bias[layer],
                        e_w_gate_up, e_w_down, e_ws_gate_up[layer], e_ws_down[layer],
                        final_gain, final_norm=(layer == depth - 1))
    return xs.reshape(batch, t, d)
```

```python
import functools

import jax
import jax.numpy as jnp
from jax import lax
from jax.experimental import pallas as pl
from jax.experimental.pallas import tpu as pltpu

F32 = jnp.float32
BF16 = jnp.bfloat16
I32 = jnp.int32
U32 = jnp.uint32
HIGHEST = lax.Precision.HIGHEST

EPS = 1e-6
ROPE_THETA = 10000.0
ROPE_DIM = 64

M_HEADS = 4
M_QK = 128
M_V = 256
GATE_SOFTCAP = 15.0
MLSTM_CHUNK = 128

A_HEADS = 8
A_NOPE = 128
A_ROPE = 64
A_V = 128
A_Q_LORA = 768
A_KV_LORA = 512
A_QK_PAD = 256

S_HEADS = 32
S_KV_HEADS = 4
S_HD = 64
S_BLOCK = 128

N_EXPERTS = 64
TOP_K = 8
N_GROUPS = 8
TOPK_GROUPS = 4
E_FF = 256
SHARED_FF = 256
ROUTED_SCALE = 2.5
E_BLOCK = 512

LANES = 128
VMEM_LIMIT_BYTES = 48 * 1024 * 1024
FLASH_VMEM_LIMIT_BYTES = 56 * 1024 * 1024
LOG2_E = 1.4426950408889634


def _params(*semantics):
    return pltpu.CompilerParams(dimension_semantics=semantics, vmem_limit_bytes=VMEM_LIMIT_BYTES)


def _dot(a, b, precision=None):
    return jnp.dot(a, b, preferred_element_type=F32, precision=precision)


def _dot_nt(a, b):
    return lax.dot_general(a, b, (((1,), (1,)), ((), ())), preferred_element_type=F32)


def _dot_tn(a, b):
    return lax.dot_general(a, b, (((0,), (0,)), ((), ())), preferred_element_type=F32)


def _modulated_norm(x, shift, scale):
    y = x * lax.rsqrt(jnp.mean(x * x, axis=-1, keepdims=True) + EPS)
    return y * (1.0 + scale) + shift


def _rope_lanes(x, cos, sin_signed):
    lane = lax.broadcasted_iota(I32, x.shape, 1)
    first_half = (lane % ROPE_DIM) < (ROPE_DIM // 2)
    swapped = jnp.where(first_half, pltpu.roll(x, LANES - ROPE_DIM // 2, 1), pltpu.roll(x, ROPE_DIM // 2, 1))
    return x * cos + swapped * sin_signed


def _pack_bf16_halves(x):
    n = x.shape[1] // 2
    lo = lax.bitcast_convert_type(x[:, :n].astype(BF16).astype(F32), U32)
    hi = lax.bitcast_convert_type(x[:, n:].astype(BF16).astype(F32), U32)
    return (hi & jnp.uint32(0xFFFF0000)) | (lo >> 16)


def _unpack_bf16_halves(w):
    lo = lax.bitcast_convert_type(w << 16, F32)
    hi = lax.bitcast_convert_type(w & jnp.uint32(0xFFFF0000), F32)
    return lo, hi


ROW_TILE = 8


def _load_rows(ref):
    m = ref.shape[0] // ROW_TILE
    return jnp.concatenate([ref[pl.ds(s, m, stride=ROW_TILE), :] for s in range(ROW_TILE)], axis=1)


def _store_rows(ref, x):
    m = x.shape[0]
    for s in range(ROW_TILE):
        ref[pl.ds(s, m, stride=ROW_TILE), :] = x[:, s * LANES:(s + 1) * LANES]


def _ada_kernel(c_ref, w_ref, b_ref, o_ref):
    c = c_ref[...]
    c_act = c * jax.nn.sigmoid(c)
    o_ref[...] = jnp.sum(c_act * w_ref[...], axis=0, keepdims=True) + b_ref[...]


def _ada(c, w_ada, b_ada):
    depth, d, n = w_ada.shape
    tn = 1024
    return pl.pallas_call(
        _ada_kernel,
        out_shape=jax.ShapeDtypeStruct((depth, 1, n), F32),
        grid=(depth, n // tn),
        in_specs=[pl.BlockSpec((d, 1), lambda l, j: (0, 0)),
                  pl.BlockSpec((None, d, tn), lambda l, j: (l, 0, j)),
                  pl.BlockSpec((None, 1, tn), lambda l, j: (l, 0, j))],
        out_specs=pl.BlockSpec((None, 1, tn), lambda l, j: (l, 0, j)),
        compiler_params=_params("parallel", "parallel"),
        name="ada_mod",
    )(c.reshape(d, 1), w_ada, b_ada.reshape(depth, 1, n))


def _rope_table_kernel(pos_ref, inv_ref, sign_ref, cos_ref, sin_ref):
    ang = pos_ref[...].astype(F32) * inv_ref[...]
    cos_ref[...] = jnp.cos(ang)
    sin_ref[...] = jnp.sin(ang) * sign_ref[...]


def _rope_tables(positions):
    t = positions.shape[0]
    half = ROPE_DIM // 2
    inv_freq = jnp.power(ROPE_THETA, -jnp.arange(half, dtype=F32) / half)
    inv = jnp.tile(inv_freq, LANES // half).reshape(1, LANES)
    sign = jnp.tile(jnp.concatenate([-jnp.ones((half,), F32), jnp.ones((half,), F32)]), LANES // ROPE_DIM)
    tm = min(t, 2048)
    return pl.pallas_call(
        _rope_table_kernel,
        out_shape=(jax.ShapeDtypeStruct((t, LANES), F32), jax.ShapeDtypeStruct((t, LANES), F32)),
        grid=(t // tm,),
        in_specs=[pl.BlockSpec((tm, 1), lambda i: (i, 0)),
                  pl.BlockSpec((1, LANES), lambda i: (0, 0)),
                  pl.BlockSpec((1, LANES), lambda i: (0, 0))],
        out_specs=(pl.BlockSpec((tm, LANES), lambda i: (i, 0)), pl.BlockSpec((tm, LANES), lambda i: (i, 0))),
        compiler_params=_params("parallel"),
        name="rope_tables",
    )(positions.reshape(t, 1), inv, sign.reshape(1, LANES))


def _norm_matmul_kernel(x_ref, sh_ref, sc_ref, w_ref, b_ref, o_ref, h_scr):
    @pl.when(pl.program_id(1) == 0)
    def _():
        h_scr[...] = _modulated_norm(x_ref[...], sh_ref[...], sc_ref[...]).astype(BF16)

    o_ref[...] = (_dot(h_scr[...], w_ref[...]) + b_ref[...]).astype(o_ref.dtype)


def _col_tile(n, cap):
    return max(c for c in range(LANES, min(n, cap) + 1, LANES) if n % c == 0)


def _norm_matmul(x, shift, scale, w, bias, out_dtype, name):
    t, d = x.shape
    n = w.shape[1]
    tm = min(t, 1024)
    tn = _col_tile(n, 1280)
    return pl.pallas_call(
        _norm_matmul_kernel,
        out_shape=jax.ShapeDtypeStruct((t, n), out_dtype),
        grid=(t // tm, n // tn),
        in_specs=[pl.BlockSpec((tm, d), lambda i, j: (i, 0)),
                  pl.BlockSpec((1, d), lambda i, j: (0, 0)),
                  pl.BlockSpec((1, d), lambda i, j: (0, 0)),
                  pl.BlockSpec((d, tn), lambda i, j: (0, j)),
                  pl.BlockSpec((1, tn), lambda i, j: (0, j))],
        out_specs=pl.BlockSpec((tm, tn), lambda i, j: (i, j)),
        scratch_shapes=[pltpu.VMEM((tm, d), BF16)],
        compiler_params=_params("parallel", "arbitrary"),
        name=name,
    )(x, shift, scale, w, bias)


def _mlstm_kernel(q_ref, k_ref, v_ref, o_ref, g_ref, gb_ref, gain_ref, out_ref, c_scr, n_scr, m_scr):
    L = MLSTM_CHUNK
    tm = q_ref.shape[0]

    @pl.when(pl.program_id(0) == 0)
    def _():
        c_scr[...] = jnp.zeros_like(c_scr)
        n_scr[...] = jnp.zeros_like(n_scr)
        m_scr[...] = jnp.zeros_like(m_scr)

    capped = GATE_SOFTCAP * jnp.tanh((g_ref[...] + gb_ref[...]) / GATE_SOFTCAP)
    log_sig = jnp.minimum(capped, 0.0) - jnp.log1p(jnp.exp(-jnp.abs(capped)))
    lane = lax.broadcasted_iota(I32, capped.shape, 1)
    gate = jnp.where(lane < M_HEADS, capped, log_sig)
    r = lax.broadcasted_iota(I32, (tm, tm), 0)
    c = lax.broadcasted_iota(I32, (tm, tm), 1)
    chunk_tril = jnp.where(((r // L) == (c // L)) & (c <= r), 1.0, 0.0).astype(F32)
    cum = _dot(chunk_tril, gate, precision=HIGHEST)
    gate_rows = gate.T
    cum_rows = cum.T
    rr = lax.broadcasted_iota(I32, (L, L), 0)
    cc = lax.broadcasted_iota(I32, (L, L), 1)
    causal = cc <= rr

    states = [(c_scr[h], n_scr[h], m_scr[h:h + 1, 0:1]) for h in range(M_HEADS)]
    for ci in range(tm // L):
        sl = slice(ci * L, (ci + 1) * L)
        for h in range(M_HEADS):
            qk = slice(h * M_QK, (h + 1) * M_QK)
            vv = slice(h * M_V, (h + 1) * M_V)
            c_state, n_state, m_prev = states[h]
            qc = q_ref[sl, qk]
            kc = k_ref[sl, qk].astype(F32) * (M_QK ** -0.5)
            kcb = kc.astype(BF16)
            vc = v_ref[sl, vv]
            ig_col = gate[sl, h:h + 1]
            b_col = cum[sl, M_HEADS + h:M_HEADS + h + 1]
            ig_row = gate_rows[h:h + 1, sl]
            b_row = cum_rows[M_HEADS + h:M_HEADS + h + 1, sl]

            dm = jnp.where(causal, b_col - b_row + ig_row, -jnp.inf)
            inter = b_col + m_prev
            m_row = jnp.maximum(inter, jnp.max(dm, axis=-1, keepdims=True))
            w_intra = jnp.exp(dm - m_row)
            w_inter = jnp.exp(inter - m_row)
            s = _dot_nt(qc, kcb) * w_intra
            num = _dot(s.astype(BF16), vc) + w_inter * _dot_nt(qc, c_state.astype(BF16))
            den = (jnp.sum(s, axis=-1, keepdims=True)
                   + w_inter * jnp.sum(qc.astype(F32) * n_state, axis=-1, keepdims=True))
            hh = num / jnp.maximum(jnp.abs(den), jnp.exp(-m_row))

            b_last = b_col[L - 1:L, :]
            g_row = b_last - b_row + ig_row
            g_col = b_last - b_col + ig_col
            m_new = jnp.maximum(b_last + m_prev, jnp.max(g_row, axis=-1, keepdims=True))
            ws_col = jnp.exp(g_col - m_new)
            decay = jnp.exp(b_last + m_prev - m_new)
            vw = (vc.astype(F32) * ws_col).astype(BF16)
            states[h] = (decay * c_state + _dot_tn(vw, kcb),
                         decay * n_state + jnp.sum(kc * ws_col, axis=0, keepdims=True),
                         m_new)

            y = hh * lax.rsqrt(jnp.mean(hh * hh, axis=-1, keepdims=True) + EPS) * gain_ref[:, vv]
            y = y * jax.nn.sigmoid(o_ref[sl, vv].astype(F32))
            out_ref[sl, vv] = y.astype(out_ref.dtype)
    for h in range(M_HEADS):
        c_scr[h], n_scr[h] = states[h][0], states[h][1]
        m_scr[h:h + 1, :] = jnp.broadcast_to(states[h][2], (1, LANES))


def _mlstm(proj_a, proj_b, gate_col_block, gate_bias, gain):
    t = proj_a.shape[0]
    tm = min(t, 512)
    nq = M_HEADS * M_QK
    nv = M_HEADS * M_V
    return pl.pallas_call(
        _mlstm_kernel,
        out_shape=jax.ShapeDtypeStruct((t, nv), BF16),
        grid=(t // tm,),
        in_specs=[pl.BlockSpec((tm, nq), lambda i: (i, 0)),
                  pl.BlockSpec((tm, nq), lambda i: (i, 1)),
                  pl.BlockSpec((tm, nv), lambda i: (i, 1)),
                  pl.BlockSpec((tm, nv), lambda i: (i, 2)),
                  pl.BlockSpec((tm, LANES), lambda i: (i, gate_col_block)),
                  pl.BlockSpec((1, LANES), lambda i: (0, 0)),
                  pl.BlockSpec((1, nv), lambda i: (0, 0))],
        out_specs=pl.BlockSpec((tm, nv), lambda i: (i, 0)),
        scratch_shapes=[pltpu.VMEM((M_HEADS, M_V, M_QK), F32),
                        pltpu.VMEM((M_HEADS, 1, M_QK), F32),
                        pltpu.VMEM((8, LANES), F32)],
        compiler_params=_params("arbitrary"),
        name="mlstm",
    )(proj_a, proj_a, proj_a, proj_a, proj_b, gate_bias, gain)


def _mla_up_kernel(pb_ref, qn_ref, kvn_ref, wq_ref, wkv_ref, cos_ref, sin_ref, q_ref, k_ref, v_ref):
    cos, sin = cos_ref[...], sin_ref[...]
    cq = pb_ref[:, 0:A_Q_LORA]
    cq = (cq * lax.rsqrt(jnp.mean(cq * cq, axis=-1, keepdims=True) + EPS) * qn_ref[...]).astype(BF16)
    ckv = pb_ref[:, A_Q_LORA:A_Q_LORA + A_KV_LORA]
    ckv = (ckv * lax.rsqrt(jnp.mean(ckv * ckv, axis=-1, keepdims=True) + EPS) * kvn_ref[...]).astype(BF16)
    k_pe = _rope_lanes(pb_ref[:, A_Q_LORA + A_KV_LORA:A_Q_LORA + A_KV_LORA + LANES], cos, sin).astype(BF16)
    scale = (A_NOPE + A_ROPE) ** -0.5 * LOG2_E
    for h in range(A_HEADS):
        qh = _dot(cq, wq_ref[:, h * A_QK_PAD:(h + 1) * A_QK_PAD])
        q_pe = _rope_lanes(qh[:, A_NOPE:], cos, sin)
        q_ref[h] = (jnp.concatenate([qh[:, :A_NOPE], q_pe], axis=1) * scale).astype(q_ref.dtype)
        kvh = _dot(ckv, wkv_ref[:, h * (A_NOPE + A_V):(h + 1) * (A_NOPE + A_V)])
        k_ref[h] = jnp.concatenate([kvh[:, :A_NOPE].astype(BF16), k_pe], axis=1)
        v_ref[h] = kvh[:, A_NOPE:].astype(v_ref.dtype)


def _mla_up(proj_b, q_norm, kv_norm, wq, wkv, cos, sin):
    t, nb = proj_b.shape
    tm = min(t, 512)
    return pl.pallas_call(
        _mla_up_kernel,
        out_shape=(jax.ShapeDtypeStruct((A_HEADS, t, A_QK_PAD), BF16),
                   jax.ShapeDtypeStruct((A_HEADS, t, A_QK_PAD), BF16),
                   jax.ShapeDtypeStruct((A_HEADS, t, A_V), BF16)),
        grid=(t // tm,),
        in_specs=[pl.BlockSpec((tm, nb), lambda i: (i, 0)),
                  pl.BlockSpec((1, A_Q_LORA), lambda i: (0, 0)),
                  pl.BlockSpec((1, A_KV_LORA), lambda i: (0, 0)),
                  pl.BlockSpec(wq.shape, lambda i: (0, 0)),
                  pl.BlockSpec(wkv.shape, lambda i: (0, 0)),
                  pl.BlockSpec((tm, LANES), lambda i: (i, 0)),
                  pl.BlockSpec((tm, LANES), lambda i: (i, 0))],
        out_specs=(pl.BlockSpec((A_HEADS, tm, A_QK_PAD), lambda i: (0, i, 0)),
                   pl.BlockSpec((A_HEADS, tm, A_QK_PAD), lambda i: (0, i, 0)),
                   pl.BlockSpec((A_HEADS, tm, A_V), lambda i: (0, i, 0))),
        compiler_params=_params("parallel"),
        name="mla_up",
    )(proj_b, q_norm, kv_norm, wq, wkv, cos, sin)


def _mla_flash_kernel(q_ref, k_ref, v_ref, o_ref, m_scr, l_scr, acc_scr, *, tq, tkb):
    i = pl.program_id(1)
    q = q_ref[...]
    m_scr[...] = jnp.full(m_scr.shape, -jnp.inf, F32)
    l_scr[...] = jnp.zeros(l_scr.shape, F32)
    acc_scr[...] = jnp.zeros(acc_scr.shape, F32)

    def tile(start, width, masked):
        s = _dot_nt(q, k_ref[pl.ds(start, width), :])
        if masked:
            row = lax.broadcasted_iota(I32, (tq, width), 0)
            col = lax.broadcasted_iota(I32, (tq, width), 1)
            s = jnp.where(col <= row, s, -jnp.inf)
        m = m_scr[...]
        m_new = jnp.maximum(m, jnp.broadcast_to(jnp.max(s, axis=-1, keepdims=True), m.shape))
        alpha = jnp.exp2(m - m_new)
        p = jnp.exp2(s - jnp.concatenate([m_new] * (width // LANES), axis=1))
        lane_sums = p[:, 0:LANES]
        for j in range(1, width // LANES):
            lane_sums = lane_sums + p[:, j * LANES:(j + 1) * LANES]
        l_scr[...] = alpha * l_scr[...] + lane_sums
        acc_scr[...] = alpha * acc_scr[...] + _dot(p.astype(BF16), v_ref[pl.ds(start, width), :])
        m_scr[...] = m_new

    per = tkb // tq
    n_big = i // per

    def body(j, carry):
        tile(pl.multiple_of(j * tkb, tkb), tkb, False)
        return carry

    lax.fori_loop(0, n_big, body, 0)
    rem = i - n_big * per
    for r in range(1, per):
        @pl.when(rem >= r)
        def _():
            tile(pl.multiple_of(n_big * tkb + (r - 1) * tq, tq), tq, False)
    tile(pl.multiple_of(i * tq, tq), tq, True)
    o_ref[...] = (acc_scr[...] / jnp.sum(l_scr[...], axis=-1, keepdims=True)).astype(o_ref.dtype)


def _mla_flash(q, k, v):
    _, t, _ = q.shape
    tq = min(t, 1024)
    tkb = min(t, 2048)
    return pl.pallas_call(
        functools.partial(_mla_flash_kernel, tq=tq, tkb=tkb),
        out_shape=jax.ShapeDtypeStruct((t, A_HEADS * A_V), BF16),
        grid=(A_HEADS, t // tq),
        in_specs=[pl.BlockSpec((None, tq, A_QK_PAD), lambda h, i: (h, i, 0)),
                  pl.BlockSpec((None, t, A_QK_PAD), lambda h, i: (h, 0, 0)),
                  pl.BlockSpec((None, t, A_V), lambda h, i: (h, 0, 0))],
        out_specs=pl.BlockSpec((tq, A_V), lambda h, i: (i, h)),
        scratch_shapes=[pltpu.VMEM((tq, LANES), F32), pltpu.VMEM((tq, LANES), F32), pltpu.VMEM((tq, A_V), F32)],
        compiler_params=pltpu.CompilerParams(dimension_semantics=("parallel", "arbitrary"),
                                             vmem_limit_bytes=FLASH_VMEM_LIMIT_BYTES),
        name="mla_flash",
    )(q, k, v)


def _proj_residual_kernel(*refs, n_lhs):
    lhs = refs[:n_lhs]
    ws = refs[n_lhs:2 * n_lhs]
    x_ref, g_ref, o_ref = refs[2 * n_lhs:]
    acc = _dot(lhs[0][...], ws[0][...])
    for a, w in zip(lhs[1:], ws[1:]):
        acc = acc + _dot(a[...], w[...])
    o_ref[...] = x_ref[...] + g_ref[...] * acc


def _proj_residual(lhs_list, w_list, x, gate, name):
    t, d = x.shape
    tm = min(t, 1024)
    tn = _col_tile(d, 1024)
    n_lhs = len(lhs_list)
    in_specs = ([pl.BlockSpec((tm, a.shape[1]), lambda i, j: (i, 0)) for a in lhs_list]
                + [pl.BlockSpec((w.shape[0], tn), lambda i, j: (0, j)) for w in w_list]
                + [pl.BlockSpec((tm, tn), lambda i, j: (i, j)), pl.BlockSpec((1, tn), lambda i, j: (0, j))])
    return pl.pallas_call(
        functools.partial(_proj_residual_kernel, n_lhs=n_lhs),
        out_shape=jax.ShapeDtypeStruct((t, d), F32),
        grid=(t // tm, d // tn),
        in_specs=in_specs,
        out_specs=pl.BlockSpec((tm, tn), lambda i, j: (i, j)),
        compiler_params=_params("parallel", "parallel"),
        name=name,
    )(*lhs_list, *w_list, x, gate)


def _swa_kernel(q_ref, kc_ref, kp_ref, vc_ref, vp_ref, cosc_ref, sinc_ref, cosp_ref, sinp_ref, sink_ref, o_ref):
    i = pl.program_id(0)
    qb = S_BLOCK
    group = S_HEADS // S_KV_HEADS
    pairs = group // 2
    cos_c, sin_c = cosc_ref[...], sinc_ref[...]
    cos_w = jnp.concatenate([cosp_ref[...], cos_c], axis=0)
    sin_w = jnp.concatenate([sinp_ref[...], sin_c], axis=0)
    kw = jnp.concatenate([kp_ref[...], kc_ref[...]], axis=0).astype(F32)
    vw = jnp.concatenate([vp_ref[...], vc_ref[...]], axis=0)

    r = lax.broadcasted_iota(I32, (pairs * qb, 2 * qb), 0) % qb
    c = lax.broadcasted_iota(I32, (pairs * qb, 2 * qb), 1)
    dist = qb + r - c
    valid = (dist >= 0) & (dist < S_BLOCK) & ((c >= qb) | (i > 0))
    lane = lax.broadcasted_iota(I32, (2 * qb, LANES), 1)
    low = lane < S_HD
    low_rows = lax.broadcasted_iota(I32, (pairs * qb, LANES), 1) < S_HD
    ones_d = jnp.concatenate([jnp.where(low, 1.0, 0.0), jnp.where(low, 0.0, 1.0)], axis=0).astype(BF16)

    for g in range(S_KV_HEADS):
        col = slice((g // 2) * LANES, (g // 2 + 1) * LANES)
        k_pair = _rope_lanes(kw[:, col], cos_w, sin_w)
        v_pair = vw[:, col]
        keep = low if g % 2 == 0 else jnp.logical_not(low)
        k_own = jnp.where(keep, k_pair, 0.0)
        v_own = jnp.where(keep, v_pair.astype(F32), 0.0)
        k_other = pltpu.roll(k_own, S_HD, 1)
        v_other = pltpu.roll(v_own, S_HD, 1)
        k_lo, k_hi = (k_own, k_other) if g % 2 == 0 else (k_other, k_own)
        v_lo, v_hi = (v_own, v_other) if g % 2 == 0 else (v_other, v_own)
        kd = jnp.concatenate([k_lo, k_hi], axis=0).astype(BF16)
        vd = jnp.concatenate([v_lo, v_hi], axis=0).astype(BF16)
        qs = []
        for p in range(pairs):
            qcol = slice((g * pairs + p) * LANES, (g * pairs + p + 1) * LANES)
            qs.append(_rope_lanes(q_ref[:, qcol].astype(F32), cos_c, sin_c) * (S_HD ** -0.5 * LOG2_E))
        qg = jnp.concatenate(qs, axis=0).astype(BF16)
        s = _dot_nt(qg, kd)
        exps, sink_terms = [], []
        for half in range(2):
            sh = jnp.where(valid, s[:, half * 2 * qb:(half + 1) * 2 * qb], -jnp.inf)
            sink = sink_ref[g, half]
            m = jnp.maximum(jnp.broadcast_to(jnp.max(sh, axis=-1, keepdims=True), sink.shape), sink)
            exps.append(jnp.exp2(sh - jnp.concatenate([m, m], axis=1)).astype(BF16))
            sink_terms.append(jnp.exp2(sink - m))
        e_all = jnp.concatenate(exps, axis=1)
        den = _dot(e_all, ones_d) + jnp.where(low_rows, sink_terms[0], sink_terms[1])
        o = _dot(e_all, vd) / den
        for p in range(pairs):
            qcol = slice((g * pairs + p) * LANES, (g * pairs + p + 1) * LANES)
            o_ref[:, qcol] = o[p * qb:(p + 1) * qb].astype(o_ref.dtype)


def _swa(qkv, cos, sin, sinks):
    t = qkv.shape[0]
    qb = S_BLOCK
    nq = S_HEADS * S_HD
    nk = S_KV_HEADS * S_HD
    group = S_HEADS // S_KV_HEADS
    pairs = group // 2
    sink_cols = jnp.repeat((sinks.astype(F32) * LOG2_E).reshape(S_KV_HEADS, pairs, 2).transpose(0, 2, 1), qb, axis=-1)
    sink_cols = jnp.broadcast_to(sink_cols.reshape(S_KV_HEADS, 2, pairs * qb, 1), (S_KV_HEADS, 2, pairs * qb, LANES))
    k_blk = nq // nk
    prev = lambda i: jnp.maximum(i - 1, 0)
    return pl.pallas_call(
        _swa_kernel,
        out_shape=jax.ShapeDtypeStruct((t, nq), BF16),
        grid=(t // qb,),
        in_specs=[pl.BlockSpec((qb, nq), lambda i: (i, 0)),
                  pl.BlockSpec((qb, nk), lambda i: (i, k_blk)),
                  pl.BlockSpec((qb, nk), lambda i: (prev(i), k_blk)),
                  pl.BlockSpec((qb, nk), lambda i: (i, k_blk + 1)),
                  pl.BlockSpec((qb, nk), lambda i: (prev(i), k_blk + 1)),
                  pl.BlockSpec((qb, LANES), lambda i: (i, 0)),
                  pl.BlockSpec((qb, LANES), lambda i: (i, 0)),
                  pl.BlockSpec((qb, LANES), lambda i: (prev(i), 0)),
                  pl.BlockSpec((qb, LANES), lambda i: (prev(i), 0)),
                  pl.BlockSpec((S_KV_HEADS, 2, pairs * qb, LANES), lambda i: (0, 0, 0, 0))],
        out_specs=pl.BlockSpec((qb, nq), lambda i: (i, 0)),
        compiler_params=_params("parallel"),
        name="swa",
    )(qkv, qkv, qkv, qkv, qkv, cos, sin, cos, sin, sink_cols)


def _expert_of_row(p):
    per_group = N_EXPERTS // N_GROUPS
    return (p % N_GROUPS) * per_group + p // N_GROUPS


def _moe_pre_kernel(x_ref, sh_ref, sc_ref, wr_ref, rb_ref, hf_ref, idx_ref, wt_ref, rank_ref, cnt_ref, carry_scr):
    tm = x_ref.shape[0]
    per_group = N_EXPERTS // N_GROUPS

    @pl.when(pl.program_id(0) == 0)
    def _():
        carry_scr[...] = jnp.zeros_like(carry_scr)

    h = _modulated_norm(x_ref[...], sh_ref[...], sc_ref[...])
    _store_rows(hf_ref, _pack_bf16_halves(h))
    h_hi = h.astype(BF16)
    h_lo = (h - h_hi.astype(F32)).astype(BF16)
    logits = _dot(h_hi, wr_ref[0]) + (_dot(h_hi, wr_ref[1]) + _dot(h_lo, wr_ref[0]))
    scores = jax.nn.sigmoid(logits.T[:N_EXPERTS, :])
    biased = scores + rb_ref[...]

    members = [biased[j * N_GROUPS:(j + 1) * N_GROUPS, :] for j in range(per_group)]
    m1 = members[0]
    for a in members[1:]:
        m1 = jnp.maximum(m1, a)
    first = jnp.full(m1.shape, per_group, I32)
    for j in reversed(range(per_group)):
        first = jnp.where(members[j] == m1, j, first)
    m2 = jnp.full(m1.shape, -jnp.inf, F32)
    for j in range(per_group):
        m2 = jnp.maximum(m2, jnp.where(first == j, -jnp.inf, members[j]))
    group_score = m1 + m2

    g_iota = lax.broadcasted_iota(I32, group_score.shape, 0).astype(F32)
    g_sel = jnp.zeros(group_score.shape, F32)
    for _ in range(TOPK_GROUPS):
        best = jnp.max(group_score, axis=0, keepdims=True)
        gi = jnp.min(jnp.where(group_score == best, g_iota, float(N_GROUPS)), axis=0, keepdims=True)
        hit = g_iota == gi
        g_sel = jnp.where(hit, 1.0, g_sel)
        group_score = jnp.where(hit, -jnp.inf, group_score)
    masked = jnp.concatenate([jnp.where(g_sel > 0.5, a, -jnp.inf) for a in members], axis=0)

    e_iota = _expert_of_row(lax.broadcasted_iota(I32, masked.shape, 0)).astype(F32)
    sel = jnp.zeros(masked.shape, F32)
    idx_rows, w_rows = [], []
    for _ in range(TOP_K):
        best = jnp.max(masked, axis=0, keepdims=True)
        ei = jnp.min(jnp.where(masked == best, e_iota, float(N_EXPERTS)), axis=0, keepdims=True)
        hit = e_iota == ei
        idx_rows.append(ei)
        w_rows.append(jnp.sum(jnp.where(hit, scores, 0.0), axis=0, keepdims=True))
        sel = jnp.where(hit, 1.0, sel)
        masked = jnp.where(hit, -jnp.inf, masked)
    idx = jnp.concatenate(idx_rows, axis=0).astype(I32)
    wts = jnp.concatenate(w_rows, axis=0)
    wts = wts / jnp.sum(wts, axis=0, keepdims=True) * ROUTED_SCALE

    r = lax.broadcasted_iota(I32, (tm, tm), 0)
    c = lax.broadcasted_iota(I32, (tm, tm), 1)
    before = jnp.where(r < c, 1.0, 0.0).astype(BF16)
    rank_excl = carry_scr[:, 0:1] + _dot(sel.astype(BF16), before)
    rank_rows = [jnp.sum(jnp.where(e_iota == idx_rows[k], rank_excl, 0.0), axis=0, keepdims=True)
                 for k in range(TOP_K)]
    carry_scr[...] = carry_scr[...] + jnp.sum(sel, axis=1, keepdims=True)

    idx_ref[...] = idx
    wt_ref[...] = wts
    rank_ref[...] = jnp.concatenate(rank_rows, axis=0).astype(I32)
    cnt_ref[...] = carry_scr[...]


def _moe_pre(x, shift, scale, w_router, router_bias):
    t, d = x.shape
    tm = min(t, 512)
    rows = jnp.arange(N_EXPERTS)
    perm = _expert_of_row(rows)
    wr = jnp.zeros((d, LANES), F32).at[:, :N_EXPERTS].set(w_router[:, perm])
    wr_hi = wr.astype(BF16)
    wr = jnp.stack([wr_hi, (wr - wr_hi.astype(F32)).astype(BF16)])
    rb = router_bias.astype(F32)[perm].reshape(N_EXPERTS, 1)
    tok = lambda n, dt: jax.ShapeDtypeStruct((TOP_K, t), dt)
    return pl.pallas_call(
        _moe_pre_kernel,
        out_shape=(jax.ShapeDtypeStruct((t * ROW_TILE, LANES), U32), tok(t, I32), tok(t, F32), tok(t, I32),
                   jax.ShapeDtypeStruct((N_EXPERTS, LANES), F32)),
        grid=(t // tm,),
        in_specs=[pl.BlockSpec((tm, d), lambda i: (i, 0)),
                  pl.BlockSpec((1, d), lambda i: (0, 0)),
                  pl.BlockSpec((1, d), lambda i: (0, 0)),
                  pl.BlockSpec((2, d, LANES), lambda i: (0, 0, 0)),
                  pl.BlockSpec((N_EXPERTS, 1), lambda i: (0, 0))],
        out_specs=(pl.BlockSpec((tm * ROW_TILE, LANES), lambda i: (i, 0)),
                   pl.BlockSpec((TOP_K, tm), lambda i: (0, i)),
                   pl.BlockSpec((TOP_K, tm), lambda i: (0, i)),
                   pl.BlockSpec((TOP_K, tm), lambda i: (0, i)),
                   pl.BlockSpec((N_EXPERTS, LANES), lambda i: (0, 0))),
        scratch_shapes=[pltpu.VMEM((N_EXPERTS, LANES), F32)],
        compiler_params=_params("arbitrary"),
        name="moe_router",
    )(x, shift, scale, wr, rb)


def _moe_dest_kernel(idx_ref, rank_ref, cnt_ref, dest_ref, be_ref, nused_ref, tail_ref):
    cnt = cnt_ref[...]
    padded = jnp.floor((cnt + (E_BLOCK - 1)) * (1.0 / E_BLOCK)) * E_BLOCK
    e_i = _expert_of_row(lax.broadcasted_iota(I32, (N_EXPERTS, N_EXPERTS), 0))
    e_j = _expert_of_row(lax.broadcasted_iota(I32, (N_EXPERTS, N_EXPERTS), 1))
    earlier = jnp.where(e_j < e_i, 1.0, 0.0).astype(F32)
    pstart = _dot(earlier, padded, precision=HIGHEST)
    pend = pstart + padded

    tm = idx_ref.shape[1]
    e_col = _expert_of_row(lax.broadcasted_iota(I32, (N_EXPERTS, tm), 0))
    rows = []
    for k in range(TOP_K):
        hit = e_col == idx_ref[k:k + 1, :]
        rows.append(jnp.sum(jnp.where(hit, pstart[:, 0:1], 0.0), axis=0, keepdims=True))
    dest_ref[...] = jnp.concatenate(rows, axis=0).astype(I32) + rank_ref[...]

    nb = be_ref.shape[1]
    block_start = (lax.broadcasted_iota(I32, (N_EXPERTS, nb), 1) * E_BLOCK).astype(F32)
    ended = jnp.sum(jnp.where(pend[:, 0:1] <= block_start, 1.0, 0.0), axis=0, keepdims=True)
    be_ref[...] = jnp.minimum(ended, float(N_EXPERTS - 1)).astype(I32)
    nused_ref[...] = (jnp.sum(padded, axis=0, keepdims=True) * (1.0 / E_BLOCK)).astype(I32)
    e_row = _expert_of_row(lax.broadcasted_iota(I32, (N_EXPERTS, LANES), 0))
    lane = lax.broadcasted_iota(I32, (N_EXPERTS, LANES), 1)
    end_lane = jnp.sum(jnp.where(e_row <= lane, padded, 0.0), axis=0, keepdims=True)
    own_lane = jnp.sum(jnp.where(e_row == lane, padded, 0.0), axis=0, keepdims=True)
    tail_ref[...] = jnp.where(own_lane > 0.0, end_lane - E_BLOCK, -1.0).astype(I32)


def _moe_dest(idx_t, rank_t, counts, nblk):
    t = idx_t.shape[1]
    tm = min(t, 2048)
    nb = ((nblk + LANES - 1) // LANES) * LANES
    return pl.pallas_call(
        _moe_dest_kernel,
        out_shape=(jax.ShapeDtypeStruct((TOP_K, t), I32),
                   jax.ShapeDtypeStruct((1, nb), I32),
                   jax.ShapeDtypeStruct((1, LANES), I32),
                   jax.ShapeDtypeStruct((1, LANES), I32)),
        grid=(t // tm,),
        in_specs=[pl.BlockSpec((TOP_K, tm), lambda i: (0, i)),
                  pl.BlockSpec((TOP_K, tm), lambda i: (0, i)),
                  pl.BlockSpec((N_EXPERTS, LANES), lambda i: (0, 0))],
        out_specs=(pl.BlockSpec((TOP_K, tm), lambda i: (0, i)),
                   pl.BlockSpec((1, nb), lambda i: (0, 0)),
                   pl.BlockSpec((1, LANES), lambda i: (0, 0)),
                   pl.BlockSpec((1, LANES), lambda i: (0, 0))),
        compiler_params=_params("arbitrary"),
        name="moe_dest",
    )(idx_t, rank_t, counts)


def _dispatch_kernel(tail_ref, nused_ref, dest_ref, hf_ref, wgu_ref, wd_ref, xbuf_hbm, sh_ref, dest_smem, zero_scr,
                     idx_sem, zero_sem, row_sem):
    tm = dest_ref.shape[1]
    block_sublanes = E_BLOCK * ROW_TILE
    nblk = xbuf_hbm.shape[0] // block_sublanes

    @pl.when(pl.program_id(0) == 0)
    def _():
        zero_scr[...] = jnp.zeros_like(zero_scr)

        def zero_copy(row):
            start = pl.multiple_of(row * ROW_TILE, block_sublanes)
            return pltpu.make_async_copy(zero_scr, xbuf_hbm.at[pl.ds(start, block_sublanes)], zero_sem)

        def start(e, carry):
            @pl.when(tail_ref[e] >= 0)
            def _():
                zero_copy(tail_ref[e]).start()
            return carry

        def wait(e, carry):
            @pl.when(tail_ref[e] >= 0)
            def _():
                zero_copy(0).wait()
            return carry

        def start_unused(b, carry):
            zero_copy(b * E_BLOCK).start()
            return carry

        def wait_unused(b, carry):
            zero_copy(0).wait()
            return carry

        lax.fori_loop(0, N_EXPERTS, start, 0)
        lax.fori_loop(nused_ref[0], nblk, start_unused, 0)
        lax.fori_loop(0, N_EXPERTS, wait, 0)
        lax.fori_loop(nused_ref[0], nblk, wait_unused, 0)

    load = pltpu.make_async_copy(dest_ref, dest_smem, idx_sem)
    load.start()
    load.wait()

    def row(ref, r):
        return ref.at[pl.ds(pl.multiple_of(r * ROW_TILE, ROW_TILE), ROW_TILE)]

    for tok in range(tm):
        for k in range(TOP_K):
            pltpu.make_async_copy(row(hf_ref, tok), row(xbuf_hbm, dest_smem[k, tok]), row_sem).start(priority=k % 2)
    sh_ref[...] = _gated_ffn(_load_rows(hf_ref), wgu_ref, wd_ref)
    for k in range(TOP_K):
        pltpu.make_async_copy(hf_ref, xbuf_hbm.at[pl.ds(0, tm * ROW_TILE)], row_sem).wait()


def _dispatch(hf, dest_t, tail_start, nused, wgu, wd, npad):
    t = hf.shape[0] // ROW_TILE
    d = wgu.shape[0]
    tm = min(t, 256)
    return pl.pallas_call(
        _dispatch_kernel,
        out_shape=(jax.ShapeDtypeStruct((npad * ROW_TILE, LANES), U32), jax.ShapeDtypeStruct((t, d), F32)),
        grid_spec=pltpu.PrefetchScalarGridSpec(
            num_scalar_prefetch=2,
            grid=(t // tm,),
            in_specs=[pl.BlockSpec((TOP_K, tm), lambda i, tail, nu: (0, i)),
                      pl.BlockSpec((tm * ROW_TILE, LANES), lambda i, tail, nu: (i, 0)),
                      pl.BlockSpec(wgu.shape, lambda i, tail, nu: (0, 0)),
                      pl.BlockSpec(wd.shape, lambda i, tail, nu: (0, 0))],
            out_specs=(pl.BlockSpec(memory_space=pl.ANY), pl.BlockSpec((tm, d), lambda i, tail, nu: (i, 0))),
            scratch_shapes=[pltpu.SMEM((TOP_K, tm), I32), pltpu.VMEM((E_BLOCK * ROW_TILE, LANES), U32),
                            pltpu.SemaphoreType.DMA, pltpu.SemaphoreType.DMA, pltpu.SemaphoreType.DMA]),
        compiler_params=_params("arbitrary"),
        name="moe_dispatch",
    )(tail_start, nused, dest_t, hf, wgu, wd)


def _expert_kernel(be_ref, nused_ref, x_ref, wgu_ref, wd_ref, y_ref, wgu_scr, wd_scr):
    b = pl.program_id(0)
    e = be_ref[b]
    e_prev = be_ref[jnp.maximum(b - 1, 0)]

    @pl.when((b == 0) | (e != e_prev))
    def _():
        wgu_scr[...] = wgu_ref[...].astype(BF16)
        wd_scr[...] = wd_ref[...].astype(BF16)

    @pl.when(b < nused_ref[0])
    def _():
        _store_rows(y_ref, _pack_bf16_halves(_gated_ffn(_load_rows(x_ref), wgu_scr, wd_scr)))

    @pl.when(b >= nused_ref[0])
    def _():
        y_ref[...] = jnp.zeros_like(y_ref)


def _gated_ffn(x_packed, wgu_ref, wd_ref):
    x_lo, x_hi = _unpack_bf16_halves(x_packed)
    half = x_packed.shape[1]
    gu = _dot(x_lo.astype(BF16), wgu_ref[0:half, :]) + _dot(x_hi.astype(BF16), wgu_ref[half:2 * half, :])
    ff = gu.shape[1] // 2
    gate = gu[:, :ff]
    act = gate * jax.nn.sigmoid(gate) * gu[:, ff:]
    return _dot(act.astype(BF16), wd_ref[...])


def _experts(xbuf, block_e, nused, w_gate_up, w_down, layer):
    nblk = xbuf.shape[0] // (E_BLOCK * ROW_TILE)
    _, _, d, ff2 = w_gate_up.shape
    last = lambda b, nu: jnp.minimum(b, nu[0] - 1)
    return pl.pallas_call(
        _expert_kernel,
        out_shape=jax.ShapeDtypeStruct(xbuf.shape, U32),
        grid_spec=pltpu.PrefetchScalarGridSpec(
            num_scalar_prefetch=2,
            grid=(nblk,),
            in_specs=[pl.BlockSpec((E_BLOCK * ROW_TILE, LANES), lambda b, be, nu: (last(b, nu), 0)),
                      pl.BlockSpec((None, None, d, ff2), lambda b, be, nu: (layer, be[b], 0, 0)),
                      pl.BlockSpec((None, None, ff2 // 2, d), lambda b, be, nu: (layer, be[b], 0, 0))],
            out_specs=pl.BlockSpec((E_BLOCK * ROW_TILE, LANES), lambda b, be, nu: (b, 0)),
            scratch_shapes=[pltpu.VMEM((d, ff2), BF16), pltpu.VMEM((ff2 // 2, d), BF16)]),
        compiler_params=_params("arbitrary"),
        name="moe_experts",
    )(block_e, nused, xbuf, w_gate_up, w_down)


def _combine_kernel(dest_ref, dest_next_ref, w_ref, x_ref, sh_ref, g_ref, ybuf_hbm, o_ref, rows_a, rows_b,
                    dest_smem, idx_sem, row_sems):
    i = pl.program_id(0)
    tm = x_ref.shape[0]

    def tile(r):
        return pl.ds(pl.multiple_of(r * ROW_TILE, ROW_TILE), ROW_TILE)

    def load_indices(indices_ref):
        load = pltpu.make_async_copy(indices_ref, dest_smem, idx_sem)
        load.start()
        load.wait()

    def row_copy(tok, k, rows, sem):
        pltpu.make_async_copy(ybuf_hbm.at[tile(dest_smem[k, tok])], rows.at[k, tile(tok)], sem).start(priority=k % 2)

    def wait_rows(rows, sem):
        for k in range(TOP_K):
            pltpu.make_async_copy(ybuf_hbm.at[pl.ds(0, tm * ROW_TILE)], rows.at[k], sem).wait()

    @pl.when(i == 0)
    def _():
        load_indices(dest_ref)

        def issue(tok, carry):
            for k in range(TOP_K):
                row_copy(tok, k, rows_a, row_sems.at[0])
            return carry

        lax.fori_loop(0, tm, issue, 0)

    load_indices(dest_next_ref)
    half = ROW_TILE * LANES
    per_chunk = tm // ROW_TILE

    def step(cur, cur_sem, nxt, nxt_sem):
        wait_rows(cur, cur_sem)
        weights = [jnp.broadcast_to(w_ref[:, k:k + 1], (tm, LANES)) for k in range(TOP_K)]
        for s in range(ROW_TILE):
            for tok in range(s * per_chunk, (s + 1) * per_chunk):
                for k in range(TOP_K):
                    row_copy(tok, k, nxt, nxt_sem)
            routed_lo = jnp.zeros((tm, LANES), F32)
            routed_hi = jnp.zeros((tm, LANES), F32)
            for k in range(TOP_K):
                lo, hi = _unpack_bf16_halves(cur[k, pl.ds(s, tm, stride=ROW_TILE), :])
                routed_lo = routed_lo + weights[k] * lo
                routed_hi = routed_hi + weights[k] * hi
            cl = slice(s * LANES, (s + 1) * LANES)
            ch = slice(half + s * LANES, half + (s + 1) * LANES)
            o_ref[:, cl] = x_ref[:, cl] + g_ref[:, cl] * (routed_lo + sh_ref[:, cl])
            o_ref[:, ch] = x_ref[:, ch] + g_ref[:, ch] * (routed_hi + sh_ref[:, ch])

        @pl.when(i + 1 == pl.num_programs(0))
        def _():
            wait_rows(nxt, nxt_sem)

    @pl.when(i % 2 == 0)
    def _():
        step(rows_a, row_sems.at[0], rows_b, row_sems.at[1])

    @pl.when(i % 2 == 1)
    def _():
        step(rows_b, row_sems.at[1], rows_a, row_sems.at[0])


def _combine(dest_t, w_tok, x, shared, gate, ybuf):
    t, d = x.shape
    tm = min(t, 256)
    n_tiles = t // tm
    return pl.pallas_call(
        _combine_kernel,
        out_shape=jax.ShapeDtypeStruct((t, d), F32),
        grid=(n_tiles,),
        in_specs=[pl.BlockSpec((TOP_K, tm), lambda i: (0, i)),
                  pl.BlockSpec((TOP_K, tm), lambda i: (0, jnp.minimum(i + 1, n_tiles - 1))),
                  pl.BlockSpec((tm, TOP_K), lambda i: (i, 0)),
                  pl.BlockSpec((tm, d), lambda i: (i, 0)),
                  pl.BlockSpec((tm, d), lambda i: (i, 0)),
                  pl.BlockSpec((1, d), lambda i: (0, 0)),
                  pl.BlockSpec(memory_space=pl.ANY)],
        out_specs=pl.BlockSpec((tm, d), lambda i: (i, 0)),
        scratch_shapes=[pltpu.VMEM((TOP_K, tm * ROW_TILE, LANES), U32), pltpu.VMEM((TOP_K, tm * ROW_TILE, LANES), U32),
                        pltpu.SMEM((TOP_K, tm), I32), pltpu.SemaphoreType.DMA, pltpu.SemaphoreType.DMA((2,))],
        compiler_params=_params("arbitrary"),
        name="moe_combine",
    )(dest_t, dest_t, w_tok, x, shared, gate, ybuf)


def _moe_layer(x, shift, scale, gate, layer, w_router, router_bias, w_gate_up, w_down, ws_gate_up, ws_down):
    t, d = x.shape
    assert d == 2 * ROW_TILE * LANES, "a packed row must be exactly one (8, 128) tile of 32-bit words"
    npad = t * TOP_K + N_EXPERTS * E_BLOCK
    nblk = npad // E_BLOCK
    hf, idx_t, wts_t, rank_t, counts = _moe_pre(x, shift, scale, w_router, router_bias)
    dest_t, block_e, nused, tail_start = _moe_dest(idx_t, rank_t, counts, nblk)
    nused = nused.reshape(-1)[:1]
    xbuf, shared = _dispatch(hf, dest_t, tail_start.reshape(-1), nused, ws_gate_up.astype(BF16),
                             ws_down.astype(BF16), npad)
    ybuf = _experts(xbuf, block_e.reshape(-1), nused, w_gate_up, w_down, layer)
    return _combine(dest_t, wts_t.T, x, shared, gate, ybuf)


def _final_norm_kernel(x_ref, g_ref, o_ref):
    x = x_ref[...]
    o_ref[...] = x * lax.rsqrt(jnp.mean(x * x, axis=-1, keepdims=True) + EPS) * g_ref[...]


def _final_norm(x, gain):
    t, d = x.shape
    tm = min(t, 1024)
    return pl.pallas_call(
        _final_norm_kernel,
        out_shape=jax.ShapeDtypeStruct((t, d), F32),
        grid=(t // tm,),
        in_specs=[pl.BlockSpec((tm, d), lambda i: (i, 0)), pl.BlockSpec((1, d), lambda i: (0, 0))],
        out_specs=pl.BlockSpec((tm, d), lambda i: (i, 0)),
        compiler_params=_params("parallel"),
        name="final_norm",
    )(x, gain.reshape(1, d))


def _even_layer(x, mod, cos, sin, w_in, b_if, mlstm_norm, q_norm, kv_norm, w_uq, w_ukv, w_out):
    t, d = x.shape
    sh1, sc1, g1 = mod[0], mod[1], mod[2]
    nq, nv = M_HEADS * M_QK, M_HEADS * M_V
    o = 0
    cols = []
    for sz in (nq, nq, nv, nv, M_HEADS, M_HEADS, A_Q_LORA, A_KV_LORA, A_ROPE):
        cols.append(w_in[:, o:o + sz])
        o += sz
    mq, mk, mv, mo, mi, mf, cq, ckv, kr = cols
    w_a = jnp.concatenate([mq, mk, mv, mo], axis=1).astype(BF16)
    zeros = lambda n: jnp.zeros((d, n), F32)
    w_b = jnp.concatenate([cq, ckv, kr, zeros(LANES - A_ROPE), mi, mf, zeros(LANES - 2 * M_HEADS)], axis=1).astype(BF16)
    gate_col_block = (A_Q_LORA + A_KV_LORA + LANES) // LANES
    proj_a = _norm_matmul(x, sh1, sc1, w_a, jnp.zeros((1, w_a.shape[1]), F32), BF16, "even_in_a")
    proj_b = _norm_matmul(x, sh1, sc1, w_b, jnp.zeros((1, w_b.shape[1]), F32), F32, "even_in_b")
    gate_bias = jnp.zeros((1, LANES), F32).at[0, :2 * M_HEADS].set(b_if.astype(F32))
    hm = _mlstm(proj_a, proj_b, gate_col_block, gate_bias, mlstm_norm.astype(F32).reshape(1, nv))

    qk = A_NOPE + A_ROPE
    wq = jnp.pad(w_uq.reshape(A_Q_LORA, A_HEADS, qk), ((0, 0), (0, 0), (0, A_QK_PAD - qk)))
    wq = wq.reshape(A_Q_LORA, A_HEADS * A_QK_PAD).astype(BF16)
    q, k, v = _mla_up(proj_b, q_norm.astype(F32).reshape(1, -1), kv_norm.astype(F32).reshape(1, -1),
                      wq, w_ukv.astype(BF16), cos, sin)
    ha = _mla_flash(q, k, v)
    w_out = w_out.astype(BF16)
    return _proj_residual([hm, ha], [w_out[:nv], w_out[nv:]], x, g1, "even_out")


def _odd_layer(x, mod, cos, sin, w_qkv, b_qkv, sinks, w_o):
    sh1, sc1, g1 = mod[0], mod[1], mod[2]
    qkv = _norm_matmul(x, sh1, sc1, w_qkv.astype(BF16), b_qkv.astype(F32).reshape(1, -1), BF16, "odd_qkv")
    o = _swa(qkv, cos, sin, sinks)
    return _proj_residual([o], [w_o.astype(BF16)], x, g1, "odd_out")


def kernel(x, c, positions, w_ada, b_ada, a_w_in, a_b_if, a_mlstm_norm, a_q_norm, a_kv_norm, a_w_uq, a_w_ukv,
           a_w_out, s_w_qkv, s_b_qkv, s_sinks, s_w_o, e_w_router, e_router_bias, e_w_gate_up, e_w_down,
           e_ws_gate_up, e_ws_down, final_norm):
    batch, t, d = x.shape
    assert batch == 1, "kernels are written for a single sequence"
    depth = w_ada.shape[0]
    xs = x.reshape(t, d)
    mods = _ada(c, w_ada, b_ada).reshape(depth, 6, 1, d)
    cos, sin = _rope_tables(positions.reshape(t))
    for layer in range(depth):
        mod = mods[layer]
        if layer % 2 == 0:
            e = layer // 2
            xs = _even_layer(xs, mod, cos, sin, a_w_in[e], a_b_if[e], a_mlstm_norm[e], a_q_norm[e], a_kv_norm[e],
                             a_w_uq[e], a_w_ukv[e], a_w_out[e])
        else:
            o = layer // 2
            xs = _odd_layer(xs, mod, cos, sin, s_w_qkv[o], s_b_qkv[o], s_sinks[o], s_w_o[o])
        xs = _moe_layer(xs, mod[3], mod[4], mod[5], layer, e_w_router[layer], e_router_bias[layer],
                        e_w_gate_up, e_w_down, e_ws_gate_up[layer], e_ws_down[layer])
    return _final_norm(xs, final_norm.astype(F32)).reshape(batch, t, d)
```

```python
import functools

import jax
import jax.numpy as jnp
from jax import lax
from jax.experimental import pallas as pl
from jax.experimental.pallas import tpu as pltpu

F32 = jnp.float32
BF16 = jnp.bfloat16
I32 = jnp.int32
U32 = jnp.uint32
HIGHEST = lax.Precision.HIGHEST

EPS = 1e-6
ROPE_THETA = 10000.0
ROPE_DIM = 64

M_HEADS = 4
M_QK = 128
M_V = 256
GATE_SOFTCAP = 15.0
MLSTM_CHUNK = 128

A_HEADS = 8
A_NOPE = 128
A_ROPE = 64
A_V = 128
A_Q_LORA = 768
A_KV_LORA = 512
A_QK_PAD = 256

S_HEADS = 32
S_KV_HEADS = 4
S_HD = 64
S_BLOCK = 128

N_EXPERTS = 64
TOP_K = 8
N_GROUPS = 8
TOPK_GROUPS = 4
E_FF = 256
SHARED_FF = 256
ROUTED_SCALE = 2.5
E_BLOCK = 512

LANES = 128
VMEM_LIMIT_BYTES = 48 * 1024 * 1024
FLASH_VMEM_LIMIT_BYTES = 56 * 1024 * 1024
LOG2_E = 1.4426950408889634


def _params(*semantics):
    return pltpu.CompilerParams(dimension_semantics=semantics, vmem_limit_bytes=VMEM_LIMIT_BYTES)


def _dot(a, b, precision=None):
    return jnp.dot(a, b, preferred_element_type=F32, precision=precision)


def _dot_nt(a, b):
    return lax.dot_general(a, b, (((1,), (1,)), ((), ())), preferred_element_type=F32)


def _dot_tn(a, b):
    return lax.dot_general(a, b, (((0,), (0,)), ((), ())), preferred_element_type=F32)


def _modulated_norm(x, shift, scale):
    y = x * lax.rsqrt(jnp.mean(x * x, axis=-1, keepdims=True) + EPS)
    return y * (1.0 + scale) + shift


def _rope_lanes(x, cos, sin_signed):
    lane = lax.broadcasted_iota(I32, x.shape, 1)
    first_half = (lane % ROPE_DIM) < (ROPE_DIM // 2)
    swapped = jnp.where(first_half, pltpu.roll(x, LANES - ROPE_DIM // 2, 1), pltpu.roll(x, ROPE_DIM // 2, 1))
    return x * cos + swapped * sin_signed


def _pack_bf16_halves(x):
    n = x.shape[1] // 2
    lo = lax.bitcast_convert_type(x[:, :n].astype(BF16).astype(F32), U32)
    hi = lax.bitcast_convert_type(x[:, n:].astype(BF16).astype(F32), U32)
    return (hi & jnp.uint32(0xFFFF0000)) | (lo >> 16)


def _unpack_bf16_halves(w):
    lo = lax.bitcast_convert_type(w << 16, F32)
    hi = lax.bitcast_convert_type(w & jnp.uint32(0xFFFF0000), F32)
    return lo, hi


ROW_TILE = 8


def _load_rows(ref):
    m = ref.shape[0] // ROW_TILE
    return jnp.concatenate([ref[pl.ds(s, m, stride=ROW_TILE), :] for s in range(ROW_TILE)], axis=1)


def _store_rows(ref, x):
    m = x.shape[0]
    for s in range(ROW_TILE):
        ref[pl.ds(s, m, stride=ROW_TILE), :] = x[:, s * LANES:(s + 1) * LANES]


def _ada_kernel(c_ref, w_ref, b_ref, o_ref):
    c = c_ref[...]
    c_act = c * jax.nn.sigmoid(c)
    o_ref[...] = jnp.sum(c_act * w_ref[...], axis=0, keepdims=True) + b_ref[...]


def _ada(c, w_ada, b_ada):
    depth, d, n = w_ada.shape
    tn = 1024
    return pl.pallas_call(
        _ada_kernel,
        out_shape=jax.ShapeDtypeStruct((depth, 1, n), F32),
        grid=(depth, n // tn),
        in_specs=[pl.BlockSpec((d, 1), lambda l, j: (0, 0)),
                  pl.BlockSpec((None, d, tn), lambda l, j: (l, 0, j)),
                  pl.BlockSpec((None, 1, tn), lambda l, j: (l, 0, j))],
        out_specs=pl.BlockSpec((None, 1, tn), lambda l, j: (l, 0, j)),
        compiler_params=_params("parallel", "parallel"),
        name="ada_mod",
    )(c.reshape(d, 1), w_ada, b_ada.reshape(depth, 1, n))


def _rope_table_kernel(pos_ref, inv_ref, sign_ref, cos_ref, sin_ref):
    ang = pos_ref[...].astype(F32) * inv_ref[...]
    cos_ref[...] = jnp.cos(ang)
    sin_ref[...] = jnp.sin(ang) * sign_ref[...]


def _rope_tables(positions):
    t = positions.shape[0]
    half = ROPE_DIM // 2
    inv_freq = jnp.power(ROPE_THETA, -jnp.arange(half, dtype=F32) / half)
    inv = jnp.tile(inv_freq, LANES // half).reshape(1, LANES)
    sign = jnp.tile(jnp.concatenate([-jnp.ones((half,), F32), jnp.ones((half,), F32)]), LANES // ROPE_DIM)
    tm = min(t, 2048)
    return pl.pallas_call(
        _rope_table_kernel,
        out_shape=(jax.ShapeDtypeStruct((t, LANES), F32), jax.ShapeDtypeStruct((t, LANES), F32)),
        grid=(t // tm,),
        in_specs=[pl.BlockSpec((tm, 1), lambda i: (i, 0)),
                  pl.BlockSpec((1, LANES), lambda i: (0, 0)),
                  pl.BlockSpec((1, LANES), lambda i: (0, 0))],
        out_specs=(pl.BlockSpec((tm, LANES), lambda i: (i, 0)), pl.BlockSpec((tm, LANES), lambda i: (i, 0))),
        compiler_params=_params("parallel"),
        name="rope_tables",
    )(positions.reshape(t, 1), inv, sign.reshape(1, LANES))


def _norm_matmul_kernel(x_ref, sh_ref, sc_ref, w_ref, b_ref, o_ref, h_scr):
    @pl.when(pl.program_id(1) == 0)
    def _():
        h_scr[...] = _modulated_norm(x_ref[...], sh_ref[...], sc_ref[...]).astype(BF16)

    o_ref[...] = (_dot(h_scr[...], w_ref[...]) + b_ref[...]).astype(o_ref.dtype)


def _col_tile(n, cap):
    return max(c for c in range(LANES, min(n, cap) + 1, LANES) if n % c == 0)


def _norm_matmul(x, shift, scale, w, bias, out_dtype, name):
    t, d = x.shape
    n = w.shape[1]
    tm = min(t, 1024)
    tn = _col_tile(n, 1280)
    return pl.pallas_call(
        _norm_matmul_kernel,
        out_shape=jax.ShapeDtypeStruct((t, n), out_dtype),
        grid=(t // tm, n // tn),
        in_specs=[pl.BlockSpec((tm, d), lambda i, j: (i, 0)),
                  pl.BlockSpec((1, d), lambda i, j: (0, 0)),
                  pl.BlockSpec((1, d), lambda i, j: (0, 0)),
                  pl.BlockSpec((d, tn), lambda i, j: (0, j)),
                  pl.BlockSpec((1, tn), lambda i, j: (0, j))],
        out_specs=pl.BlockSpec((tm, tn), lambda i, j: (i, j)),
        scratch_shapes=[pltpu.VMEM((tm, d), BF16)],
        compiler_params=_params("parallel", "arbitrary"),
        name=name,
    )(x, shift, scale, w, bias)


def _norm_matmul_pair_kernel(x_ref, sh_ref, sc_ref, wa_ref, wb_ref, oa_ref, ob_ref, h_scr, *, steps_a):
    j = pl.program_id(1)

    @pl.when(j == 0)
    def _():
        h_scr[...] = _modulated_norm(x_ref[...], sh_ref[...], sc_ref[...]).astype(BF16)

    @pl.when(j < steps_a)
    def _():
        oa_ref[...] = _dot(h_scr[...], wa_ref[...]).astype(oa_ref.dtype)

    @pl.when(j >= steps_a)
    def _():
        ob_ref[...] = _dot(h_scr[...], wb_ref[...]).astype(ob_ref.dtype)


def _norm_matmul_pair(x, shift, scale, wa, wb, dtype_a, dtype_b, name):
    t, d = x.shape
    na, nb = wa.shape[1], wb.shape[1]
    tm = min(t, 1024)
    ta, tb = _col_tile(na, 768), _col_tile(nb, 768)
    steps_a, steps_b = na // ta, nb // tb
    col_a = lambda j: jnp.minimum(j, steps_a - 1)
    col_b = lambda j: jnp.maximum(j - steps_a, 0)
    return pl.pallas_call(
        functools.partial(_norm_matmul_pair_kernel, steps_a=steps_a),
        out_shape=(jax.ShapeDtypeStruct((t, na), dtype_a), jax.ShapeDtypeStruct((t, nb), dtype_b)),
        grid=(t // tm, steps_a + steps_b),
        in_specs=[pl.BlockSpec((tm, d), lambda i, j: (i, 0)),
                  pl.BlockSpec((1, d), lambda i, j: (0, 0)),
                  pl.BlockSpec((1, d), lambda i, j: (0, 0)),
                  pl.BlockSpec((d, ta), lambda i, j: (0, col_a(j))),
                  pl.BlockSpec((d, tb), lambda i, j: (0, col_b(j)))],
        out_specs=(pl.BlockSpec((tm, ta), lambda i, j: (i, col_a(j))),
                   pl.BlockSpec((tm, tb), lambda i, j: (i, col_b(j)))),
        scratch_shapes=[pltpu.VMEM((tm, d), BF16)],
        compiler_params=_params("parallel", "arbitrary"),
        name=name,
    )(x, shift, scale, wa, wb)


def _mlstm_kernel(q_ref, k_ref, v_ref, o_ref, g_ref, gb_ref, gain_ref, out_ref, c_scr, n_scr, m_scr):
    L = MLSTM_CHUNK
    tm = q_ref.shape[0]

    @pl.when(pl.program_id(0) == 0)
    def _():
        c_scr[...] = jnp.zeros_like(c_scr)
        n_scr[...] = jnp.zeros_like(n_scr)
        m_scr[...] = jnp.zeros_like(m_scr)

    capped = GATE_SOFTCAP * jnp.tanh((g_ref[...] + gb_ref[...]) / GATE_SOFTCAP)
    log_sig = jnp.minimum(capped, 0.0) - jnp.log1p(jnp.exp(-jnp.abs(capped)))
    lane = lax.broadcasted_iota(I32, capped.shape, 1)
    gate = jnp.where(lane < M_HEADS, capped, log_sig)
    r = lax.broadcasted_iota(I32, (tm, tm), 0)
    c = lax.broadcasted_iota(I32, (tm, tm), 1)
    chunk_tril = jnp.where(((r // L) == (c // L)) & (c <= r), 1.0, 0.0).astype(F32)
    cum = _dot(chunk_tril, gate, precision=HIGHEST)
    gate_rows = gate.T
    cum_rows = cum.T
    heads = range(M_HEADS)
    rr = lax.broadcasted_iota(I32, (M_HEADS * L, L), 0) % L
    cc = lax.broadcasted_iota(I32, (M_HEADS * L, L), 1)
    causal = cc <= rr

    def stack(parts):
        return jnp.concatenate(parts, axis=0)

    def rows_of(x, n=L):
        return stack([jnp.broadcast_to(x[h:h + 1, :], (n, LANES)) for h in heads])

    def wide(x):
        return jnp.concatenate([x] * (M_V // LANES), axis=1)

    c_states = [c_scr[h] for h in heads]
    n_states = [n_scr[h] for h in heads]
    m_prev = m_scr[0:M_HEADS, :]
    gain = stack([jnp.broadcast_to(gain_ref[:, h * M_V:(h + 1) * M_V], (L, M_V)) for h in heads])
    for ci in range(tm // L):
        sl = slice(ci * L, (ci + 1) * L)
        q_heads = [q_ref[sl, h * M_QK:(h + 1) * M_QK] for h in heads]
        k_all = k_ref[sl, :].astype(F32) * (M_QK ** -0.5)
        k_heads = [k_all[:, h * M_QK:(h + 1) * M_QK] for h in heads]
        kb_heads = [k.astype(BF16) for k in k_heads]
        v_heads = [v_ref[sl, h * M_V:(h + 1) * M_V] for h in heads]

        ig_col = stack([jnp.broadcast_to(gate[sl, h:h + 1], (L, LANES)) for h in heads])
        b_col = stack([jnp.broadcast_to(cum[sl, M_HEADS + h:M_HEADS + h + 1], (L, LANES)) for h in heads])
        ig_rows = gate_rows[0:M_HEADS, sl]
        b_rows = cum_rows[M_HEADS:2 * M_HEADS, sl]
        ig_row = rows_of(ig_rows)
        b_row = rows_of(b_rows)

        dm = jnp.where(causal, b_col - b_row + ig_row, -jnp.inf)
        inter = b_col + rows_of(m_prev)
        m_row = jnp.maximum(inter, jnp.broadcast_to(jnp.max(dm, axis=-1, keepdims=True), inter.shape))
        w_intra = jnp.exp(dm - m_row)
        w_inter = jnp.exp(inter - m_row)
        s = stack([_dot_nt(q_heads[h], kb_heads[h]) for h in heads]) * w_intra
        s_b = s.astype(BF16)
        intra = stack([_dot(s_b[h * L:(h + 1) * L], v_heads[h]) for h in heads])
        carried = stack([_dot_nt(q_heads[h], c_states[h].astype(BF16)) for h in heads])
        num = intra + wide(w_inter) * carried
        q_all = stack([q.astype(F32) for q in q_heads])
        n_all = stack([jnp.broadcast_to(n_states[h], (L, M_QK)) for h in heads])
        qn = jnp.broadcast_to(jnp.sum(q_all * n_all, axis=-1, keepdims=True), inter.shape)
        den = jnp.broadcast_to(jnp.sum(s, axis=-1, keepdims=True), inter.shape) + w_inter * qn
        hh = num / wide(jnp.maximum(jnp.abs(den), jnp.exp(-m_row)))

        b_last = jnp.broadcast_to(b_rows[:, L - 1:L], (M_HEADS, LANES))
        g_rows = b_last - b_rows + ig_rows
        m_new = jnp.maximum(b_last + m_prev, jnp.broadcast_to(jnp.max(g_rows, axis=-1, keepdims=True), b_last.shape))
        decay = jnp.exp(b_last + m_prev - m_new)
        ws_col = jnp.exp(rows_of(b_last) - b_col + ig_col - rows_of(m_new))
        for h in heads:
            ws_h = ws_col[h * L:(h + 1) * L]
            vw = (v_heads[h].astype(F32) * wide(ws_h)).astype(BF16)
            c_states[h] = (jnp.broadcast_to(decay[h:h + 1, :], (M_V, M_QK)) * c_states[h]
                           + _dot_tn(vw, kb_heads[h]))
            n_states[h] = decay[h:h + 1, :] * n_states[h] + jnp.sum(k_heads[h] * ws_h, axis=0, keepdims=True)
        m_prev = m_new

        inv_rms = lax.rsqrt(jnp.mean(hh * hh, axis=-1, keepdims=True) + EPS)
        o_gate = stack([o_ref[sl, h * M_V:(h + 1) * M_V] for h in heads]).astype(F32)
        y = hh * inv_rms * gain * jax.nn.sigmoid(o_gate)
        for h in heads:
            out_ref[sl, h * M_V:(h + 1) * M_V] = y[h * L:(h + 1) * L].astype(out_ref.dtype)
    for h in heads:
        c_scr[h], n_scr[h] = c_states[h], n_states[h]
    m_scr[0:M_HEADS, :] = m_prev


def _mlstm(proj_a, proj_b, gate_col_block, gate_bias, gain):
    t = proj_a.shape[0]
    tm = min(t, 512)
    nq = M_HEADS * M_QK
    nv = M_HEADS * M_V
    return pl.pallas_call(
        _mlstm_kernel,
        out_shape=jax.ShapeDtypeStruct((t, nv), BF16),
        grid=(t // tm,),
        in_specs=[pl.BlockSpec((tm, nq), lambda i: (i, 0)),
                  pl.BlockSpec((tm, nq), lambda i: (i, 1)),
                  pl.BlockSpec((tm, nv), lambda i: (i, 1)),
                  pl.BlockSpec((tm, nv), lambda i: (i, 2)),
                  pl.BlockSpec((tm, LANES), lambda i: (i, gate_col_block)),
                  pl.BlockSpec((1, LANES), lambda i: (0, 0)),
                  pl.BlockSpec((1, nv), lambda i: (0, 0))],
        out_specs=pl.BlockSpec((tm, nv), lambda i: (i, 0)),
        scratch_shapes=[pltpu.VMEM((M_HEADS, M_V, M_QK), F32),
                        pltpu.VMEM((M_HEADS, 1, M_QK), F32),
                        pltpu.VMEM((8, LANES), F32)],
        compiler_params=_params("arbitrary"),
        name="mlstm",
    )(proj_a, proj_a, proj_a, proj_a, proj_b, gate_bias, gain)


def _mla_up_kernel(pb_ref, qn_ref, kvn_ref, wq_ref, wkv_ref, cos_ref, sin_ref, q_ref, k_ref, v_ref):
    cos, sin = cos_ref[...], sin_ref[...]
    cq = pb_ref[:, 0:A_Q_LORA]
    cq = (cq * lax.rsqrt(jnp.mean(cq * cq, axis=-1, keepdims=True) + EPS) * qn_ref[...]).astype(BF16)
    ckv = pb_ref[:, A_Q_LORA:A_Q_LORA + A_KV_LORA]
    ckv = (ckv * lax.rsqrt(jnp.mean(ckv * ckv, axis=-1, keepdims=True) + EPS) * kvn_ref[...]).astype(BF16)
    k_pe = _rope_lanes(pb_ref[:, A_Q_LORA + A_KV_LORA:A_Q_LORA + A_KV_LORA + LANES], cos, sin).astype(BF16)
    scale = (A_NOPE + A_ROPE) ** -0.5 * LOG2_E
    for h in range(A_HEADS):
        qh = _dot(cq, wq_ref[:, h * A_QK_PAD:(h + 1) * A_QK_PAD])
        q_pe = _rope_lanes(qh[:, A_NOPE:], cos, sin)
        q_ref[h] = (jnp.concatenate([qh[:, :A_NOPE], q_pe], axis=1) * scale).astype(q_ref.dtype)
        kvh = _dot(ckv, wkv_ref[:, h * (A_NOPE + A_V):(h + 1) * (A_NOPE + A_V)])
        k_ref[h] = jnp.concatenate([kvh[:, :A_NOPE].astype(BF16), k_pe], axis=1)
        v_ref[h] = kvh[:, A_NOPE:].astype(v_ref.dtype)


def _mla_up(proj_b, q_norm, kv_norm, wq, wkv, cos, sin):
    t, nb = proj_b.shape
    tm = min(t, 512)
    return pl.pallas_call(
        _mla_up_kernel,
        out_shape=(jax.ShapeDtypeStruct((A_HEADS, t, A_QK_PAD), BF16),
                   jax.ShapeDtypeStruct((A_HEADS, t, A_QK_PAD), BF16),
                   jax.ShapeDtypeStruct((A_HEADS, t, A_V), BF16)),
        grid=(t // tm,),
        in_specs=[pl.BlockSpec((tm, nb), lambda i: (i, 0)),
                  pl.BlockSpec((1, A_Q_LORA), lambda i: (0, 0)),
                  pl.BlockSpec((1, A_KV_LORA), lambda i: (0, 0)),
                  pl.BlockSpec(wq.shape, lambda i: (0, 0)),
                  pl.BlockSpec(wkv.shape, lambda i: (0, 0)),
                  pl.BlockSpec((tm, LANES), lambda i: (i, 0)),
                  pl.BlockSpec((tm, LANES), lambda i: (i, 0))],
        out_specs=(pl.BlockSpec((A_HEADS, tm, A_QK_PAD), lambda i: (0, i, 0)),
                   pl.BlockSpec((A_HEADS, tm, A_QK_PAD), lambda i: (0, i, 0)),
                   pl.BlockSpec((A_HEADS, tm, A_V), lambda i: (0, i, 0))),
        compiler_params=_params("parallel"),
        name="mla_up",
    )(proj_b, q_norm, kv_norm, wq, wkv, cos, sin)


def _mla_flash_kernel(q_ref, k_ref, v_ref, o_ref, m_scr, l_scr, acc_scr, *, tq, tkb):
    i = pl.program_id(1)
    q = q_ref[...]
    m_scr[...] = jnp.full(m_scr.shape, -jnp.inf, F32)
    l_scr[...] = jnp.zeros(l_scr.shape, F32)
    acc_scr[...] = jnp.zeros(acc_scr.shape, F32)

    def tile(start, width, masked):
        s = _dot_nt(q, k_ref[pl.ds(start, width), :])
        if masked:
            row = lax.broadcasted_iota(I32, (tq, width), 0)
            col = lax.broadcasted_iota(I32, (tq, width), 1)
            s = jnp.where(col <= row, s, -jnp.inf)
        m = m_scr[...]
        m_new = jnp.maximum(m, jnp.broadcast_to(jnp.max(s, axis=-1, keepdims=True), m.shape))
        alpha = jnp.exp2(m - m_new)
        p = jnp.exp2(s - jnp.concatenate([m_new] * (width // LANES), axis=1))
        lane_sums = p[:, 0:LANES]
        for j in range(1, width // LANES):
            lane_sums = lane_sums + p[:, j * LANES:(j + 1) * LANES]
        l_scr[...] = alpha * l_scr[...] + lane_sums
        acc_scr[...] = alpha * acc_scr[...] + _dot(p.astype(BF16), v_ref[pl.ds(start, width), :])
        m_scr[...] = m_new

    per = tkb // tq
    n_big = i // per

    def body(j, carry):
        tile(pl.multiple_of(j * tkb, tkb), tkb, False)
        return carry

    lax.fori_loop(0, n_big, body, 0)
    rem = i - n_big * per
    for r in range(1, per):
        @pl.when(rem >= r)
        def _():
            tile(pl.multiple_of(n_big * tkb + (r - 1) * tq, tq), tq, False)
    tile(pl.multiple_of(i * tq, tq), tq, True)
    o_ref[...] = (acc_scr[...] / jnp.sum(l_scr[...], axis=-1, keepdims=True)).astype(o_ref.dtype)


def _mla_flash(q, k, v):
    _, t, _ = q.shape
    tq = min(t, 1024)
    tkb = min(t, 2048)
    return pl.pallas_call(
        functools.partial(_mla_flash_kernel, tq=tq, tkb=tkb),
        out_shape=jax.ShapeDtypeStruct((t, A_HEADS * A_V), BF16),
        grid=(A_HEADS, t // tq),
        in_specs=[pl.BlockSpec((None, tq, A_QK_PAD), lambda h, i: (h, i, 0)),
                  pl.BlockSpec((None, t, A_QK_PAD), lambda h, i: (h, 0, 0)),
                  pl.BlockSpec((None, t, A_V), lambda h, i: (h, 0, 0))],
        out_specs=pl.BlockSpec((tq, A_V), lambda h, i: (i, h)),
        scratch_shapes=[pltpu.VMEM((tq, LANES), F32), pltpu.VMEM((tq, LANES), F32), pltpu.VMEM((tq, A_V), F32)],
        compiler_params=pltpu.CompilerParams(dimension_semantics=("parallel", "arbitrary"),
                                             vmem_limit_bytes=FLASH_VMEM_LIMIT_BYTES),
        name="mla_flash",
    )(q, k, v)


def _proj_residual_kernel(*refs, n_lhs):
    lhs = refs[:n_lhs]
    ws = refs[n_lhs:2 * n_lhs]
    x_ref, g_ref, o_ref = refs[2 * n_lhs:]
    acc = _dot(lhs[0][...], ws[0][...])
    for a, w in zip(lhs[1:], ws[1:]):
        acc = acc + _dot(a[...], w[...])
    o_ref[...] = x_ref[...] + g_ref[...] * acc


def _proj_residual(lhs_list, w_list, x, gate, name):
    t, d = x.shape
    tm = min(t, 1024)
    tn = _col_tile(d, 1024)
    n_lhs = len(lhs_list)
    in_specs = ([pl.BlockSpec((tm, a.shape[1]), lambda i, j: (i, 0)) for a in lhs_list]
                + [pl.BlockSpec((w.shape[0], tn), lambda i, j: (0, j)) for w in w_list]
                + [pl.BlockSpec((tm, tn), lambda i, j: (i, j)), pl.BlockSpec((1, tn), lambda i, j: (0, j))])
    return pl.pallas_call(
        functools.partial(_proj_residual_kernel, n_lhs=n_lhs),
        out_shape=jax.ShapeDtypeStruct((t, d), F32),
        grid=(t // tm, d // tn),
        in_specs=in_specs,
        out_specs=pl.BlockSpec((tm, tn), lambda i, j: (i, j)),
        compiler_params=_params("parallel", "parallel"),
        name=name,
    )(*lhs_list, *w_list, x, gate)


def _swa_kernel(q_ref, kc_ref, kp_ref, vc_ref, vp_ref, cosc_ref, sinc_ref, cosp_ref, sinp_ref, sink_ref, o_ref):
    i = pl.program_id(0)
    qb = S_BLOCK
    group = S_HEADS // S_KV_HEADS
    pairs = group // 2
    cos_c, sin_c = cosc_ref[...], sinc_ref[...]
    cos_w = jnp.concatenate([cosp_ref[...], cos_c], axis=0)
    sin_w = jnp.concatenate([sinp_ref[...], sin_c], axis=0)
    kw = jnp.concatenate([kp_ref[...], kc_ref[...]], axis=0).astype(F32)
    vw = jnp.concatenate([vp_ref[...], vc_ref[...]], axis=0)

    r = lax.broadcasted_iota(I32, (pairs * qb, 2 * qb), 0) % qb
    c = lax.broadcasted_iota(I32, (pairs * qb, 2 * qb), 1)
    dist = qb + r - c
    valid = (dist >= 0) & (dist < S_BLOCK) & ((c >= qb) | (i > 0))
    lane = lax.broadcasted_iota(I32, (2 * qb, LANES), 1)
    low = lane < S_HD
    low_rows = lax.broadcasted_iota(I32, (pairs * qb, LANES), 1) < S_HD
    ones_d = jnp.concatenate([jnp.where(low, 1.0, 0.0), jnp.where(low, 0.0, 1.0)], axis=0).astype(BF16)

    for g in range(S_KV_HEADS):
        col = slice((g // 2) * LANES, (g // 2 + 1) * LANES)
        k_pair = _rope_lanes(kw[:, col], cos_w, sin_w)
        v_pair = vw[:, col]
        keep = low if g % 2 == 0 else jnp.logical_not(low)
        k_own = jnp.where(keep, k_pair, 0.0)
        v_own = jnp.where(keep, v_pair.astype(F32), 0.0)
        k_other = pltpu.roll(k_own, S_HD, 1)
        v_other = pltpu.roll(v_own, S_HD, 1)
        k_lo, k_hi = (k_own, k_other) if g % 2 == 0 else (k_other, k_own)
        v_lo, v_hi = (v_own, v_other) if g % 2 == 0 else (v_other, v_own)
        kd = jnp.concatenate([k_lo, k_hi], axis=0).astype(BF16)
        vd = jnp.concatenate([v_lo, v_hi], axis=0).astype(BF16)
        qs = []
        for p in range(pairs):
            qcol = slice((g * pairs + p) * LANES, (g * pairs + p + 1) * LANES)
            qs.append(_rope_lanes(q_ref[:, qcol].astype(F32), cos_c, sin_c) * (S_HD ** -0.5 * LOG2_E))
        qg = jnp.concatenate(qs, axis=0).astype(BF16)
        s = _dot_nt(qg, kd)
        exps, sink_terms = [], []
        for half in range(2):
            sh = jnp.where(valid, s[:, half * 2 * qb:(half + 1) * 2 * qb], -jnp.inf)
            sink = sink_ref[g, half]
            m = jnp.maximum(jnp.broadcast_to(jnp.max(sh, axis=-1, keepdims=True), sink.shape), sink)
            exps.append(jnp.exp2(sh - jnp.concatenate([m, m], axis=1)).astype(BF16))
            sink_terms.append(jnp.exp2(sink - m))
        e_all = jnp.concatenate(exps, axis=1)
        den = _dot(e_all, ones_d) + jnp.where(low_rows, sink_terms[0], sink_terms[1])
        o = _dot(e_all, vd) / den
        for p in range(pairs):
            qcol = slice((g * pairs + p) * LANES, (g * pairs + p + 1) * LANES)
            o_ref[:, qcol] = o[p * qb:(p + 1) * qb].astype(o_ref.dtype)


def _swa(qkv, cos, sin, sinks):
    t = qkv.shape[0]
    qb = S_BLOCK
    nq = S_HEADS * S_HD
    nk = S_KV_HEADS * S_HD
    group = S_HEADS // S_KV_HEADS
    pairs = group // 2
    sink_cols = jnp.repeat((sinks.astype(F32) * LOG2_E).reshape(S_KV_HEADS, pairs, 2).transpose(0, 2, 1), qb, axis=-1)
    sink_cols = jnp.broadcast_to(sink_cols.reshape(S_KV_HEADS, 2, pairs * qb, 1), (S_KV_HEADS, 2, pairs * qb, LANES))
    k_blk = nq // nk
    prev = lambda i: jnp.maximum(i - 1, 0)
    return pl.pallas_call(
        _swa_kernel,
        out_shape=jax.ShapeDtypeStruct((t, nq), BF16),
        grid=(t // qb,),
        in_specs=[pl.BlockSpec((qb, nq), lambda i: (i, 0)),
                  pl.BlockSpec((qb, nk), lambda i: (i, k_blk)),
                  pl.BlockSpec((qb, nk), lambda i: (prev(i), k_blk)),
                  pl.BlockSpec((qb, nk), lambda i: (i, k_blk + 1)),
                  pl.BlockSpec((qb, nk), lambda i: (prev(i), k_blk + 1)),
                  pl.BlockSpec((qb, LANES), lambda i: (i, 0)),
                  pl.BlockSpec((qb, LANES), lambda i: (i, 0)),
                  pl.BlockSpec((qb, LANES), lambda i: (prev(i), 0)),
                  pl.BlockSpec((qb, LANES), lambda i: (prev(i), 0)),
                  pl.BlockSpec((S_KV_HEADS, 2, pairs * qb, LANES), lambda i: (0, 0, 0, 0))],
        out_specs=pl.BlockSpec((qb, nq), lambda i: (i, 0)),
        compiler_params=_params("parallel"),
        name="swa",
    )(qkv, qkv, qkv, qkv, qkv, cos, sin, cos, sin, sink_cols)


def _expert_of_row(p):
    per_group = N_EXPERTS // N_GROUPS
    return (p % N_GROUPS) * per_group + p // N_GROUPS


def _moe_pre_kernel(x_ref, sh_ref, sc_ref, wr_ref, rb_ref, hf_ref, idx_ref, wt_ref, rank_ref, cnt_ref, carry_scr):
    tm = x_ref.shape[0]
    per_group = N_EXPERTS // N_GROUPS

    @pl.when(pl.program_id(0) == 0)
    def _():
        carry_scr[...] = jnp.zeros_like(carry_scr)

    h = _modulated_norm(x_ref[...], sh_ref[...], sc_ref[...])
    _store_rows(hf_ref, _pack_bf16_halves(h))
    h_hi = h.astype(BF16)
    h_lo = (h - h_hi.astype(F32)).astype(BF16)
    logits = _dot(h_hi, wr_ref[0]) + (_dot(h_hi, wr_ref[1]) + _dot(h_lo, wr_ref[0]))
    scores = jax.nn.sigmoid(logits.T[:N_EXPERTS, :])
    biased = scores + rb_ref[...]

    members = [biased[j * N_GROUPS:(j + 1) * N_GROUPS, :] for j in range(per_group)]
    m1 = members[0]
    for a in members[1:]:
        m1 = jnp.maximum(m1, a)
    first = jnp.full(m1.shape, per_group, I32)
    for j in reversed(range(per_group)):
        first = jnp.where(members[j] == m1, j, first)
    m2 = jnp.full(m1.shape, -jnp.inf, F32)
    for j in range(per_group):
        m2 = jnp.maximum(m2, jnp.where(first == j, -jnp.inf, members[j]))
    group_score = m1 + m2

    g_iota = lax.broadcasted_iota(I32, group_score.shape, 0).astype(F32)
    g_sel = jnp.zeros(group_score.shape, F32)
    for _ in range(TOPK_GROUPS):
        best = jnp.max(group_score, axis=0, keepdims=True)
        gi = jnp.min(jnp.where(group_score == best, g_iota, float(N_GROUPS)), axis=0, keepdims=True)
        hit = g_iota == gi
        g_sel = jnp.where(hit, 1.0, g_sel)
        group_score = jnp.where(hit, -jnp.inf, group_score)
    masked = jnp.concatenate([jnp.where(g_sel > 0.5, a, -jnp.inf) for a in members], axis=0)

    e_iota = _expert_of_row(lax.broadcasted_iota(I32, masked.shape, 0)).astype(F32)
    sel = jnp.zeros(masked.shape, F32)
    idx_rows, w_rows = [], []
    for _ in range(TOP_K):
        best = jnp.max(masked, axis=0, keepdims=True)
        ei = jnp.min(jnp.where(masked == best, e_iota, float(N_EXPERTS)), axis=0, keepdims=True)
        hit = e_iota == ei
        idx_rows.append(ei)
        w_rows.append(jnp.sum(jnp.where(hit, scores, 0.0), axis=0, keepdims=True))
        sel = jnp.where(hit, 1.0, sel)
        masked = jnp.where(hit, -jnp.inf, masked)
    idx = jnp.concatenate(idx_rows, axis=0).astype(I32)
    wts = jnp.concatenate(w_rows, axis=0)
    wts = wts / jnp.sum(wts, axis=0, keepdims=True) * ROUTED_SCALE

    r = lax.broadcasted_iota(I32, (tm, tm), 0)
    c = lax.broadcasted_iota(I32, (tm, tm), 1)
    before = jnp.where(r < c, 1.0, 0.0).astype(BF16)
    rank_excl = carry_scr[:, 0:1] + _dot(sel.astype(BF16), before)
    rank_rows = [jnp.sum(jnp.where(e_iota == idx_rows[k], rank_excl, 0.0), axis=0, keepdims=True)
                 for k in range(TOP_K)]
    carry_scr[...] = carry_scr[...] + jnp.sum(sel, axis=1, keepdims=True)

    idx_ref[...] = idx
    wt_ref[...] = wts
    rank_ref[...] = jnp.concatenate(rank_rows, axis=0).astype(I32)
    cnt_ref[...] = carry_scr[...]


def _moe_pre(x, shift, scale, w_router, router_bias):
    t, d = x.shape
    tm = min(t, 512)
    rows = jnp.arange(N_EXPERTS)
    perm = _expert_of_row(rows)
    wr = jnp.zeros((d, LANES), F32).at[:, :N_EXPERTS].set(w_router[:, perm])
    wr_hi = wr.astype(BF16)
    wr = jnp.stack([wr_hi, (wr - wr_hi.astype(F32)).astype(BF16)])
    rb = router_bias.astype(F32)[perm].reshape(N_EXPERTS, 1)
    tok = lambda n, dt: jax.ShapeDtypeStruct((TOP_K, t), dt)
    return pl.pallas_call(
        _moe_pre_kernel,
        out_shape=(jax.ShapeDtypeStruct((t * ROW_TILE, LANES), U32), tok(t, I32), tok(t, F32), tok(t, I32),
                   jax.ShapeDtypeStruct((N_EXPERTS, LANES), F32)),
        grid=(t // tm,),
        in_specs=[pl.BlockSpec((tm, d), lambda i: (i, 0)),
                  pl.BlockSpec((1, d), lambda i: (0, 0)),
                  pl.BlockSpec((1, d), lambda i: (0, 0)),
                  pl.BlockSpec((2, d, LANES), lambda i: (0, 0, 0)),
                  pl.BlockSpec((N_EXPERTS, 1), lambda i: (0, 0))],
        out_specs=(pl.BlockSpec((tm * ROW_TILE, LANES), lambda i: (i, 0)),
                   pl.BlockSpec((TOP_K, tm), lambda i: (0, i)),
                   pl.BlockSpec((TOP_K, tm), lambda i: (0, i)),
                   pl.BlockSpec((TOP_K, tm), lambda i: (0, i)),
                   pl.BlockSpec((N_EXPERTS, LANES), lambda i: (0, 0))),
        scratch_shapes=[pltpu.VMEM((N_EXPERTS, LANES), F32)],
        compiler_params=_params("arbitrary"),
        name="moe_router",
    )(x, shift, scale, wr, rb)


def _moe_dest_kernel(idx_ref, rank_ref, cnt_ref, dest_ref, be_ref, nused_ref, tail_ref):
    cnt = cnt_ref[...]
    padded = jnp.floor((cnt + (E_BLOCK - 1)) * (1.0 / E_BLOCK)) * E_BLOCK
    e_i = _expert_of_row(lax.broadcasted_iota(I32, (N_EXPERTS, N_EXPERTS), 0))
    e_j = _expert_of_row(lax.broadcasted_iota(I32, (N_EXPERTS, N_EXPERTS), 1))
    earlier = jnp.where(e_j < e_i, 1.0, 0.0).astype(F32)
    pstart = _dot(earlier, padded, precision=HIGHEST)
    pend = pstart + padded

    tm = idx_ref.shape[1]
    e_col = _expert_of_row(lax.broadcasted_iota(I32, (N_EXPERTS, tm), 0))
    rows = []
    for k in range(TOP_K):
        hit = e_col == idx_ref[k:k + 1, :]
        rows.append(jnp.sum(jnp.where(hit, pstart[:, 0:1], 0.0), axis=0, keepdims=True))
    dest_ref[...] = jnp.concatenate(rows, axis=0).astype(I32) + rank_ref[...]

    nb = be_ref.shape[1]
    block_start = (lax.broadcasted_iota(I32, (N_EXPERTS, nb), 1) * E_BLOCK).astype(F32)
    ended = jnp.sum(jnp.where(pend[:, 0:1] <= block_start, 1.0, 0.0), axis=0, keepdims=True)
    be_ref[...] = jnp.minimum(ended, float(N_EXPERTS - 1)).astype(I32)
    nused_ref[...] = (jnp.sum(padded, axis=0, keepdims=True) * (1.0 / E_BLOCK)).astype(I32)
    e_row = _expert_of_row(lax.broadcasted_iota(I32, (N_EXPERTS, LANES), 0))
    lane = lax.broadcasted_iota(I32, (N_EXPERTS, LANES), 1)
    end_lane = jnp.sum(jnp.where(e_row <= lane, padded, 0.0), axis=0, keepdims=True)
    own_lane = jnp.sum(jnp.where(e_row == lane, padded, 0.0), axis=0, keepdims=True)
    tail_ref[...] = jnp.where(own_lane > 0.0, end_lane - E_BLOCK, -1.0).astype(I32)


def _moe_dest(idx_t, rank_t, counts, nblk):
    t = idx_t.shape[1]
    tm = min(t, 2048)
    nb = ((nblk + LANES - 1) // LANES) * LANES
    return pl.pallas_call(
        _moe_dest_kernel,
        out_shape=(jax.ShapeDtypeStruct((TOP_K, t), I32),
                   jax.ShapeDtypeStruct((1, nb), I32),
                   jax.ShapeDtypeStruct((1, LANES), I32),
                   jax.ShapeDtypeStruct((1, LANES), I32)),
        grid=(t // tm,),
        in_specs=[pl.BlockSpec((TOP_K, tm), lambda i: (0, i)),
                  pl.BlockSpec((TOP_K, tm), lambda i: (0, i)),
                  pl.BlockSpec((N_EXPERTS, LANES), lambda i: (0, 0))],
        out_specs=(pl.BlockSpec((TOP_K, tm), lambda i: (0, i)),
                   pl.BlockSpec((1, nb), lambda i: (0, 0)),
                   pl.BlockSpec((1, LANES), lambda i: (0, 0)),
                   pl.BlockSpec((1, LANES), lambda i: (0, 0))),
        compiler_params=_params("arbitrary"),
        name="moe_dest",
    )(idx_t, rank_t, counts)


def _dispatch_kernel(tail_ref, nused_ref, dest_ref, hf_ref, wgu_ref, wd_ref, xbuf_hbm, sh_ref, dest_smem, zero_scr,
                     idx_sem, zero_sem, row_sem):
    tm = dest_ref.shape[1]
    block_sublanes = E_BLOCK * ROW_TILE
    nblk = xbuf_hbm.shape[0] // block_sublanes

    @pl.when(pl.program_id(0) == 0)
    def _():
        zero_scr[...] = jnp.zeros_like(zero_scr)

        def zero_copy(row):
            start = pl.multiple_of(row * ROW_TILE, block_sublanes)
            return pltpu.make_async_copy(zero_scr, xbuf_hbm.at[pl.ds(start, block_sublanes)], zero_sem)

        def start(e, carry):
            @pl.when(tail_ref[e] >= 0)
            def _():
                zero_copy(tail_ref[e]).start()
            return carry

        def wait(e, carry):
            @pl.when(tail_ref[e] >= 0)
            def _():
                zero_copy(0).wait()
            return carry

        def start_unused(b, carry):
            zero_copy(b * E_BLOCK).start()
            return carry

        def wait_unused(b, carry):
            zero_copy(0).wait()
            return carry

        lax.fori_loop(0, N_EXPERTS, start, 0)
        lax.fori_loop(nused_ref[0], nblk, start_unused, 0)
        lax.fori_loop(0, N_EXPERTS, wait, 0)
        lax.fori_loop(nused_ref[0], nblk, wait_unused, 0)

    load = pltpu.make_async_copy(dest_ref, dest_smem, idx_sem)
    load.start()
    load.wait()

    def row(ref, r):
        return ref.at[pl.ds(pl.multiple_of(r * ROW_TILE, ROW_TILE), ROW_TILE)]

    for tok in range(tm):
        for k in range(TOP_K):
            pltpu.make_async_copy(row(hf_ref, tok), row(xbuf_hbm, dest_smem[k, tok]), row_sem).start(priority=k % 2)
    sh_ref[...] = _gated_ffn(_load_rows(hf_ref), wgu_ref, wd_ref)
    for k in range(TOP_K):
        pltpu.make_async_copy(hf_ref, xbuf_hbm.at[pl.ds(0, tm * ROW_TILE)], row_sem).wait()


def _dispatch(hf, dest_t, tail_start, nused, wgu, wd, npad):
    t = hf.shape[0] // ROW_TILE
    d = wgu.shape[0]
    tm = min(t, 256)
    return pl.pallas_call(
        _dispatch_kernel,
        out_shape=(jax.ShapeDtypeStruct((npad * ROW_TILE, LANES), U32), jax.ShapeDtypeStruct((t, d), F32)),
        grid_spec=pltpu.PrefetchScalarGridSpec(
            num_scalar_prefetch=2,
            grid=(t // tm,),
            in_specs=[pl.BlockSpec((TOP_K, tm), lambda i, tail, nu: (0, i)),
                      pl.BlockSpec((tm * ROW_TILE, LANES), lambda i, tail, nu: (i, 0)),
                      pl.BlockSpec(wgu.shape, lambda i, tail, nu: (0, 0)),
                      pl.BlockSpec(wd.shape, lambda i, tail, nu: (0, 0))],
            out_specs=(pl.BlockSpec(memory_space=pl.ANY), pl.BlockSpec((tm, d), lambda i, tail, nu: (i, 0))),
            scratch_shapes=[pltpu.SMEM((TOP_K, tm), I32), pltpu.VMEM((E_BLOCK * ROW_TILE, LANES), U32),
                            pltpu.SemaphoreType.DMA, pltpu.SemaphoreType.DMA, pltpu.SemaphoreType.DMA]),
        compiler_params=_params("arbitrary"),
        name="moe_dispatch",
    )(tail_start, nused, dest_t, hf, wgu, wd)


def _expert_kernel(be_ref, nused_ref, x_ref, wgu_ref, wd_ref, y_ref, wgu_scr, wd_scr):
    b = pl.program_id(0)
    e = be_ref[b]
    e_prev = be_ref[jnp.maximum(b - 1, 0)]

    @pl.when((b == 0) | (e != e_prev))
    def _():
        wgu_scr[...] = wgu_ref[...].astype(BF16)
        wd_scr[...] = wd_ref[...].astype(BF16)

    @pl.when(b < nused_ref[0])
    def _():
        _store_rows(y_ref, _pack_bf16_halves(_gated_ffn(_load_rows(x_ref), wgu_scr, wd_scr)))

    @pl.when(b >= nused_ref[0])
    def _():
        y_ref[...] = jnp.zeros_like(y_ref)


def _gated_ffn(x_packed, wgu_ref, wd_ref):
    x_lo, x_hi = _unpack_bf16_halves(x_packed)
    half = x_packed.shape[1]
    gu = _dot(x_lo.astype(BF16), wgu_ref[0:half, :]) + _dot(x_hi.astype(BF16), wgu_ref[half:2 * half, :])
    ff = gu.shape[1] // 2
    gate = gu[:, :ff]
    act = gate * jax.nn.sigmoid(gate) * gu[:, ff:]
    return _dot(act.astype(BF16), wd_ref[...])


def _experts(xbuf, block_e, nused, w_gate_up, w_down, layer):
    nblk = xbuf.shape[0] // (E_BLOCK * ROW_TILE)
    _, _, d, ff2 = w_gate_up.shape
    last = lambda b, nu: jnp.minimum(b, nu[0] - 1)
    return pl.pallas_call(
        _expert_kernel,
        out_shape=jax.ShapeDtypeStruct(xbuf.shape, U32),
        grid_spec=pltpu.PrefetchScalarGridSpec(
            num_scalar_prefetch=2,
            grid=(nblk,),
            in_specs=[pl.BlockSpec((E_BLOCK * ROW_TILE, LANES), lambda b, be, nu: (last(b, nu), 0)),
                      pl.BlockSpec((None, None, d, ff2), lambda b, be, nu: (layer, be[b], 0, 0)),
                      pl.BlockSpec((None, None, ff2 // 2, d), lambda b, be, nu: (layer, be[b], 0, 0))],
            out_specs=pl.BlockSpec((E_BLOCK * ROW_TILE, LANES), lambda b, be, nu: (b, 0)),
            scratch_shapes=[pltpu.VMEM((d, ff2), BF16), pltpu.VMEM((ff2 // 2, d), BF16)]),
        compiler_params=_params("arbitrary"),
        name="moe_experts",
    )(block_e, nused, xbuf, w_gate_up, w_down)


def _combine_kernel(dest_ref, dest_next_ref, w_ref, x_ref, sh_ref, g_ref, ybuf_hbm, o_ref, rows_a, rows_b,
                    dest_smem, idx_sem, row_sems):
    i = pl.program_id(0)
    tm = x_ref.shape[0]

    def tile(r):
        return pl.ds(pl.multiple_of(r * ROW_TILE, ROW_TILE), ROW_TILE)

    def load_indices(indices_ref):
        load = pltpu.make_async_copy(indices_ref, dest_smem, idx_sem)
        load.start()
        load.wait()

    def row_copy(tok, k, rows, sem):
        pltpu.make_async_copy(ybuf_hbm.at[tile(dest_smem[k, tok])], rows.at[k, tile(tok)], sem).start(priority=k % 2)

    def wait_rows(rows, sem):
        for k in range(TOP_K):
            pltpu.make_async_copy(ybuf_hbm.at[pl.ds(0, tm * ROW_TILE)], rows.at[k], sem).wait()

    @pl.when(i == 0)
    def _():
        load_indices(dest_ref)

        def issue(tok, carry):
            for k in range(TOP_K):
                row_copy(tok, k, rows_a, row_sems.at[0])
            return carry

        lax.fori_loop(0, tm, issue, 0)

    load_indices(dest_next_ref)
    half = ROW_TILE * LANES
    per_chunk = tm // ROW_TILE

    def step(cur, cur_sem, nxt, nxt_sem):
        wait_rows(cur, cur_sem)
        weights = [jnp.broadcast_to(w_ref[:, k:k + 1], (tm, LANES)) for k in range(TOP_K)]
        for s in range(ROW_TILE):
            for tok in range(s * per_chunk, (s + 1) * per_chunk):
                for k in range(TOP_K):
                    row_copy(tok, k, nxt, nxt_sem)
            routed_lo = jnp.zeros((tm, LANES), F32)
            routed_hi = jnp.zeros((tm, LANES), F32)
            for k in range(TOP_K):
                lo, hi = _unpack_bf16_halves(cur[k, pl.ds(s, tm, stride=ROW_TILE), :])
                routed_lo = routed_lo + weights[k] * lo
                routed_hi = routed_hi + weights[k] * hi
            cl = slice(s * LANES, (s + 1) * LANES)
            ch = slice(half + s * LANES, half + (s + 1) * LANES)
            o_ref[:, cl] = x_ref[:, cl] + g_ref[:, cl] * (routed_lo + sh_ref[:, cl])
            o_ref[:, ch] = x_ref[:, ch] + g_ref[:, ch] * (routed_hi + sh_ref[:, ch])

        @pl.when(i + 1 == pl.num_programs(0))
        def _():
            wait_rows(nxt, nxt_sem)

    @pl.when(i % 2 == 0)
    def _():
        step(rows_a, row_sems.at[0], rows_b, row_sems.at[1])

    @pl.when(i % 2 == 1)
    def _():
        step(rows_b, row_sems.at[1], rows_a, row_sems.at[0])


def _combine(dest_t, w_tok, x, shared, gate, ybuf):
    t, d = x.shape
    tm = min(t, 256)
    n_tiles = t // tm
    return pl.pallas_call(
        _combine_kernel,
        out_shape=jax.ShapeDtypeStruct((t, d), F32),
        grid=(n_tiles,),
        in_specs=[pl.BlockSpec((TOP_K, tm), lambda i: (0, i)),
                  pl.BlockSpec((TOP_K, tm), lambda i: (0, jnp.minimum(i + 1, n_tiles - 1))),
                  pl.BlockSpec((tm, TOP_K), lambda i: (i, 0)),
                  pl.BlockSpec((tm, d), lambda i: (i, 0)),
                  pl.BlockSpec((tm, d), lambda i: (i, 0)),
                  pl.BlockSpec((1, d), lambda i: (0, 0)),
                  pl.BlockSpec(memory_space=pl.ANY)],
        out_specs=pl.BlockSpec((tm, d), lambda i: (i, 0)),
        scratch_shapes=[pltpu.VMEM((TOP_K, tm * ROW_TILE, LANES), U32), pltpu.VMEM((TOP_K, tm * ROW_TILE, LANES), U32),
                        pltpu.SMEM((TOP_K, tm), I32), pltpu.SemaphoreType.DMA, pltpu.SemaphoreType.DMA((2,))],
        compiler_params=_params("arbitrary"),
        name="moe_combine",
    )(dest_t, dest_t, w_tok, x, shared, gate, ybuf)


def _moe_layer(x, shift, scale, gate, layer, w_router, router_bias, w_gate_up, w_down, ws_gate_up, ws_down):
    t, d = x.shape
    assert d == 2 * ROW_TILE * LANES, "a packed row must be exactly one (8, 128) tile of 32-bit words"
    npad = t * TOP_K + N_EXPERTS * E_BLOCK
    nblk = npad // E_BLOCK
    hf, idx_t, wts_t, rank_t, counts = _moe_pre(x, shift, scale, w_router, router_bias)
    dest_t, block_e, nused, tail_start = _moe_dest(idx_t, rank_t, counts, nblk)
    nused = nused.reshape(-1)[:1]
    xbuf, shared = _dispatch(hf, dest_t, tail_start.reshape(-1), nused, ws_gate_up.astype(BF16),
                             ws_down.astype(BF16), npad)
    ybuf = _experts(xbuf, block_e.reshape(-1), nused, w_gate_up, w_down, layer)
    return _combine(dest_t, wts_t.T, x, shared, gate, ybuf)


def _final_norm_kernel(x_ref, g_ref, o_ref):
    x = x_ref[...]
    o_ref[...] = x * lax.rsqrt(jnp.mean(x * x, axis=-1, keepdims=True) + EPS) * g_ref[...]


def _final_norm(x, gain):
    t, d = x.shape
    tm = min(t, 1024)
    return pl.pallas_call(
        _final_norm_kernel,
        out_shape=jax.ShapeDtypeStruct((t, d), F32),
        grid=(t // tm,),
        in_specs=[pl.BlockSpec((tm, d), lambda i: (i, 0)), pl.BlockSpec((1, d), lambda i: (0, 0))],
        out_specs=pl.BlockSpec((tm, d), lambda i: (i, 0)),
        compiler_params=_params("parallel"),
        name="final_norm",
    )(x, gain.reshape(1, d))


def _even_layer(x, mod, cos, sin, w_in, b_if, mlstm_norm, q_norm, kv_norm, w_uq, w_ukv, w_out):
    t, d = x.shape
    sh1, sc1, g1 = mod[0], mod[1], mod[2]
    nq, nv = M_HEADS * M_QK, M_HEADS * M_V
    o = 0
    cols = []
    for sz in (nq, nq, nv, nv, M_HEADS, M_HEADS, A_Q_LORA, A_KV_LORA, A_ROPE):
        cols.append(w_in[:, o:o + sz])
        o += sz
    mq, mk, mv, mo, mi, mf, cq, ckv, kr = cols
    w_a = jnp.concatenate([mq, mk, mv, mo], axis=1).astype(BF16)
    zeros = lambda n: jnp.zeros((d, n), F32)
    w_b = jnp.concatenate([cq, ckv, kr, zeros(LANES - A_ROPE), mi, mf, zeros(LANES - 2 * M_HEADS)], axis=1).astype(BF16)
    gate_col_block = (A_Q_LORA + A_KV_LORA + LANES) // LANES
    proj_a, proj_b = _norm_matmul_pair(x, sh1, sc1, w_a, w_b, BF16, F32, "even_in")
    gate_bias = jnp.zeros((1, LANES), F32).at[0, :2 * M_HEADS].set(b_if.astype(F32))
    hm = _mlstm(proj_a, proj_b, gate_col_block, gate_bias, mlstm_norm.astype(F32).reshape(1, nv))

    qk = A_NOPE + A_ROPE
    wq = jnp.pad(w_uq.reshape(A_Q_LORA, A_HEADS, qk), ((0, 0), (0, 0), (0, A_QK_PAD - qk)))
    wq = wq.reshape(A_Q_LORA, A_HEADS * A_QK_PAD).astype(BF16)
    q, k, v = _mla_up(proj_b, q_norm.astype(F32).reshape(1, -1), kv_norm.astype(F32).reshape(1, -1),
                      wq, w_ukv.astype(BF16), cos, sin)
    ha = _mla_flash(q, k, v)
    w_out = w_out.astype(BF16)
    return _proj_residual([hm, ha], [w_out[:nv], w_out[nv:]], x, g1, "even_out")


def _odd_layer(x, mod, cos, sin, w_qkv, b_qkv, sinks, w_o):
    sh1, sc1, g1 = mod[0], mod[1], mod[2]
    qkv = _norm_matmul(x, sh1, sc1, w_qkv.astype(BF16), b_qkv.astype(F32).reshape(1, -1), BF16, "odd_qkv")
    o = _swa(qkv, cos, sin, sinks)
    return _proj_residual([o], [w_o.astype(BF16)], x, g1, "odd_out")


def kernel(x, c, positions, w_ada, b_ada, a_w_in, a_b_if, a_mlstm_norm, a_q_norm, a_kv_norm, a_w_uq, a_w_ukv,
           a_w_out, s_w_qkv, s_b_qkv, s_sinks, s_w_o, e_w_router, e_router_bias, e_w_gate_up, e_w_down,
           e_ws_gate_up, e_ws_down, final_norm):
    batch, t, d = x.shape
    assert batch == 1, "kernels are written for a single sequence"
    depth = w_ada.shape[0]
    xs = x.reshape(t, d)
    mods = _ada(c, w_ada, b_ada).reshape(depth, 6, 1, d)
    cos, sin = _rope_tables(positions.reshape(t))
    for layer in range(depth):
        mod = mods[layer]
        if layer % 2 == 0:
            e = layer // 2
            xs = _even_layer(xs, mod, cos, sin, a_w_in[e], a_b_if[e], a_mlstm_norm[e], a_q_norm[e], a_kv_norm[e],
                             a_w_uq[e], a_w_ukv[e], a_w_out[e])
        else:
            o = layer // 2
            xs = _odd_layer(xs, mod, cos, sin, s_w_qkv[o], s_b_qkv[o], s_sinks[o], s_w_o[o])
        xs = _moe_layer(xs, mod[3], mod[4], mod[5], layer, e_w_router[layer], e_router_bias[layer],
                        e_w_gate_up, e_w_down, e_ws_gate_up[layer], e_ws_down[layer])
    return _final_norm(xs, final_norm.astype(F32)).reshape(batch, t, d)
```

```python
import functools

import jax
import jax.numpy as jnp
from jax import lax
from jax.experimental import pallas as pl
from jax.experimental.pallas import tpu as pltpu

F32 = jnp.float32
BF16 = jnp.bfloat16
I32 = jnp.int32
U32 = jnp.uint32
HIGHEST = lax.Precision.HIGHEST

EPS = 1e-6
ROPE_THETA = 10000.0
ROPE_DIM = 64

M_HEADS = 4
M_QK = 128
M_V = 256
GATE_SOFTCAP = 15.0
MLSTM_CHUNK = 128

A_HEADS = 8
A_NOPE = 128
A_ROPE = 64
A_V = 128
A_Q_LORA = 768
A_KV_LORA = 512
A_QK_PAD = 256

S_HEADS = 32
S_KV_HEADS = 4
S_HD = 64
S_BLOCK = 128

N_EXPERTS = 64
TOP_K = 8
N_GROUPS = 8
TOPK_GROUPS = 4
E_FF = 256
SHARED_FF = 256
ROUTED_SCALE = 2.5
E_BLOCK = 512

PROJ_ROW_TILE = 1024
PROJ_COL_CAP = 1280
PAIR_COL_CAP = 1024
RESIDUAL_COL_CAP = 1024
ADA_COL_TILE = 1024
ROPE_ROW_TILE = 2048
MLSTM_ROW_TILE = 512
MLA_UP_ROW_TILE = 512
FLASH_Q_TILE = 1024
FLASH_KV_TILE = 2048
ROUTER_ROW_TILE = 512
DEST_ROW_TILE = 2048
MOE_ROW_TILE = 256

LANES = 128
VMEM_LIMIT_BYTES = 48 * 1024 * 1024
LARGE_VMEM_LIMIT_BYTES = 56 * 1024 * 1024
LOG2_E = 1.4426950408889634


def _params(*semantics):
    return pltpu.CompilerParams(dimension_semantics=semantics, vmem_limit_bytes=VMEM_LIMIT_BYTES)


def _dot(a, b, precision=None):
    return jnp.dot(a, b, preferred_element_type=F32, precision=precision)


def _dot_nt(a, b):
    return lax.dot_general(a, b, (((1,), (1,)), ((), ())), preferred_element_type=F32)


def _dot_tn(a, b):
    return lax.dot_general(a, b, (((0,), (0,)), ((), ())), preferred_element_type=F32)


def _modulated_norm(x, shift, scale):
    y = x * lax.rsqrt(jnp.mean(x * x, axis=-1, keepdims=True) + EPS)
    return y * (1.0 + scale) + shift


def _rope_lanes(x, cos, sin_signed):
    lane = lax.broadcasted_iota(I32, x.shape, 1)
    first_half = (lane % ROPE_DIM) < (ROPE_DIM // 2)
    swapped = jnp.where(first_half, pltpu.roll(x, LANES - ROPE_DIM // 2, 1), pltpu.roll(x, ROPE_DIM // 2, 1))
    return x * cos + swapped * sin_signed


def _pack_bf16_halves(x):
    n = x.shape[1] // 2
    lo = lax.bitcast_convert_type(x[:, :n].astype(BF16).astype(F32), U32)
    hi = lax.bitcast_convert_type(x[:, n:].astype(BF16).astype(F32), U32)
    return (hi & jnp.uint32(0xFFFF0000)) | (lo >> 16)


def _unpack_bf16_halves(w):
    lo = lax.bitcast_convert_type(w << 16, F32)
    hi = lax.bitcast_convert_type(w & jnp.uint32(0xFFFF0000), F32)
    return lo, hi


ROW_TILE = 8


def _load_rows(ref):
    m = ref.shape[0] // ROW_TILE
    return jnp.concatenate([ref[pl.ds(s, m, stride=ROW_TILE), :] for s in range(ROW_TILE)], axis=1)


def _store_rows(ref, x):
    m = x.shape[0]
    for s in range(ROW_TILE):
        ref[pl.ds(s, m, stride=ROW_TILE), :] = x[:, s * LANES:(s + 1) * LANES]


def _ada_kernel(c_ref, w_ref, b_ref, o_ref):
    c = c_ref[...]
    c_act = c * jax.nn.sigmoid(c)
    o_ref[...] = jnp.sum(c_act * w_ref[...], axis=0, keepdims=True) + b_ref[...]


def _ada(c, w_ada, b_ada):
    depth, d, n = w_ada.shape
    tn = ADA_COL_TILE
    return pl.pallas_call(
        _ada_kernel,
        out_shape=jax.ShapeDtypeStruct((depth, 1, n), F32),
        grid=(depth, n // tn),
        in_specs=[pl.BlockSpec((d, 1), lambda l, j: (0, 0)),
                  pl.BlockSpec((None, d, tn), lambda l, j: (l, 0, j)),
                  pl.BlockSpec((None, 1, tn), lambda l, j: (l, 0, j))],
        out_specs=pl.BlockSpec((None, 1, tn), lambda l, j: (l, 0, j)),
        compiler_params=_params("parallel", "parallel"),
        name="ada_mod",
    )(c.reshape(d, 1), w_ada, b_ada.reshape(depth, 1, n))


def _rope_table_kernel(pos_ref, inv_ref, sign_ref, cos_ref, sin_ref):
    ang = pos_ref[...].astype(F32) * inv_ref[...]
    cos_ref[...] = jnp.cos(ang)
    sin_ref[...] = jnp.sin(ang) * sign_ref[...]


def _rope_tables(positions):
    t = positions.shape[0]
    half = ROPE_DIM // 2
    inv_freq = jnp.power(ROPE_THETA, -jnp.arange(half, dtype=F32) / half)
    inv = jnp.tile(inv_freq, LANES // half).reshape(1, LANES)
    sign = jnp.tile(jnp.concatenate([-jnp.ones((half,), F32), jnp.ones((half,), F32)]), LANES // ROPE_DIM)
    tm = min(t, ROPE_ROW_TILE)
    return pl.pallas_call(
        _rope_table_kernel,
        out_shape=(jax.ShapeDtypeStruct((t, LANES), F32), jax.ShapeDtypeStruct((t, LANES), F32)),
        grid=(t // tm,),
        in_specs=[pl.BlockSpec((tm, 1), lambda i: (i, 0)),
                  pl.BlockSpec((1, LANES), lambda i: (0, 0)),
                  pl.BlockSpec((1, LANES), lambda i: (0, 0))],
        out_specs=(pl.BlockSpec((tm, LANES), lambda i: (i, 0)), pl.BlockSpec((tm, LANES), lambda i: (i, 0))),
        compiler_params=_params("parallel"),
        name="rope_tables",
    )(positions.reshape(t, 1), inv, sign.reshape(1, LANES))


def _norm_matmul_kernel(x_ref, sh_ref, sc_ref, w_ref, b_ref, o_ref, h_scr):
    @pl.when(pl.program_id(1) == 0)
    def _():
        h_scr[...] = _modulated_norm(x_ref[...], sh_ref[...], sc_ref[...]).astype(BF16)

    o_ref[...] = (_dot(h_scr[...], w_ref[...]) + b_ref[...]).astype(o_ref.dtype)


def _col_tile(n, cap):
    return max(c for c in range(LANES, min(n, cap) + 1, LANES) if n % c == 0)


def _norm_matmul(x, shift, scale, w, bias, out_dtype, name):
    t, d = x.shape
    n = w.shape[1]
    tm = min(t, PROJ_ROW_TILE)
    tn = _col_tile(n, PROJ_COL_CAP)
    return pl.pallas_call(
        _norm_matmul_kernel,
        out_shape=jax.ShapeDtypeStruct((t, n), out_dtype),
        grid=(t // tm, n // tn),
        in_specs=[pl.BlockSpec((tm, d), lambda i, j: (i, 0)),
                  pl.BlockSpec((1, d), lambda i, j: (0, 0)),
                  pl.BlockSpec((1, d), lambda i, j: (0, 0)),
                  pl.BlockSpec((d, tn), lambda i, j: (0, j)),
                  pl.BlockSpec((1, tn), lambda i, j: (0, j))],
        out_specs=pl.BlockSpec((tm, tn), lambda i, j: (i, j)),
        scratch_shapes=[pltpu.VMEM((tm, d), BF16)],
        compiler_params=_params("parallel", "arbitrary"),
        name=name,
    )(x, shift, scale, w, bias)


def _norm_matmul_pair_kernel(x_ref, sh_ref, sc_ref, wa_ref, wb_ref, oa_ref, ob_ref, h_scr, *, steps_a):
    j = pl.program_id(1)

    @pl.when(j == 0)
    def _():
        h_scr[...] = _modulated_norm(x_ref[...], sh_ref[...], sc_ref[...]).astype(BF16)

    @pl.when(j < steps_a)
    def _():
        oa_ref[...] = _dot(h_scr[...], wa_ref[...]).astype(oa_ref.dtype)

    @pl.when(j >= steps_a)
    def _():
        ob_ref[...] = _dot(h_scr[...], wb_ref[...]).astype(ob_ref.dtype)


def _norm_matmul_pair(x, shift, scale, wa, wb, dtype_a, dtype_b, name):
    t, d = x.shape
    na, nb = wa.shape[1], wb.shape[1]
    tm = min(t, PROJ_ROW_TILE)
    ta, tb = _col_tile(na, PAIR_COL_CAP), _col_tile(nb, PAIR_COL_CAP)
    steps_a, steps_b = na // ta, nb // tb
    col_a = lambda j: jnp.minimum(j, steps_a - 1)
    col_b = lambda j: jnp.maximum(j - steps_a, 0)
    return pl.pallas_call(
        functools.partial(_norm_matmul_pair_kernel, steps_a=steps_a),
        out_shape=(jax.ShapeDtypeStruct((t, na), dtype_a), jax.ShapeDtypeStruct((t, nb), dtype_b)),
        grid=(t // tm, steps_a + steps_b),
        in_specs=[pl.BlockSpec((tm, d), lambda i, j: (i, 0)),
                  pl.BlockSpec((1, d), lambda i, j: (0, 0)),
                  pl.BlockSpec((1, d), lambda i, j: (0, 0)),
                  pl.BlockSpec((d, ta), lambda i, j: (0, col_a(j))),
                  pl.BlockSpec((d, tb), lambda i, j: (0, col_b(j)))],
        out_specs=(pl.BlockSpec((tm, ta), lambda i, j: (i, col_a(j))),
                   pl.BlockSpec((tm, tb), lambda i, j: (i, col_b(j)))),
        scratch_shapes=[pltpu.VMEM((tm, d), BF16)],
        compiler_params=pltpu.CompilerParams(dimension_semantics=("parallel", "arbitrary"),
                                             vmem_limit_bytes=LARGE_VMEM_LIMIT_BYTES),
        name=name,
    )(x, shift, scale, wa, wb)


def _mlstm_kernel(q_ref, k_ref, v_ref, o_ref, g_ref, gb_ref, gain_ref, out_ref, c_scr, n_scr, m_scr):
    L = MLSTM_CHUNK
    tm = q_ref.shape[0]

    @pl.when(pl.program_id(0) == 0)
    def _():
        c_scr[...] = jnp.zeros_like(c_scr)
        n_scr[...] = jnp.zeros_like(n_scr)
        m_scr[...] = jnp.zeros_like(m_scr)

    capped = GATE_SOFTCAP * jnp.tanh((g_ref[...] + gb_ref[...]) / GATE_SOFTCAP)
    log_sig = jnp.minimum(capped, 0.0) - jnp.log1p(jnp.exp(-jnp.abs(capped)))
    lane = lax.broadcasted_iota(I32, capped.shape, 1)
    gate = jnp.where(lane < M_HEADS, capped, log_sig)
    r = lax.broadcasted_iota(I32, (tm, tm), 0)
    c = lax.broadcasted_iota(I32, (tm, tm), 1)
    chunk_tril = jnp.where(((r // L) == (c // L)) & (c <= r), 1.0, 0.0).astype(F32)
    cum = _dot(chunk_tril, gate, precision=HIGHEST)
    gate_rows = gate.T
    cum_rows = cum.T
    heads = range(M_HEADS)
    rr = lax.broadcasted_iota(I32, (M_HEADS * L, L), 0) % L
    cc = lax.broadcasted_iota(I32, (M_HEADS * L, L), 1)
    causal = cc <= rr

    def stack(parts):
        return jnp.concatenate(parts, axis=0)

    def rows_of(x, n=L):
        return stack([jnp.broadcast_to(x[h:h + 1, :], (n, LANES)) for h in heads])

    def wide(x):
        return jnp.concatenate([x] * (M_V // LANES), axis=1)

    c_states = [c_scr[h] for h in heads]
    n_states = [n_scr[h] for h in heads]
    m_prev = m_scr[0:M_HEADS, :]
    gain = stack([jnp.broadcast_to(gain_ref[:, h * M_V:(h + 1) * M_V], (L, M_V)) for h in heads])
    for ci in range(tm // L):
        sl = slice(ci * L, (ci + 1) * L)
        q_heads = [q_ref[sl, h * M_QK:(h + 1) * M_QK] for h in heads]
        k_all = k_ref[sl, :].astype(F32) * (M_QK ** -0.5)
        k_heads = [k_all[:, h * M_QK:(h + 1) * M_QK] for h in heads]
        kb_heads = [k.astype(BF16) for k in k_heads]
        v_heads = [v_ref[sl, h * M_V:(h + 1) * M_V] for h in heads]

        ig_col = stack([jnp.broadcast_to(gate[sl, h:h + 1], (L, LANES)) for h in heads])
        b_col = stack([jnp.broadcast_to(cum[sl, M_HEADS + h:M_HEADS + h + 1], (L, LANES)) for h in heads])
        ig_rows = gate_rows[0:M_HEADS, sl]
        b_rows = cum_rows[M_HEADS:2 * M_HEADS, sl]
        ig_row = rows_of(ig_rows)
        b_row = rows_of(b_rows)

        dm = jnp.where(causal, b_col - b_row + ig_row, -jnp.inf)
        inter = b_col + rows_of(m_prev)
        m_row = jnp.maximum(inter, jnp.broadcast_to(jnp.max(dm, axis=-1, keepdims=True), inter.shape))
        w_intra = jnp.exp(dm - m_row)
        w_inter = jnp.exp(inter - m_row)
        s = stack([_dot_nt(q_heads[h], kb_heads[h]) for h in heads]) * w_intra
        s_b = s.astype(BF16)
        intra = stack([_dot(s_b[h * L:(h + 1) * L], v_heads[h]) for h in heads])
        carried = stack([_dot_nt(q_heads[h], c_states[h].astype(BF16)) for h in heads])
        num = intra + wide(w_inter) * carried
        q_all = stack([q.astype(F32) for q in q_heads])
        n_all = stack([jnp.broadcast_to(n_states[h], (L, M_QK)) for h in heads])
        qn = jnp.broadcast_to(jnp.sum(q_all * n_all, axis=-1, keepdims=True), inter.shape)
        den = jnp.broadcast_to(jnp.sum(s, axis=-1, keepdims=True), inter.shape) + w_inter * qn
        hh = num / wide(jnp.maximum(jnp.abs(den), jnp.exp(-m_row)))

        b_last = jnp.broadcast_to(b_rows[:, L - 1:L], (M_HEADS, LANES))
        g_rows = b_last - b_rows + ig_rows
        m_new = jnp.maximum(b_last + m_prev, jnp.broadcast_to(jnp.max(g_rows, axis=-1, keepdims=True), b_last.shape))
        decay = jnp.exp(b_last + m_prev - m_new)
        ws_col = jnp.exp(rows_of(b_last) - b_col + ig_col - rows_of(m_new))
        for h in heads:
            ws_h = ws_col[h * L:(h + 1) * L]
            vw = (v_heads[h].astype(F32) * wide(ws_h)).astype(BF16)
            c_states[h] = (jnp.broadcast_to(decay[h:h + 1, :], (M_V, M_QK)) * c_states[h]
                           + _dot_tn(vw, kb_heads[h]))
            n_states[h] = decay[h:h + 1, :] * n_states[h] + jnp.sum(k_heads[h] * ws_h, axis=0, keepdims=True)
        m_prev = m_new

        inv_rms = lax.rsqrt(jnp.mean(hh * hh, axis=-1, keepdims=True) + EPS)
        o_gate = stack([o_ref[sl, h * M_V:(h + 1) * M_V] for h in heads]).astype(F32)
        y = hh * inv_rms * gain * jax.nn.sigmoid(o_gate)
        for h in heads:
            out_ref[sl, h * M_V:(h + 1) * M_V] = y[h * L:(h + 1) * L].astype(out_ref.dtype)
    for h in heads:
        c_scr[h], n_scr[h] = c_states[h], n_states[h]
    m_scr[0:M_HEADS, :] = m_prev


def _mlstm(proj_a, proj_b, gate_col_block, gate_bias, gain):
    t = proj_a.shape[0]
    tm = min(t, MLSTM_ROW_TILE)
    nq = M_HEADS * M_QK
    nv = M_HEADS * M_V
    return pl.pallas_call(
        _mlstm_kernel,
        out_shape=jax.ShapeDtypeStruct((t, nv), BF16),
        grid=(t // tm,),
        in_specs=[pl.BlockSpec((tm, nq), lambda i: (i, 0)),
                  pl.BlockSpec((tm, nq), lambda i: (i, 1)),
                  pl.BlockSpec((tm, nv), lambda i: (i, 1)),
                  pl.BlockSpec((tm, nv), lambda i: (i, 2)),
                  pl.BlockSpec((tm, LANES), lambda i: (i, gate_col_block)),
                  pl.BlockSpec((1, LANES), lambda i: (0, 0)),
                  pl.BlockSpec((1, nv), lambda i: (0, 0))],
        out_specs=pl.BlockSpec((tm, nv), lambda i: (i, 0)),
        scratch_shapes=[pltpu.VMEM((M_HEADS, M_V, M_QK), F32),
                        pltpu.VMEM((M_HEADS, 1, M_QK), F32),
                        pltpu.VMEM((8, LANES), F32)],
        compiler_params=_params("arbitrary"),
        name="mlstm",
    )(proj_a, proj_a, proj_a, proj_a, proj_b, gate_bias, gain)


def _mla_up_kernel(pb_ref, qn_ref, kvn_ref, wq_ref, wkv_ref, cos_ref, sin_ref, q_ref, k_ref, v_ref):
    cos, sin = cos_ref[...], sin_ref[...]
    cq = pb_ref[:, 0:A_Q_LORA]
    cq = (cq * lax.rsqrt(jnp.mean(cq * cq, axis=-1, keepdims=True) + EPS) * qn_ref[...]).astype(BF16)
    ckv = pb_ref[:, A_Q_LORA:A_Q_LORA + A_KV_LORA]
    ckv = (ckv * lax.rsqrt(jnp.mean(ckv * ckv, axis=-1, keepdims=True) + EPS) * kvn_ref[...]).astype(BF16)
    k_pe = _rope_lanes(pb_ref[:, A_Q_LORA + A_KV_LORA:A_Q_LORA + A_KV_LORA + LANES], cos, sin).astype(BF16)
    scale = (A_NOPE + A_ROPE) ** -0.5 * LOG2_E
    for h in range(A_HEADS):
        qh = _dot(cq, wq_ref[:, h * A_QK_PAD:(h + 1) * A_QK_PAD])
        q_pe = _rope_lanes(qh[:, A_NOPE:], cos, sin)
        q_ref[h] = (jnp.concatenate([qh[:, :A_NOPE], q_pe], axis=1) * scale).astype(q_ref.dtype)
        kvh = _dot(ckv, wkv_ref[:, h * (A_NOPE + A_V):(h + 1) * (A_NOPE + A_V)])
        k_ref[h] = jnp.concatenate([kvh[:, :A_NOPE].astype(BF16), k_pe], axis=1)
        v_ref[h] = kvh[:, A_NOPE:].astype(v_ref.dtype)


def _mla_up(proj_b, q_norm, kv_norm, wq, wkv, cos, sin):
    t, nb = proj_b.shape
    tm = min(t, MLA_UP_ROW_TILE)
    return pl.pallas_call(
        _mla_up_kernel,
        out_shape=(jax.ShapeDtypeStruct((A_HEADS, t, A_QK_PAD), BF16),
                   jax.ShapeDtypeStruct((A_HEADS, t, A_QK_PAD), BF16),
                   jax.ShapeDtypeStruct((A_HEADS, t, A_V), BF16)),
        grid=(t // tm,),
        in_specs=[pl.BlockSpec((tm, nb), lambda i: (i, 0)),
                  pl.BlockSpec((1, A_Q_LORA), lambda i: (0, 0)),
                  pl.BlockSpec((1, A_KV_LORA), lambda i: (0, 0)),
                  pl.BlockSpec(wq.shape, lambda i: (0, 0)),
                  pl.BlockSpec(wkv.shape, lambda i: (0, 0)),
                  pl.BlockSpec((tm, LANES), lambda i: (i, 0)),
                  pl.BlockSpec((tm, LANES), lambda i: (i, 0))],
        out_specs=(pl.BlockSpec((A_HEADS, tm, A_QK_PAD), lambda i: (0, i, 0)),
                   pl.BlockSpec((A_HEADS, tm, A_QK_PAD), lambda i: (0, i, 0)),
                   pl.BlockSpec((A_HEADS, tm, A_V), lambda i: (0, i, 0))),
        compiler_params=_params("parallel"),
        name="mla_up",
    )(proj_b, q_norm, kv_norm, wq, wkv, cos, sin)


def _mla_flash_kernel(q_ref, k_ref, v_ref, o_ref, m_scr, l_scr, acc_scr, *, tq, tkb):
    i = pl.program_id(1)
    q = q_ref[...]
    m_scr[...] = jnp.full(m_scr.shape, -jnp.inf, F32)
    l_scr[...] = jnp.zeros(l_scr.shape, F32)
    acc_scr[...] = jnp.zeros(acc_scr.shape, F32)

    def tile(start, width, masked):
        s = _dot_nt(q, k_ref[pl.ds(start, width), :])
        if masked:
            row = lax.broadcasted_iota(I32, (tq, width), 0)
            col = lax.broadcasted_iota(I32, (tq, width), 1)
            s = jnp.where(col <= row, s, -jnp.inf)
        m = m_scr[...]
        m_new = jnp.maximum(m, jnp.broadcast_to(jnp.max(s, axis=-1, keepdims=True), m.shape))
        alpha = jnp.exp2(m - m_new)
        p = jnp.exp2(s - jnp.concatenate([m_new] * (width // LANES), axis=1))
        lane_sums = p[:, 0:LANES]
        for j in range(1, width // LANES):
            lane_sums = lane_sums + p[:, j * LANES:(j + 1) * LANES]
        l_scr[...] = alpha * l_scr[...] + lane_sums
        acc_scr[...] = alpha * acc_scr[...] + _dot(p.astype(BF16), v_ref[pl.ds(start, width), :])
        m_scr[...] = m_new

    per = tkb // tq
    n_big = i // per

    def body(j, carry):
        tile(pl.multiple_of(j * tkb, tkb), tkb, False)
        return carry

    lax.fori_loop(0, n_big, body, 0)
    rem = i - n_big * per
    for r in range(1, per):
        @pl.when(rem >= r)
        def _():
            tile(pl.multiple_of(n_big * tkb + (r - 1) * tq, tq), tq, False)
    tile(pl.multiple_of(i * tq, tq), tq, True)
    o_ref[...] = (acc_scr[...] / jnp.sum(l_scr[...], axis=-1, keepdims=True)).astype(o_ref.dtype)


def _mla_flash(q, k, v):
    _, t, _ = q.shape
    tq = min(t, FLASH_Q_TILE)
    tkb = min(t, FLASH_KV_TILE)
    return pl.pallas_call(
        functools.partial(_mla_flash_kernel, tq=tq, tkb=tkb),
        out_shape=jax.ShapeDtypeStruct((t, A_HEADS * A_V), BF16),
        grid=(A_HEADS, t // tq),
        in_specs=[pl.BlockSpec((None, tq, A_QK_PAD), lambda h, i: (h, i, 0)),
                  pl.BlockSpec((None, t, A_QK_PAD), lambda h, i: (h, 0, 0)),
                  pl.BlockSpec((None, t, A_V), lambda h, i: (h, 0, 0))],
        out_specs=pl.BlockSpec((tq, A_V), lambda h, i: (i, h)),
        scratch_shapes=[pltpu.VMEM((tq, LANES), F32), pltpu.VMEM((tq, LANES), F32), pltpu.VMEM((tq, A_V), F32)],
        compiler_params=pltpu.CompilerParams(dimension_semantics=("parallel", "arbitrary"),
                                             vmem_limit_bytes=LARGE_VMEM_LIMIT_BYTES),
        name="mla_flash",
    )(q, k, v)


def _proj_residual_kernel(*refs, n_lhs):
    lhs = refs[:n_lhs]
    ws = refs[n_lhs:2 * n_lhs]
    x_ref, g_ref, o_ref = refs[2 * n_lhs:]
    acc = _dot(lhs[0][...], ws[0][...])
    for a, w in zip(lhs[1:], ws[1:]):
        acc = acc + _dot(a[...], w[...])
    o_ref[...] = x_ref[...] + g_ref[...] * acc


def _proj_residual(lhs_list, w_list, x, gate, name):
    t, d = x.shape
    tm = min(t, PROJ_ROW_TILE)
    tn = _col_tile(d, RESIDUAL_COL_CAP)
    n_lhs = len(lhs_list)
    in_specs = ([pl.BlockSpec((tm, a.shape[1]), lambda i, j: (i, 0)) for a in lhs_list]
                + [pl.BlockSpec((w.shape[0], tn), lambda i, j: (0, j)) for w in w_list]
                + [pl.BlockSpec((tm, tn), lambda i, j: (i, j)), pl.BlockSpec((1, tn), lambda i, j: (0, j))])
    return pl.pallas_call(
        functools.partial(_proj_residual_kernel, n_lhs=n_lhs),
        out_shape=jax.ShapeDtypeStruct((t, d), F32),
        grid=(t // tm, d // tn),
        in_specs=in_specs,
        out_specs=pl.BlockSpec((tm, tn), lambda i, j: (i, j)),
        compiler_params=_params("parallel", "parallel"),
        name=name,
    )(*lhs_list, *w_list, x, gate)


def _swa_kernel(q_ref, kc_ref, kp_ref, vc_ref, vp_ref, cosc_ref, sinc_ref, cosp_ref, sinp_ref, sink_ref, o_ref):
    i = pl.program_id(0)
    qb = S_BLOCK
    group = S_HEADS // S_KV_HEADS
    pairs = group // 2
    cos_c, sin_c = cosc_ref[...], sinc_ref[...]
    cos_w = jnp.concatenate([cosp_ref[...], cos_c], axis=0)
    sin_w = jnp.concatenate([sinp_ref[...], sin_c], axis=0)
    kw = jnp.concatenate([kp_ref[...], kc_ref[...]], axis=0).astype(F32)
    vw = jnp.concatenate([vp_ref[...], vc_ref[...]], axis=0)

    r = lax.broadcasted_iota(I32, (pairs * qb, 2 * qb), 0) % qb
    c = lax.broadcasted_iota(I32, (pairs * qb, 2 * qb), 1)
    dist = qb + r - c
    valid = (dist >= 0) & (dist < S_BLOCK) & ((c >= qb) | (i > 0))
    lane = lax.broadcasted_iota(I32, (2 * qb, LANES), 1)
    low = lane < S_HD
    low_rows = lax.broadcasted_iota(I32, (pairs * qb, LANES), 1) < S_HD
    ones_d = jnp.concatenate([jnp.where(low, 1.0, 0.0), jnp.where(low, 0.0, 1.0)], axis=0).astype(BF16)

    for g in range(S_KV_HEADS):
        col = slice((g // 2) * LANES, (g // 2 + 1) * LANES)
        k_pair = _rope_lanes(kw[:, col], cos_w, sin_w)
        v_pair = vw[:, col]
        keep = low if g % 2 == 0 else jnp.logical_not(low)
        k_own = jnp.where(keep, k_pair, 0.0)
        v_own = jnp.where(keep, v_pair.astype(F32), 0.0)
        k_other = pltpu.roll(k_own, S_HD, 1)
        v_other = pltpu.roll(v_own, S_HD, 1)
        k_lo, k_hi = (k_own, k_other) if g % 2 == 0 else (k_other, k_own)
        v_lo, v_hi = (v_own, v_other) if g % 2 == 0 else (v_other, v_own)
        kd = jnp.concatenate([k_lo, k_hi], axis=0).astype(BF16)
        vd = jnp.concatenate([v_lo, v_hi], axis=0).astype(BF16)
        qs = []
        for p in range(pairs):
            qcol = slice((g * pairs + p) * LANES, (g * pairs + p + 1) * LANES)
            qs.append(_rope_lanes(q_ref[:, qcol].astype(F32), cos_c, sin_c) * (S_HD ** -0.5 * LOG2_E))
        qg = jnp.concatenate(qs, axis=0).astype(BF16)
        s = _dot_nt(qg, kd)
        exps, sink_terms = [], []
        for half in range(2):
            sh = jnp.where(valid, s[:, half * 2 * qb:(half + 1) * 2 * qb], -jnp.inf)
            sink = sink_ref[g, half]
            m = jnp.maximum(jnp.broadcast_to(jnp.max(sh, axis=-1, keepdims=True), sink.shape), sink)
            exps.append(jnp.exp2(sh - jnp.concatenate([m, m], axis=1)).astype(BF16))
            sink_terms.append(jnp.exp2(sink - m))
        e_all = jnp.concatenate(exps, axis=1)
        den = _dot(e_all, ones_d) + jnp.where(low_rows, sink_terms[0], sink_terms[1])
        o = _dot(e_all, vd) / den
        for p in range(pairs):
            qcol = slice((g * pairs + p) * LANES, (g * pairs + p + 1) * LANES)
            o_ref[:, qcol] = o[p * qb:(p + 1) * qb].astype(o_ref.dtype)


def _swa(qkv, cos, sin, sinks):
    t = qkv.shape[0]
    qb = S_BLOCK
    nq = S_HEADS * S_HD
    nk = S_KV_HEADS * S_HD
    group = S_HEADS // S_KV_HEADS
    pairs = group // 2
    sink_cols = jnp.repeat((sinks.astype(F32) * LOG2_E).reshape(S_KV_HEADS, pairs, 2).transpose(0, 2, 1), qb, axis=-1)
    sink_cols = jnp.broadcast_to(sink_cols.reshape(S_KV_HEADS, 2, pairs * qb, 1), (S_KV_HEADS, 2, pairs * qb, LANES))
    k_blk = nq // nk
    prev = lambda i: jnp.maximum(i - 1, 0)
    return pl.pallas_call(
        _swa_kernel,
        out_shape=jax.ShapeDtypeStruct((t, nq), BF16),
        grid=(t // qb,),
        in_specs=[pl.BlockSpec((qb, nq), lambda i: (i, 0)),
                  pl.BlockSpec((qb, nk), lambda i: (i, k_blk)),
                  pl.BlockSpec((qb, nk), lambda i: (prev(i), k_blk)),
                  pl.BlockSpec((qb, nk), lambda i: (i, k_blk + 1)),
                  pl.BlockSpec((qb, nk), lambda i: (prev(i), k_blk + 1)),
                  pl.BlockSpec((qb, LANES), lambda i: (i, 0)),
                  pl.BlockSpec((qb, LANES), lambda i: (i, 0)),
                  pl.BlockSpec((qb, LANES), lambda i: (prev(i), 0)),
                  pl.BlockSpec((qb, LANES), lambda i: (prev(i), 0)),
                  pl.BlockSpec((S_KV_HEADS, 2, pairs * qb, LANES), lambda i: (0, 0, 0, 0))],
        out_specs=pl.BlockSpec((qb, nq), lambda i: (i, 0)),
        compiler_params=_params("parallel"),
        name="swa",
    )(qkv, qkv, qkv, qkv, qkv, cos, sin, cos, sin, sink_cols)


def _expert_of_row(p):
    per_group = N_EXPERTS // N_GROUPS
    return (p % N_GROUPS) * per_group + p // N_GROUPS


def _moe_pre_kernel(x_ref, sh_ref, sc_ref, wr_ref, rb_ref, hf_ref, idx_ref, wt_ref, rank_ref, cnt_ref, carry_scr):
    tm = x_ref.shape[0]
    per_group = N_EXPERTS // N_GROUPS

    @pl.when(pl.program_id(0) == 0)
    def _():
        carry_scr[...] = jnp.zeros_like(carry_scr)

    h = _modulated_norm(x_ref[...], sh_ref[...], sc_ref[...])
    _store_rows(hf_ref, _pack_bf16_halves(h))
    h_hi = h.astype(BF16)
    h_lo = (h - h_hi.astype(F32)).astype(BF16)
    logits = _dot(h_hi, wr_ref[0]) + (_dot(h_hi, wr_ref[1]) + _dot(h_lo, wr_ref[0]))
    scores = jax.nn.sigmoid(logits.T[:N_EXPERTS, :])
    biased = scores + rb_ref[...]

    members = [biased[j * N_GROUPS:(j + 1) * N_GROUPS, :] for j in range(per_group)]
    m1 = members[0]
    for a in members[1:]:
        m1 = jnp.maximum(m1, a)
    first = jnp.full(m1.shape, per_group, I32)
    for j in reversed(range(per_group)):
        first = jnp.where(members[j] == m1, j, first)
    m2 = jnp.full(m1.shape, -jnp.inf, F32)
    for j in range(per_group):
        m2 = jnp.maximum(m2, jnp.where(first == j, -jnp.inf, members[j]))
    group_score = m1 + m2

    g_iota = lax.broadcasted_iota(I32, group_score.shape, 0).astype(F32)
    g_sel = jnp.zeros(group_score.shape, F32)
    for _ in range(TOPK_GROUPS):
        best = jnp.max(group_score, axis=0, keepdims=True)
        gi = jnp.min(jnp.where(group_score == best, g_iota, float(N_GROUPS)), axis=0, keepdims=True)
        hit = g_iota == gi
        g_sel = jnp.where(hit, 1.0, g_sel)
        group_score = jnp.where(hit, -jnp.inf, group_score)
    masked = jnp.concatenate([jnp.where(g_sel > 0.5, a, -jnp.inf) for a in members], axis=0)

    e_iota = _expert_of_row(lax.broadcasted_iota(I32, masked.shape, 0)).astype(F32)
    sel = jnp.zeros(masked.shape, F32)
    idx_rows, w_rows = [], []
    for _ in range(TOP_K):
        best = jnp.max(masked, axis=0, keepdims=True)
        ei = jnp.min(jnp.where(masked == best, e_iota, float(N_EXPERTS)), axis=0, keepdims=True)
        hit = e_iota == ei
        idx_rows.append(ei)
        w_rows.append(jnp.sum(jnp.where(hit, scores, 0.0), axis=0, keepdims=True))
        sel = jnp.where(hit, 1.0, sel)
        masked = jnp.where(hit, -jnp.inf, masked)
    idx = jnp.concatenate(idx_rows, axis=0).astype(I32)
    wts = jnp.concatenate(w_rows, axis=0)
    wts = wts / jnp.sum(wts, axis=0, keepdims=True) * ROUTED_SCALE

    r = lax.broadcasted_iota(I32, (tm, tm), 0)
    c = lax.broadcasted_iota(I32, (tm, tm), 1)
    before = jnp.where(r < c, 1.0, 0.0).astype(BF16)
    rank_excl = carry_scr[:, 0:1] + _dot(sel.astype(BF16), before)
    rank_rows = [jnp.sum(jnp.where(e_iota == idx_rows[k], rank_excl, 0.0), axis=0, keepdims=True)
                 for k in range(TOP_K)]
    carry_scr[...] = carry_scr[...] + jnp.sum(sel, axis=1, keepdims=True)

    idx_ref[...] = idx
    wt_ref[...] = wts
    rank_ref[...] = jnp.concatenate(rank_rows, axis=0).astype(I32)
    cnt_ref[...] = carry_scr[...]


def _moe_pre(x, shift, scale, w_router, router_bias):
    t, d = x.shape
    tm = min(t, ROUTER_ROW_TILE)
    rows = jnp.arange(N_EXPERTS)
    perm = _expert_of_row(rows)
    wr = jnp.zeros((d, LANES), F32).at[:, :N_EXPERTS].set(w_router[:, perm])
    wr_hi = wr.astype(BF16)
    wr = jnp.stack([wr_hi, (wr - wr_hi.astype(F32)).astype(BF16)])
    rb = router_bias.astype(F32)[perm].reshape(N_EXPERTS, 1)
    per_token = lambda dtype: jax.ShapeDtypeStruct((TOP_K, t), dtype)
    return pl.pallas_call(
        _moe_pre_kernel,
        out_shape=(jax.ShapeDtypeStruct((t * ROW_TILE, LANES), U32), per_token(I32), per_token(F32), per_token(I32),
                   jax.ShapeDtypeStruct((N_EXPERTS, LANES), F32)),
        grid=(t // tm,),
        in_specs=[pl.BlockSpec((tm, d), lambda i: (i, 0)),
                  pl.BlockSpec((1, d), lambda i: (0, 0)),
                  pl.BlockSpec((1, d), lambda i: (0, 0)),
                  pl.BlockSpec((2, d, LANES), lambda i: (0, 0, 0)),
                  pl.BlockSpec((N_EXPERTS, 1), lambda i: (0, 0))],
        out_specs=(pl.BlockSpec((tm * ROW_TILE, LANES), lambda i: (i, 0)),
                   pl.BlockSpec((TOP_K, tm), lambda i: (0, i)),
                   pl.BlockSpec((TOP_K, tm), lambda i: (0, i)),
                   pl.BlockSpec((TOP_K, tm), lambda i: (0, i)),
                   pl.BlockSpec((N_EXPERTS, LANES), lambda i: (0, 0))),
        scratch_shapes=[pltpu.VMEM((N_EXPERTS, LANES), F32)],
        compiler_params=_params("arbitrary"),
        name="moe_router",
    )(x, shift, scale, wr, rb)


def _moe_dest_kernel(idx_ref, rank_ref, cnt_ref, dest_ref, be_ref, nused_ref, tail_ref):
    cnt = cnt_ref[...]
    padded = jnp.floor((cnt + (E_BLOCK - 1)) * (1.0 / E_BLOCK)) * E_BLOCK
    e_i = _expert_of_row(lax.broadcasted_iota(I32, (N_EXPERTS, N_EXPERTS), 0))
    e_j = _expert_of_row(lax.broadcasted_iota(I32, (N_EXPERTS, N_EXPERTS), 1))
    earlier = jnp.where(e_j < e_i, 1.0, 0.0).astype(F32)
    pstart = _dot(earlier, padded, precision=HIGHEST)
    pend = pstart + padded

    tm = idx_ref.shape[1]
    e_col = _expert_of_row(lax.broadcasted_iota(I32, (N_EXPERTS, tm), 0))
    rows = []
    for k in range(TOP_K):
        hit = e_col == idx_ref[k:k + 1, :]
        rows.append(jnp.sum(jnp.where(hit, pstart[:, 0:1], 0.0), axis=0, keepdims=True))
    dest_ref[...] = jnp.concatenate(rows, axis=0).astype(I32) + rank_ref[...]

    nb = be_ref.shape[1]
    block_start = (lax.broadcasted_iota(I32, (N_EXPERTS, nb), 1) * E_BLOCK).astype(F32)
    ended = jnp.sum(jnp.where(pend[:, 0:1] <= block_start, 1.0, 0.0), axis=0, keepdims=True)
    be_ref[...] = jnp.minimum(ended, float(N_EXPERTS - 1)).astype(I32)
    nused_ref[...] = (jnp.sum(padded, axis=0, keepdims=True) * (1.0 / E_BLOCK)).astype(I32)
    e_row = _expert_of_row(lax.broadcasted_iota(I32, (N_EXPERTS, LANES), 0))
    lane = lax.broadcasted_iota(I32, (N_EXPERTS, LANES), 1)
    end_lane = jnp.sum(jnp.where(e_row <= lane, padded, 0.0), axis=0, keepdims=True)
    own_lane = jnp.sum(jnp.where(e_row == lane, padded, 0.0), axis=0, keepdims=True)
    tail_ref[...] = jnp.where(own_lane > 0.0, end_lane - E_BLOCK, -1.0).astype(I32)


def _moe_dest(idx_t, rank_t, counts, nblk):
    t = idx_t.shape[1]
    tm = min(t, DEST_ROW_TILE)
    nb = ((nblk + LANES - 1) // LANES) * LANES
    return pl.pallas_call(
        _moe_dest_kernel,
        out_shape=(jax.ShapeDtypeStruct((TOP_K, t), I32),
                   jax.ShapeDtypeStruct((1, nb), I32),
                   jax.ShapeDtypeStruct((1, LANES), I32),
                   jax.ShapeDtypeStruct((1, LANES), I32)),
        grid=(t // tm,),
        in_specs=[pl.BlockSpec((TOP_K, tm), lambda i: (0, i)),
                  pl.BlockSpec((TOP_K, tm), lambda i: (0, i)),
                  pl.BlockSpec((N_EXPERTS, LANES), lambda i: (0, 0))],
        out_specs=(pl.BlockSpec((TOP_K, tm), lambda i: (0, i)),
                   pl.BlockSpec((1, nb), lambda i: (0, 0)),
                   pl.BlockSpec((1, LANES), lambda i: (0, 0)),
                   pl.BlockSpec((1, LANES), lambda i: (0, 0))),
        compiler_params=_params("arbitrary"),
        name="moe_dest",
    )(idx_t, rank_t, counts)


def _dispatch_kernel(tail_ref, nused_ref, dest_ref, hf_ref, wgu_ref, wd_ref, xbuf_hbm, sh_ref, dest_smem, zero_scr,
                     idx_sem, zero_sem, row_sem):
    tm = dest_ref.shape[1]
    block_sublanes = E_BLOCK * ROW_TILE
    nblk = xbuf_hbm.shape[0] // block_sublanes

    @pl.when(pl.program_id(0) == 0)
    def _():
        zero_scr[...] = jnp.zeros_like(zero_scr)

        def zero_copy(row):
            start = pl.multiple_of(row * ROW_TILE, block_sublanes)
            return pltpu.make_async_copy(zero_scr, xbuf_hbm.at[pl.ds(start, block_sublanes)], zero_sem)

        def start(e, carry):
            @pl.when(tail_ref[e] >= 0)
            def _():
                zero_copy(tail_ref[e]).start()
            return carry

        def wait(e, carry):
            @pl.when(tail_ref[e] >= 0)
            def _():
                zero_copy(0).wait()
            return carry

        def start_unused(b, carry):
            zero_copy(b * E_BLOCK).start()
            return carry

        def wait_unused(b, carry):
            zero_copy(0).wait()
            return carry

        lax.fori_loop(0, N_EXPERTS, start, 0)
        lax.fori_loop(nused_ref[0], nblk, start_unused, 0)
        lax.fori_loop(0, N_EXPERTS, wait, 0)
        lax.fori_loop(nused_ref[0], nblk, wait_unused, 0)

    load = pltpu.make_async_copy(dest_ref, dest_smem, idx_sem)
    load.start()
    load.wait()

    def row(ref, r):
        return ref.at[pl.ds(pl.multiple_of(r * ROW_TILE, ROW_TILE), ROW_TILE)]

    for tok in range(tm):
        for k in range(TOP_K):
            pltpu.make_async_copy(row(hf_ref, tok), row(xbuf_hbm, dest_smem[k, tok]), row_sem).start(priority=k % 2)
    sh_ref[...] = _gated_ffn(_load_rows(hf_ref), wgu_ref, wd_ref)
    for k in range(TOP_K):
        pltpu.make_async_copy(hf_ref, xbuf_hbm.at[pl.ds(0, tm * ROW_TILE)], row_sem).wait()


def _dispatch(hf, dest_t, tail_start, nused, wgu, wd, npad):
    t = hf.shape[0] // ROW_TILE
    d = wgu.shape[0]
    tm = min(t, MOE_ROW_TILE)
    return pl.pallas_call(
        _dispatch_kernel,
        out_shape=(jax.ShapeDtypeStruct((npad * ROW_TILE, LANES), U32), jax.ShapeDtypeStruct((t, d), F32)),
        grid_spec=pltpu.PrefetchScalarGridSpec(
            num_scalar_prefetch=2,
            grid=(t // tm,),
            in_specs=[pl.BlockSpec((TOP_K, tm), lambda i, tail, nu: (0, i)),
                      pl.BlockSpec((tm * ROW_TILE, LANES), lambda i, tail, nu: (i, 0)),
                      pl.BlockSpec(wgu.shape, lambda i, tail, nu: (0, 0)),
                      pl.BlockSpec(wd.shape, lambda i, tail, nu: (0, 0))],
            out_specs=(pl.BlockSpec(memory_space=pl.ANY), pl.BlockSpec((tm, d), lambda i, tail, nu: (i, 0))),
            scratch_shapes=[pltpu.SMEM((TOP_K, tm), I32), pltpu.VMEM((E_BLOCK * ROW_TILE, LANES), U32),
                            pltpu.SemaphoreType.DMA, pltpu.SemaphoreType.DMA, pltpu.SemaphoreType.DMA]),
        compiler_params=_params("arbitrary"),
        name="moe_dispatch",
    )(tail_start, nused, dest_t, hf, wgu, wd)


def _expert_kernel(be_ref, nused_ref, x_ref, wgu_ref, wd_ref, y_ref, wgu_scr, wd_scr):
    b = pl.program_id(0)
    e = be_ref[b]
    e_prev = be_ref[jnp.maximum(b - 1, 0)]

    @pl.when((b == 0) | (e != e_prev))
    def _():
        wgu_scr[...] = wgu_ref[...].astype(BF16)
        wd_scr[...] = wd_ref[...].astype(BF16)

    @pl.when(b < nused_ref[0])
    def _():
        _store_rows(y_ref, _pack_bf16_halves(_gated_ffn(_load_rows(x_ref), wgu_scr, wd_scr)))

    @pl.when(b >= nused_ref[0])
    def _():
        y_ref[...] = jnp.zeros_like(y_ref)


def _gated_ffn(x_packed, wgu_ref, wd_ref):
    x_lo, x_hi = _unpack_bf16_halves(x_packed)
    half = x_packed.shape[1]
    gu = _dot(x_lo.astype(BF16), wgu_ref[0:half, :]) + _dot(x_hi.astype(BF16), wgu_ref[half:2 * half, :])
    ff = gu.shape[1] // 2
    gate = gu[:, :ff]
    act = gate * jax.nn.sigmoid(gate) * gu[:, ff:]
    return _dot(act.astype(BF16), wd_ref[...])


def _experts(xbuf, block_e, nused, w_gate_up, w_down, layer):
    nblk = xbuf.shape[0] // (E_BLOCK * ROW_TILE)
    _, _, d, ff2 = w_gate_up.shape
    last = lambda b, nu: jnp.minimum(b, nu[0] - 1)
    return pl.pallas_call(
        _expert_kernel,
        out_shape=jax.ShapeDtypeStruct(xbuf.shape, U32),
        grid_spec=pltpu.PrefetchScalarGridSpec(
            num_scalar_prefetch=2,
            grid=(nblk,),
            in_specs=[pl.BlockSpec((E_BLOCK * ROW_TILE, LANES), lambda b, be, nu: (last(b, nu), 0)),
                      pl.BlockSpec((None, None, d, ff2), lambda b, be, nu: (layer, be[b], 0, 0)),
                      pl.BlockSpec((None, None, ff2 // 2, d), lambda b, be, nu: (layer, be[b], 0, 0))],
            out_specs=pl.BlockSpec((E_BLOCK * ROW_TILE, LANES), lambda b, be, nu: (b, 0)),
            scratch_shapes=[pltpu.VMEM((d, ff2), BF16), pltpu.VMEM((ff2 // 2, d), BF16)]),
        compiler_params=_params("arbitrary"),
        name="moe_experts",
    )(block_e, nused, xbuf, w_gate_up, w_down)


def _combine_kernel(dest_ref, dest_next_ref, w_ref, x_ref, sh_ref, g_ref, ybuf_hbm, o_ref, rows_a, rows_b,
                    dest_smem, idx_sem, row_sems):
    i = pl.program_id(0)
    tm = x_ref.shape[0]

    def tile(r):
        return pl.ds(pl.multiple_of(r * ROW_TILE, ROW_TILE), ROW_TILE)

    def load_indices(indices_ref):
        load = pltpu.make_async_copy(indices_ref, dest_smem, idx_sem)
        load.start()
        load.wait()

    def row_copy(tok, k, rows, sem):
        pltpu.make_async_copy(ybuf_hbm.at[tile(dest_smem[k, tok])], rows.at[k, tile(tok)], sem).start(priority=k % 2)

    def wait_rows(rows, sem):
        for k in range(TOP_K):
            pltpu.make_async_copy(ybuf_hbm.at[pl.ds(0, tm * ROW_TILE)], rows.at[k], sem).wait()

    @pl.when(i == 0)
    def _():
        load_indices(dest_ref)

        def issue(tok, carry):
            for k in range(TOP_K):
                row_copy(tok, k, rows_a, row_sems.at[0])
            return carry

        lax.fori_loop(0, tm, issue, 0)

    load_indices(dest_next_ref)
    half = ROW_TILE * LANES
    per_chunk = tm // ROW_TILE

    def step(cur, cur_sem, nxt, nxt_sem):
        wait_rows(cur, cur_sem)
        weights = [jnp.broadcast_to(w_ref[:, k:k + 1], (tm, LANES)) for k in range(TOP_K)]
        for s in range(ROW_TILE):
            for tok in range(s * per_chunk, (s + 1) * per_chunk):
                for k in range(TOP_K):
                    row_copy(tok, k, nxt, nxt_sem)
            routed_lo = jnp.zeros((tm, LANES), F32)
            routed_hi = jnp.zeros((tm, LANES), F32)
            for k in range(TOP_K):
                lo, hi = _unpack_bf16_halves(cur[k, pl.ds(s, tm, stride=ROW_TILE), :])
                routed_lo = routed_lo + weights[k] * lo
                routed_hi = routed_hi + weights[k] * hi
            cl = slice(s * LANES, (s + 1) * LANES)
            ch = slice(half + s * LANES, half + (s + 1) * LANES)
            o_ref[:, cl] = x_ref[:, cl] + g_ref[:, cl] * (routed_lo + sh_ref[:, cl])
            o_ref[:, ch] = x_ref[:, ch] + g_ref[:, ch] * (routed_hi + sh_ref[:, ch])

        @pl.when(i + 1 == pl.num_programs(0))
        def _():
            wait_rows(nxt, nxt_sem)

    @pl.when(i % 2 == 0)
    def _():
        step(rows_a, row_sems.at[0], rows_b, row_sems.at[1])

    @pl.when(i % 2 == 1)
    def _():
        step(rows_b, row_sems.at[1], rows_a, row_sems.at[0])


def _combine(dest_t, w_tok, x, shared, gate, ybuf):
    t, d = x.shape
    tm = min(t, MOE_ROW_TILE)
    n_tiles = t // tm
    return pl.pallas_call(
        _combine_kernel,
        out_shape=jax.ShapeDtypeStruct((t, d), F32),
        grid=(n_tiles,),
        in_specs=[pl.BlockSpec((TOP_K, tm), lambda i: (0, i)),
                  pl.BlockSpec((TOP_K, tm), lambda i: (0, jnp.minimum(i + 1, n_tiles - 1))),
                  pl.BlockSpec((tm, TOP_K), lambda i: (i, 0)),
                  pl.BlockSpec((tm, d), lambda i: (i, 0)),
                  pl.BlockSpec((tm, d), lambda i: (i, 0)),
                  pl.BlockSpec((1, d), lambda i: (0, 0)),
                  pl.BlockSpec(memory_space=pl.ANY)],
        out_specs=pl.BlockSpec((tm, d), lambda i: (i, 0)),
        scratch_shapes=[pltpu.VMEM((TOP_K, tm * ROW_TILE, LANES), U32), pltpu.VMEM((TOP_K, tm * ROW_TILE, LANES), U32),
                        pltpu.SMEM((TOP_K, tm), I32), pltpu.SemaphoreType.DMA, pltpu.SemaphoreType.DMA((2,))],
        compiler_params=_params("arbitrary"),
        name="moe_combine",
    )(dest_t, dest_t, w_tok, x, shared, gate, ybuf)


def _moe_layer(x, shift, scale, gate, layer, w_router, router_bias, w_gate_up, w_down, ws_gate_up, ws_down):
    t, d = x.shape
    assert d == 2 * ROW_TILE * LANES, "a packed row must be exactly one (8, 128) tile of 32-bit words"
    npad = t * TOP_K + N_EXPERTS * E_BLOCK
    nblk = npad // E_BLOCK
    hf, idx_t, wts_t, rank_t, counts = _moe_pre(x, shift, scale, w_router, router_bias)
    dest_t, block_e, nused, tail_start = _moe_dest(idx_t, rank_t, counts, nblk)
    nused = nused.reshape(-1)[:1]
    xbuf, shared = _dispatch(hf, dest_t, tail_start.reshape(-1), nused, ws_gate_up.astype(BF16),
                             ws_down.astype(BF16), npad)
    ybuf = _experts(xbuf, block_e.reshape(-1), nused, w_gate_up, w_down, layer)
    return _combine(dest_t, wts_t.T, x, shared, gate, ybuf)


def _final_norm_kernel(x_ref, g_ref, o_ref):
    x = x_ref[...]
    o_ref[...] = x * lax.rsqrt(jnp.mean(x * x, axis=-1, keepdims=True) + EPS) * g_ref[...]


def _final_norm(x, gain):
    t, d = x.shape
    tm = min(t, PROJ_ROW_TILE)
    return pl.pallas_call(
        _final_norm_kernel,
        out_shape=jax.ShapeDtypeStruct((t, d), F32),
        grid=(t // tm,),
        in_specs=[pl.BlockSpec((tm, d), lambda i: (i, 0)), pl.BlockSpec((1, d), lambda i: (0, 0))],
        out_specs=pl.BlockSpec((tm, d), lambda i: (i, 0)),
        compiler_params=_params("parallel"),
        name="final_norm",
    )(x, gain.reshape(1, d))


def _even_layer(x, mod, cos, sin, w_in, b_if, mlstm_norm, q_norm, kv_norm, w_uq, w_ukv, w_out):
    t, d = x.shape
    sh1, sc1, g1 = mod[0], mod[1], mod[2]
    nq, nv = M_HEADS * M_QK, M_HEADS * M_V
    o = 0
    cols = []
    for sz in (nq, nq, nv, nv, M_HEADS, M_HEADS, A_Q_LORA, A_KV_LORA, A_ROPE):
        cols.append(w_in[:, o:o + sz])
        o += sz
    mq, mk, mv, mo, mi, mf, cq, ckv, kr = cols
    w_a = jnp.concatenate([mq, mk, mv, mo], axis=1).astype(BF16)
    zeros = lambda n: jnp.zeros((d, n), F32)
    w_b = jnp.concatenate([cq, ckv, kr, zeros(LANES - A_ROPE), mi, mf, zeros(LANES - 2 * M_HEADS)], axis=1).astype(BF16)
    gate_col_block = (A_Q_LORA + A_KV_LORA + LANES) // LANES
    proj_a, proj_b = _norm_matmul_pair(x, sh1, sc1, w_a, w_b, BF16, F32, "even_in")
    gate_bias = jnp.zeros((1, LANES), F32).at[0, :2 * M_HEADS].set(b_if.astype(F32))
    hm = _mlstm(proj_a, proj_b, gate_col_block, gate_bias, mlstm_norm.astype(F32).reshape(1, nv))

    qk = A_NOPE + A_ROPE
    wq = jnp.pad(w_uq.reshape(A_Q_LORA, A_HEADS, qk), ((0, 0), (0, 0), (0, A_QK_PAD - qk)))
    wq = wq.reshape(A_Q_LORA, A_HEADS * A_QK_PAD).astype(BF16)
    q, k, v = _mla_up(proj_b, q_norm.astype(F32).reshape(1, -1), kv_norm.astype(F32).reshape(1, -1),
                      wq, w_ukv.astype(BF16), cos, sin)
    ha = _mla_flash(q, k, v)
    w_out = w_out.astype(BF16)
    return _proj_residual([hm, ha], [w_out[:nv], w_out[nv:]], x, g1, "even_out")


def _odd_layer(x, mod, cos, sin, w_qkv, b_qkv, sinks, w_o):
    sh1, sc1, g1 = mod[0], mod[1], mod[2]
    qkv = _norm_matmul(x, sh1, sc1, w_qkv.astype(BF16), b_qkv.astype(F32).reshape(1, -1), BF16, "odd_qkv")
    o = _swa(qkv, cos, sin, sinks)
    return _proj_residual([o], [w_o.astype(BF16)], x, g1, "odd_out")


def kernel(x, c, positions, w_ada, b_ada, a_w_in, a_b_if, a_mlstm_norm, a_q_norm, a_kv_norm, a_w_uq, a_w_ukv,
           a_w_out, s_w_qkv, s_b_qkv, s_sinks, s_w_o, e_w_router, e_router_bias, e_w_gate_up, e_w_down,
           e_ws_gate_up, e_ws_down, final_norm):
    batch, t, d = x.shape
    assert batch == 1, "kernels are written for a single sequence"
    depth = w_ada.shape[0]
    xs = x.reshape(t, d)
    mods = _ada(c, w_ada, b_ada).reshape(depth, 6, 1, d)
    cos, sin = _rope_tables(positions.reshape(t))
    for layer in range(depth):
        mod = mods[layer]
        if layer % 2 == 0:
            e = layer // 2
            xs = _even_layer(xs, mod, cos, sin, a_w_in[e], a_b_if[e], a_mlstm_norm[e], a_q_norm[e], a_kv_norm[e],
                             a_w_uq[e], a_w_ukv[e], a_w_out[e])
        else:
            o = layer // 2
            xs = _odd_layer(xs, mod, cos, sin, s_w_qkv[o], s_b_qkv[o], s_sinks[o], s_w_o[o])
        xs = _moe_layer(xs, mod[3], mod[4], mod[5], layer, e_w_router[layer], e_router_bias[layer],
                        e_w_gate_up, e_w_down, e_ws_gate_up[layer], e_ws_down[layer])
    return _final_norm(xs, final_norm.astype(F32)).reshape(batch, t, d)
```

```python
import functools

import jax
import jax.numpy as jnp
from jax import lax
from jax.experimental import pallas as pl
from jax.experimental.pallas import tpu as pltpu

F32 = jnp.float32
BF16 = jnp.bfloat16
I32 = jnp.int32
U32 = jnp.uint32
HIGHEST = lax.Precision.HIGHEST

EPS = 1e-6
ROPE_THETA = 10000.0
ROPE_DIM = 64

M_HEADS = 4
M_QK = 128
M_V = 256
GATE_SOFTCAP = 15.0
MLSTM_CHUNK = 128

A_HEADS = 8
A_NOPE = 128
A_ROPE = 64
A_V = 128
A_Q_LORA = 768
A_KV_LORA = 512
A_QK_PAD = 256

S_HEADS = 32
S_KV_HEADS = 4
S_HD = 64
S_BLOCK = 128

N_EXPERTS = 64
TOP_K = 8
N_GROUPS = 8
TOPK_GROUPS = 4
E_FF = 256
SHARED_FF = 256
ROUTED_SCALE = 2.5
E_BLOCK = 512

PROJ_ROW_TILE = 1024
PROJ_COL_CAP = 1280
PAIR_COL_CAP = 1024
RESIDUAL_COL_CAP = 1024
ADA_COL_TILE = 1024
ROPE_ROW_TILE = 2048
MLSTM_ROW_TILE = 512
MLA_UP_ROW_TILE = 512
FLASH_Q_TILE = 1024
FLASH_KV_TILE = 2048
ROUTER_ROW_TILE = 512
DEST_ROW_TILE = 2048
MOE_ROW_TILE = 256

LANES = 128
VMEM_LIMIT_BYTES = 48 * 1024 * 1024
LARGE_VMEM_LIMIT_BYTES = 56 * 1024 * 1024
LOG2_E = 1.4426950408889634


def _params(*semantics):
    return pltpu.CompilerParams(dimension_semantics=semantics, vmem_limit_bytes=VMEM_LIMIT_BYTES)


def _dot(a, b, precision=None):
    return jnp.dot(a, b, preferred_element_type=F32, precision=precision)


def _dot_nt(a, b):
    return lax.dot_general(a, b, (((1,), (1,)), ((), ())), preferred_element_type=F32)


def _dot_tn(a, b):
    return lax.dot_general(a, b, (((0,), (0,)), ((), ())), preferred_element_type=F32)


def _modulated_norm(x, shift, scale):
    y = x * lax.rsqrt(jnp.mean(x * x, axis=-1, keepdims=True) + EPS)
    return y * (1.0 + scale) + shift


def _rope_lanes(x, cos, sin_signed):
    lane = lax.broadcasted_iota(I32, x.shape, 1)
    first_half = (lane % ROPE_DIM) < (ROPE_DIM // 2)
    swapped = jnp.where(first_half, pltpu.roll(x, LANES - ROPE_DIM // 2, 1), pltpu.roll(x, ROPE_DIM // 2, 1))
    return x * cos + swapped * sin_signed


def _pack_bf16_halves(x):
    n = x.shape[1] // 2
    lo = lax.bitcast_convert_type(x[:, :n].astype(BF16).astype(F32), U32)
    hi = lax.bitcast_convert_type(x[:, n:].astype(BF16).astype(F32), U32)
    return (hi & jnp.uint32(0xFFFF0000)) | (lo >> 16)


def _unpack_bf16_halves(w):
    lo = lax.bitcast_convert_type(w << 16, F32)
    hi = lax.bitcast_convert_type(w & jnp.uint32(0xFFFF0000), F32)
    return lo, hi


ROW_TILE = 8


def _load_rows(ref):
    m = ref.shape[0] // ROW_TILE
    return jnp.concatenate([ref[pl.ds(s, m, stride=ROW_TILE), :] for s in range(ROW_TILE)], axis=1)


def _store_rows(ref, x):
    m = x.shape[0]
    for s in range(ROW_TILE):
        ref[pl.ds(s, m, stride=ROW_TILE), :] = x[:, s * LANES:(s + 1) * LANES]


def _ada_kernel(c_ref, w_ref, b_ref, o_ref):
    c = c_ref[...]
    c_act = c * jax.nn.sigmoid(c)
    o_ref[...] = jnp.sum(c_act * w_ref[...], axis=0, keepdims=True) + b_ref[...]


def _ada(c, w_ada, b_ada):
    depth, d, n = w_ada.shape
    tn = ADA_COL_TILE
    return pl.pallas_call(
        _ada_kernel,
        out_shape=jax.ShapeDtypeStruct((depth, 1, n), F32),
        grid=(depth, n // tn),
        in_specs=[pl.BlockSpec((d, 1), lambda l, j: (0, 0)),
                  pl.BlockSpec((None, d, tn), lambda l, j: (l, 0, j)),
                  pl.BlockSpec((None, 1, tn), lambda l, j: (l, 0, j))],
        out_specs=pl.BlockSpec((None, 1, tn), lambda l, j: (l, 0, j)),
        compiler_params=_params("parallel", "parallel"),
        name="ada_mod",
    )(c.reshape(d, 1), w_ada, b_ada.reshape(depth, 1, n))


def _rope_table_kernel(pos_ref, inv_ref, sign_ref, cos_ref, sin_ref):
    ang = pos_ref[...].astype(F32) * inv_ref[...]
    cos_ref[...] = jnp.cos(ang)
    sin_ref[...] = jnp.sin(ang) * sign_ref[...]


def _rope_tables(positions):
    t = positions.shape[0]
    half = ROPE_DIM // 2
    inv_freq = jnp.power(ROPE_THETA, -jnp.arange(half, dtype=F32) / half)
    inv = jnp.tile(inv_freq, LANES // half).reshape(1, LANES)
    sign = jnp.tile(jnp.concatenate([-jnp.ones((half,), F32), jnp.ones((half,), F32)]), LANES // ROPE_DIM)
    tm = min(t, ROPE_ROW_TILE)
    return pl.pallas_call(
        _rope_table_kernel,
        out_shape=(jax.ShapeDtypeStruct((t, LANES), F32), jax.ShapeDtypeStruct((t, LANES), F32)),
        grid=(t // tm,),
        in_specs=[pl.BlockSpec((tm, 1), lambda i: (i, 0)),
                  pl.BlockSpec((1, LANES), lambda i: (0, 0)),
                  pl.BlockSpec((1, LANES), lambda i: (0, 0))],
        out_specs=(pl.BlockSpec((tm, LANES), lambda i: (i, 0)), pl.BlockSpec((tm, LANES), lambda i: (i, 0))),
        compiler_params=_params("parallel"),
        name="rope_tables",
    )(positions.reshape(t, 1), inv, sign.reshape(1, LANES))


def _norm_matmul_kernel(x_ref, sh_ref, sc_ref, w_ref, b_ref, o_ref, h_scr):
    @pl.when(pl.program_id(1) == 0)
    def _():
        h_scr[...] = _modulated_norm(x_ref[...], sh_ref[...], sc_ref[...]).astype(BF16)

    o_ref[...] = (_dot(h_scr[...], w_ref[...]) + b_ref[...]).astype(o_ref.dtype)


def _col_tile(n, cap):
    return max(c for c in range(LANES, min(n, cap) + 1, LANES) if n % c == 0)


def _norm_matmul(x, shift, scale, w, bias, out_dtype, name):
    t, d = x.shape
    n = w.shape[1]
    tm = min(t, PROJ_ROW_TILE)
    tn = _col_tile(n, PROJ_COL_CAP)
    return pl.pallas_call(
        _norm_matmul_kernel,
        out_shape=jax.ShapeDtypeStruct((t, n), out_dtype),
        grid=(t // tm, n // tn),
        in_specs=[pl.BlockSpec((tm, d), lambda i, j: (i, 0)),
                  pl.BlockSpec((1, d), lambda i, j: (0, 0)),
                  pl.BlockSpec((1, d), lambda i, j: (0, 0)),
                  pl.BlockSpec((d, tn), lambda i, j: (0, j)),
                  pl.BlockSpec((1, tn), lambda i, j: (0, j))],
        out_specs=pl.BlockSpec((tm, tn), lambda i, j: (i, j)),
        scratch_shapes=[pltpu.VMEM((tm, d), BF16)],
        compiler_params=_params("parallel", "arbitrary"),
        name=name,
    )(x, shift, scale, w, bias)


def _norm_matmul_pair_kernel(x_ref, sh_ref, sc_ref, wa_ref, wb_ref, oa_ref, ob_ref, h_scr, *, steps_a):
    j = pl.program_id(1)

    @pl.when(j == 0)
    def _():
        h_scr[...] = _modulated_norm(x_ref[...], sh_ref[...], sc_ref[...]).astype(BF16)

    @pl.when(j < steps_a)
    def _():
        oa_ref[...] = _dot(h_scr[...], wa_ref[...]).astype(oa_ref.dtype)

    @pl.when(j >= steps_a)
    def _():
        ob_ref[...] = _dot(h_scr[...], wb_ref[...]).astype(ob_ref.dtype)


def _norm_matmul_pair(x, shift, scale, wa, wb, dtype_a, dtype_b, name):
    t, d = x.shape
    na, nb = wa.shape[1], wb.shape[1]
    tm = min(t, PROJ_ROW_TILE)
    ta, tb = _col_tile(na, PAIR_COL_CAP), _col_tile(nb, PAIR_COL_CAP)
    steps_a, steps_b = na // ta, nb // tb
    col_a = lambda j: jnp.minimum(j, steps_a - 1)
    col_b = lambda j: jnp.maximum(j - steps_a, 0)
    return pl.pallas_call(
        functools.partial(_norm_matmul_pair_kernel, steps_a=steps_a),
        out_shape=(jax.ShapeDtypeStruct((t, na), dtype_a), jax.ShapeDtypeStruct((t, nb), dtype_b)),
        grid=(t // tm, steps_a + steps_b),
        in_specs=[pl.BlockSpec((tm, d), lambda i, j: (i, 0)),
                  pl.BlockSpec((1, d), lambda i, j: (0, 0)),
                  pl.BlockSpec((1, d), lambda i, j: (0, 0)),
                  pl.BlockSpec((d, ta), lambda i, j: (0, col_a(j))),
                  pl.BlockSpec((d, tb), lambda i, j: (0, col_b(j)))],
        out_specs=(pl.BlockSpec((tm, ta), lambda i, j: (i, col_a(j))),
                   pl.BlockSpec((tm, tb), lambda i, j: (i, col_b(j)))),
        scratch_shapes=[pltpu.VMEM((tm, d), BF16)],
        compiler_params=pltpu.CompilerParams(dimension_semantics=("parallel", "arbitrary"),
                                             vmem_limit_bytes=LARGE_VMEM_LIMIT_BYTES),
        name=name,
    )(x, shift, scale, wa, wb)


def _mlstm_kernel(q_ref, k_ref, v_ref, o_ref, g_ref, gb_ref, gain_ref, out_ref, c_scr, n_scr, m_scr):
    L = MLSTM_CHUNK
    tm = q_ref.shape[0]

    @pl.when(pl.program_id(0) == 0)
    def _():
        c_scr[...] = jnp.zeros_like(c_scr)
        n_scr[...] = jnp.zeros_like(n_scr)
        m_scr[...] = jnp.zeros_like(m_scr)

    capped = GATE_SOFTCAP * jnp.tanh((g_ref[...] + gb_ref[...]) / GATE_SOFTCAP)
    log_sig = jnp.minimum(capped, 0.0) - jnp.log1p(jnp.exp(-jnp.abs(capped)))
    lane = lax.broadcasted_iota(I32, capped.shape, 1)
    gate = jnp.where(lane < M_HEADS, capped, log_sig)
    r = lax.broadcasted_iota(I32, (tm, tm), 0)
    c = lax.broadcasted_iota(I32, (tm, tm), 1)
    chunk_tril = jnp.where(((r // L) == (c // L)) & (c <= r), 1.0, 0.0).astype(F32)
    cum = _dot(chunk_tril, gate, precision=HIGHEST)
    gate_rows = gate.T
    cum_rows = cum.T
    heads = range(M_HEADS)
    rr = lax.broadcasted_iota(I32, (M_HEADS * L, L), 0) % L
    cc = lax.broadcasted_iota(I32, (M_HEADS * L, L), 1)
    causal = cc <= rr

    def stack(parts):
        return jnp.concatenate(parts, axis=0)

    def rows_of(x, n=L):
        return stack([jnp.broadcast_to(x[h:h + 1, :], (n, LANES)) for h in heads])

    def wide(x):
        return jnp.concatenate([x] * (M_V // LANES), axis=1)

    c_states = [c_scr[h] for h in heads]
    n_states = [n_scr[h] for h in heads]
    m_prev = m_scr[0:M_HEADS, :]
    gain = stack([jnp.broadcast_to(gain_ref[:, h * M_V:(h + 1) * M_V], (L, M_V)) for h in heads])
    for ci in range(tm // L):
        sl = slice(ci * L, (ci + 1) * L)
        q_heads = [q_ref[sl, h * M_QK:(h + 1) * M_QK] for h in heads]
        k_all = k_ref[sl, :].astype(F32) * (M_QK ** -0.5)
        k_heads = [k_all[:, h * M_QK:(h + 1) * M_QK] for h in heads]
        kb_heads = [k.astype(BF16) for k in k_heads]
        v_heads = [v_ref[sl, h * M_V:(h + 1) * M_V] for h in heads]

        ig_col = stack([jnp.broadcast_to(gate[sl, h:h + 1], (L, LANES)) for h in heads])
        b_col = stack([jnp.broadcast_to(cum[sl, M_HEADS + h:M_HEADS + h + 1], (L, LANES)) for h in heads])
        ig_rows = gate_rows[0:M_HEADS, sl]
        b_rows = cum_rows[M_HEADS:2 * M_HEADS, sl]
        ig_row = rows_of(ig_rows)
        b_row = rows_of(b_rows)

        dm = jnp.where(causal, b_col - b_row + ig_row, -jnp.inf)
        inter = b_col + rows_of(m_prev)
        m_row = jnp.maximum(inter, jnp.broadcast_to(jnp.max(dm, axis=-1, keepdims=True), inter.shape))
        w_intra = jnp.exp(dm - m_row)
        w_inter = jnp.exp(inter - m_row)
        s = stack([_dot_nt(q_heads[h], kb_heads[h]) for h in heads]) * w_intra
        s_b = s.astype(BF16)
        intra = stack([_dot(s_b[h * L:(h + 1) * L], v_heads[h]) for h in heads])
        carried = stack([_dot_nt(q_heads[h], c_states[h].astype(BF16)) for h in heads])
        num = intra + wide(w_inter) * carried
        q_all = stack([q.astype(F32) for q in q_heads])
        n_all = stack([jnp.broadcast_to(n_states[h], (L, M_QK)) for h in heads])
        qn = jnp.broadcast_to(jnp.sum(q_all * n_all, axis=-1, keepdims=True), inter.shape)
        den = jnp.broadcast_to(jnp.sum(s, axis=-1, keepdims=True), inter.shape) + w_inter * qn
        hh = num / wide(jnp.maximum(jnp.abs(den), jnp.exp(-m_row)))

        b_last = jnp.broadcast_to(b_rows[:, L - 1:L], (M_HEADS, LANES))
        g_rows = b_last - b_rows + ig_rows
        m_new = jnp.maximum(b_last + m_prev, jnp.broadcast_to(jnp.max(g_rows, axis=-1, keepdims=True), b_last.shape))
        decay = jnp.exp(b_last + m_prev - m_new)
        ws_col = jnp.exp(rows_of(b_last) - b_col + ig_col - rows_of(m_new))
        for h in heads:
            ws_h = ws_col[h * L:(h + 1) * L]
            vw = (v_heads[h].astype(F32) * wide(ws_h)).astype(BF16)
            c_states[h] = (jnp.broadcast_to(decay[h:h + 1, :], (M_V, M_QK)) * c_states[h]
                           + _dot_tn(vw, kb_heads[h]))
            n_states[h] = decay[h:h + 1, :] * n_states[h] + jnp.sum(k_heads[h] * ws_h, axis=0, keepdims=True)
        m_prev = m_new

        inv_rms = lax.rsqrt(jnp.mean(hh * hh, axis=-1, keepdims=True) + EPS)
        o_gate = stack([o_ref[sl, h * M_V:(h + 1) * M_V] for h in heads]).astype(F32)
        y = hh * inv_rms * gain * jax.nn.sigmoid(o_gate)
        for h in heads:
            out_ref[sl, h * M_V:(h + 1) * M_V] = y[h * L:(h + 1) * L].astype(out_ref.dtype)
    for h in heads:
        c_scr[h], n_scr[h] = c_states[h], n_states[h]
    m_scr[0:M_HEADS, :] = m_prev


def _mlstm(proj_a, proj_b, gate_col_block, gate_bias, gain):
    t = proj_a.shape[0]
    tm = min(t, MLSTM_ROW_TILE)
    nq = M_HEADS * M_QK
    nv = M_HEADS * M_V
    return pl.pallas_call(
        _mlstm_kernel,
        out_shape=jax.ShapeDtypeStruct((t, nv), BF16),
        grid=(t // tm,),
        in_specs=[pl.BlockSpec((tm, nq), lambda i: (i, 0)),
                  pl.BlockSpec((tm, nq), lambda i: (i, 1)),
                  pl.BlockSpec((tm, nv), lambda i: (i, 1)),
                  pl.BlockSpec((tm, nv), lambda i: (i, 2)),
                  pl.BlockSpec((tm, LANES), lambda i: (i, gate_col_block)),
                  pl.BlockSpec((1, LANES), lambda i: (0, 0)),
                  pl.BlockSpec((1, nv), lambda i: (0, 0))],
        out_specs=pl.BlockSpec((tm, nv), lambda i: (i, 0)),
        scratch_shapes=[pltpu.VMEM((M_HEADS, M_V, M_QK), F32),
                        pltpu.VMEM((M_HEADS, 1, M_QK), F32),
                        pltpu.VMEM((8, LANES), F32)],
        compiler_params=_params("arbitrary"),
        name="mlstm",
    )(proj_a, proj_a, proj_a, proj_a, proj_b, gate_bias, gain)


def _mla_up_kernel(pb_ref, qn_ref, kvn_ref, wq_ref, wkv_ref, cos_ref, sin_ref, q_ref, k_ref, v_ref):
    cos, sin = cos_ref[...], sin_ref[...]
    cq = pb_ref[:, 0:A_Q_LORA]
    cq = (cq * lax.rsqrt(jnp.mean(cq * cq, axis=-1, keepdims=True) + EPS) * qn_ref[...]).astype(BF16)
    ckv = pb_ref[:, A_Q_LORA:A_Q_LORA + A_KV_LORA]
    ckv = (ckv * lax.rsqrt(jnp.mean(ckv * ckv, axis=-1, keepdims=True) + EPS) * kvn_ref[...]).astype(BF16)
    k_pe = _rope_lanes(pb_ref[:, A_Q_LORA + A_KV_LORA:A_Q_LORA + A_KV_LORA + LANES], cos, sin).astype(BF16)
    scale = (A_NOPE + A_ROPE) ** -0.5 * LOG2_E
    for h in range(A_HEADS):
        qh = _dot(cq, wq_ref[:, h * A_QK_PAD:(h + 1) * A_QK_PAD])
        q_pe = _rope_lanes(qh[:, A_NOPE:], cos, sin)
        q_ref[h] = (jnp.concatenate([qh[:, :A_NOPE], q_pe], axis=1) * scale).astype(q_ref.dtype)
        kvh = _dot(ckv, wkv_ref[:, h * (A_NOPE + A_V):(h + 1) * (A_NOPE + A_V)])
        k_ref[h] = jnp.concatenate([kvh[:, :A_NOPE].astype(BF16), k_pe], axis=1)
        v_ref[h] = kvh[:, A_NOPE:].astype(v_ref.dtype)


def _mla_up(proj_b, q_norm, kv_norm, wq, wkv, cos, sin):
    t, nb = proj_b.shape
    tm = min(t, MLA_UP_ROW_TILE)
    return pl.pallas_call(
        _mla_up_kernel,
        out_shape=(jax.ShapeDtypeStruct((A_HEADS, t, A_QK_PAD), BF16),
                   jax.ShapeDtypeStruct((A_HEADS, t, A_QK_PAD), BF16),
                   jax.ShapeDtypeStruct((A_HEADS, t, A_V), BF16)),
        grid=(t // tm,),
        in_specs=[pl.BlockSpec((tm, nb), lambda i: (i, 0)),
                  pl.BlockSpec((1, A_Q_LORA), lambda i: (0, 0)),
                  pl.BlockSpec((1, A_KV_LORA), lambda i: (0, 0)),
                  pl.BlockSpec(wq.shape, lambda i: (0, 0)),
                  pl.BlockSpec(wkv.shape, lambda i: (0, 0)),
                  pl.BlockSpec((tm, LANES), lambda i: (i, 0)),
                  pl.BlockSpec((tm, LANES), lambda i: (i, 0))],
        out_specs=(pl.BlockSpec((A_HEADS, tm, A_QK_PAD), lambda i: (0, i, 0)),
                   pl.BlockSpec((A_HEADS, tm, A_QK_PAD), lambda i: (0, i, 0)),
                   pl.BlockSpec((A_HEADS, tm, A_V), lambda i: (0, i, 0))),
        compiler_params=_params("parallel"),
        name="mla_up",
    )(proj_b, q_norm, kv_norm, wq, wkv, cos, sin)


def _mla_flash_kernel(q_ref, k_ref, v_ref, o_ref, m_scr, l_scr, acc_scr, *, tq, tkb):
    i = pl.program_id(1)
    q = q_ref[...]
    m_scr[...] = jnp.full(m_scr.shape, -jnp.inf, F32)
    l_scr[...] = jnp.zeros(l_scr.shape, F32)
    acc_scr[...] = jnp.zeros(acc_scr.shape, F32)

    def tile(start, width, masked):
        s = _dot_nt(q, k_ref[pl.ds(start, width), :])
        if masked:
            row = lax.broadcasted_iota(I32, (tq, width), 0)
            col = lax.broadcasted_iota(I32, (tq, width), 1)
            s = jnp.where(col <= row, s, -jnp.inf)
        m = m_scr[...]
        m_new = jnp.maximum(m, jnp.broadcast_to(jnp.max(s, axis=-1, keepdims=True), m.shape))
        alpha = jnp.exp2(m - m_new)
        p = jnp.exp2(s - jnp.concatenate([m_new] * (width // LANES), axis=1))
        lane_sums = p[:, 0:LANES]
        for j in range(1, width // LANES):
            lane_sums = lane_sums + p[:, j * LANES:(j + 1) * LANES]
        l_scr[...] = alpha * l_scr[...] + lane_sums
        acc_scr[...] = alpha * acc_scr[...] + _dot(p.astype(BF16), v_ref[pl.ds(start, width), :])
        m_scr[...] = m_new

    per = tkb // tq
    n_big = i // per

    def body(j, carry):
        tile(pl.multiple_of(j * tkb, tkb), tkb, False)
        return carry

    lax.fori_loop(0, n_big, body, 0)
    rem = i - n_big * per
    for r in range(1, per):
        @pl.when(rem >= r)
        def _():
            tile(pl.multiple_of(n_big * tkb + (r - 1) * tq, tq), tq, False)
    tile(pl.multiple_of(i * tq, tq), tq, True)
    o_ref[...] = (acc_scr[...] / jnp.sum(l_scr[...], axis=-1, keepdims=True)).astype(o_ref.dtype)


def _mla_flash(q, k, v):
    _, t, _ = q.shape
    tq = min(t, FLASH_Q_TILE)
    tkb = min(t, FLASH_KV_TILE)
    return pl.pallas_call(
        functools.partial(_mla_flash_kernel, tq=tq, tkb=tkb),
        out_shape=jax.ShapeDtypeStruct((t, A_HEADS * A_V), BF16),
        grid=(A_HEADS, t // tq),
        in_specs=[pl.BlockSpec((None, tq, A_QK_PAD), lambda h, i: (h, i, 0)),
                  pl.BlockSpec((None, t, A_QK_PAD), lambda h, i: (h, 0, 0)),
                  pl.BlockSpec((None, t, A_V), lambda h, i: (h, 0, 0))],
        out_specs=pl.BlockSpec((tq, A_V), lambda h, i: (i, h)),
        scratch_shapes=[pltpu.VMEM((tq, LANES), F32), pltpu.VMEM((tq, LANES), F32), pltpu.VMEM((tq, A_V), F32)],
        compiler_params=pltpu.CompilerParams(dimension_semantics=("parallel", "arbitrary"),
                                             vmem_limit_bytes=LARGE_VMEM_LIMIT_BYTES),
        name="mla_flash",
    )(q, k, v)


def _proj_residual_kernel(*refs, n_lhs):
    lhs = refs[:n_lhs]
    ws = refs[n_lhs:2 * n_lhs]
    x_ref, g_ref, o_ref = refs[2 * n_lhs:]
    acc = _dot(lhs[0][...], ws[0][...])
    for a, w in zip(lhs[1:], ws[1:]):
        acc = acc + _dot(a[...], w[...])
    o_ref[...] = x_ref[...] + g_ref[...] * acc


def _proj_residual(lhs_list, w_list, x, gate, name):
    t, d = x.shape
    tm = min(t, PROJ_ROW_TILE)
    tn = _col_tile(d, RESIDUAL_COL_CAP)
    n_lhs = len(lhs_list)
    in_specs = ([pl.BlockSpec((tm, a.shape[1]), lambda i, j: (i, 0)) for a in lhs_list]
                + [pl.BlockSpec((w.shape[0], tn), lambda i, j: (0, j)) for w in w_list]
                + [pl.BlockSpec((tm, tn), lambda i, j: (i, j)), pl.BlockSpec((1, tn), lambda i, j: (0, j))])
    return pl.pallas_call(
        functools.partial(_proj_residual_kernel, n_lhs=n_lhs),
        out_shape=jax.ShapeDtypeStruct((t, d), F32),
        grid=(t // tm, d // tn),
        in_specs=in_specs,
        out_specs=pl.BlockSpec((tm, tn), lambda i, j: (i, j)),
        compiler_params=_params("parallel", "parallel"),
        name=name,
    )(*lhs_list, *w_list, x, gate)


def _swa_kernel(q_ref, kc_ref, kp_ref, vc_ref, vp_ref, cosc_ref, sinc_ref, cosp_ref, sinp_ref, sink_ref, o_ref):
    i = pl.program_id(0)
    qb = S_BLOCK
    group = S_HEADS // S_KV_HEADS
    pairs = group // 2
    cos_c, sin_c = cosc_ref[...], sinc_ref[...]
    cos_w = jnp.concatenate([cosp_ref[...], cos_c], axis=0)
    sin_w = jnp.concatenate([sinp_ref[...], sin_c], axis=0)
    kw = jnp.concatenate([kp_ref[...], kc_ref[...]], axis=0).astype(F32)
    vw = jnp.concatenate([vp_ref[...], vc_ref[...]], axis=0)

    r = lax.broadcasted_iota(I32, (pairs * qb, 2 * qb), 0) % qb
    c = lax.broadcasted_iota(I32, (pairs * qb, 2 * qb), 1)
    dist = qb + r - c
    valid = (dist >= 0) & (dist < S_BLOCK) & ((c >= qb) | (i > 0))
    lane = lax.broadcasted_iota(I32, (2 * qb, LANES), 1)
    low = lane < S_HD
    low_rows = lax.broadcasted_iota(I32, (pairs * qb, LANES), 1) < S_HD
    ones_d = jnp.concatenate([jnp.where(low, 1.0, 0.0), jnp.where(low, 0.0, 1.0)], axis=0).astype(BF16)

    for g in range(S_KV_HEADS):
        col = slice((g // 2) * LANES, (g // 2 + 1) * LANES)
        k_pair = _rope_lanes(kw[:, col], cos_w, sin_w)
        v_pair = vw[:, col]
        keep = low if g % 2 == 0 else jnp.logical_not(low)
        k_own = jnp.where(keep, k_pair, 0.0)
        v_own = jnp.where(keep, v_pair.astype(F32), 0.0)
        k_other = pltpu.roll(k_own, S_HD, 1)
        v_other = pltpu.roll(v_own, S_HD, 1)
        k_lo, k_hi = (k_own, k_other) if g % 2 == 0 else (k_other, k_own)
        v_lo, v_hi = (v_own, v_other) if g % 2 == 0 else (v_other, v_own)
        kd = jnp.concatenate([k_lo, k_hi], axis=0).astype(BF16)
        vd = jnp.concatenate([v_lo, v_hi], axis=0).astype(BF16)
        qs = []
        for p in range(pairs):
            qcol = slice((g * pairs + p) * LANES, (g * pairs + p + 1) * LANES)
            qs.append(_rope_lanes(q_ref[:, qcol].astype(F32), cos_c, sin_c) * (S_HD ** -0.5 * LOG2_E))
        qg = jnp.concatenate(qs, axis=0).astype(BF16)
        s = _dot_nt(qg, kd)
        exps, sink_terms = [], []
        for half in range(2):
            sh = jnp.where(valid, s[:, half * 2 * qb:(half + 1) * 2 * qb], -jnp.inf)
            sink = sink_ref[g, half]
            m = jnp.maximum(jnp.broadcast_to(jnp.max(sh, axis=-1, keepdims=True), sink.shape), sink)
            exps.append(jnp.exp2(sh - jnp.concatenate([m, m], axis=1)).astype(BF16))
            sink_terms.append(jnp.exp2(sink - m))
        e_all = jnp.concatenate(exps, axis=1)
        den = _dot(e_all, ones_d) + jnp.where(low_rows, sink_terms[0], sink_terms[1])
        o = _dot(e_all, vd) / den
        for p in range(pairs):
            qcol = slice((g * pairs + p) * LANES, (g * pairs + p + 1) * LANES)
            o_ref[:, qcol] = o[p * qb:(p + 1) * qb].astype(o_ref.dtype)


def _swa(qkv, cos, sin, sinks):
    t = qkv.shape[0]
    qb = S_BLOCK
    nq = S_HEADS * S_HD
    nk = S_KV_HEADS * S_HD
    group = S_HEADS // S_KV_HEADS
    pairs = group // 2
    sink_cols = jnp.repeat((sinks.astype(F32) * LOG2_E).reshape(S_KV_HEADS, pairs, 2).transpose(0, 2, 1), qb, axis=-1)
    sink_cols = jnp.broadcast_to(sink_cols.reshape(S_KV_HEADS, 2, pairs * qb, 1), (S_KV_HEADS, 2, pairs * qb, LANES))
    k_blk = nq // nk
    prev = lambda i: jnp.maximum(i - 1, 0)
    return pl.pallas_call(
        _swa_kernel,
        out_shape=jax.ShapeDtypeStruct((t, nq), BF16),
        grid=(t // qb,),
        in_specs=[pl.BlockSpec((qb, nq), lambda i: (i, 0)),
                  pl.BlockSpec((qb, nk), lambda i: (i, k_blk)),
                  pl.BlockSpec((qb, nk), lambda i: (prev(i), k_blk)),
                  pl.BlockSpec((qb, nk), lambda i: (i, k_blk + 1)),
                  pl.BlockSpec((qb, nk), lambda i: (prev(i), k_blk + 1)),
                  pl.BlockSpec((qb, LANES), lambda i: (i, 0)),
                  pl.BlockSpec((qb, LANES), lambda i: (i, 0)),
                  pl.BlockSpec((qb, LANES), lambda i: (prev(i), 0)),
                  pl.BlockSpec((qb, LANES), lambda i: (prev(i), 0)),
                  pl.BlockSpec((S_KV_HEADS, 2, pairs * qb, LANES), lambda i: (0, 0, 0, 0))],
        out_specs=pl.BlockSpec((qb, nq), lambda i: (i, 0)),
        compiler_params=_params("parallel"),
        name="swa",
    )(qkv, qkv, qkv, qkv, qkv, cos, sin, cos, sin, sink_cols)


def _expert_of_row(p):
    per_group = N_EXPERTS // N_GROUPS
    return (p % N_GROUPS) * per_group + p // N_GROUPS


def _moe_pre_kernel(x_ref, sh_ref, sc_ref, wr_ref, rb_ref, hf_ref, idx_ref, wt_ref, rank_ref, cnt_ref, carry_scr):
    tm = x_ref.shape[0]
    per_group = N_EXPERTS // N_GROUPS

    @pl.when(pl.program_id(0) == 0)
    def _():
        carry_scr[...] = jnp.zeros_like(carry_scr)

    h = _modulated_norm(x_ref[...], sh_ref[...], sc_ref[...])
    _store_rows(hf_ref, _pack_bf16_halves(h))
    h_hi = h.astype(BF16)
    h_lo = (h - h_hi.astype(F32)).astype(BF16)
    logits = _dot(h_hi, wr_ref[0]) + (_dot(h_hi, wr_ref[1]) + _dot(h_lo, wr_ref[0]))
    scores = jax.nn.sigmoid(logits.T[:N_EXPERTS, :])
    biased = scores + rb_ref[...]

    members = [biased[j * N_GROUPS:(j + 1) * N_GROUPS, :] for j in range(per_group)]
    m1 = members[0]
    for a in members[1:]:
        m1 = jnp.maximum(m1, a)
    first = jnp.full(m1.shape, per_group, I32)
    for j in reversed(range(per_group)):
        first = jnp.where(members[j] == m1, j, first)
    m2 = jnp.full(m1.shape, -jnp.inf, F32)
    for j in range(per_group):
        m2 = jnp.maximum(m2, jnp.where(first == j, -jnp.inf, members[j]))
    group_score = m1 + m2

    g_iota = lax.broadcasted_iota(I32, group_score.shape, 0).astype(F32)
    g_sel = jnp.zeros(group_score.shape, F32)
    for _ in range(TOPK_GROUPS):
        best = jnp.max(group_score, axis=0, keepdims=True)
        gi = jnp.min(jnp.where(group_score == best, g_iota, float(N_GROUPS)), axis=0, keepdims=True)
        hit = g_iota == gi
        g_sel = jnp.where(hit, 1.0, g_sel)
        group_score = jnp.where(hit, -jnp.inf, group_score)
    masked = jnp.concatenate([jnp.where(g_sel > 0.5, a, -jnp.inf) for a in members], axis=0)

    e_iota = _expert_of_row(lax.broadcasted_iota(I32, masked.shape, 0)).astype(F32)
    sel = jnp.zeros(masked.shape, F32)
    idx_rows, w_rows = [], []
    for _ in range(TOP_K):
        best = jnp.max(masked, axis=0, keepdims=True)
        ei = jnp.min(jnp.where(masked == best, e_iota, float(N_EXPERTS)), axis=0, keepdims=True)
        hit = e_iota == ei
        idx_rows.append(ei)
        w_rows.append(jnp.sum(jnp.where(hit, scores, 0.0), axis=0, keepdims=True))
        sel = jnp.where(hit, 1.0, sel)
        masked = jnp.where(hit, -jnp.inf, masked)
    idx = jnp.concatenate(idx_rows, axis=0).astype(I32)
    wts = jnp.concatenate(w_rows, axis=0)
    wts = wts / jnp.sum(wts, axis=0, keepdims=True) * ROUTED_SCALE

    r = lax.broadcasted_iota(I32, (tm, tm), 0)
    c = lax.broadcasted_iota(I32, (tm, tm), 1)
    before = jnp.where(r < c, 1.0, 0.0).astype(BF16)
    rank_excl = carry_scr[:, 0:1] + _dot(sel.astype(BF16), before)
    rank_rows = [jnp.sum(jnp.where(e_iota == idx_rows[k], rank_excl, 0.0), axis=0, keepdims=True)
                 for k in range(TOP_K)]
    carry_scr[...] = carry_scr[...] + jnp.sum(sel, axis=1, keepdims=True)

    idx_ref[...] = idx
    wt_ref[...] = wts
    rank_ref[...] = jnp.concatenate(rank_rows, axis=0).astype(I32)
    cnt_ref[...] = carry_scr[...]


def _moe_pre(x, shift, scale, w_router, router_bias):
    t, d = x.shape
    tm = min(t, ROUTER_ROW_TILE)
    rows = jnp.arange(N_EXPERTS)
    perm = _expert_of_row(rows)
    wr = jnp.zeros((d, LANES), F32).at[:, :N_EXPERTS].set(w_router[:, perm])
    wr_hi = wr.astype(BF16)
    wr = jnp.stack([wr_hi, (wr - wr_hi.astype(F32)).astype(BF16)])
    rb = router_bias.astype(F32)[perm].reshape(N_EXPERTS, 1)
    per_token = lambda dtype: jax.ShapeDtypeStruct((TOP_K, t), dtype)
    return pl.pallas_call(
        _moe_pre_kernel,
        out_shape=(jax.ShapeDtypeStruct((t * ROW_TILE, LANES), U32), per_token(I32), per_token(F32), per_token(I32),
                   jax.ShapeDtypeStruct((N_EXPERTS, LANES), F32)),
        grid=(t // tm,),
        in_specs=[pl.BlockSpec((tm, d), lambda i: (i, 0)),
                  pl.BlockSpec((1, d), lambda i: (0, 0)),
                  pl.BlockSpec((1, d), lambda i: (0, 0)),
                  pl.BlockSpec((2, d, LANES), lambda i: (0, 0, 0)),
                  pl.BlockSpec((N_EXPERTS, 1), lambda i: (0, 0))],
        out_specs=(pl.BlockSpec((tm * ROW_TILE, LANES), lambda i: (i, 0)),
                   pl.BlockSpec((TOP_K, tm), lambda i: (0, i)),
                   pl.BlockSpec((TOP_K, tm), lambda i: (0, i)),
                   pl.BlockSpec((TOP_K, tm), lambda i: (0, i)),
                   pl.BlockSpec((N_EXPERTS, LANES), lambda i: (0, 0))),
        scratch_shapes=[pltpu.VMEM((N_EXPERTS, LANES), F32)],
        compiler_params=_params("arbitrary"),
        name="moe_router",
    )(x, shift, scale, wr, rb)


def _moe_dest_kernel(idx_ref, rank_ref, cnt_ref, dest_ref, be_ref, nused_ref, tail_ref):
    cnt = cnt_ref[...]
    padded = jnp.floor((cnt + (E_BLOCK - 1)) * (1.0 / E_BLOCK)) * E_BLOCK
    e_i = _expert_of_row(lax.broadcasted_iota(I32, (N_EXPERTS, N_EXPERTS), 0))
    e_j = _expert_of_row(lax.broadcasted_iota(I32, (N_EXPERTS, N_EXPERTS), 1))
    earlier = jnp.where(e_j < e_i, 1.0, 0.0).astype(F32)
    pstart = _dot(earlier, padded, precision=HIGHEST)
    pend = pstart + padded

    tm = idx_ref.shape[1]
    e_col = _expert_of_row(lax.broadcasted_iota(I32, (N_EXPERTS, tm), 0))
    rows = []
    for k in range(TOP_K):
        hit = e_col == idx_ref[k:k + 1, :]
        rows.append(jnp.sum(jnp.where(hit, pstart[:, 0:1], 0.0), axis=0, keepdims=True))
    dest_ref[...] = jnp.concatenate(rows, axis=0).astype(I32) + rank_ref[...]

    nb = be_ref.shape[1]
    block_start = (lax.broadcasted_iota(I32, (N_EXPERTS, nb), 1) * E_BLOCK).astype(F32)
    ended = jnp.sum(jnp.where(pend[:, 0:1] <= block_start, 1.0, 0.0), axis=0, keepdims=True)
    be_ref[...] = jnp.minimum(ended, float(N_EXPERTS - 1)).astype(I32)
    nused_ref[...] = (jnp.sum(padded, axis=0, keepdims=True) * (1.0 / E_BLOCK)).astype(I32)
    e_row = _expert_of_row(lax.broadcasted_iota(I32, (N_EXPERTS, LANES), 0))
    lane = lax.broadcasted_iota(I32, (N_EXPERTS, LANES), 1)
    end_lane = jnp.sum(jnp.where(e_row <= lane, padded, 0.0), axis=0, keepdims=True)
    own_lane = jnp.sum(jnp.where(e_row == lane, padded, 0.0), axis=0, keepdims=True)
    tail_ref[...] = jnp.where(own_lane > 0.0, end_lane - E_BLOCK, -1.0).astype(I32)


def _moe_dest(idx_t, rank_t, counts, nblk):
    t = idx_t.shape[1]
    tm = min(t, DEST_ROW_TILE)
    nb = ((nblk + LANES - 1) // LANES) * LANES
    return pl.pallas_call(
        _moe_dest_kernel,
        out_shape=(jax.ShapeDtypeStruct((TOP_K, t), I32),
                   jax.ShapeDtypeStruct((1, nb), I32),
                   jax.ShapeDtypeStruct((1, LANES), I32),
                   jax.ShapeDtypeStruct((1, LANES), I32)),
        grid=(t // tm,),
        in_specs=[pl.BlockSpec((TOP_K, tm), lambda i: (0, i)),
                  pl.BlockSpec((TOP_K, tm), lambda i: (0, i)),
                  pl.BlockSpec((N_EXPERTS, LANES), lambda i: (0, 0))],
        out_specs=(pl.BlockSpec((TOP_K, tm), lambda i: (0, i)),
                   pl.BlockSpec((1, nb), lambda i: (0, 0)),
                   pl.BlockSpec((1, LANES), lambda i: (0, 0)),
                   pl.BlockSpec((1, LANES), lambda i: (0, 0))),
        compiler_params=_params("arbitrary"),
        name="moe_dest",
    )(idx_t, rank_t, counts)


def _dispatch_kernel(tail_ref, nused_ref, dest_ref, dest_next_ref, hf_ref, wgu_ref, wd_ref, xbuf_hbm, sh_ref,
                     idx_a, idx_b, zero_scr, idx_sems, zero_sem, row_sem):
    tm = dest_ref.shape[1]
    block_sublanes = E_BLOCK * ROW_TILE
    nblk = xbuf_hbm.shape[0] // block_sublanes

    @pl.when(pl.program_id(0) == 0)
    def _():
        zero_scr[...] = jnp.zeros_like(zero_scr)

        def zero_copy(row):
            start = pl.multiple_of(row * ROW_TILE, block_sublanes)
            return pltpu.make_async_copy(zero_scr, xbuf_hbm.at[pl.ds(start, block_sublanes)], zero_sem)

        def start(e, carry):
            @pl.when(tail_ref[e] >= 0)
            def _():
                zero_copy(tail_ref[e]).start()
            return carry

        def wait(e, carry):
            @pl.when(tail_ref[e] >= 0)
            def _():
                zero_copy(0).wait()
            return carry

        def start_unused(b, carry):
            zero_copy(b * E_BLOCK).start()
            return carry

        def wait_unused(b, carry):
            zero_copy(0).wait()
            return carry

        lax.fori_loop(0, N_EXPERTS, start, 0)
        lax.fori_loop(nused_ref[0], nblk, start_unused, 0)
        lax.fori_loop(0, N_EXPERTS, wait, 0)
        lax.fori_loop(nused_ref[0], nblk, wait_unused, 0)

        first = pltpu.make_async_copy(dest_ref, idx_a, idx_sems.at[0])
        first.start()
        first.wait()

    def row(ref, r):
        return ref.at[pl.ds(pl.multiple_of(r * ROW_TILE, ROW_TILE), ROW_TILE)]

    def step(cur_idx, nxt_idx, nxt_idx_sem):
        prefetch = pltpu.make_async_copy(dest_next_ref, nxt_idx, nxt_idx_sem)
        prefetch.start()
        for tok in range(tm):
            for k in range(TOP_K):
                pltpu.make_async_copy(row(hf_ref, tok), row(xbuf_hbm, cur_idx[k, tok]), row_sem).start(priority=k % 2)
        sh_ref[...] = _gated_ffn(_load_rows(hf_ref), wgu_ref, wd_ref)
        for k in range(TOP_K):
            pltpu.make_async_copy(hf_ref, xbuf_hbm.at[pl.ds(0, tm * ROW_TILE)], row_sem).wait()
        prefetch.wait()

    @pl.when(pl.program_id(0) % 2 == 0)
    def _():
        step(idx_a, idx_b, idx_sems.at[1])

    @pl.when(pl.program_id(0) % 2 == 1)
    def _():
        step(idx_b, idx_a, idx_sems.at[0])


def _dispatch(hf, dest_t, tail_start, nused, wgu, wd, npad):
    t = hf.shape[0] // ROW_TILE
    d = wgu.shape[0]
    tm = min(t, MOE_ROW_TILE)
    n_tiles = t // tm
    return pl.pallas_call(
        _dispatch_kernel,
        out_shape=(jax.ShapeDtypeStruct((npad * ROW_TILE, LANES), U32), jax.ShapeDtypeStruct((t, d), F32)),
        grid_spec=pltpu.PrefetchScalarGridSpec(
            num_scalar_prefetch=2,
            grid=(n_tiles,),
            in_specs=[pl.BlockSpec((TOP_K, tm), lambda i, tail, nu: (0, i)),
                      pl.BlockSpec((TOP_K, tm), lambda i, tail, nu: (0, jnp.minimum(i + 1, n_tiles - 1))),
                      pl.BlockSpec((tm * ROW_TILE, LANES), lambda i, tail, nu: (i, 0)),
                      pl.BlockSpec(wgu.shape, lambda i, tail, nu: (0, 0)),
                      pl.BlockSpec(wd.shape, lambda i, tail, nu: (0, 0))],
            out_specs=(pl.BlockSpec(memory_space=pl.ANY), pl.BlockSpec((tm, d), lambda i, tail, nu: (i, 0))),
            scratch_shapes=[pltpu.SMEM((TOP_K, tm), I32), pltpu.SMEM((TOP_K, tm), I32),
                            pltpu.VMEM((E_BLOCK * ROW_TILE, LANES), U32),
                            pltpu.SemaphoreType.DMA((2,)), pltpu.SemaphoreType.DMA, pltpu.SemaphoreType.DMA]),
        compiler_params=_params("arbitrary"),
        name="moe_dispatch",
    )(tail_start, nused, dest_t, dest_t, hf, wgu, wd)


def _expert_kernel(be_ref, nused_ref, x_ref, wgu_ref, wd_ref, y_ref, wgu_scr, wd_scr):
    b = pl.program_id(0)
    e = be_ref[b]
    e_prev = be_ref[jnp.maximum(b - 1, 0)]

    @pl.when((b == 0) | (e != e_prev))
    def _():
        wgu_scr[...] = wgu_ref[...].astype(BF16)
        wd_scr[...] = wd_ref[...].astype(BF16)

    @pl.when(b < nused_ref[0])
    def _():
        _store_rows(y_ref, _pack_bf16_halves(_gated_ffn(_load_rows(x_ref), wgu_scr, wd_scr)))

    @pl.when(b >= nused_ref[0])
    def _():
        y_ref[...] = jnp.zeros_like(y_ref)


def _gated_ffn(x_packed, wgu_ref, wd_ref):
    x_lo, x_hi = _unpack_bf16_halves(x_packed)
    half = x_packed.shape[1]
    gu = _dot(x_lo.astype(BF16), wgu_ref[0:half, :]) + _dot(x_hi.astype(BF16), wgu_ref[half:2 * half, :])
    ff = gu.shape[1] // 2
    gate = gu[:, :ff]
    act = gate * jax.nn.sigmoid(gate) * gu[:, ff:]
    return _dot(act.astype(BF16), wd_ref[...])


def _experts(xbuf, block_e, nused, w_gate_up, w_down, layer):
    nblk = xbuf.shape[0] // (E_BLOCK * ROW_TILE)
    _, _, d, ff2 = w_gate_up.shape
    last = lambda b, nu: jnp.minimum(b, nu[0] - 1)
    return pl.pallas_call(
        _expert_kernel,
        out_shape=jax.ShapeDtypeStruct(xbuf.shape, U32),
        grid_spec=pltpu.PrefetchScalarGridSpec(
            num_scalar_prefetch=2,
            grid=(nblk,),
            in_specs=[pl.BlockSpec((E_BLOCK * ROW_TILE, LANES), lambda b, be, nu: (last(b, nu), 0)),
                      pl.BlockSpec((None, None, d, ff2), lambda b, be, nu: (layer, be[b], 0, 0)),
                      pl.BlockSpec((None, None, ff2 // 2, d), lambda b, be, nu: (layer, be[b], 0, 0))],
            out_specs=pl.BlockSpec((E_BLOCK * ROW_TILE, LANES), lambda b, be, nu: (b, 0)),
            scratch_shapes=[pltpu.VMEM((d, ff2), BF16), pltpu.VMEM((ff2 // 2, d), BF16)]),
        compiler_params=_params("arbitrary"),
        name="moe_experts",
    )(block_e, nused, xbuf, w_gate_up, w_down)


def _combine_kernel(dest_ref, dest_n1_ref, dest_n2_ref, w_ref, x_ref, sh_ref, g_ref, ybuf_hbm, o_ref, rows_a, rows_b,
                    idx_a, idx_b, idx_sems, row_sems):
    i = pl.program_id(0)
    last = i + 1 == pl.num_programs(0)
    tm = x_ref.shape[0]

    def tile(r):
        return pl.ds(pl.multiple_of(r * ROW_TILE, ROW_TILE), ROW_TILE)

    def row_copy(tok, k, idx, rows, sem):
        pltpu.make_async_copy(ybuf_hbm.at[tile(idx[k, tok])], rows.at[k, tile(tok)], sem).start(priority=k % 2)

    def wait_rows(rows, sem):
        for k in range(TOP_K):
            pltpu.make_async_copy(ybuf_hbm.at[pl.ds(0, tm * ROW_TILE)], rows.at[k], sem).wait()

    @pl.when(i == 0)
    def _():
        first = pltpu.make_async_copy(dest_ref, idx_a, idx_sems.at[0])
        first.start()
        first.wait()

        def issue(tok, carry):
            for k in range(TOP_K):
                row_copy(tok, k, idx_a, rows_a, row_sems.at[0])
            return carry

        lax.fori_loop(0, tm, issue, 0)
        second = pltpu.make_async_copy(dest_n1_ref, idx_b, idx_sems.at[1])
        second.start()
        second.wait()

    half = ROW_TILE * LANES
    per_chunk = tm // ROW_TILE

    def step(cur, cur_sem, cur_idx, cur_idx_sem, nxt, nxt_sem, nxt_idx):
        prefetch = pltpu.make_async_copy(dest_n2_ref, cur_idx, cur_idx_sem)
        prefetch.start()
        wait_rows(cur, cur_sem)
        weights = [jnp.broadcast_to(w_ref[:, k:k + 1], (tm, LANES)) for k in range(TOP_K)]
        for s in range(ROW_TILE):
            for tok in range(s * per_chunk, (s + 1) * per_chunk):
                for k in range(TOP_K):
                    row_copy(tok, k, nxt_idx, nxt, nxt_sem)
            routed_lo = jnp.zeros((tm, LANES), F32)
            routed_hi = jnp.zeros((tm, LANES), F32)
            for k in range(TOP_K):
                lo, hi = _unpack_bf16_halves(cur[k, pl.ds(s, tm, stride=ROW_TILE), :])
                routed_lo = routed_lo + weights[k] * lo
                routed_hi = routed_hi + weights[k] * hi
            cl = slice(s * LANES, (s + 1) * LANES)
            ch = slice(half + s * LANES, half + (s + 1) * LANES)
            o_ref[:, cl] = x_ref[:, cl] + g_ref[:, cl] * (routed_lo + sh_ref[:, cl])
            o_ref[:, ch] = x_ref[:, ch] + g_ref[:, ch] * (routed_hi + sh_ref[:, ch])

        prefetch.wait()

        @pl.when(last)
        def _():
            wait_rows(nxt, nxt_sem)

    @pl.when(i % 2 == 0)
    def _():
        step(rows_a, row_sems.at[0], idx_a, idx_sems.at[0], rows_b, row_sems.at[1], idx_b)

    @pl.when(i % 2 == 1)
    def _():
        step(rows_b, row_sems.at[1], idx_b, idx_sems.at[1], rows_a, row_sems.at[0], idx_a)


def _combine(dest_t, w_tok, x, shared, gate, ybuf):
    t, d = x.shape
    tm = min(t, MOE_ROW_TILE)
    n_tiles = t // tm
    return pl.pallas_call(
        _combine_kernel,
        out_shape=jax.ShapeDtypeStruct((t, d), F32),
        grid=(n_tiles,),
        in_specs=[pl.BlockSpec((TOP_K, tm), lambda i: (0, i)),
                  pl.BlockSpec((TOP_K, tm), lambda i: (0, jnp.minimum(i + 1, n_tiles - 1))),
                  pl.BlockSpec((TOP_K, tm), lambda i: (0, jnp.minimum(i + 2, n_tiles - 1))),
                  pl.BlockSpec((tm, TOP_K), lambda i: (i, 0)),
                  pl.BlockSpec((tm, d), lambda i: (i, 0)),
                  pl.BlockSpec((tm, d), lambda i: (i, 0)),
                  pl.BlockSpec((1, d), lambda i: (0, 0)),
                  pl.BlockSpec(memory_space=pl.ANY)],
        out_specs=pl.BlockSpec((tm, d), lambda i: (i, 0)),
        scratch_shapes=[pltpu.VMEM((TOP_K, tm * ROW_TILE, LANES), U32), pltpu.VMEM((TOP_K, tm * ROW_TILE, LANES), U32),
                        pltpu.SMEM((TOP_K, tm), I32), pltpu.SMEM((TOP_K, tm), I32),
                        pltpu.SemaphoreType.DMA((2,)), pltpu.SemaphoreType.DMA((2,))],
        compiler_params=_params("arbitrary"),
        name="moe_combine",
    )(dest_t, dest_t, dest_t, w_tok, x, shared, gate, ybuf)


def _moe_layer(x, shift, scale, gate, layer, w_router, router_bias, w_gate_up, w_down, ws_gate_up, ws_down):
    t, d = x.shape
    assert d == 2 * ROW_TILE * LANES, "a packed row must be exactly one (8, 128) tile of 32-bit words"
    npad = t * TOP_K + N_EXPERTS * E_BLOCK
    nblk = npad // E_BLOCK
    hf, idx_t, wts_t, rank_t, counts = _moe_pre(x, shift, scale, w_router, router_bias)
    dest_t, block_e, nused, tail_start = _moe_dest(idx_t, rank_t, counts, nblk)
    nused = nused.reshape(-1)[:1]
    xbuf, shared = _dispatch(hf, dest_t, tail_start.reshape(-1), nused, ws_gate_up.astype(BF16),
                             ws_down.astype(BF16), npad)
    ybuf = _experts(xbuf, block_e.reshape(-1), nused, w_gate_up, w_down, layer)
    return _combine(dest_t, wts_t.T, x, shared, gate, ybuf)


def _final_norm_kernel(x_ref, g_ref, o_ref):
    x = x_ref[...]
    o_ref[...] = x * lax.rsqrt(jnp.mean(x * x, axis=-1, keepdims=True) + EPS) * g_ref[...]


def _final_norm(x, gain):
    t, d = x.shape
    tm = min(t, PROJ_ROW_TILE)
    return pl.pallas_call(
        _final_norm_kernel,
        out_shape=jax.ShapeDtypeStruct((t, d), F32),
        grid=(t // tm,),
        in_specs=[pl.BlockSpec((tm, d), lambda i: (i, 0)), pl.BlockSpec((1, d), lambda i: (0, 0))],
        out_specs=pl.BlockSpec((tm, d), lambda i: (i, 0)),
        compiler_params=_params("parallel"),
        name="final_norm",
    )(x, gain.reshape(1, d))


def _even_layer(x, mod, cos, sin, w_in, b_if, mlstm_norm, q_norm, kv_norm, w_uq, w_ukv, w_out):
    t, d = x.shape
    sh1, sc1, g1 = mod[0], mod[1], mod[2]
    nq, nv = M_HEADS * M_QK, M_HEADS * M_V
    o = 0
    cols = []
    for sz in (nq, nq, nv, nv, M_HEADS, M_HEADS, A_Q_LORA, A_KV_LORA, A_ROPE):
        cols.append(w_in[:, o:o + sz])
        o += sz
    mq, mk, mv, mo, mi, mf, cq, ckv, kr = cols
    w_a = jnp.concatenate([mq, mk, mv, mo], axis=1).astype(BF16)
    zeros = lambda n: jnp.zeros((d, n), F32)
    w_b = jnp.concatenate([cq, ckv, kr, zeros(LANES - A_ROPE), mi, mf, zeros(LANES - 2 * M_HEADS)], axis=1).astype(BF16)
    gate_col_block = (A_Q_LORA + A_KV_LORA + LANES) // LANES
    proj_a, proj_b = _norm_matmul_pair(x, sh1, sc1, w_a, w_b, BF16, F32, "even_in")
    gate_bias = jnp.zeros((1, LANES), F32).at[0, :2 * M_HEADS].set(b_if.astype(F32))
    hm = _mlstm(proj_a, proj_b, gate_col_block, gate_bias, mlstm_norm.astype(F32).reshape(1, nv))

    qk = A_NOPE + A_ROPE
    wq = jnp.pad(w_uq.reshape(A_Q_LORA, A_HEADS, qk), ((0, 0), (0, 0), (0, A_QK_PAD - qk)))
    wq = wq.reshape(A_Q_LORA, A_HEADS * A_QK_PAD).astype(BF16)
    q, k, v = _mla_up(proj_b, q_norm.astype(F32).reshape(1, -1), kv_norm.astype(F32).reshape(1, -1),
                      wq, w_ukv.astype(BF16), cos, sin)
    ha = _mla_flash(q, k, v)
    w_out = w_out.astype(BF16)
    return _proj_residual([hm, ha], [w_out[:nv], w_out[nv:]], x, g1, "even_out")


def _odd_layer(x, mod, cos, sin, w_qkv, b_qkv, sinks, w_o):
    sh1, sc1, g1 = mod[0], mod[1], mod[2]
    qkv = _norm_matmul(x, sh1, sc1, w_qkv.astype(BF16), b_qkv.astype(F32).reshape(1, -1), BF16, "odd_qkv")
    o = _swa(qkv, cos, sin, sinks)
    return _proj_residual([o], [w_o.astype(BF16)], x, g1, "odd_out")


def kernel(x, c, positions, w_ada, b_ada, a_w_in, a_b_if, a_mlstm_norm, a_q_norm, a_kv_norm, a_w_uq, a_w_ukv,
           a_w_out, s_w_qkv, s_b_qkv, s_sinks, s_w_o, e_w_router, e_router_bias, e_w_gate_up, e_w_down,
           e_ws_gate_up, e_ws_down, final_norm):
    batch, t, d = x.shape
    assert batch == 1, "kernels are written for a single sequence"
    depth = w_ada.shape[0]
    xs = x.reshape(t, d)
    mods = _ada(c, w_ada, b_ada).reshape(depth, 6, 1, d)
    cos, sin = _rope_tables(positions.reshape(t))
    for layer in range(depth):
        mod = mods[layer]
        if layer % 2 == 0:
            e = layer // 2
            xs = _even_layer(xs, mod, cos, sin, a_w_in[e], a_b_if[e], a_mlstm_norm[e], a_q_norm[e], a_kv_norm[e],
                             a_w_uq[e], a_w_ukv[e], a_w_out[e])
        else:
            o = layer // 2
            xs = _odd_layer(xs, mod, cos, sin, s_w_qkv[o], s_b_qkv[o], s_sinks[o], s_w_o[o])
        xs = _moe_layer(xs, mod[3], mod[4], mod[5], layer, e_w_router[layer], e_router_bias[layer],
                        e_w_gate_up, e_w_down, e_ws_gate_up[layer], e_ws_down[layer])
    return _final_norm(xs, final_norm.astype(F32)).reshape(batch, t, d)
```

```python
import functools

import jax
import jax.numpy as jnp
from jax import lax
from jax.experimental import pallas as pl
from jax.experimental.pallas import tpu as pltpu

F32 = jnp.float32
BF16 = jnp.bfloat16
I32 = jnp.int32
U32 = jnp.uint32
HIGHEST = lax.Precision.HIGHEST

EPS = 1e-6
ROPE_THETA = 10000.0
ROPE_DIM = 64

M_HEADS = 4
M_QK = 128
M_V = 256
GATE_SOFTCAP = 15.0
MLSTM_CHUNK = 128

A_HEADS = 8
A_NOPE = 128
A_ROPE = 64
A_V = 128
A_Q_LORA = 768
A_KV_LORA = 512
A_QK_PAD = 256

S_HEADS = 32
S_KV_HEADS = 4
S_HD = 64
S_BLOCK = 128

N_EXPERTS = 64
TOP_K = 8
N_GROUPS = 8
TOPK_GROUPS = 4
E_FF = 256
SHARED_FF = 256
ROUTED_SCALE = 2.5
E_BLOCK = 512

PROJ_ROW_TILE = 1024
PROJ_COL_CAP = 1280
PAIR_COL_CAP = 1024
RESIDUAL_COL_CAP = 1024
ADA_COL_TILE = 1024
ROPE_ROW_TILE = 2048
MLSTM_ROW_TILE = 512
MLA_UP_ROW_TILE = 512
FLASH_Q_TILE = 1024
FLASH_KV_TILE = 2048
ROUTER_ROW_TILE = 512
DEST_ROW_TILE = 2048
MOE_ROW_TILE = 256

LANES = 128
VMEM_LIMIT_BYTES = 48 * 1024 * 1024
LARGE_VMEM_LIMIT_BYTES = 56 * 1024 * 1024
LOG2_E = 1.4426950408889634


def _params(*semantics):
    return pltpu.CompilerParams(dimension_semantics=semantics, vmem_limit_bytes=VMEM_LIMIT_BYTES)


def _dot(a, b, precision=None):
    return jnp.dot(a, b, preferred_element_type=F32, precision=precision)


def _dot_nt(a, b):
    return lax.dot_general(a, b, (((1,), (1,)), ((), ())), preferred_element_type=F32)


def _dot_tn(a, b):
    return lax.dot_general(a, b, (((0,), (0,)), ((), ())), preferred_element_type=F32)


def _modulated_norm(x, shift, scale):
    y = x * lax.rsqrt(jnp.mean(x * x, axis=-1, keepdims=True) + EPS)
    return y * (1.0 + scale) + shift


def _rope_lanes(x, cos, sin_signed):
    lane = lax.broadcasted_iota(I32, x.shape, 1)
    first_half = (lane % ROPE_DIM) < (ROPE_DIM // 2)
    swapped = jnp.where(first_half, pltpu.roll(x, LANES - ROPE_DIM // 2, 1), pltpu.roll(x, ROPE_DIM // 2, 1))
    return x * cos + swapped * sin_signed


def _pack_bf16_halves(x):
    n = x.shape[1] // 2
    return lax.bitcast_convert_type(pltpu.pack_elementwise([x[:, :n], x[:, n:]], packed_dtype=BF16), U32)


def _unpack_bf16_halves(w):
    lo = pltpu.unpack_elementwise(w, index=0, packed_dtype=BF16, unpacked_dtype=F32)
    hi = pltpu.unpack_elementwise(w, index=1, packed_dtype=BF16, unpacked_dtype=F32)
    return lo, hi


ROW_TILE = 8


def _load_rows(ref):
    m = ref.shape[0] // ROW_TILE
    return jnp.concatenate([ref[pl.ds(s, m, stride=ROW_TILE), :] for s in range(ROW_TILE)], axis=1)


def _store_rows(ref, x):
    m = x.shape[0]
    for s in range(ROW_TILE):
        ref[pl.ds(s, m, stride=ROW_TILE), :] = x[:, s * LANES:(s + 1) * LANES]


def _ada_kernel(c_ref, w_ref, b_ref, o_ref):
    c = c_ref[...]
    c_act = c * jax.nn.sigmoid(c)
    o_ref[...] = jnp.sum(c_act * w_ref[...], axis=0, keepdims=True) + b_ref[...]


def _ada(c, w_ada, b_ada):
    depth, d, n = w_ada.shape
    tn = ADA_COL_TILE
    return pl.pallas_call(
        _ada_kernel,
        out_shape=jax.ShapeDtypeStruct((depth, 1, n), F32),
        grid=(depth, n // tn),
        in_specs=[pl.BlockSpec((d, 1), lambda l, j: (0, 0)),
                  pl.BlockSpec((None, d, tn), lambda l, j: (l, 0, j)),
                  pl.BlockSpec((None, 1, tn), lambda l, j: (l, 0, j))],
        out_specs=pl.BlockSpec((None, 1, tn), lambda l, j: (l, 0, j)),
        compiler_params=_params("parallel", "parallel"),
        name="ada_mod",
    )(c.reshape(d, 1), w_ada, b_ada.reshape(depth, 1, n))


def _rope_table_kernel(pos_ref, inv_ref, sign_ref, cos_ref, sin_ref):
    ang = pos_ref[...].astype(F32) * inv_ref[...]
    cos_ref[...] = jnp.cos(ang)
    sin_ref[...] = jnp.sin(ang) * sign_ref[...]


def _rope_tables(positions):
    t = positions.shape[0]
    half = ROPE_DIM // 2
    inv_freq = jnp.power(ROPE_THETA, -jnp.arange(half, dtype=F32) / half)
    inv = jnp.tile(inv_freq, LANES // half).reshape(1, LANES)
    sign = jnp.tile(jnp.concatenate([-jnp.ones((half,), F32), jnp.ones((half,), F32)]), LANES // ROPE_DIM)
    tm = min(t, ROPE_ROW_TILE)
    return pl.pallas_call(
        _rope_table_kernel,
        out_shape=(jax.ShapeDtypeStruct((t, LANES), F32), jax.ShapeDtypeStruct((t, LANES), F32)),
        grid=(t // tm,),
        in_specs=[pl.BlockSpec((tm, 1), lambda i: (i, 0)),
                  pl.BlockSpec((1, LANES), lambda i: (0, 0)),
                  pl.BlockSpec((1, LANES), lambda i: (0, 0))],
        out_specs=(pl.BlockSpec((tm, LANES), lambda i: (i, 0)), pl.BlockSpec((tm, LANES), lambda i: (i, 0))),
        compiler_params=_params("parallel"),
        name="rope_tables",
    )(positions.reshape(t, 1), inv, sign.reshape(1, LANES))


def _norm_matmul_kernel(x_ref, sh_ref, sc_ref, w_ref, b_ref, o_ref, h_scr):
    @pl.when(pl.program_id(1) == 0)
    def _():
        h_scr[...] = _modulated_norm(x_ref[...], sh_ref[...], sc_ref[...]).astype(BF16)

    o_ref[...] = (_dot(h_scr[...], w_ref[...]) + b_ref[...]).astype(o_ref.dtype)


def _col_tile(n, cap):
    return max(c for c in range(LANES, min(n, cap) + 1, LANES) if n % c == 0)


def _norm_matmul(x, shift, scale, w, bias, out_dtype, name):
    t, d = x.shape
    n = w.shape[1]
    tm = min(t, PROJ_ROW_TILE)
    tn = _col_tile(n, PROJ_COL_CAP)
    return pl.pallas_call(
        _norm_matmul_kernel,
        out_shape=jax.ShapeDtypeStruct((t, n), out_dtype),
        grid=(t // tm, n // tn),
        in_specs=[pl.BlockSpec((tm, d), lambda i, j: (i, 0)),
                  pl.BlockSpec((1, d), lambda i, j: (0, 0)),
                  pl.BlockSpec((1, d), lambda i, j: (0, 0)),
                  pl.BlockSpec((d, tn), lambda i, j: (0, j)),
                  pl.BlockSpec((1, tn), lambda i, j: (0, j))],
        out_specs=pl.BlockSpec((tm, tn), lambda i, j: (i, j)),
        scratch_shapes=[pltpu.VMEM((tm, d), BF16)],
        compiler_params=_params("parallel", "arbitrary"),
        name=name,
    )(x, shift, scale, w, bias)


def _norm_matmul_pair_kernel(x_ref, sh_ref, sc_ref, wa_ref, wb_ref, oa_ref, ob_ref, h_scr, *, steps_a):
    j = pl.program_id(1)

    @pl.when(j == 0)
    def _():
        h_scr[...] = _modulated_norm(x_ref[...], sh_ref[...], sc_ref[...]).astype(BF16)

    @pl.when(j < steps_a)
    def _():
        oa_ref[...] = _dot(h_scr[...], wa_ref[...]).astype(oa_ref.dtype)

    @pl.when(j >= steps_a)
    def _():
        ob_ref[...] = _dot(h_scr[...], wb_ref[...]).astype(ob_ref.dtype)


def _norm_matmul_pair(x, shift, scale, wa, wb, dtype_a, dtype_b, name):
    t, d = x.shape
    na, nb = wa.shape[1], wb.shape[1]
    tm = min(t, PROJ_ROW_TILE)
    ta, tb = _col_tile(na, PAIR_COL_CAP), _col_tile(nb, PAIR_COL_CAP)
    steps_a, steps_b = na // ta, nb // tb
    col_a = lambda j: jnp.minimum(j, steps_a - 1)
    col_b = lambda j: jnp.maximum(j - steps_a, 0)
    return pl.pallas_call(
        functools.partial(_norm_matmul_pair_kernel, steps_a=steps_a),
        out_shape=(jax.ShapeDtypeStruct((t, na), dtype_a), jax.ShapeDtypeStruct((t, nb), dtype_b)),
        grid=(t // tm, steps_a + steps_b),
        in_specs=[pl.BlockSpec((tm, d), lambda i, j: (i, 0)),
                  pl.BlockSpec((1, d), lambda i, j: (0, 0)),
                  pl.BlockSpec((1, d), lambda i, j: (0, 0)),
                  pl.BlockSpec((d, ta), lambda i, j: (0, col_a(j))),
                  pl.BlockSpec((d, tb), lambda i, j: (0, col_b(j)))],
        out_specs=(pl.BlockSpec((tm, ta), lambda i, j: (i, col_a(j))),
                   pl.BlockSpec((tm, tb), lambda i, j: (i, col_b(j)))),
        scratch_shapes=[pltpu.VMEM((tm, d), BF16)],
        compiler_params=pltpu.CompilerParams(dimension_semantics=("parallel", "arbitrary"),
                                             vmem_limit_bytes=LARGE_VMEM_LIMIT_BYTES),
        name=name,
    )(x, shift, scale, wa, wb)


def _mlstm_kernel(q_ref, k_ref, v_ref, o_ref, g_ref, gb_ref, gain_ref, out_ref, c_scr, n_scr, m_scr):
    L = MLSTM_CHUNK
    tm = q_ref.shape[0]

    @pl.when(pl.program_id(0) == 0)
    def _():
        c_scr[...] = jnp.zeros_like(c_scr)
        n_scr[...] = jnp.zeros_like(n_scr)
        m_scr[...] = jnp.zeros_like(m_scr)

    capped = GATE_SOFTCAP * jnp.tanh((g_ref[...] + gb_ref[...]) / GATE_SOFTCAP)
    log_sig = jnp.minimum(capped, 0.0) - jnp.log1p(jnp.exp(-jnp.abs(capped)))
    lane = lax.broadcasted_iota(I32, capped.shape, 1)
    gate = jnp.where(lane < M_HEADS, capped, log_sig)
    r = lax.broadcasted_iota(I32, (tm, tm), 0)
    c = lax.broadcasted_iota(I32, (tm, tm), 1)
    chunk_tril = jnp.where(((r // L) == (c // L)) & (c <= r), 1.0, 0.0).astype(F32)
    cum = _dot(chunk_tril, gate, precision=HIGHEST)
    gate_rows = gate.T
    cum_rows = cum.T
    heads = range(M_HEADS)
    rr = lax.broadcasted_iota(I32, (M_HEADS * L, L), 0) % L
    cc = lax.broadcasted_iota(I32, (M_HEADS * L, L), 1)
    causal = cc <= rr

    def stack(parts):
        return jnp.concatenate(parts, axis=0)

    def rows_of(x, n=L):
        return stack([jnp.broadcast_to(x[h:h + 1, :], (n, LANES)) for h in heads])

    def wide(x):
        return jnp.concatenate([x] * (M_V // LANES), axis=1)

    c_states = [c_scr[h] for h in heads]
    n_states = [n_scr[h] for h in heads]
    m_prev = m_scr[0:M_HEADS, :]
    gain = stack([jnp.broadcast_to(gain_ref[:, h * M_V:(h + 1) * M_V], (L, M_V)) for h in heads])
    for ci in range(tm // L):
        sl = slice(ci * L, (ci + 1) * L)
        q_heads = [q_ref[sl, h * M_QK:(h + 1) * M_QK] for h in heads]
        k_all = k_ref[sl, :].astype(F32) * (M_QK ** -0.5)
        k_heads = [k_all[:, h * M_QK:(h + 1) * M_QK] for h in heads]
        kb_heads = [k.astype(BF16) for k in k_heads]
        v_heads = [v_ref[sl, h * M_V:(h + 1) * M_V] for h in heads]

        ig_col = stack([jnp.broadcast_to(gate[sl, h:h + 1], (L, LANES)) for h in heads])
        b_col = stack([jnp.broadcast_to(cum[sl, M_HEADS + h:M_HEADS + h + 1], (L, LANES)) for h in heads])
        ig_rows = gate_rows[0:M_HEADS, sl]
        b_rows = cum_rows[M_HEADS:2 * M_HEADS, sl]
        ig_row = rows_of(ig_rows)
        b_row = rows_of(b_rows)

        dm = jnp.where(causal, b_col - b_row + ig_row, -jnp.inf)
        inter = b_col + rows_of(m_prev)
        m_row = jnp.maximum(inter, jnp.broadcast_to(jnp.max(dm, axis=-1, keepdims=True), inter.shape))
        w_intra = jnp.exp(dm - m_row)
        w_inter = jnp.exp(inter - m_row)
        s = stack([_dot_nt(q_heads[h], kb_heads[h]) for h in heads]) * w_intra
        s_b = s.astype(BF16)
        intra = stack([_dot(s_b[h * L:(h + 1) * L], v_heads[h]) for h in heads])
        carried = stack([_dot_nt(q_heads[h], c_states[h].astype(BF16)) for h in heads])
        num = intra + wide(w_inter) * carried
        q_all = stack([q.astype(F32) for q in q_heads])
        n_all = stack([jnp.broadcast_to(n_states[h], (L, M_QK)) for h in heads])
        qn = jnp.broadcast_to(jnp.sum(q_all * n_all, axis=-1, keepdims=True), inter.shape)
        den = jnp.broadcast_to(jnp.sum(s, axis=-1, keepdims=True), inter.shape) + w_inter * qn
        hh = num / wide(jnp.maximum(jnp.abs(den), jnp.exp(-m_row)))

        b_last = jnp.broadcast_to(b_rows[:, L - 1:L], (M_HEADS, LANES))
        g_rows = b_last - b_rows + ig_rows
        m_new = jnp.maximum(b_last + m_prev, jnp.broadcast_to(jnp.max(g_rows, axis=-1, keepdims=True), b_last.shape))
        decay = jnp.exp(b_last + m_prev - m_new)
        ws_col = jnp.exp(rows_of(b_last) - b_col + ig_col - rows_of(m_new))
        for h in heads:
            ws_h = ws_col[h * L:(h + 1) * L]
            vw = (v_heads[h].astype(F32) * wide(ws_h)).astype(BF16)
            c_states[h] = (jnp.broadcast_to(decay[h:h + 1, :], (M_V, M_QK)) * c_states[h]
                           + _dot_tn(vw, kb_heads[h]))
            n_states[h] = decay[h:h + 1, :] * n_states[h] + jnp.sum(k_heads[h] * ws_h, axis=0, keepdims=True)
        m_prev = m_new

        inv_rms = lax.rsqrt(jnp.mean(hh * hh, axis=-1, keepdims=True) + EPS)
        o_gate = stack([o_ref[sl, h * M_V:(h + 1) * M_V] for h in heads]).astype(F32)
        y = hh * inv_rms * gain * jax.nn.sigmoid(o_gate)
        for h in heads:
            out_ref[sl, h * M_V:(h + 1) * M_V] = y[h * L:(h + 1) * L].astype(out_ref.dtype)
    for h in heads:
        c_scr[h], n_scr[h] = c_states[h], n_states[h]
    m_scr[0:M_HEADS, :] = m_prev


def _mlstm(proj_a, proj_b, gate_col_block, gate_bias, gain):
    t = proj_a.shape[0]
    tm = min(t, MLSTM_ROW_TILE)
    nq = M_HEADS * M_QK
    nv = M_HEADS * M_V
    return pl.pallas_call(
        _mlstm_kernel,
        out_shape=jax.ShapeDtypeStruct((t, nv), BF16),
        grid=(t // tm,),
        in_specs=[pl.BlockSpec((tm, nq), lambda i: (i, 0)),
                  pl.BlockSpec((tm, nq), lambda i: (i, 1)),
                  pl.BlockSpec((tm, nv), lambda i: (i, 1)),
                  pl.BlockSpec((tm, nv), lambda i: (i, 2)),
                  pl.BlockSpec((tm, LANES), lambda i: (i, gate_col_block)),
                  pl.BlockSpec((1, LANES), lambda i: (0, 0)),
                  pl.BlockSpec((1, nv), lambda i: (0, 0))],
        out_specs=pl.BlockSpec((tm, nv), lambda i: (i, 0)),
        scratch_shapes=[pltpu.VMEM((M_HEADS, M_V, M_QK), F32),
                        pltpu.VMEM((M_HEADS, 1, M_QK), F32),
                        pltpu.VMEM((8, LANES), F32)],
        compiler_params=_params("arbitrary"),
        name="mlstm",
    )(proj_a, proj_a, proj_a, proj_a, proj_b, gate_bias, gain)


def _mla_up_kernel(pb_ref, qn_ref, kvn_ref, wq_ref, wkv_ref, cos_ref, sin_ref, q_ref, k_ref, v_ref):
    cos, sin = cos_ref[...], sin_ref[...]
    cq = pb_ref[:, 0:A_Q_LORA]
    cq = (cq * lax.rsqrt(jnp.mean(cq * cq, axis=-1, keepdims=True) + EPS) * qn_ref[...]).astype(BF16)
    ckv = pb_ref[:, A_Q_LORA:A_Q_LORA + A_KV_LORA]
    ckv = (ckv * lax.rsqrt(jnp.mean(ckv * ckv, axis=-1, keepdims=True) + EPS) * kvn_ref[...]).astype(BF16)
    k_pe = _rope_lanes(pb_ref[:, A_Q_LORA + A_KV_LORA:A_Q_LORA + A_KV_LORA + LANES], cos, sin).astype(BF16)
    scale = (A_NOPE + A_ROPE) ** -0.5 * LOG2_E
    for h in range(A_HEADS):
        qh = _dot(cq, wq_ref[:, h * A_QK_PAD:(h + 1) * A_QK_PAD])
        q_pe = _rope_lanes(qh[:, A_NOPE:], cos, sin)
        q_ref[h] = (jnp.concatenate([qh[:, :A_NOPE], q_pe], axis=1) * scale).astype(q_ref.dtype)
        kvh = _dot(ckv, wkv_ref[:, h * (A_NOPE + A_V):(h + 1) * (A_NOPE + A_V)])
        k_ref[h] = jnp.concatenate([kvh[:, :A_NOPE].astype(BF16), k_pe], axis=1)
        v_ref[h] = kvh[:, A_NOPE:].astype(v_ref.dtype)


def _mla_up(proj_b, q_norm, kv_norm, wq, wkv, cos, sin):
    t, nb = proj_b.shape
    tm = min(t, MLA_UP_ROW_TILE)
    return pl.pallas_call(
        _mla_up_kernel,
        out_shape=(jax.ShapeDtypeStruct((A_HEADS, t, A_QK_PAD), BF16),
                   jax.ShapeDtypeStruct((A_HEADS, t, A_QK_PAD), BF16),
                   jax.ShapeDtypeStruct((A_HEADS, t, A_V), BF16)),
        grid=(t // tm,),
        in_specs=[pl.BlockSpec((tm, nb), lambda i: (i, 0)),
                  pl.BlockSpec((1, A_Q_LORA), lambda i: (0, 0)),
                  pl.BlockSpec((1, A_KV_LORA), lambda i: (0, 0)),
                  pl.BlockSpec(wq.shape, lambda i: (0, 0)),
                  pl.BlockSpec(wkv.shape, lambda i: (0, 0)),
                  pl.BlockSpec((tm, LANES), lambda i: (i, 0)),
                  pl.BlockSpec((tm, LANES), lambda i: (i, 0))],
        out_specs=(pl.BlockSpec((A_HEADS, tm, A_QK_PAD), lambda i: (0, i, 0)),
                   pl.BlockSpec((A_HEADS, tm, A_QK_PAD), lambda i: (0, i, 0)),
                   pl.BlockSpec((A_HEADS, tm, A_V), lambda i: (0, i, 0))),
        compiler_params=_params("parallel"),
        name="mla_up",
    )(proj_b, q_norm, kv_norm, wq, wkv, cos, sin)


def _mla_flash_kernel(q_ref, k_ref, v_ref, o_ref, m_scr, l_scr, acc_scr, *, tq, tkb):
    i = pl.program_id(1)
    q = q_ref[...]
    m_scr[...] = jnp.full(m_scr.shape, -jnp.inf, F32)
    l_scr[...] = jnp.zeros(l_scr.shape, F32)
    acc_scr[...] = jnp.zeros(acc_scr.shape, F32)

    def tile(start, width, masked):
        s = _dot_nt(q, k_ref[pl.ds(start, width), :])
        if masked:
            row = lax.broadcasted_iota(I32, (tq, width), 0)
            col = lax.broadcasted_iota(I32, (tq, width), 1)
            s = jnp.where(col <= row, s, -jnp.inf)
        m = m_scr[...]
        m_new = jnp.maximum(m, jnp.broadcast_to(jnp.max(s, axis=-1, keepdims=True), m.shape))
        alpha = jnp.exp2(m - m_new)
        p = jnp.exp2(s - jnp.concatenate([m_new] * (width // LANES), axis=1))
        lane_sums = p[:, 0:LANES]
        for j in range(1, width // LANES):
            lane_sums = lane_sums + p[:, j * LANES:(j + 1) * LANES]
        l_scr[...] = alpha * l_scr[...] + lane_sums
        acc_scr[...] = alpha * acc_scr[...] + _dot(p.astype(BF16), v_ref[pl.ds(start, width), :])
        m_scr[...] = m_new

    per = tkb // tq
    n_big = i // per

    def body(j, carry):
        tile(pl.multiple_of(j * tkb, tkb), tkb, False)
        return carry

    lax.fori_loop(0, n_big, body, 0)
    rem = i - n_big * per
    for r in range(1, per):
        @pl.when(rem >= r)
        def _():
            tile(pl.multiple_of(n_big * tkb + (r - 1) * tq, tq), tq, False)
    tile(pl.multiple_of(i * tq, tq), tq, True)
    o_ref[...] = (acc_scr[...] / jnp.sum(l_scr[...], axis=-1, keepdims=True)).astype(o_ref.dtype)


def _mla_flash(q, k, v):
    _, t, _ = q.shape
    tq = min(t, FLASH_Q_TILE)
    tkb = min(t, FLASH_KV_TILE)
    return pl.pallas_call(
        functools.partial(_mla_flash_kernel, tq=tq, tkb=tkb),
        out_shape=jax.ShapeDtypeStruct((t, A_HEADS * A_V), BF16),
        grid=(A_HEADS, t // tq),
        in_specs=[pl.BlockSpec((None, tq, A_QK_PAD), lambda h, i: (h, i, 0)),
                  pl.BlockSpec((None, t, A_QK_PAD), lambda h, i: (h, 0, 0)),
                  pl.BlockSpec((None, t, A_V), lambda h, i: (h, 0, 0))],
        out_specs=pl.BlockSpec((tq, A_V), lambda h, i: (i, h)),
        scratch_shapes=[pltpu.VMEM((tq, LANES), F32), pltpu.VMEM((tq, LANES), F32), pltpu.VMEM((tq, A_V), F32)],
        compiler_params=pltpu.CompilerParams(dimension_semantics=("parallel", "arbitrary"),
                                             vmem_limit_bytes=LARGE_VMEM_LIMIT_BYTES),
        name="mla_flash",
    )(q, k, v)


def _proj_residual_kernel(*refs, n_lhs):
    lhs = refs[:n_lhs]
    ws = refs[n_lhs:2 * n_lhs]
    x_ref, g_ref, o_ref = refs[2 * n_lhs:]
    acc = _dot(lhs[0][...], ws[0][...])
    for a, w in zip(lhs[1:], ws[1:]):
        acc = acc + _dot(a[...], w[...])
    o_ref[...] = x_ref[...] + g_ref[...] * acc


def _proj_residual(lhs_list, w_list, x, gate, name):
    t, d = x.shape
    tm = min(t, PROJ_ROW_TILE)
    tn = _col_tile(d, RESIDUAL_COL_CAP)
    n_lhs = len(lhs_list)
    in_specs = ([pl.BlockSpec((tm, a.shape[1]), lambda i, j: (i, 0)) for a in lhs_list]
                + [pl.BlockSpec((w.shape[0], tn), lambda i, j: (0, j)) for w in w_list]
                + [pl.BlockSpec((tm, tn), lambda i, j: (i, j)), pl.BlockSpec((1, tn), lambda i, j: (0, j))])
    return pl.pallas_call(
        functools.partial(_proj_residual_kernel, n_lhs=n_lhs),
        out_shape=jax.ShapeDtypeStruct((t, d), F32),
        grid=(t // tm, d // tn),
        in_specs=in_specs,
        out_specs=pl.BlockSpec((tm, tn), lambda i, j: (i, j)),
        compiler_params=_params("parallel", "parallel"),
        name=name,
    )(*lhs_list, *w_list, x, gate)


def _swa_kernel(q_ref, kc_ref, kp_ref, vc_ref, vp_ref, cosc_ref, sinc_ref, cosp_ref, sinp_ref, sink_ref, o_ref):
    i = pl.program_id(0)
    qb = S_BLOCK
    group = S_HEADS // S_KV_HEADS
    pairs = group // 2
    cos_c, sin_c = cosc_ref[...], sinc_ref[...]
    cos_w = jnp.concatenate([cosp_ref[...], cos_c], axis=0)
    sin_w = jnp.concatenate([sinp_ref[...], sin_c], axis=0)
    kw = jnp.concatenate([kp_ref[...], kc_ref[...]], axis=0).astype(F32)
    vw = jnp.concatenate([vp_ref[...], vc_ref[...]], axis=0)

    r = lax.broadcasted_iota(I32, (pairs * qb, 2 * qb), 0) % qb
    c = lax.broadcasted_iota(I32, (pairs * qb, 2 * qb), 1)
    dist = qb + r - c
    valid = (dist >= 0) & (dist < S_BLOCK) & ((c >= qb) | (i > 0))
    lane = lax.broadcasted_iota(I32, (2 * qb, LANES), 1)
    low = lane < S_HD
    low_rows = lax.broadcasted_iota(I32, (pairs * qb, LANES), 1) < S_HD
    ones_d = jnp.concatenate([jnp.where(low, 1.0, 0.0), jnp.where(low, 0.0, 1.0)], axis=0).astype(BF16)

    for g in range(S_KV_HEADS):
        col = slice((g // 2) * LANES, (g // 2 + 1) * LANES)
        k_pair = _rope_lanes(kw[:, col], cos_w, sin_w)
        v_pair = vw[:, col]
        keep = low if g % 2 == 0 else jnp.logical_not(low)
        k_own = jnp.where(keep, k_pair, 0.0)
        v_own = jnp.where(keep, v_pair.astype(F32), 0.0)
        k_other = pltpu.roll(k_own, S_HD, 1)
        v_other = pltpu.roll(v_own, S_HD, 1)
        k_lo, k_hi = (k_own, k_other) if g % 2 == 0 else (k_other, k_own)
        v_lo, v_hi = (v_own, v_other) if g % 2 == 0 else (v_other, v_own)
        kd = jnp.concatenate([k_lo, k_hi], axis=0).astype(BF16)
        vd = jnp.concatenate([v_lo, v_hi], axis=0).astype(BF16)
        qs = []
        for p in range(pairs):
            qcol = slice((g * pairs + p) * LANES, (g * pairs + p + 1) * LANES)
            qs.append(_rope_lanes(q_ref[:, qcol].astype(F32), cos_c, sin_c) * (S_HD ** -0.5 * LOG2_E))
        qg = jnp.concatenate(qs, axis=0).astype(BF16)
        s = _dot_nt(qg, kd)
        exps, sink_terms = [], []
        for half in range(2):
            sh = jnp.where(valid, s[:, half * 2 * qb:(half + 1) * 2 * qb], -jnp.inf)
            sink = sink_ref[g, half]
            m = jnp.maximum(jnp.broadcast_to(jnp.max(sh, axis=-1, keepdims=True), sink.shape), sink)
            exps.append(jnp.exp2(sh - jnp.concatenate([m, m], axis=1)).astype(BF16))
            sink_terms.append(jnp.exp2(sink - m))
        e_all = jnp.concatenate(exps, axis=1)
        den = _dot(e_all, ones_d) + jnp.where(low_rows, sink_terms[0], sink_terms[1])
        o = _dot(e_all, vd) / den
        for p in range(pairs):
            qcol = slice((g * pairs + p) * LANES, (g * pairs + p + 1) * LANES)
            o_ref[:, qcol] = o[p * qb:(p + 1) * qb].astype(o_ref.dtype)


def _swa(qkv, cos, sin, sinks):
    t = qkv.shape[0]
    qb = S_BLOCK
    nq = S_HEADS * S_HD
    nk = S_KV_HEADS * S_HD
    group = S_HEADS // S_KV_HEADS
    pairs = group // 2
    sink_cols = jnp.repeat((sinks.astype(F32) * LOG2_E).reshape(S_KV_HEADS, pairs, 2).transpose(0, 2, 1), qb, axis=-1)
    sink_cols = jnp.broadcast_to(sink_cols.reshape(S_KV_HEADS, 2, pairs * qb, 1), (S_KV_HEADS, 2, pairs * qb, LANES))
    k_blk = nq // nk
    prev = lambda i: jnp.maximum(i - 1, 0)
    return pl.pallas_call(
        _swa_kernel,
        out_shape=jax.ShapeDtypeStruct((t, nq), BF16),
        grid=(t // qb,),
        in_specs=[pl.BlockSpec((qb, nq), lambda i: (i, 0)),
                  pl.BlockSpec((qb, nk), lambda i: (i, k_blk)),
                  pl.BlockSpec((qb, nk), lambda i: (prev(i), k_blk)),
                  pl.BlockSpec((qb, nk), lambda i: (i, k_blk + 1)),
                  pl.BlockSpec((qb, nk), lambda i: (prev(i), k_blk + 1)),
                  pl.BlockSpec((qb, LANES), lambda i: (i, 0)),
                  pl.BlockSpec((qb, LANES), lambda i: (i, 0)),
                  pl.BlockSpec((qb, LANES), lambda i: (prev(i), 0)),
                  pl.BlockSpec((qb, LANES), lambda i: (prev(i), 0)),
                  pl.BlockSpec((S_KV_HEADS, 2, pairs * qb, LANES), lambda i: (0, 0, 0, 0))],
        out_specs=pl.BlockSpec((qb, nq), lambda i: (i, 0)),
        compiler_params=_params("parallel"),
        name="swa",
    )(qkv, qkv, qkv, qkv, qkv, cos, sin, cos, sin, sink_cols)


def _expert_of_row(p):
    per_group = N_EXPERTS // N_GROUPS
    return (p % N_GROUPS) * per_group + p // N_GROUPS


def _moe_pre_kernel(x_ref, sh_ref, sc_ref, wr_ref, rb_ref, hf_ref, idx_ref, wt_ref, rank_ref, cnt_ref, carry_scr):
    tm = x_ref.shape[0]
    per_group = N_EXPERTS // N_GROUPS

    @pl.when(pl.program_id(0) == 0)
    def _():
        carry_scr[...] = jnp.zeros_like(carry_scr)

    h = _modulated_norm(x_ref[...], sh_ref[...], sc_ref[...])
    _store_rows(hf_ref, _pack_bf16_halves(h))
    h_hi = h.astype(BF16)
    h_lo = (h - h_hi.astype(F32)).astype(BF16)
    logits = _dot(h_hi, wr_ref[0]) + (_dot(h_hi, wr_ref[1]) + _dot(h_lo, wr_ref[0]))
    scores = jax.nn.sigmoid(logits.T[:N_EXPERTS, :])
    biased = scores + rb_ref[...]

    members = [biased[j * N_GROUPS:(j + 1) * N_GROUPS, :] for j in range(per_group)]
    m1 = members[0]
    for a in members[1:]:
        m1 = jnp.maximum(m1, a)
    first = jnp.full(m1.shape, per_group, I32)
    for j in reversed(range(per_group)):
        first = jnp.where(members[j] == m1, j, first)
    m2 = jnp.full(m1.shape, -jnp.inf, F32)
    for j in range(per_group):
        m2 = jnp.maximum(m2, jnp.where(first == j, -jnp.inf, members[j]))
    group_score = m1 + m2

    g_iota = lax.broadcasted_iota(I32, group_score.shape, 0).astype(F32)
    g_sel = jnp.zeros(group_score.shape, F32)
    for _ in range(TOPK_GROUPS):
        best = jnp.max(group_score, axis=0, keepdims=True)
        gi = jnp.min(jnp.where(group_score == best, g_iota, float(N_GROUPS)), axis=0, keepdims=True)
        hit = g_iota == gi
        g_sel = jnp.where(hit, 1.0, g_sel)
        group_score = jnp.where(hit, -jnp.inf, group_score)
    masked = jnp.concatenate([jnp.where(g_sel > 0.5, a, -jnp.inf) for a in members], axis=0)

    e_iota = _expert_of_row(lax.broadcasted_iota(I32, masked.shape, 0)).astype(F32)
    sel = jnp.zeros(masked.shape, F32)
    idx_rows, w_rows = [], []
    for _ in range(TOP_K):
        best = jnp.max(masked, axis=0, keepdims=True)
        ei = jnp.min(jnp.where(masked == best, e_iota, float(N_EXPERTS)), axis=0, keepdims=True)
        hit = e_iota == ei
        idx_rows.append(ei)
        w_rows.append(jnp.sum(jnp.where(hit, scores, 0.0), axis=0, keepdims=True))
        sel = jnp.where(hit, 1.0, sel)
        masked = jnp.where(hit, -jnp.inf, masked)
    idx = jnp.concatenate(idx_rows, axis=0).astype(I32)
    wts = jnp.concatenate(w_rows, axis=0)
    wts = wts / jnp.sum(wts, axis=0, keepdims=True) * ROUTED_SCALE

    r = lax.broadcasted_iota(I32, (tm, tm), 0)
    c = lax.broadcasted_iota(I32, (tm, tm), 1)
    before = jnp.where(r < c, 1.0, 0.0).astype(BF16)
    rank_excl = carry_scr[:, 0:1] + _dot(sel.astype(BF16), before)
    rank_rows = [jnp.sum(jnp.where(e_iota == idx_rows[k], rank_excl, 0.0), axis=0, keepdims=True)
                 for k in range(TOP_K)]
    carry_scr[...] = carry_scr[...] + jnp.sum(sel, axis=1, keepdims=True)

    idx_ref[...] = idx
    wt_ref[...] = wts
    rank_ref[...] = jnp.concatenate(rank_rows, axis=0).astype(I32)
    cnt_ref[...] = carry_scr[...]


def _moe_pre(x, shift, scale, w_router, router_bias):
    t, d = x.shape
    tm = min(t, ROUTER_ROW_TILE)
    rows = jnp.arange(N_EXPERTS)
    perm = _expert_of_row(rows)
    wr = jnp.zeros((d, LANES), F32).at[:, :N_EXPERTS].set(w_router[:, perm])
    wr_hi = wr.astype(BF16)
    wr = jnp.stack([wr_hi, (wr - wr_hi.astype(F32)).astype(BF16)])
    rb = router_bias.astype(F32)[perm].reshape(N_EXPERTS, 1)
    per_token = lambda dtype: jax.ShapeDtypeStruct((TOP_K, t), dtype)
    return pl.pallas_call(
        _moe_pre_kernel,
        out_shape=(jax.ShapeDtypeStruct((t * ROW_TILE, LANES), U32), per_token(I32), per_token(F32), per_token(I32),
                   jax.ShapeDtypeStruct((N_EXPERTS, LANES), F32)),
        grid=(t // tm,),
        in_specs=[pl.BlockSpec((tm, d), lambda i: (i, 0)),
                  pl.BlockSpec((1, d), lambda i: (0, 0)),
                  pl.BlockSpec((1, d), lambda i: (0, 0)),
                  pl.BlockSpec((2, d, LANES), lambda i: (0, 0, 0)),
                  pl.BlockSpec((N_EXPERTS, 1), lambda i: (0, 0))],
        out_specs=(pl.BlockSpec((tm * ROW_TILE, LANES), lambda i: (i, 0)),
                   pl.BlockSpec((TOP_K, tm), lambda i: (0, i)),
                   pl.BlockSpec((TOP_K, tm), lambda i: (0, i)),
                   pl.BlockSpec((TOP_K, tm), lambda i: (0, i)),
                   pl.BlockSpec((N_EXPERTS, LANES), lambda i: (0, 0))),
        scratch_shapes=[pltpu.VMEM((N_EXPERTS, LANES), F32)],
        compiler_params=_params("arbitrary"),
        name="moe_router",
    )(x, shift, scale, wr, rb)


def _moe_dest_kernel(idx_ref, rank_ref, cnt_ref, dest_ref, be_ref, nused_ref, tail_ref):
    cnt = cnt_ref[...]
    padded = jnp.floor((cnt + (E_BLOCK - 1)) * (1.0 / E_BLOCK)) * E_BLOCK
    e_i = _expert_of_row(lax.broadcasted_iota(I32, (N_EXPERTS, N_EXPERTS), 0))
    e_j = _expert_of_row(lax.broadcasted_iota(I32, (N_EXPERTS, N_EXPERTS), 1))
    earlier = jnp.where(e_j < e_i, 1.0, 0.0).astype(F32)
    pstart = _dot(earlier, padded, precision=HIGHEST)
    pend = pstart + padded

    tm = idx_ref.shape[1]
    e_col = _expert_of_row(lax.broadcasted_iota(I32, (N_EXPERTS, tm), 0))
    rows = []
    for k in range(TOP_K):
        hit = e_col == idx_ref[k:k + 1, :]
        rows.append(jnp.sum(jnp.where(hit, pstart[:, 0:1], 0.0), axis=0, keepdims=True))
    dest_ref[...] = jnp.concatenate(rows, axis=0).astype(I32) + rank_ref[...]

    nb = be_ref.shape[1]
    block_start = (lax.broadcasted_iota(I32, (N_EXPERTS, nb), 1) * E_BLOCK).astype(F32)
    ended = jnp.sum(jnp.where(pend[:, 0:1] <= block_start, 1.0, 0.0), axis=0, keepdims=True)
    be_ref[...] = jnp.minimum(ended, float(N_EXPERTS - 1)).astype(I32)
    nused_ref[...] = (jnp.sum(padded, axis=0, keepdims=True) * (1.0 / E_BLOCK)).astype(I32)
    e_row = _expert_of_row(lax.broadcasted_iota(I32, (N_EXPERTS, LANES), 0))
    lane = lax.broadcasted_iota(I32, (N_EXPERTS, LANES), 1)
    end_lane = jnp.sum(jnp.where(e_row <= lane, padded, 0.0), axis=0, keepdims=True)
    own_lane = jnp.sum(jnp.where(e_row == lane, padded, 0.0), axis=0, keepdims=True)
    tail_ref[...] = jnp.where(own_lane > 0.0, end_lane - E_BLOCK, -1.0).astype(I32)


def _moe_dest(idx_t, rank_t, counts, nblk):
    t = idx_t.shape[1]
    tm = min(t, DEST_ROW_TILE)
    nb = ((nblk + LANES - 1) // LANES) * LANES
    return pl.pallas_call(
        _moe_dest_kernel,
        out_shape=(jax.ShapeDtypeStruct((TOP_K, t), I32),
                   jax.ShapeDtypeStruct((1, nb), I32),
                   jax.ShapeDtypeStruct((1, LANES), I32),
                   jax.ShapeDtypeStruct((1, LANES), I32)),
        grid=(t // tm,),
        in_specs=[pl.BlockSpec((TOP_K, tm), lambda i: (0, i)),
                  pl.BlockSpec((TOP_K, tm), lambda i: (0, i)),
                  pl.BlockSpec((N_EXPERTS, LANES), lambda i: (0, 0))],
        out_specs=(pl.BlockSpec((TOP_K, tm), lambda i: (0, i)),
                   pl.BlockSpec((1, nb), lambda i: (0, 0)),
                   pl.BlockSpec((1, LANES), lambda i: (0, 0)),
                   pl.BlockSpec((1, LANES), lambda i: (0, 0))),
        compiler_params=_params("arbitrary"),
        name="moe_dest",
    )(idx_t, rank_t, counts)


def _dispatch_kernel(tail_ref, nused_ref, dest_ref, hf_ref, wgu_ref, wd_ref, xbuf_hbm, sh_ref, dest_smem, zero_scr,
                     idx_sem, zero_sem, row_sem):
    tm = dest_ref.shape[1]
    block_sublanes = E_BLOCK * ROW_TILE
    nblk = xbuf_hbm.shape[0] // block_sublanes

    @pl.when(pl.program_id(0) == 0)
    def _():
        zero_scr[...] = jnp.zeros_like(zero_scr)

        def zero_copy(row):
            start = pl.multiple_of(row * ROW_TILE, block_sublanes)
            return pltpu.make_async_copy(zero_scr, xbuf_hbm.at[pl.ds(start, block_sublanes)], zero_sem)

        def start(e, carry):
            @pl.when(tail_ref[e] >= 0)
            def _():
                zero_copy(tail_ref[e]).start()
            return carry

        def wait(e, carry):
            @pl.when(tail_ref[e] >= 0)
            def _():
                zero_copy(0).wait()
            return carry

        def start_unused(b, carry):
            zero_copy(b * E_BLOCK).start()
            return carry

        def wait_unused(b, carry):
            zero_copy(0).wait()
            return carry

        lax.fori_loop(0, N_EXPERTS, start, 0)
        lax.fori_loop(nused_ref[0], nblk, start_unused, 0)
        lax.fori_loop(0, N_EXPERTS, wait, 0)
        lax.fori_loop(nused_ref[0], nblk, wait_unused, 0)

    load = pltpu.make_async_copy(dest_ref, dest_smem, idx_sem)
    load.start()
    load.wait()

    def row(ref, r):
        return ref.at[pl.ds(pl.multiple_of(r * ROW_TILE, ROW_TILE), ROW_TILE)]

    for tok in range(tm):
        for k in range(TOP_K):
            pltpu.make_async_copy(row(hf_ref, tok), row(xbuf_hbm, dest_smem[k, tok]), row_sem).start(priority=k % 2)
    sh_ref[...] = _gated_ffn(_load_rows(hf_ref), wgu_ref, wd_ref)
    for k in range(TOP_K):
        pltpu.make_async_copy(hf_ref, xbuf_hbm.at[pl.ds(0, tm * ROW_TILE)], row_sem).wait()


def _dispatch(hf, dest_t, tail_start, nused, wgu, wd, npad):
    t = hf.shape[0] // ROW_TILE
    d = wgu.shape[0]
    tm = min(t, MOE_ROW_TILE)
    return pl.pallas_call(
        _dispatch_kernel,
        out_shape=(jax.ShapeDtypeStruct((npad * ROW_TILE, LANES), U32), jax.ShapeDtypeStruct((t, d), F32)),
        grid_spec=pltpu.PrefetchScalarGridSpec(
            num_scalar_prefetch=2,
            grid=(t // tm,),
            in_specs=[pl.BlockSpec((TOP_K, tm), lambda i, tail, nu: (0, i)),
                      pl.BlockSpec((tm * ROW_TILE, LANES), lambda i, tail, nu: (i, 0)),
                      pl.BlockSpec(wgu.shape, lambda i, tail, nu: (0, 0)),
                      pl.BlockSpec(wd.shape, lambda i, tail, nu: (0, 0))],
            out_specs=(pl.BlockSpec(memory_space=pl.ANY), pl.BlockSpec((tm, d), lambda i, tail, nu: (i, 0))),
            scratch_shapes=[pltpu.SMEM((TOP_K, tm), I32), pltpu.VMEM((E_BLOCK * ROW_TILE, LANES), U32),
                            pltpu.SemaphoreType.DMA, pltpu.SemaphoreType.DMA, pltpu.SemaphoreType.DMA]),
        compiler_params=_params("arbitrary"),
        name="moe_dispatch",
    )(tail_start, nused, dest_t, hf, wgu, wd)


def _expert_kernel(be_ref, nused_ref, x_ref, wgu_ref, wd_ref, y_ref, wgu_scr, wd_scr):
    b = pl.program_id(0)
    e = be_ref[b]
    e_prev = be_ref[jnp.maximum(b - 1, 0)]

    @pl.when((b == 0) | (e != e_prev))
    def _():
        wgu_scr[...] = wgu_ref[...].astype(BF16)
        wd_scr[...] = wd_ref[...].astype(BF16)

    @pl.when(b < nused_ref[0])
    def _():
        _store_rows(y_ref, _pack_bf16_halves(_gated_ffn(_load_rows(x_ref), wgu_scr, wd_scr)))

    @pl.when(b >= nused_ref[0])
    def _():
        y_ref[...] = jnp.zeros_like(y_ref)


def _gated_ffn(x_packed, wgu_ref, wd_ref):
    x_lo, x_hi = _unpack_bf16_halves(x_packed)
    half = x_packed.shape[1]
    gu = _dot(x_lo.astype(BF16), wgu_ref[0:half, :]) + _dot(x_hi.astype(BF16), wgu_ref[half:2 * half, :])
    ff = gu.shape[1] // 2
    gate = gu[:, :ff]
    act = gate * jax.nn.sigmoid(gate) * gu[:, ff:]
    return _dot(act.astype(BF16), wd_ref[...])


def _experts(xbuf, block_e, nused, w_gate_up, w_down, layer):
    nblk = xbuf.shape[0] // (E_BLOCK * ROW_TILE)
    _, _, d, ff2 = w_gate_up.shape
    last = lambda b, nu: jnp.minimum(b, nu[0] - 1)
    return pl.pallas_call(
        _expert_kernel,
        out_shape=jax.ShapeDtypeStruct(xbuf.shape, U32),
        grid_spec=pltpu.PrefetchScalarGridSpec(
            num_scalar_prefetch=2,
            grid=(nblk,),
            in_specs=[pl.BlockSpec((E_BLOCK * ROW_TILE, LANES), lambda b, be, nu: (last(b, nu), 0)),
                      pl.BlockSpec((None, None, d, ff2), lambda b, be, nu: (layer, be[b], 0, 0)),
                      pl.BlockSpec((None, None, ff2 // 2, d), lambda b, be, nu: (layer, be[b], 0, 0))],
            out_specs=pl.BlockSpec((E_BLOCK * ROW_TILE, LANES), lambda b, be, nu: (b, 0)),
            scratch_shapes=[pltpu.VMEM((d, ff2), BF16), pltpu.VMEM((ff2 // 2, d), BF16)]),
        compiler_params=_params("arbitrary"),
        name="moe_experts",
    )(block_e, nused, xbuf, w_gate_up, w_down)


def _combine_kernel(dest_ref, dest_next_ref, w_ref, x_ref, sh_ref, g_ref, ybuf_hbm, o_ref, rows_a, rows_b,
                    dest_smem, idx_sem, row_sems):
    i = pl.program_id(0)
    tm = x_ref.shape[0]

    def tile(r):
        return pl.ds(pl.multiple_of(r * ROW_TILE, ROW_TILE), ROW_TILE)

    def load_indices(indices_ref):
        load = pltpu.make_async_copy(indices_ref, dest_smem, idx_sem)
        load.start()
        load.wait()

    def row_copy(tok, k, rows, sem):
        pltpu.make_async_copy(ybuf_hbm.at[tile(dest_smem[k, tok])], rows.at[k, tile(tok)], sem).start(priority=k % 2)

    def wait_rows(rows, sem):
        for k in range(TOP_K):
            pltpu.make_async_copy(ybuf_hbm.at[pl.ds(0, tm * ROW_TILE)], rows.at[k], sem).wait()

    @pl.when(i == 0)
    def _():
        load_indices(dest_ref)

        def issue(tok, carry):
            for k in range(TOP_K):
                row_copy(tok, k, rows_a, row_sems.at[0])
            return carry

        lax.fori_loop(0, tm, issue, 0)

    load_indices(dest_next_ref)
    half = ROW_TILE * LANES
    per_chunk = tm // ROW_TILE

    def step(cur, cur_sem, nxt, nxt_sem):
        wait_rows(cur, cur_sem)
        weights = [jnp.broadcast_to(w_ref[:, k:k + 1], (tm, LANES)) for k in range(TOP_K)]
        for s in range(ROW_TILE):
            for tok in range(s * per_chunk, (s + 1) * per_chunk):
                for k in range(TOP_K):
                    row_copy(tok, k, nxt, nxt_sem)
            routed_lo = jnp.zeros((tm, LANES), F32)
            routed_hi = jnp.zeros((tm, LANES), F32)
            for k in range(TOP_K):
                lo, hi = _unpack_bf16_halves(cur[k, pl.ds(s, tm, stride=ROW_TILE), :])
                routed_lo = routed_lo + weights[k] * lo
                routed_hi = routed_hi + weights[k] * hi
            cl = slice(s * LANES, (s + 1) * LANES)
            ch = slice(half + s * LANES, half + (s + 1) * LANES)
            o_ref[:, cl] = x_ref[:, cl] + g_ref[:, cl] * (routed_lo + sh_ref[:, cl])
            o_ref[:, ch] = x_ref[:, ch] + g_ref[:, ch] * (routed_hi + sh_ref[:, ch])

        @pl.when(i + 1 == pl.num_programs(0))
        def _():
            wait_rows(nxt, nxt_sem)

    @pl.when(i % 2 == 0)
    def _():
        step(rows_a, row_sems.at[0], rows_b, row_sems.at[1])

    @pl.when(i % 2 == 1)
    def _():
        step(rows_b, row_sems.at[1], rows_a, row_sems.at[0])


def _combine(dest_t, w_tok, x, shared, gate, ybuf):
    t, d = x.shape
    tm = min(t, MOE_ROW_TILE)
    n_tiles = t // tm
    return pl.pallas_call(
        _combine_kernel,
        out_shape=jax.ShapeDtypeStruct((t, d), F32),
        grid=(n_tiles,),
        in_specs=[pl.BlockSpec((TOP_K, tm), lambda i: (0, i)),
                  pl.BlockSpec((TOP_K, tm), lambda i: (0, jnp.minimum(i + 1, n_tiles - 1))),
                  pl.BlockSpec((tm, TOP_K), lambda i: (i, 0)),
                  pl.BlockSpec((tm, d), lambda i: (i, 0)),
                  pl.BlockSpec((tm, d), lambda i: (i, 0)),
                  pl.BlockSpec((1, d), lambda i: (0, 0)),
                  pl.BlockSpec(memory_space=pl.ANY)],
        out_specs=pl.BlockSpec((tm, d), lambda i: (i, 0)),
        scratch_shapes=[pltpu.VMEM((TOP_K, tm * ROW_TILE, LANES), U32), pltpu.VMEM((TOP_K, tm * ROW_TILE, LANES), U32),
                        pltpu.SMEM((TOP_K, tm), I32), pltpu.SemaphoreType.DMA, pltpu.SemaphoreType.DMA((2,))],
        compiler_params=_params("arbitrary"),
        name="moe_combine",
    )(dest_t, dest_t, w_tok, x, shared, gate, ybuf)


def _moe_layer(x, shift, scale, gate, layer, w_router, router_bias, w_gate_up, w_down, ws_gate_up, ws_down):
    t, d = x.shape
    assert d == 2 * ROW_TILE * LANES, "a packed row must be exactly one (8, 128) tile of 32-bit words"
    npad = t * TOP_K + N_EXPERTS * E_BLOCK
    nblk = npad // E_BLOCK
    hf, idx_t, wts_t, rank_t, counts = _moe_pre(x, shift, scale, w_router, router_bias)
    dest_t, block_e, nused, tail_start = _moe_dest(idx_t, rank_t, counts, nblk)
    nused = nused.reshape(-1)[:1]
    xbuf, shared = _dispatch(hf, dest_t, tail_start.reshape(-1), nused, ws_gate_up.astype(BF16),
                             ws_down.astype(BF16), npad)
    ybuf = _experts(xbuf, block_e.reshape(-1), nused, w_gate_up, w_down, layer)
    return _combine(dest_t, wts_t.T, x, shared, gate, ybuf)


def _final_norm_kernel(x_ref, g_ref, o_ref):
    x = x_ref[...]
    o_ref[...] = x * lax.rsqrt(jnp.mean(x * x, axis=-1, keepdims=True) + EPS) * g_ref[...]


def _final_norm(x, gain):
    t, d = x.shape
    tm = min(t, PROJ_ROW_TILE)
    return pl.pallas_call(
        _final_norm_kernel,
        out_shape=jax.ShapeDtypeStruct((t, d), F32),
        grid=(t // tm,),
        in_specs=[pl.BlockSpec((tm, d), lambda i: (i, 0)), pl.BlockSpec((1, d), lambda i: (0, 0))],
        out_specs=pl.BlockSpec((tm, d), lambda i: (i, 0)),
        compiler_params=_params("parallel"),
        name="final_norm",
    )(x, gain.reshape(1, d))


def _even_layer(x, mod, cos, sin, w_in, b_if, mlstm_norm, q_norm, kv_norm, w_uq, w_ukv, w_out):
    t, d = x.shape
    sh1, sc1, g1 = mod[0], mod[1], mod[2]
    nq, nv = M_HEADS * M_QK, M_HEADS * M_V
    o = 0
    cols = []
    for sz in (nq, nq, nv, nv, M_HEADS, M_HEADS, A_Q_LORA, A_KV_LORA, A_ROPE):
        cols.append(w_in[:, o:o + sz])
        o += sz
    mq, mk, mv, mo, mi, mf, cq, ckv, kr = cols
    w_a = jnp.concatenate([mq, mk, mv, mo], axis=1).astype(BF16)
    zeros = lambda n: jnp.zeros((d, n), F32)
    w_b = jnp.concatenate([cq, ckv, kr, zeros(LANES - A_ROPE), mi, mf, zeros(LANES - 2 * M_HEADS)], axis=1).astype(BF16)
    gate_col_block = (A_Q_LORA + A_KV_LORA + LANES) // LANES
    proj_a, proj_b = _norm_matmul_pair(x, sh1, sc1, w_a, w_b, BF16, F32, "even_in")
    gate_bias = jnp.zeros((1, LANES), F32).at[0, :2 * M_HEADS].set(b_if.astype(F32))
    hm = _mlstm(proj_a, proj_b, gate_col_block, gate_bias, mlstm_norm.astype(F32).reshape(1, nv))

    qk = A_NOPE + A_ROPE
    wq = jnp.pad(w_uq.reshape(A_Q_LORA, A_HEADS, qk), ((0, 0), (0, 0), (0, A_QK_PAD - qk)))
    wq = wq.reshape(A_Q_LORA, A_HEADS * A_QK_PAD).astype(BF16)
    q, k, v = _mla_up(proj_b, q_norm.astype(F32).reshape(1, -1), kv_norm.astype(F32).reshape(1, -1),
                      wq, w_ukv.astype(BF16), cos, sin)
    ha = _mla_flash(q, k, v)
    w_out = w_out.astype(BF16)
    return _proj_residual([hm, ha], [w_out[:nv], w_out[nv:]], x, g1, "even_out")


def _odd_layer(x, mod, cos, sin, w_qkv, b_qkv, sinks, w_o):
    sh1, sc1, g1 = mod[0], mod[1], mod[2]
    qkv = _norm_matmul(x, sh1, sc1, w_qkv.astype(BF16), b_qkv.astype(F32).reshape(1, -1), BF16, "odd_qkv")
    o = _swa(qkv, cos, sin, sinks)
    return _proj_residual([o], [w_o.astype(BF16)], x, g1, "odd_out")


def kernel(x, c, positions, w_ada, b_ada, a_w_in, a_b_if, a_mlstm_norm, a_q_norm, a_kv_norm, a_w_uq, a_w_ukv,
           a_w_out, s_w_qkv, s_b_qkv, s_sinks, s_w_o, e_w_router, e_router_bias, e_w_gate_up, e_w_down,
           e_ws_gate_up, e_ws_down, final_norm):
    batch, t, d = x.shape
    assert batch == 1, "kernels are written for a single sequence"
    depth = w_ada.shape[0]
    xs = x.reshape(t, d)
    mods = _ada(c, w_ada, b_ada).reshape(depth, 6, 1, d)
    cos, sin = _rope_tables(positions.reshape(t))
    for layer in range(depth):
        mod = mods[layer]
        if layer % 2 == 0:
            e = layer // 2
            xs = _even_layer(xs, mod, cos, sin, a_w_in[e], a_b_if[e], a_mlstm_norm[e], a_q_norm[e], a_kv_norm[e],
                             a_w_uq[e], a_w_ukv[e], a_w_out[e])
        else:
            o = layer // 2
            xs = _odd_layer(xs, mod, cos, sin, s_w_qkv[o], s_b_qkv[o], s_sinks[o], s_w_o[o])
        xs = _moe_layer(xs, mod[3], mod[4], mod[5], layer, e_w_router[layer], e_router_bias[layer],
                        e_w_gate_up, e_w_down, e_ws_gate_up[layer], e_ws_down[layer])
    return _final_norm(xs, final_norm.astype(F32)).reshape(batch, t, d)
```
